```python
import jax, jax.numpy as jnp
from jax import lax
import numpy as np

D_MODEL = 2048
BATCH = 8
SEQ = 8192
DEPTH = 4

GRID_W = 64
MLA_HEADS = 8
Q_LORA = 512
KV_LORA = 512
QK_NOPE = 128
QK_ROPE = 64
V_HEAD = 128
ROPE_THETA = 10000.0
Q_BLOCK = 128
NA_HEADS = 8
NA_HEAD_DIM = 128
NA_KH = 8
NA_KW = 16
D_FF = 4 * D_MODEL
EPS = 1e-6

MLA_W = MLA_HEADS * V_HEAD
NA_W = NA_HEADS * NA_HEAD_DIM
IN_WIDTHS = (Q_LORA, KV_LORA, QK_ROPE, NA_W, NA_W, NA_W, D_MODEL, D_MODEL)
IN_SPLITS = tuple(int(v) for v in np.cumsum(IN_WIDTHS)[:-1])
IN_TOTAL = int(sum(IN_WIDTHS))

kernel_name = "hybrid_mla_natten_sqrelu_encoder"


def rmsnorm(x, g):
    xf = x.astype(jnp.float32)
    y = xf * lax.rsqrt(jnp.mean(xf * xf, axis=-1, keepdims=True) + EPS)
    return (y * g.astype(jnp.float32)).astype(x.dtype)


def rope(x, cos, sin):
    x1, x2 = jnp.split(x, 2, axis=-1)
    return jnp.concatenate([x1 * cos - x2 * sin, x2 * cos + x1 * sin], axis=-1)


def mla_attention(q_nope, q_pe, k_nope, k_pe, v):
    B, S, H, _ = q_nope.shape
    nblk = S // Q_BLOCK
    scale = (QK_NOPE + QK_ROPE) ** -0.5
    qn_b = q_nope.reshape(B, nblk, Q_BLOCK, H, QK_NOPE).transpose(1, 0, 2, 3, 4)
    qp_b = q_pe.reshape(B, nblk, Q_BLOCK, H, QK_ROPE).transpose(1, 0, 2, 3, 4)

    def block(args):
        qn, qp = args
        s = (jnp.einsum('bqhd,bkhd->bhqk', qn, k_nope)
             + jnp.einsum('bqhd,bkd->bhqk', qp, k_pe)).astype(jnp.float32) * scale
        p = jax.nn.softmax(s, axis=-1).astype(v.dtype)
        return jnp.einsum('bhqk,bkhd->bqhd', p, v)

    out = lax.map(block, (qn_b, qp_b))
    return out.transpose(1, 0, 2, 3, 4).reshape(B, S, H * V_HEAD)


def neighbourhood_attention(q, k, v, rpb):
    B, S, H, d = q.shape
    rows = S // GRID_W
    kh = min(NA_KH, rows)
    r = jnp.arange(rows)
    row_start = jnp.clip(r - kh // 2, 0, rows - kh)
    row_idx = row_start[:, None] + jnp.arange(kh)[None, :]
    c = jnp.arange(GRID_W)
    col_start = jnp.clip(c - NA_KW // 2, 0, GRID_W - NA_KW)
    col_ok = (c[None, :] >= col_start[:, None]) & (c[None, :] < col_start[:, None] + NA_KW)

    qg = q.reshape(B, rows, GRID_W, H, d)
    kg = k.reshape(B, rows, GRID_W, H, d)[:, row_idx]
    vg = v.reshape(B, rows, GRID_W, H, d)[:, row_idx]
    s = jnp.einsum('brqhd,brikhd->brhqik', qg, kg).astype(jnp.float32) * (d ** -0.5)

    dy = row_idx - r[:, None] + (NA_KH - 1)
    dx = jnp.clip(c[None, :] - c[:, None], -(NA_KW - 1), NA_KW - 1) + (NA_KW - 1)
    bias = rpb.astype(jnp.float32)[:, dy][..., dx]
    bias = bias.transpose(1, 0, 3, 2, 4)
    s = jnp.where(col_ok[:, None, :], s + bias[None], -jnp.inf)
    p = jax.nn.softmax(s.reshape(B, rows, H, GRID_W, kh * GRID_W), axis=-1)
    p = p.reshape(B, rows, H, GRID_W, kh, GRID_W).astype(v.dtype)
    out = jnp.einsum('brhqik,brikhd->brqhd', p, vg)
    return out.reshape(B, S, H * d)


def _fwd_setup_inputs(seed: int = 0) -> dict:
    key = jax.random.key(seed)
    ks = jax.random.split(key, 16)
    f32 = jnp.float32

    def w(k, shape, fan_in):
        return jax.random.normal(k, shape, f32) * fan_in ** -0.5

    def gain(k, shape):
        return 1.0 + 0.01 * jax.random.normal(k, shape, f32)

    return {
        "x": jax.random.normal(ks[0], (BATCH, SEQ, D_MODEL), f32),
        "norm_mix": gain(ks[1], (DEPTH, D_MODEL)),
        "w_in": w(ks[2], (DEPTH, D_MODEL, IN_TOTAL), D_MODEL),
        "norm_qa": gain(ks[3], (DEPTH, Q_LORA)),
        "w_uq": w(ks[4], (DEPTH, Q_LORA, MLA_HEADS * (QK_NOPE + QK_ROPE)), Q_LORA),
        "norm_kva": gain(ks[5], (DEPTH, KV_LORA)),
        "w_ukv": w(ks[6], (DEPTH, KV_LORA, MLA_HEADS * (QK_NOPE + V_HEAD)), KV_LORA),
        "rpb": 0.02 * jax.random.normal(ks[7], (DEPTH, NA_HEADS, 2 * NA_KH - 1, 2 * NA_KW - 1), f32),
        "w_o_mla": w(ks[8], (DEPTH, MLA_W, D_MODEL), MLA_W),
        "w_o_na": w(ks[9], (DEPTH, NA_W, D_MODEL), NA_W),
        "w_out": w(ks[10], (DEPTH, D_MODEL, D_MODEL), D_MODEL),
        "norm_mlp": gain(ks[11], (DEPTH, D_MODEL)),
        "w_ff1": w(ks[12], (DEPTH, D_MODEL, D_FF), D_MODEL),
        "w_ff2": w(ks[13], (DEPTH, D_FF, D_MODEL), D_FF),
        "norm_final": gain(ks[14], (D_MODEL,)),
    }


def _fwd_reference(x, norm_mix, w_in, norm_qa, w_uq, norm_kva, w_ukv, rpb, w_o_mla, w_o_na,
              w_out, norm_mlp, w_ff1, w_ff2, norm_final):
    B, S, _ = x.shape
    pos = jnp.arange(S, dtype=jnp.float32)
    inv_freq = 1.0 / (ROPE_THETA ** (jnp.arange(0, QK_ROPE, 2, dtype=jnp.float32) / QK_ROPE))
    ang = pos[:, None] * inv_freq[None, :]
    cos = jnp.cos(ang).astype(x.dtype)
    sin = jnp.sin(ang).astype(x.dtype)

    for l in range(DEPTH):
        u = rmsnorm(x, norm_mix[l])
        proj = u @ w_in[l]
        c_q, c_kv, k_pe, q_na, k_na, v_na, gate_a, gate_b = jnp.split(proj, IN_SPLITS, axis=-1)

        q = (rmsnorm(c_q, norm_qa[l]) @ w_uq[l]).reshape(B, S, MLA_HEADS, QK_NOPE + QK_ROPE)
        kv = (rmsnorm(c_kv, norm_kva[l]) @ w_ukv[l]).reshape(B, S, MLA_HEADS, QK_NOPE + V_HEAD)
        q_nope, q_pe = q[..., :QK_NOPE], q[..., QK_NOPE:]
        k_nope, v = kv[..., :QK_NOPE], kv[..., QK_NOPE:]
        q_pe = rope(q_pe, cos[:, None, :], sin[:, None, :])
        k_pe = rope(k_pe, cos, sin)
        y_a = mla_attention(q_nope, q_pe, k_nope, k_pe, v) @ w_o_mla[l]

        hs = (B, S, NA_HEADS, NA_HEAD_DIM)
        y_b = neighbourhood_attention(q_na.reshape(hs), k_na.reshape(hs), v_na.reshape(hs), rpb[l]) @ w_o_na[l]

        merged = jax.nn.sigmoid(gate_a) * y_a + jax.nn.sigmoid(gate_b) * y_b
        x = x + merged @ w_out[l]

        h = rmsnorm(x, norm_mlp[l]) @ w_ff1[l]
        x = x + jnp.square(jax.nn.relu(h)) @ w_ff2[l]

    return rmsnorm(x, norm_final)


import jax as _jax
import jax.numpy as _jnp

TWIN_FORMAT = 'train_step'
FWD_PARAMS = ['x', 'norm_mix', 'w_in', 'norm_qa', 'w_uq', 'norm_kva', 'w_ukv', 'rpb', 'w_o_mla', 'w_o_na', 'w_out', 'norm_mlp', 'w_ff1', 'w_ff2', 'norm_final']
TWIN_WEIGHTS = ['norm_mix', 'w_in', 'norm_qa', 'w_uq', 'norm_kva', 'w_ukv', 'rpb', 'w_o_mla', 'w_o_na', 'w_out', 'norm_mlp', 'w_ff1', 'w_ff2', 'norm_final']
TWIN_DIFF_INPUT = 'x'
TWIN_INPUTS = ['x', 'norm_mix', 'w_in', 'norm_qa', 'w_uq', 'norm_kva', 'w_ukv', 'rpb', 'w_o_mla', 'w_o_na', 'w_out', 'norm_mlp', 'w_ff1', 'w_ff2', 'norm_final', 'loss_target', 'm_norm_mix', 'm_w_in', 'm_norm_qa', 'm_w_uq', 'm_norm_kva', 'm_w_ukv', 'm_rpb', 'm_w_o_mla', 'm_w_o_na', 'm_w_out', 'm_norm_mlp', 'm_w_ff1', 'm_w_ff2', 'm_norm_final', 'v_norm_mix', 'v_w_in', 'v_norm_qa', 'v_w_uq', 'v_norm_kva', 'v_w_ukv', 'v_rpb', 'v_w_o_mla', 'v_w_o_na', 'v_w_out', 'v_norm_mlp', 'v_w_ff1', 'v_w_ff2', 'v_norm_final']
TWIN_OUTPUTS = ['loss', 'grad_x', 'grad_norm_mix', 'grad_w_in', 'grad_norm_qa', 'grad_w_uq', 'grad_norm_kva', 'grad_w_ukv', 'grad_rpb', 'grad_w_o_mla', 'grad_w_o_na', 'grad_w_out', 'grad_norm_mlp', 'grad_w_ff1', 'grad_w_ff2', 'grad_norm_final', 'delta_norm_mix', 'delta_w_in', 'delta_norm_qa', 'delta_w_uq', 'delta_norm_kva', 'delta_w_ukv', 'delta_rpb', 'delta_w_o_mla', 'delta_w_o_na', 'delta_w_out', 'delta_norm_mlp', 'delta_w_ff1', 'delta_w_ff2', 'delta_norm_final', 'new_m_norm_mix', 'new_m_w_in', 'new_m_norm_qa', 'new_m_w_uq', 'new_m_norm_kva', 'new_m_w_ukv', 'new_m_rpb', 'new_m_w_o_mla', 'new_m_w_o_na', 'new_m_w_out', 'new_m_norm_mlp', 'new_m_w_ff1', 'new_m_w_ff2', 'new_m_norm_final', 'new_v_norm_mix', 'new_v_w_in', 'new_v_norm_qa', 'new_v_w_uq', 'new_v_norm_kva', 'new_v_w_ukv', 'new_v_rpb', 'new_v_w_o_mla', 'new_v_w_o_na', 'new_v_w_out', 'new_v_norm_mlp', 'new_v_w_ff1', 'new_v_w_ff2', 'new_v_norm_final']
TWIN_LEAF_KINDS = {'loss': 'loss', 'grad_x': 'grad_x', 'grad_norm_mix': 'grad_w', 'grad_w_in': 'grad_w', 'grad_norm_qa': 'grad_w', 'grad_w_uq': 'grad_w', 'grad_norm_kva': 'grad_w', 'grad_w_ukv': 'grad_w', 'grad_rpb': 'grad_w', 'grad_w_o_mla': 'grad_w', 'grad_w_o_na': 'grad_w', 'grad_w_out': 'grad_w', 'grad_norm_mlp': 'grad_w', 'grad_w_ff1': 'grad_w', 'grad_w_ff2': 'grad_w', 'grad_norm_final': 'grad_w', 'delta_norm_mix': 'delta_w', 'delta_w_in': 'delta_w', 'delta_norm_qa': 'delta_w', 'delta_w_uq': 'delta_w', 'delta_norm_kva': 'delta_w', 'delta_w_ukv': 'delta_w', 'delta_rpb': 'delta_w', 'delta_w_o_mla': 'delta_w', 'delta_w_o_na': 'delta_w', 'delta_w_out': 'delta_w', 'delta_norm_mlp': 'delta_w', 'delta_w_ff1': 'delta_w', 'delta_w_ff2': 'delta_w', 'delta_norm_final': 'delta_w', 'new_m_norm_mix': 'new_m', 'new_m_w_in': 'new_m', 'new_m_norm_qa': 'new_m', 'new_m_w_uq': 'new_m', 'new_m_norm_kva': 'new_m', 'new_m_w_ukv': 'new_m', 'new_m_rpb': 'new_m', 'new_m_w_o_mla': 'new_m', 'new_m_w_o_na': 'new_m', 'new_m_w_out': 'new_m', 'new_m_norm_mlp': 'new_m', 'new_m_w_ff1': 'new_m', 'new_m_w_ff2': 'new_m', 'new_m_norm_final': 'new_m', 'new_v_norm_mix': 'new_v', 'new_v_w_in': 'new_v', 'new_v_norm_qa': 'new_v', 'new_v_w_uq': 'new_v', 'new_v_norm_kva': 'new_v', 'new_v_w_ukv': 'new_v', 'new_v_rpb': 'new_v', 'new_v_w_o_mla': 'new_v', 'new_v_w_o_na': 'new_v', 'new_v_w_out': 'new_v', 'new_v_norm_mlp': 'new_v', 'new_v_w_ff1': 'new_v', 'new_v_w_ff2': 'new_v', 'new_v_norm_final': 'new_v'}


def _forward(args):
    return _fwd_reference(*[args[k] for k in FWD_PARAMS])


def _output_shape():
    def fwd():
        inp = _fwd_setup_inputs(0)
        return _fwd_reference(*[inp[k] for k in FWD_PARAMS])
    out = _jax.eval_shape(fwd)
    return out.shape, out.dtype

N_MICROBATCH = 1
ADAM_LR = 0.001
ADAM_B1 = 0.9
ADAM_B2 = 0.999
ADAM_EPS = 1e-08
ADAM_WD = 0.01
ADAM_STEP = 10
PER_EXAMPLE_BATCH_AXIS = {'x': 0, 'loss_target': 0}
SHARED_INPUTS = []
_WEIGHT_DTYPES = {'norm_mix': _jnp.float32, 'w_in': _jnp.float32, 'norm_qa': _jnp.float32, 'w_uq': _jnp.float32, 'norm_kva': _jnp.float32, 'w_ukv': _jnp.float32, 'rpb': _jnp.float32, 'w_o_mla': _jnp.float32, 'w_o_na': _jnp.float32, 'w_out': _jnp.float32, 'norm_mlp': _jnp.float32, 'w_ff1': _jnp.float32, 'w_ff2': _jnp.float32, 'norm_final': _jnp.float32}
MOMENT_SCALE = {'norm_mix': 2.792406e-02, 'w_in': 1.383121e-02, 'norm_qa': 1.058760e-02, 'w_uq': 6.258144e-03, 'norm_kva': 3.495293e-02, 'w_ukv': 1.779641e-02, 'rpb': 7.019969e-03, 'w_o_mla': 1.788479e-02, 'w_o_na': 1.828823e-02, 'w_out': 2.431662e-02, 'norm_mlp': 9.500537e-02, 'w_ff1': 4.773074e-02, 'w_ff2': 9.455452e-02, 'norm_final': 3.328121e+01}


def _to_microbatches(a, axis):
    t = _jnp.moveaxis(a, axis, 0)
    t = t.reshape((N_MICROBATCH, t.shape[0] // N_MICROBATCH) + t.shape[1:])
    return _jnp.moveaxis(t, 1, axis + 1)


def setup_inputs(seed: int = 0) -> dict:
    inp = _fwd_setup_inputs(seed)
    key = _jax.random.fold_in(_jax.random.key(seed), 7919)
    shape, _ = _output_shape()
    out = dict(inp)
    out["loss_target"] = _jax.random.normal(_jax.random.fold_in(key, 0), shape, _jnp.float32)
    for i, name in enumerate(TWIN_WEIGHTS):
        w = inp[name].astype(_jnp.float32)
        if MOMENT_SCALE is None:
            s = _jnp.sqrt(_jnp.mean(_jnp.square(w)) + 1e-30)
        else:
            s = MOMENT_SCALE[name]
        km, kv = _jax.random.split(_jax.random.fold_in(key, i + 1))
        out[name] = w
        out["m_" + name] = s * _jax.random.normal(km, w.shape, _jnp.float32)
        out["v_" + name] = (s * s) * _jax.random.uniform(kv, w.shape, _jnp.float32, 0.5, 1.5)
    if N_MICROBATCH > 1:
        for name, axis in PER_EXAMPLE_BATCH_AXIS.items():
            out[name] = _to_microbatches(out[name], axis)
    return {'x': out['x'], 'norm_mix': out['norm_mix'], 'w_in': out['w_in'], 'norm_qa': out['norm_qa'], 'w_uq': out['w_uq'], 'norm_kva': out['norm_kva'], 'w_ukv': out['w_ukv'], 'rpb': out['rpb'], 'w_o_mla': out['w_o_mla'], 'w_o_na': out['w_o_na'], 'w_out': out['w_out'], 'norm_mlp': out['norm_mlp'], 'w_ff1': out['w_ff1'], 'w_ff2': out['w_ff2'], 'norm_final': out['norm_final'], 'loss_target': out['loss_target'], 'm_norm_mix': out['m_norm_mix'], 'm_w_in': out['m_w_in'], 'm_norm_qa': out['m_norm_qa'], 'm_w_uq': out['m_w_uq'], 'm_norm_kva': out['m_norm_kva'], 'm_w_ukv': out['m_w_ukv'], 'm_rpb': out['m_rpb'], 'm_w_o_mla': out['m_w_o_mla'], 'm_w_o_na': out['m_w_o_na'], 'm_w_out': out['m_w_out'], 'm_norm_mlp': out['m_norm_mlp'], 'm_w_ff1': out['m_w_ff1'], 'm_w_ff2': out['m_w_ff2'], 'm_norm_final': out['m_norm_final'], 'v_norm_mix': out['v_norm_mix'], 'v_w_in': out['v_w_in'], 'v_norm_qa': out['v_norm_qa'], 'v_w_uq': out['v_w_uq'], 'v_norm_kva': out['v_norm_kva'], 'v_w_ukv': out['v_w_ukv'], 'v_rpb': out['v_rpb'], 'v_w_o_mla': out['v_w_o_mla'], 'v_w_o_na': out['v_w_o_na'], 'v_w_out': out['v_w_out'], 'v_norm_mlp': out['v_norm_mlp'], 'v_w_ff1': out['v_w_ff1'], 'v_w_ff2': out['v_w_ff2'], 'v_norm_final': out['v_norm_final']}


def _loss(weights, diff, rest, loss_target):
    with _jax.named_scope("forward"):
        args = {**rest, TWIN_DIFF_INPUT: diff, **{k: w.astype(_WEIGHT_DTYPES[k]) for k, w in weights.items()}}
        y = _forward(args)
    with _jax.named_scope("loss_head"):
        err = _jnp.square(y.astype(_jnp.float32) - loss_target)
        return 0.5 * _jnp.sum(_jnp.mean(err, axis=-1)) if err.ndim else 0.5 * err


def _adamw(w, g, m, v):
    m = ADAM_B1 * m + (1.0 - ADAM_B1) * g
    v = ADAM_B2 * v + (1.0 - ADAM_B2) * _jnp.square(g)
    m_hat = m / (1.0 - ADAM_B1 ** ADAM_STEP)
    v_hat = v / (1.0 - ADAM_B2 ** ADAM_STEP)
    delta = -ADAM_LR * (m_hat / (_jnp.sqrt(v_hat) + ADAM_EPS) + ADAM_WD * w)
    return delta, m, v


def reference(x, norm_mix, w_in, norm_qa, w_uq, norm_kva, w_ukv, rpb, w_o_mla, w_o_na, w_out, norm_mlp, w_ff1, w_ff2, norm_final, loss_target, m_norm_mix, m_w_in, m_norm_qa, m_w_uq, m_norm_kva, m_w_ukv, m_rpb, m_w_o_mla, m_w_o_na, m_w_out, m_norm_mlp, m_w_ff1, m_w_ff2, m_norm_final, v_norm_mix, v_w_in, v_norm_qa, v_w_uq, v_norm_kva, v_w_ukv, v_rpb, v_w_o_mla, v_w_o_na, v_w_out, v_norm_mlp, v_w_ff1, v_w_ff2, v_norm_final):
    given = dict(x=x, norm_mix=norm_mix, w_in=w_in, norm_qa=norm_qa, w_uq=w_uq, norm_kva=norm_kva, w_ukv=w_ukv, rpb=rpb, w_o_mla=w_o_mla, w_o_na=w_o_na, w_out=w_out, norm_mlp=norm_mlp, w_ff1=w_ff1, w_ff2=w_ff2, norm_final=norm_final, loss_target=loss_target, m_norm_mix=m_norm_mix, m_w_in=m_w_in, m_norm_qa=m_norm_qa, m_w_uq=m_w_uq, m_norm_kva=m_norm_kva, m_w_ukv=m_w_ukv, m_rpb=m_rpb, m_w_o_mla=m_w_o_mla, m_w_o_na=m_w_o_na, m_w_out=m_w_out, m_norm_mlp=m_norm_mlp, m_w_ff1=m_w_ff1, m_w_ff2=m_w_ff2, m_norm_final=m_norm_final, v_norm_mix=v_norm_mix, v_w_in=v_w_in, v_norm_qa=v_norm_qa, v_w_uq=v_w_uq, v_norm_kva=v_norm_kva, v_w_ukv=v_w_ukv, v_rpb=v_rpb, v_w_o_mla=v_w_o_mla, v_w_o_na=v_w_o_na, v_w_out=v_w_out, v_norm_mlp=v_norm_mlp, v_w_ff1=v_w_ff1, v_w_ff2=v_w_ff2, v_norm_final=v_norm_final)
    weights = {n: given[n] for n in TWIN_WEIGHTS}
    shared = {n: given[n] for n in SHARED_INPUTS}
    per_example = {n: given[n] for n in ['x']}
    grad_fn = _jax.value_and_grad(_loss, argnums=(0, 1))

    def one_microbatch(ex, loss_target):
        ex = dict(ex)
        diff = ex.pop(TWIN_DIFF_INPUT)
        return grad_fn(weights, diff, {**shared, **ex}, loss_target)

    if N_MICROBATCH == 1:
        loss, (grad_w, grad_x) = one_microbatch(per_example, given["loss_target"])
    else:
        def body(carry, xs):
            loss_sum, grad_sum = carry
            l_k, (gw_k, gx_k) = one_microbatch(xs[0], xs[1])
            with _jax.named_scope("update"):
                return (loss_sum + l_k, _jax.tree.map(_jnp.add, grad_sum, gw_k)), gx_k

        init = (_jnp.zeros((), _jnp.float32), _jax.tree.map(_jnp.zeros_like, weights))
        (loss, grad_w), grad_x = _jax.lax.scan(body, init, (per_example, given["loss_target"]))
    with _jax.named_scope("update"):
        delta_w, new_m, new_v = {}, {}, {}
        for n in TWIN_WEIGHTS:
            delta_w[n], new_m[n], new_v[n] = _adamw(weights[n], grad_w[n], given["m_" + n], given["v_" + n])
    return (loss, grad_x, *[grad_w[n] for n in TWIN_WEIGHTS], *[delta_w[n] for n in TWIN_WEIGHTS],
            *[new_m[n] for n in TWIN_WEIGHTS], *[new_v[n] for n in TWIN_WEIGHTS])
```

```python
import functools

import numpy as np
import jax
import jax.numpy as jnp
from jax import lax
from jax.experimental import pallas as pl
from jax.experimental.pallas import tpu as pltpu

F32 = jnp.float32
BF16 = jnp.bfloat16
MESH_AXES = ("x", "y", "c")
N_DEV = 8
LANES = 128

QK_NOPE = 128
QK_ROPE = 64
HALF_ROPE = QK_ROPE // 2
V_HEAD = 128
QK_DIM = QK_NOPE + QK_ROPE
NA_HEAD_DIM = 128
GRID_W = 64
NA_KH = 8
NA_KW = 16
NA_SPAN_ROWS = 2 * NA_KH
NA_SPAN = NA_SPAN_ROWS * GRID_W
NA_PAD_LO = (NA_KH - 1) * GRID_W
NA_PAD_HI = NA_KH * GRID_W
ROPE_THETA = 10000.0
EPS = 1e-6
NEG = -1e30

ADAM_LR = 0.001
ADAM_B1 = 0.9
ADAM_B2 = 0.999
ADAM_EPS = 1e-08
ADAM_WD = 0.01
ADAM_STEP = 10

VMEM_LIMIT_V7X = 56 * 1024 * 1024

NN = (((1,), (0,)), ((), ()))
NT = (((1,), (1,)), ((), ()))
TN = (((0,), (0,)), ((), ()))


def _pick(dim, target, align=LANES):
    if dim <= target:
        return dim
    t = (target // align) * align
    while t >= align:
        if dim % t == 0:
            return t
        t -= align
    return dim


def _params(sem):
    return pltpu.CompilerParams(dimension_semantics=sem, vmem_limit_bytes=VMEM_LIMIT_V7X)


def mm(a, b, *, mode, name, out_dtypes=(F32,), epi=None, extras=(), tm=1024, tn=1024, tk=512, exact=False):
    if mode == "nn":
        (m, k), (k2, n) = a.shape, b.shape
    elif mode == "nt":
        (m, k), (n, k2) = a.shape, b.shape
    else:
        (k, m), (k2, n) = a.shape, b.shape
    assert k == k2, (a.shape, b.shape, mode)
    tm, tn, tk = _pick(m, tm), _pick(n, tn), _pick(k, tk)
    nk = k // tk
    a_spec = pl.BlockSpec((tk, tm), lambda i, j, s: (s, i)) if mode == "tn" else pl.BlockSpec((tm, tk), lambda i, j, s: (i, s))
    b_spec = pl.BlockSpec((tn, tk), lambda i, j, s: (j, s)) if mode == "nt" else pl.BlockSpec((tk, tn), lambda i, j, s: (s, j))
    tile = pl.BlockSpec((tm, tn), lambda i, j, s: (i, j))
    dims = {"nn": NN, "nt": NT, "tn": TN}[mode]
    n_extra, n_out = len(extras), len(out_dtypes)

    def body(a_ref, b_ref, *rest):
        extra_refs, out_refs, acc = rest[:n_extra], rest[n_extra:n_extra + n_out], rest[-1]
        step = pl.program_id(2)

        @pl.when(step == 0)
        def _():
            acc[...] = jnp.zeros_like(acc)

        if exact:
            acc[...] += lax.dot_general(a_ref[...], b_ref[...], dims, precision=lax.Precision.HIGHEST,
                                        preferred_element_type=F32)
        else:
            acc[...] += lax.dot_general(a_ref[...].astype(BF16), b_ref[...].astype(BF16), dims,
                                        preferred_element_type=F32)

        @pl.when(step == nk - 1)
        def _():
            r = acc[...]
            res = (r,) if epi is None else epi(r, *[e[...] for e in extra_refs])
            for o, v in zip(out_refs, res):
                o[...] = v.astype(o.dtype)

    outs = pl.pallas_call(
        body, name=name, grid=(m // tm, n // tn, nk),
        in_specs=[a_spec, b_spec] + [tile] * n_extra,
        out_specs=[tile] * n_out,
        out_shape=[jax.ShapeDtypeStruct((m, n), d) for d in out_dtypes],
        scratch_shapes=[pltpu.VMEM((tm, tn), F32)],
        compiler_params=_params(("parallel", "parallel", "arbitrary")),
    )(a, b, *extras)
    return outs[0] if n_out == 1 else outs


def blockwise(fn, ins, outs, *, grid, name, sums=()):
    n_in, n_axes = len(ins), len(grid)

    def body(*refs):
        res = fn(*[r[...] for r in refs[:n_in]])
        first = functools.reduce(jnp.logical_and, [pl.program_id(ax) == 0 for ax in range(n_axes)])
        for idx, (o, v) in enumerate(zip(refs[n_in:], res)):
            if idx in sums:
                @pl.when(first)
                def _(o=o):
                    o[...] = jnp.zeros_like(o)

                o[...] += v.astype(o.dtype)
            else:
                o[...] = v.astype(o.dtype)

    sem = ("arbitrary" if sums else "parallel",) * n_axes
    res = pl.pallas_call(
        body, name=name, grid=grid,
        in_specs=[pl.BlockSpec(blk, imap) for _, blk, imap in ins],
        out_specs=[pl.BlockSpec(blk, imap) for _, _, blk, imap in outs],
        out_shape=[jax.ShapeDtypeStruct(shape, dt) for shape, dt, _, _ in outs],
        compiler_params=_params(sem),
    )(*[a for a, _, _ in ins])
    return res


def rows(arr, tr, width=None, col=0):
    width = arr.shape[1] if width is None else width
    return (arr, (tr, width), lambda i, col=col: (i, col))


def whole(arr):
    return (arr, arr.shape, lambda i: (0, 0))


def out_rows(n_rows, width, dtype, tr):
    return ((n_rows, width), dtype, (tr, width), lambda i: (i, 0))


def out_sum(shape, dtype=F32):
    return (shape, dtype, shape, lambda i: (0, 0))


def _rstd(x):
    return lax.rsqrt(jnp.mean(x * x, axis=-1, keepdims=True) + EPS)


def _colsum(v):
    return jnp.sum(v, axis=0, keepdims=True)


def rmsnorm_fwd(x, g, *, width=None, col=0, tr=256, name):
    n = x.shape[0]
    tr = _pick(n, tr, 8)
    width = x.shape[1] if width is None else width

    def fn(xv, gv):
        return ((xv * _rstd(xv)) * gv,)

    return blockwise(fn, [rows(x, tr, width, col), whole(g)], [out_rows(n, width, BF16, tr)],
                     grid=(n // tr,), name=name)[0]


def rmsnorm_bwd(x, g, dy, res=None, *, width=None, col=0, out_dtype=F32, tr=256, name):
    n = x.shape[0]
    tr = _pick(n, tr, 8)
    width = x.shape[1] if width is None else width

    def fn(xv, gv, dyv, *resv):
        dyv = dyv.astype(F32)
        r = _rstd(xv)
        xhat = xv * r
        dxhat = dyv * gv
        dx = r * (dxhat - xhat * jnp.mean(dxhat * xhat, axis=-1, keepdims=True))
        if resv:
            dx = dx + resv[0]
        return dx, _colsum(dyv * xhat)

    ins = [rows(x, tr, width, col), whole(g), rows(dy, tr)] + ([rows(res, tr)] if res is not None else [])
    return blockwise(fn, ins, [out_rows(n, width, out_dtype, tr), out_sum((1, width))],
                     grid=(n // tr,), name=name, sums=(1,))


def rope_pair(x1, x2, cos, sin, *, tr=512, name):
    n, w = x1.shape
    tr = _pick(n, tr, 8)

    def fn(a, b, c, s):
        a, b = a.astype(F32), b.astype(F32)
        return a * c - b * s, b * c + a * s

    return blockwise(fn, [rows(x1, tr), rows(x2, tr), rows(cos, tr), rows(sin, tr)],
                     [out_rows(n, w, BF16, tr), out_rows(n, w, BF16, tr)], grid=(n // tr,), name=name)


def rope_pair_headsum(d1, d2, cos, sin, *, tr=512, name):
    h, n, w = d1.shape
    tr = _pick(n, tr, 8)

    def fn(a, b, c, s):
        a, b = jnp.sum(a.astype(F32), axis=0), jnp.sum(b.astype(F32), axis=0)
        return a * c + b * s, b * c - a * s

    lead = lambda arr: (arr, (h, tr, w), lambda i: (0, i, 0))
    return blockwise(fn, [lead(d1), lead(d2), rows(cos, tr), rows(sin, tr)],
                     [out_rows(n, w, BF16, tr), out_rows(n, w, BF16, tr)], grid=(n // tr,), name=name)


def gate_fwd(proj, y_a, y_b, *, d, ga_col, gb_col, tr=256, name):
    n = proj.shape[0]
    tr = _pick(n, tr, 8)

    def fn(ga, gb, ya, yb):
        return (jax.nn.sigmoid(ga) * ya + jax.nn.sigmoid(gb) * yb,)

    return blockwise(fn, [rows(proj, tr, d, ga_col), rows(proj, tr, d, gb_col), rows(y_a, tr), rows(y_b, tr)],
                     [out_rows(n, d, BF16, tr)], grid=(n // tr,), name=name)[0]


def gate_bwd(proj, y_a, y_b, dmerged, *, d, ga_col, gb_col, tr=256, name):
    n = proj.shape[0]
    tr = _pick(n, tr, 8)

    def fn(ga, gb, ya, yb, dm):
        sa, sb = jax.nn.sigmoid(ga), jax.nn.sigmoid(gb)
        return dm * sa, dm * sb, dm * ya * (sa * (1.0 - sa)), dm * yb * (sb * (1.0 - sb))

    return blockwise(fn, [rows(proj, tr, d, ga_col), rows(proj, tr, d, gb_col), rows(y_a, tr), rows(y_b, tr),
                          rows(dmerged, tr)],
                     [out_rows(n, d, BF16, tr)] * 4, grid=(n // tr,), name=name)


def loss_head(x, g, target, *, tr=256, name):
    n, d = x.shape
    tr = _pick(n, tr, 8)

    def fn(xv, gv, tv):
        r = _rstd(xv)
        xhat = xv * r
        diff = xhat * gv - tv
        loss = 0.5 * jnp.sum(jnp.sum(diff * diff, axis=-1, keepdims=True) / d, axis=0, keepdims=True)
        dy = diff / d
        dxhat = dy * gv
        dx = r * (dxhat - xhat * jnp.mean(dxhat * xhat, axis=-1, keepdims=True))
        return dx, _colsum(dy * xhat), jnp.broadcast_to(loss, (8, LANES))

    return blockwise(fn, [rows(x, tr), whole(g), rows(target, tr)],
                     [out_rows(n, d, F32, tr), out_sum((1, d)), out_sum((8, LANES))],
                     grid=(n // tr,), name=name, sums=(1, 2))


def adamw(w, m, v, g_slots, *, tr=256, name):
    n, c = w.shape
    tr = _pick(n, tr, 8)
    c1 = 1.0 / (1.0 - ADAM_B1 ** ADAM_STEP)
    c2 = 1.0 / (1.0 - ADAM_B2 ** ADAM_STEP)

    def fn(wv, mv, vv, gs):
        g = gs[0].astype(F32)
        for dev in range(1, N_DEV):
            g = g + gs[dev].astype(F32)
        m_new = ADAM_B1 * mv + (1.0 - ADAM_B1) * g
        v_new = ADAM_B2 * vv + (1.0 - ADAM_B2) * (g * g)
        delta = -ADAM_LR * ((m_new * c1) / (jnp.sqrt(v_new * c2) + ADAM_EPS) + ADAM_WD * wv)
        return g, delta, m_new, v_new

    slots = (g_slots, (N_DEV, tr, c), lambda i: (0, i, 0))
    return blockwise(fn, [rows(w, tr), rows(m, tr), rows(v, tr), slots],
                     [out_rows(n, c, F32, tr)] * 4, grid=(n // tr,), name=name)


def mla_fwd(q, k, kv, *, v_col0, bq=512, bk=1024, name):
    h, s, dq = q.shape
    bq, bk = _pick(s, bq), _pick(s, bk)
    nk = s // bk
    scale = QK_DIM ** -0.5

    def body(q_ref, k_ref, v_ref, o_ref, lse_ref, m_s, l_s, acc):
        j = pl.program_id(2)

        @pl.when(j == 0)
        def _():
            m_s[...] = jnp.full_like(m_s, NEG)
            l_s[...] = jnp.zeros_like(l_s)
            acc[...] = jnp.zeros_like(acc)

        sc = lax.dot_general(q_ref[...], k_ref[...], NT, preferred_element_type=F32) * scale
        m_prev = m_s[...]
        m_new = jnp.maximum(m_prev, jnp.max(sc, axis=-1, keepdims=True))
        alpha = jnp.exp(m_prev - m_new)
        p = jnp.exp(sc - m_new)
        l_s[...] = alpha * l_s[...] + jnp.sum(p, axis=-1, keepdims=True)
        acc[...] = alpha * acc[...] + lax.dot_general(p.astype(BF16), v_ref[...], NN, preferred_element_type=F32)
        m_s[...] = m_new

        @pl.when(j == nk - 1)
        def _():
            o_ref[...] = acc[...] / l_s[...]
            lse_ref[...] = m_s[...] + jnp.log(l_s[...])

    return pl.pallas_call(
        body, name=name, grid=(h, s // bq, nk),
        in_specs=[pl.BlockSpec((None, bq, dq), lambda hh, i, j: (hh, i, 0)),
                  pl.BlockSpec((None, bk, dq), lambda hh, i, j: (hh, j, 0)),
                  pl.BlockSpec((bk, V_HEAD), lambda hh, i, j: (j, v_col0 + hh))],
        out_specs=[pl.BlockSpec((bq, V_HEAD), lambda hh, i, j: (i, hh)),
                   pl.BlockSpec((None, bq, 1), lambda hh, i, j: (hh, i, 0))],
        out_shape=[jax.ShapeDtypeStruct((s, h * V_HEAD), F32), jax.ShapeDtypeStruct((h, s, 1), F32)],
        scratch_shapes=[pltpu.VMEM((bq, 1), F32), pltpu.VMEM((bq, 1), F32), pltpu.VMEM((bq, V_HEAD), F32)],
        compiler_params=_params(("parallel", "parallel", "arbitrary")),
    )(q, k, kv)


def mla_bwd_dq(q, k, kv, o, do, lse, *, v_col0, bq=512, bk=1024, name):
    h, s, dq = q.shape
    bq, bk = _pick(s, bq), _pick(s, bk)
    nk = s // bk
    scale = QK_DIM ** -0.5

    def body(q_ref, k_ref, v_ref, o_ref, do_ref, lse_ref, dq_ref, delta_ref, acc):
        j = pl.program_id(2)

        @pl.when(j == 0)
        def _():
            acc[...] = jnp.zeros_like(acc)
            delta_ref[...] = jnp.sum(do_ref[...].astype(F32) * o_ref[...], axis=-1, keepdims=True)

        sc = lax.dot_general(q_ref[...], k_ref[...], NT, preferred_element_type=F32) * scale
        p = jnp.exp(sc - lse_ref[...])
        dp = lax.dot_general(do_ref[...], v_ref[...], NT, preferred_element_type=F32)
        ds = (p * (dp - delta_ref[...])) * scale
        acc[...] += lax.dot_general(ds.astype(BF16), k_ref[...], NN, preferred_element_type=F32)

        @pl.when(j == nk - 1)
        def _():
            dq_ref[...] = acc[...]

    return pl.pallas_call(
        body, name=name, grid=(h, s // bq, nk),
        in_specs=[pl.BlockSpec((None, bq, dq), lambda hh, i, j: (hh, i, 0)),
                  pl.BlockSpec((None, bk, dq), lambda hh, i, j: (hh, j, 0)),
                  pl.BlockSpec((bk, V_HEAD), lambda hh, i, j: (j, v_col0 + hh)),
                  pl.BlockSpec((bq, V_HEAD), lambda hh, i, j: (i, hh)),
                  pl.BlockSpec((bq, V_HEAD), lambda hh, i, j: (i, hh)),
                  pl.BlockSpec((None, bq, 1), lambda hh, i, j: (hh, i, 0))],
        out_specs=[pl.BlockSpec((None, bq, dq), lambda hh, i, j: (hh, i, 0)),
                   pl.BlockSpec((None, bq, 1), lambda hh, i, j: (hh, i, 0))],
        out_shape=[jax.ShapeDtypeStruct((h, s, dq), F32), jax.ShapeDtypeStruct((h, s, 1), F32)],
        scratch_shapes=[pltpu.VMEM((bq, dq), F32)],
        compiler_params=_params(("parallel", "parallel", "arbitrary")),
    )(q, k, kv, o, do, lse)


def mla_bwd_dkv(q, k, kv, do, lse, delta, *, v_col0, bq=512, bk=1024, name):
    h, s, dq = q.shape
    bq, bk = _pick(s, bq), _pick(s, bk)
    nq = s // bq
    scale = QK_DIM ** -0.5

    def body(q_ref, k_ref, v_ref, do_ref, lse_ref, delta_ref, dk_ref, dv_ref, dk_acc, dv_acc):
        i = pl.program_id(2)

        @pl.when(i == 0)
        def _():
            dk_acc[...] = jnp.zeros_like(dk_acc)
            dv_acc[...] = jnp.zeros_like(dv_acc)

        sc = lax.dot_general(q_ref[...], k_ref[...], NT, preferred_element_type=F32) * scale
        p = jnp.exp(sc - lse_ref[...])
        dv_acc[...] += lax.dot_general(p.astype(BF16), do_ref[...], TN, preferred_element_type=F32)
        dp = lax.dot_general(do_ref[...], v_ref[...], NT, preferred_element_type=F32)
        ds = (p * (dp - delta_ref[...])) * scale
        dk_acc[...] += lax.dot_general(ds.astype(BF16), q_ref[...], TN, preferred_element_type=F32)

        @pl.when(i == nq - 1)
        def _():
            dk_ref[...] = dk_acc[...]
            dv_ref[...] = dv_acc[...].astype(dv_ref.dtype)

    return pl.pallas_call(
        body, name=name, grid=(h, s // bk, nq),
        in_specs=[pl.BlockSpec((None, bq, dq), lambda hh, j, i: (hh, i, 0)),
                  pl.BlockSpec((None, bk, dq), lambda hh, j, i: (hh, j, 0)),
                  pl.BlockSpec((bk, V_HEAD), lambda hh, j, i: (j, v_col0 + hh)),
                  pl.BlockSpec((bq, V_HEAD), lambda hh, j, i: (i, hh)),
                  pl.BlockSpec((None, bq, 1), lambda hh, j, i: (hh, i, 0)),
                  pl.BlockSpec((None, bq, 1), lambda hh, j, i: (hh, i, 0))],
        out_specs=[pl.BlockSpec((None, bk, dq), lambda hh, j, i: (hh, j, 0)),
                   pl.BlockSpec((bk, V_HEAD), lambda hh, j, i: (j, hh))],
        out_shape=[jax.ShapeDtypeStruct((h, s, dq), F32), jax.ShapeDtypeStruct((s, h * V_HEAD), BF16)],
        scratch_shapes=[pltpu.VMEM((bk, dq), F32), pltpu.VMEM((bk, V_HEAD), F32)],
        compiler_params=_params(("parallel", "parallel", "arbitrary")),
    )(q, k, kv, do, lse, delta)


def _na_probs(q_row, k_span, bias, r, n_rows):
    scale = NA_HEAD_DIM ** -0.5
    sc = lax.dot_general(q_row, k_span, NT, preferred_element_type=F32) * scale + bias
    lo = jnp.clip(r - NA_KH // 2, 0, n_rows - NA_KH) - r + (NA_KH - 1)
    dy = lax.broadcasted_iota(jnp.int32, (1, NA_SPAN), 1) // GRID_W
    sc = jnp.where((dy >= lo) & (dy < lo + NA_KH), sc, NEG)
    p = jnp.exp(sc - jnp.max(sc, axis=-1, keepdims=True))
    return p / jnp.sum(p, axis=-1, keepdims=True)


def na_fwd(proj, kp, vp, biasm, *, q_col0, rows_per_step=8, name):
    s = proj.shape[0]
    sp, hw = kp.shape
    h = hw // NA_HEAD_DIM
    n_rows = s // GRID_W
    rb = min(rows_per_step, n_rows)
    tq = rb * GRID_W

    def body(q_ref, k_ref, v_ref, b_ref, o_ref):
        i = pl.program_id(1)
        bias = b_ref[...]
        for rl in range(rb):
            r = i * rb + rl
            start = pl.multiple_of(r * GRID_W, GRID_W)
            q_row = q_ref[pl.ds(rl * GRID_W, GRID_W), :].astype(BF16)
            p = _na_probs(q_row, k_ref[pl.ds(start, NA_SPAN), :], bias, r, n_rows)
            o_ref[pl.ds(rl * GRID_W, GRID_W), :] = lax.dot_general(
                p.astype(BF16), v_ref[pl.ds(start, NA_SPAN), :], NN, preferred_element_type=F32)

    return pl.pallas_call(
        body, name=name, grid=(h, n_rows // rb),
        in_specs=[pl.BlockSpec((tq, NA_HEAD_DIM), lambda hh, i: (i, q_col0 + hh)),
                  pl.BlockSpec((sp, NA_HEAD_DIM), lambda hh, i: (0, hh)),
                  pl.BlockSpec((sp, NA_HEAD_DIM), lambda hh, i: (0, hh)),
                  pl.BlockSpec((None, GRID_W, NA_SPAN), lambda hh, i: (hh, 0, 0))],
        out_specs=pl.BlockSpec((tq, NA_HEAD_DIM), lambda hh, i: (i, hh)),
        out_shape=jax.ShapeDtypeStruct((s, hw), F32),
        compiler_params=_params(("parallel", "arbitrary")),
    )(proj, kp, vp, biasm)


def na_bwd(proj, kp, vp, biasm, do, *, q_col0, rows_per_step=8, name):
    s = proj.shape[0]
    sp, hw = kp.shape
    h = hw // NA_HEAD_DIM
    n_rows = s // GRID_W
    rb = min(rows_per_step, n_rows)
    tq = rb * GRID_W
    scale = NA_HEAD_DIM ** -0.5

    def body(q_ref, k_ref, v_ref, b_ref, do_ref, dq_ref, dk_ref, dv_ref, db_ref):
        i = pl.program_id(1)

        @pl.when(i == 0)
        def _():
            dk_ref[...] = jnp.zeros_like(dk_ref)
            dv_ref[...] = jnp.zeros_like(dv_ref)
            db_ref[...] = jnp.zeros_like(db_ref)

        bias = b_ref[...]
        for rl in range(rb):
            r = i * rb + rl
            start = pl.multiple_of(r * GRID_W, GRID_W)
            q_row = q_ref[pl.ds(rl * GRID_W, GRID_W), :].astype(BF16)
            do_row = do_ref[pl.ds(rl * GRID_W, GRID_W), :]
            k_span = k_ref[pl.ds(start, NA_SPAN), :]
            p = _na_probs(q_row, k_span, bias, r, n_rows)
            dp = lax.dot_general(do_row, v_ref[pl.ds(start, NA_SPAN), :], NT, preferred_element_type=F32)
            ds = p * (dp - jnp.sum(dp * p, axis=-1, keepdims=True))
            db_ref[...] += ds
            ds_b = (ds * scale).astype(BF16)
            dq_ref[pl.ds(rl * GRID_W, GRID_W), :] = lax.dot_general(
                ds_b, k_span, NN, preferred_element_type=F32).astype(dq_ref.dtype)
            dk_ref[pl.ds(start, NA_SPAN), :] += lax.dot_general(ds_b, q_row, TN, preferred_element_type=F32)
            dv_ref[pl.ds(start, NA_SPAN), :] += lax.dot_general(p.astype(BF16), do_row, TN,
                                                                 preferred_element_type=F32)

    return pl.pallas_call(
        body, name=name, grid=(h, n_rows // rb),
        in_specs=[pl.BlockSpec((tq, NA_HEAD_DIM), lambda hh, i: (i, q_col0 + hh)),
                  pl.BlockSpec((sp, NA_HEAD_DIM), lambda hh, i: (0, hh)),
                  pl.BlockSpec((sp, NA_HEAD_DIM), lambda hh, i: (0, hh)),
                  pl.BlockSpec((None, GRID_W, NA_SPAN), lambda hh, i: (hh, 0, 0)),
                  pl.BlockSpec((tq, NA_HEAD_DIM), lambda hh, i: (i, hh))],
        out_specs=[pl.BlockSpec((tq, NA_HEAD_DIM), lambda hh, i: (i, hh)),
                   pl.BlockSpec((sp, NA_HEAD_DIM), lambda hh, i: (0, hh)),
                   pl.BlockSpec((sp, NA_HEAD_DIM), lambda hh, i: (0, hh)),
                   pl.BlockSpec((None, GRID_W, NA_SPAN), lambda hh, i: (hh, 0, 0))],
        out_shape=[jax.ShapeDtypeStruct((s, hw), BF16), jax.ShapeDtypeStruct((sp, hw), F32),
                   jax.ShapeDtypeStruct((sp, hw), F32), jax.ShapeDtypeStruct((h, GRID_W, NA_SPAN), F32)],
        compiler_params=_params(("parallel", "arbitrary")),
    )(proj, kp, vp, biasm, do)


def _na_tables():
    qc = np.arange(GRID_W)[:, None]
    kc = np.arange(GRID_W)[None, :]
    col_start = np.clip(qc - NA_KW // 2, 0, GRID_W - NA_KW)
    col_ok = (kc >= col_start) & (kc < col_start + NA_KW)
    dx = np.clip(kc - qc, -(NA_KW - 1), NA_KW - 1) + (NA_KW - 1)
    return col_ok, dx


def na_bias_table(rpb_l):
    col_ok, dx = _na_tables()
    t = rpb_l[:, :, dx]
    t = jnp.where(col_ok[None, None], t, NEG)
    t = jnp.pad(t, ((0, 0), (0, 1), (0, 0), (0, 0)), constant_values=NEG)
    return t.transpose(0, 2, 1, 3).reshape(rpb_l.shape[0], GRID_W, NA_SPAN)


def na_bias_grad(dbias, *, name):
    h = dbias.shape[0]
    _, dx = _na_tables()
    n_dx = 2 * NA_KW - 1
    onehot = np.zeros((GRID_W * GRID_W, LANES), np.float32)
    onehot[np.arange(GRID_W * GRID_W), dx.reshape(-1)] = 1.0
    t = dbias.reshape(h, GRID_W, NA_SPAN_ROWS, GRID_W)[:, :, :2 * NA_KH - 1]
    t = t.transpose(0, 2, 1, 3).reshape(h * (2 * NA_KH - 1), GRID_W * GRID_W)
    t = jnp.pad(t, ((0, (-t.shape[0]) % 8), (0, 0)))
    out = mm(t, jnp.asarray(onehot), mode="nn", name=name, exact=True, tk=1024)
    return out[:h * (2 * NA_KH - 1), :n_dx].reshape(h, 2 * NA_KH - 1, n_dx)


def _flip(v, bit):
    return 1 - v if bit else v


def all_gather(block, *, name):
    def body(x_ref, out_ref, send_sems, recv_sems, local_sem):
        x, y, c = lax.axis_index("x"), lax.axis_index("y"), lax.axis_index("c")
        me, sibling = (x, y, c), (x, y, 1 - c)
        chips = [(1 - x, y), (x, 1 - y), (1 - x, 1 - y)]

        def slot(px, py, pc):
            return out_ref.at[4 * px + 2 * py + pc]

        def copy(k, blk, to, src=None):
            return pltpu.make_async_remote_copy(
                src_ref=slot(*blk) if src is None else src, dst_ref=slot(*blk),
                send_sem=send_sems.at[k], recv_sem=recv_sems.at[k],
                device_id=to, device_id_type=pl.DeviceIdType.MESH)

        mine = pltpu.make_async_copy(x_ref, slot(*me), local_sem)
        mine.start()
        first = [copy(0, me, sibling, src=x_ref)]
        first += [copy(1 + j, me, (*chip, c), src=x_ref) for j, chip in enumerate(chips)]
        for cp in first:
            cp.start()
        passed = [copy(4 + j, (*chip, c), sibling) for j, chip in enumerate(chips)]
        for j, chip in enumerate(chips):
            copy(1 + j, (*chip, c), me).wait_recv()
            passed[j].start()
        copy(0, sibling, me).wait_recv()
        for j, chip in enumerate(chips):
            copy(4 + j, (*chip, 1 - c), me).wait_recv()
        for cp in first + passed:
            cp.wait_send()
        mine.wait()

    return pl.pallas_call(
        body, name=name,
        out_shape=jax.ShapeDtypeStruct((N_DEV,) + block.shape, block.dtype),
        in_specs=[pl.BlockSpec(memory_space=pl.ANY)],
        out_specs=pl.BlockSpec(memory_space=pl.ANY),
        scratch_shapes=[pltpu.SemaphoreType.DMA((N_DEV - 1,)), pltpu.SemaphoreType.DMA((N_DEV - 1,)),
                        pltpu.SemaphoreType.DMA],
    )(block)


def all_to_all(slots, *, name):
    def body(x_ref, out_ref, send_sems, recv_sems, local_sem):
        x, y, c = lax.axis_index("x"), lax.axis_index("y"), lax.axis_index("c")
        me = 4 * x + 2 * y + c
        mine = pltpu.make_async_copy(x_ref.at[me], out_ref.at[me], local_sem)
        mine.start()
        sends, recvs = [], []
        for k in range(1, N_DEV):
            peer = (_flip(x, k & 4), _flip(y, k & 2), _flip(c, k & 1))
            theirs = 4 * peer[0] + 2 * peer[1] + peer[2]
            sends.append(pltpu.make_async_remote_copy(
                src_ref=x_ref.at[theirs], dst_ref=out_ref.at[me],
                send_sem=send_sems.at[k - 1], recv_sem=recv_sems.at[k - 1],
                device_id=peer, device_id_type=pl.DeviceIdType.MESH))
            recvs.append(pltpu.make_async_remote_copy(
                src_ref=x_ref.at[me], dst_ref=out_ref.at[theirs],
                send_sem=send_sems.at[k - 1], recv_sem=recv_sems.at[k - 1],
                device_id=peer, device_id_type=pl.DeviceIdType.MESH))
        for cp in sends:
            cp.start()
        for cp in recvs:
            cp.wait_recv()
        for cp in sends:
            cp.wait_send()
        mine.wait()

    return pl.pallas_call(
        body, name=name,
        out_shape=jax.ShapeDtypeStruct(slots.shape, slots.dtype),
        in_specs=[pl.BlockSpec(memory_space=pl.ANY)],
        out_specs=pl.BlockSpec(memory_space=pl.ANY),
        scratch_shapes=[pltpu.SemaphoreType.DMA((N_DEV - 1,)), pltpu.SemaphoreType.DMA((N_DEV - 1,)),
                        pltpu.SemaphoreType.DMA],
    )(slots)


SHARDED = ("w_in", "w_uq", "w_ukv", "w_o_mla", "w_o_na", "w_out", "w_ff1", "w_ff2")
ROW_SHARDED = ("w_out", "w_ff2")
REPLICATED = ("norm_mix", "norm_qa", "norm_kva", "rpb", "norm_mlp", "norm_final")
WEIGHTS = ("norm_mix", "w_in", "norm_qa", "w_uq", "norm_kva", "w_ukv", "rpb", "w_o_mla", "w_o_na", "w_out",
           "norm_mlp", "w_ff1", "w_ff2", "norm_final")


def _to_flat(shards):
    return jnp.concatenate([shards[n].reshape(-1, LANES) for n in SHARDED], axis=0)


def _from_flat(flat, shapes):
    out, r0 = {}, 0
    for n in SHARDED:
        r, c = shapes[n]
        nr = r * c // LANES
        out[n] = flat[:, r0:r0 + nr].reshape(N_DEV, r, c)
        r0 += nr
    return out


def _full(name, gathered):
    if name in ROW_SHARDED:
        return gathered.reshape(-1, gathered.shape[2])
    return gathered.transpose(1, 0, 2).reshape(gathered.shape[1], -1)


def _slots(name, full):
    if name in ROW_SHARDED:
        return full.reshape(N_DEV, -1, full.shape[1])
    return full.reshape(full.shape[0], N_DEV, -1).transpose(1, 0, 2)


class Dims:
    def __init__(self, x, w_in, norm_qa, norm_kva, rpb, w_o_mla, w_o_na, w_ff1):
        self.s, self.d = x.shape[1], x.shape[2]
        self.depth = w_in.shape[0]
        self.q_lora, self.kv_lora = norm_qa.shape[1], norm_kva.shape[1]
        self.mla_w, self.na_w = w_o_mla.shape[1], w_o_na.shape[1]
        self.mla_h, self.na_h = self.mla_w // V_HEAD, self.na_w // NA_HEAD_DIM
        self.d_ff = w_ff1.shape[2] * N_DEV
        assert rpb.shape[1] == self.na_h and self.s % GRID_W == 0 and self.s // GRID_W >= NA_SPAN_ROWS
        self.in_lo = self.q_lora + self.kv_lora
        self.main_w = self.in_lo + 3 * self.na_w + 2 * self.d
        assert self.q_lora == self.kv_lora and self.in_lo % self.na_w == 0 and self.in_lo % LANES == 0
        assert (self.in_lo + 3 * self.na_w) % self.d == 0
        self.q_col0 = self.in_lo // NA_HEAD_DIM
        self.k_off = self.in_lo + self.na_w
        self.v_off = self.in_lo + 2 * self.na_w
        self.ga_col = (self.in_lo + 3 * self.na_w) // self.d
        self.gb_col = self.ga_col + 1


def _split_w_in(dm, w):
    lo = dm.in_lo
    main = jnp.concatenate([w[:, :lo], w[:, lo + QK_ROPE:]], axis=1)
    kpe = jnp.pad(w[:, lo:lo + QK_ROPE], ((0, 0), (0, LANES - QK_ROPE)))
    return main, kpe


def _join_w_in(dm, main, kpe):
    lo = dm.in_lo
    return jnp.concatenate([main[:, :lo], kpe[:, :QK_ROPE], main[:, lo:]], axis=1)


def _split_heads(w, h, widths):
    r = w.shape[0]
    w3 = w.reshape(r, h, sum(widths))
    out, o = [], 0
    for wd in widths:
        out.append(w3[:, :, o:o + wd].reshape(r, h * wd))
        o += wd
    return out


def _join_heads(parts, h):
    r = parts[0].shape[0]
    return jnp.concatenate([p.reshape(r, h, -1) for p in parts], axis=2).reshape(r, -1)


UQ_WIDTHS = (QK_NOPE, HALF_ROPE, HALF_ROPE)
UKV_WIDTHS = (QK_NOPE, V_HEAD)


def _by_head(parts, h):
    s = parts[0].shape[0]
    return jnp.concatenate([p.reshape(s, h, -1) for p in parts], axis=2).transpose(1, 0, 2)


def _from_head(t, widths):
    h, s, _ = t.shape
    t = t.transpose(1, 0, 2)
    out, o = [], 0
    for wd in widths:
        out.append(t[:, :, o:o + wd].reshape(s, h * wd))
        o += wd
    return out


def layer_fwd(dm, lname, x, w, g, cos_q, sin_q, cos_k, sin_k):
    s, h = dm.s, dm.mla_h
    u = rmsnorm_fwd(x, g["norm_mix"], name=f"{lname}_norm_mix")
    proj = mm(u, w["in_main"], mode="nn", name=f"{lname}_proj")
    kpe = mm(u, w["in_kpe"], mode="nn", name=f"{lname}_proj_kpe")
    qn = rmsnorm_fwd(proj, g["norm_qa"], width=dm.q_lora, col=0, name=f"{lname}_norm_qa")
    kvn = rmsnorm_fwd(proj, g["norm_kva"], width=dm.kv_lora, col=1, name=f"{lname}_norm_kva")
    q = mm(qn, w["uq"], mode="nn", name=f"{lname}_uq")
    kv = mm(kvn, w["ukv"], mode="nn", out_dtypes=(BF16,), name=f"{lname}_ukv")
    nope_w, half_w = h * QK_NOPE, h * HALF_ROPE
    q1, q2 = rope_pair(q[:, nope_w:nope_w + half_w], q[:, nope_w + half_w:], cos_q, sin_q, name=f"{lname}_rope_q")
    k1, k2 = rope_pair(kpe[:, :HALF_ROPE], kpe[:, HALF_ROPE:QK_ROPE], cos_k, sin_k, name=f"{lname}_rope_k")
    qh = _by_head([q[:, :nope_w].astype(BF16), q1, q2], h)
    kh = _by_head([kv[:, :nope_w], jnp.tile(k1, (1, h)), jnp.tile(k2, (1, h))], h)
    o_a, lse = mla_fwd(qh, kh, kv, v_col0=h, name=f"{lname}_mla")
    y_a = mm(o_a, w["o_mla"], mode="nn", name=f"{lname}_o_mla")

    pad = ((NA_PAD_LO, NA_PAD_HI), (0, 0))
    kp = jnp.pad(proj[:, dm.k_off:dm.k_off + dm.na_w].astype(BF16), pad)
    vp = jnp.pad(proj[:, dm.v_off:dm.v_off + dm.na_w].astype(BF16), pad)
    biasm = na_bias_table(g["rpb"])
    o_b = na_fwd(proj, kp, vp, biasm, q_col0=dm.q_col0, name=f"{lname}_na")
    y_b = mm(o_b, w["o_na"], mode="nn", name=f"{lname}_o_na")

    merged = gate_fwd(proj, y_a, y_b, d=dm.d, ga_col=dm.ga_col, gb_col=dm.gb_col, name=f"{lname}_gate")
    x1 = mm(merged, w["out"], mode="nn", epi=lambda r, res: (r + res,), extras=(x,), name=f"{lname}_out")
    u2 = rmsnorm_fwd(x1, g["norm_mlp"], name=f"{lname}_norm_mlp")
    hid, act = mm(u2, w["ff1"], mode="nn", out_dtypes=(F32, BF16),
                  epi=lambda r: (r, jnp.square(jnp.maximum(r, 0.0))), name=f"{lname}_ff1")
    x2 = mm(act, w["ff2"], mode="nn", epi=lambda r, res: (r + res,), extras=(x1,), name=f"{lname}_ff2")
    saved = dict(x=x, u=u, proj=proj, qn=qn, kvn=kvn, kv=kv, qh=qh, kh=kh, o_a=o_a, lse=lse, y_a=y_a, kp=kp, vp=vp,
                 biasm=biasm, o_b=o_b, y_b=y_b, merged=merged, x1=x1, u2=u2, hid=hid, act=act)
    return x2, saved


def layer_bwd(dm, lname, dx2, w, g, sv, cos_q, sin_q, cos_k, sin_k):
    h = dm.mla_h
    gw, gr = {}, {}
    gw["ff2"] = mm(sv["act"], dx2, mode="tn", out_dtypes=(BF16,), name=f"{lname}_d_ff2")
    dh = mm(dx2, w["ff2"], mode="nt", out_dtypes=(BF16,), extras=(sv["hid"],),
            epi=lambda r, hv: (r * (2.0 * jnp.maximum(hv, 0.0)),), name=f"{lname}_d_act")
    gw["ff1"] = mm(sv["u2"], dh, mode="tn", out_dtypes=(BF16,), name=f"{lname}_d_ff1")
    du2 = mm(dh, w["ff1"], mode="nt", name=f"{lname}_d_u2")
    dx1, gr["norm_mlp"] = rmsnorm_bwd(sv["x1"], g["norm_mlp"], du2, dx2, name=f"{lname}_d_norm_mlp")
    gw["out"] = mm(sv["merged"], dx1, mode="tn", out_dtypes=(BF16,), name=f"{lname}_d_out")
    dmerged = mm(dx1, w["out"], mode="nt", name=f"{lname}_d_merged")
    dy_a, dy_b, dga, dgb = gate_bwd(sv["proj"], sv["y_a"], sv["y_b"], dmerged, d=dm.d, ga_col=dm.ga_col,
                                    gb_col=dm.gb_col, name=f"{lname}_d_gate")
    gw["o_na"] = mm(sv["o_b"], dy_b, mode="tn", out_dtypes=(BF16,), name=f"{lname}_d_o_na")
    do_b = mm(dy_b, w["o_na"], mode="nt", out_dtypes=(BF16,), name=f"{lname}_d_ob")
    dq_na, dkp, dvp, dbias = na_bwd(sv["proj"], sv["kp"], sv["vp"], sv["biasm"], do_b, q_col0=dm.q_col0,
                                    name=f"{lname}_d_na")
    gr["rpb"] = na_bias_grad(dbias, name=f"{lname}_d_rpb")
    dk_na = dkp[NA_PAD_LO:NA_PAD_LO + dm.s].astype(BF16)
    dv_na = dvp[NA_PAD_LO:NA_PAD_LO + dm.s].astype(BF16)
    gw["o_mla"] = mm(sv["o_a"], dy_a, mode="tn", out_dtypes=(BF16,), name=f"{lname}_d_o_mla")
    do_a = mm(dy_a, w["o_mla"], mode="nt", out_dtypes=(BF16,), name=f"{lname}_d_oa")
    dqh, delta = mla_bwd_dq(sv["qh"], sv["kh"], sv["kv"], sv["o_a"], do_a, sv["lse"], v_col0=h,
                            name=f"{lname}_d_mla_q")
    dkh, dv = mla_bwd_dkv(sv["qh"], sv["kh"], sv["kv"], do_a, sv["lse"], delta, v_col0=h, name=f"{lname}_d_mla_kv")
    dq_nope, dq1, dq2 = _from_head(dqh, UQ_WIDTHS)
    dq1, dq2 = rope_pair(dq1, dq2, cos_q, -sin_q, name=f"{lname}_d_rope_q")
    dq = jnp.concatenate([dq_nope.astype(BF16), dq1, dq2], axis=1)
    gw["uq"] = mm(sv["qn"], dq, mode="tn", out_dtypes=(BF16,), name=f"{lname}_d_uq")
    dqn = mm(dq, w["uq"], mode="nt", name=f"{lname}_d_qn")
    dk_nope = _from_head(dkh[:, :, :QK_NOPE], (QK_NOPE,))[0]
    dk1, dk2 = rope_pair_headsum(dkh[:, :, QK_NOPE:QK_NOPE + HALF_ROPE], dkh[:, :, QK_NOPE + HALF_ROPE:],
                                 cos_k, sin_k, name=f"{lname}_d_rope_k")
    dkv = jnp.concatenate([dk_nope.astype(BF16), dv], axis=1)
    gw["ukv"] = mm(sv["kvn"], dkv, mode="tn", out_dtypes=(BF16,), name=f"{lname}_d_ukv")
    dkvn = mm(dkv, w["ukv"], mode="nt", name=f"{lname}_d_kvn")
    dc_q, gr["norm_qa"] = rmsnorm_bwd(sv["proj"], g["norm_qa"], dqn, width=dm.q_lora, col=0, out_dtype=BF16,
                                      name=f"{lname}_d_norm_qa")
    dc_kv, gr["norm_kva"] = rmsnorm_bwd(sv["proj"], g["norm_kva"], dkvn, width=dm.kv_lora, col=1, out_dtype=BF16,
                                        name=f"{lname}_d_norm_kva")
    dproj = jnp.concatenate([dc_q, dc_kv, dq_na, dk_na, dv_na, dga, dgb], axis=1)
    dkpe = jnp.concatenate([dk1, dk2, jnp.zeros((dm.s, LANES - QK_ROPE), BF16)], axis=1)
    gw["in_main"] = mm(sv["u"], dproj, mode="tn", out_dtypes=(BF16,), name=f"{lname}_d_in")
    gw["in_kpe"] = mm(sv["u"], dkpe, mode="tn", out_dtypes=(BF16,), name=f"{lname}_d_in_kpe")
    du_k = mm(dkpe, w["in_kpe"], mode="nt", name=f"{lname}_d_u_kpe")
    du = mm(dproj, w["in_main"], mode="nt", epi=lambda r, res: (r + res,), extras=(du_k,), name=f"{lname}_d_u")
    dx, gr["norm_mix"] = rmsnorm_bwd(sv["x"], g["norm_mix"], du, dx1, name=f"{lname}_d_norm_mix")
    return dx, gw, gr


def _kernel_weights(dm, full):
    main, kpe = _split_w_in(dm, full["w_in"])
    return dict(in_main=main, in_kpe=kpe,
                uq=jnp.concatenate(_split_heads(full["w_uq"], dm.mla_h, UQ_WIDTHS), axis=1),
                ukv=jnp.concatenate(_split_heads(full["w_ukv"], dm.mla_h, UKV_WIDTHS), axis=1),
                o_mla=full["w_o_mla"], o_na=full["w_o_na"], out=full["w_out"], ff1=full["w_ff1"], ff2=full["w_ff2"])


def _reference_order(dm, gw):
    h = dm.mla_h
    nope_w, half_w = h * QK_NOPE, h * HALF_ROPE
    uq = gw["uq"]
    ukv = gw["ukv"]
    return {"w_in": _join_w_in(dm, gw["in_main"], gw["in_kpe"]),
            "w_uq": _join_heads([uq[:, :nope_w], uq[:, nope_w:nope_w + half_w], uq[:, nope_w + half_w:]], h),
            "w_ukv": _join_heads([ukv[:, :nope_w], ukv[:, nope_w:]], h),
            "w_o_mla": gw["o_mla"], "w_o_na": gw["o_na"], "w_out": gw["out"], "w_ff1": gw["ff1"], "w_ff2": gw["ff2"]}


def _pack_replicated(dm, parts):
    flat = jnp.concatenate([parts[n].reshape(-1) for n in REPLICATED])
    n = flat.shape[0]
    rows_ = -(-n // (8 * LANES)) * 8
    return jnp.pad(flat, (0, rows_ * LANES - n)).reshape(rows_, LANES)


def kernel(x, norm_mix, w_in, norm_qa, w_uq, norm_kva, w_ukv, rpb, w_o_mla, w_o_na, w_out, norm_mlp, w_ff1, w_ff2, norm_final, loss_target, m_norm_mix, m_w_in, m_norm_qa, m_w_uq, m_norm_kva, m_w_ukv, m_rpb, m_w_o_mla, m_w_o_na, m_w_out, m_norm_mlp, m_w_ff1, m_w_ff2, m_norm_final, v_norm_mix, v_w_in, v_norm_qa, v_w_uq, v_norm_kva, v_w_ukv, v_rpb, v_w_o_mla, v_w_o_na, v_w_out, v_norm_mlp, v_w_ff1, v_w_ff2, v_norm_final):
    dm = Dims(x, w_in, norm_qa, norm_kva, rpb, w_o_mla, w_o_na, w_ff1)
    params = dict(norm_mix=norm_mix, w_in=w_in, norm_qa=norm_qa, w_uq=w_uq, norm_kva=norm_kva, w_ukv=w_ukv, rpb=rpb,
                  w_o_mla=w_o_mla, w_o_na=w_o_na, w_out=w_out, norm_mlp=norm_mlp, w_ff1=w_ff1, w_ff2=w_ff2,
                  norm_final=norm_final)
    mom_m = dict(norm_mix=m_norm_mix, w_in=m_w_in, norm_qa=m_norm_qa, w_uq=m_w_uq, norm_kva=m_norm_kva, w_ukv=m_w_ukv,
                 rpb=m_rpb, w_o_mla=m_w_o_mla, w_o_na=m_w_o_na, w_out=m_w_out, norm_mlp=m_norm_mlp, w_ff1=m_w_ff1,
                 w_ff2=m_w_ff2, norm_final=m_norm_final)
    mom_v = dict(norm_mix=v_norm_mix, w_in=v_w_in, norm_qa=v_norm_qa, w_uq=v_w_uq, norm_kva=v_norm_kva, w_ukv=v_w_ukv,
                 rpb=v_rpb, w_o_mla=v_w_o_mla, w_o_na=v_w_o_na, w_out=v_w_out, norm_mlp=v_norm_mlp, w_ff1=v_w_ff1,
                 w_ff2=v_w_ff2, norm_final=v_norm_final)
    depth, s, h = dm.depth, dm.s, dm.mla_h
    shard_shapes = {n: params[n].shape[1:] for n in SHARDED}

    pos = jnp.arange(s, dtype=F32)
    inv_freq = 1.0 / (ROPE_THETA ** (jnp.arange(0, QK_ROPE, 2, dtype=F32) / QK_ROPE))
    ang = pos[:, None] * inv_freq[None, :]
    cos_k, sin_k = jnp.cos(ang), jnp.sin(ang)
    cos_q, sin_q = jnp.tile(cos_k, (1, h)), jnp.tile(sin_k, (1, h))

    weights = []
    for l in range(depth):
        flat = _to_flat({n: params[n][l].astype(BF16) for n in SHARDED})
        got = _from_flat(all_gather(flat, name="gather_weights"), shard_shapes)
        weights.append(_kernel_weights(dm, {n: _full(n, got[n]) for n in SHARDED}))
    gains = [dict(norm_mix=norm_mix[l][None], norm_qa=norm_qa[l][None], norm_kva=norm_kva[l][None],
                  norm_mlp=norm_mlp[l][None], rpb=rpb[l]) for l in range(depth)]

    xl = x[0]
    saved = []
    for l in range(depth):
        xl, sv = layer_fwd(dm, f"l{l}", xl, weights[l], gains[l], cos_q, sin_q, cos_k, sin_k)
        saved.append(sv)
    dx, g_final, loss_part = loss_head(xl, norm_final[None], loss_target[0], name="loss_head")
    loss = lax.psum(loss_part[0, 0], MESH_AXES)

    rep = {n: [None] * depth for n in REPLICATED if n != "norm_final"}
    recv = [None] * depth
    for l in reversed(range(depth)):
        dx, gw, gr = layer_bwd(dm, f"l{l}", dx, weights[l], gains[l], saved[l], cos_q, sin_q, cos_k, sin_k)
        for n in gr:
            rep[n][l] = gr[n]
        full = _reference_order(dm, gw)
        slots = jnp.concatenate([_slots(n, full[n]).reshape(N_DEV, -1, LANES) for n in SHARDED], axis=1)
        recv[l] = _from_flat_slots(all_to_all(slots, name="scatter_grads"), shard_shapes)
    rep_parts = {n: jnp.stack(rep[n]) for n in rep}
    rep_parts["norm_final"] = g_final
    rep_all = all_gather(_pack_replicated(dm, rep_parts), name="gather_small_grads")

    outs = {}
    for n in SHARDED:
        r, c = shard_shapes[n]
        g_slots = jnp.stack([recv[l][n] for l in range(depth)], axis=1).reshape(N_DEV, depth * r, c)
        flat2 = lambda a: a.reshape(depth * r, c)
        res = adamw(flat2(params[n]), flat2(mom_m[n]), flat2(mom_v[n]), g_slots, name=f"adamw_{n}")
        outs[n] = [a.reshape(params[n].shape) for a in res]
    n_rep = sum(int(np.prod(params[n].shape)) for n in REPLICATED)
    pack = lambda d: _pack_replicated(dm, d)
    res = adamw(pack(params), pack(mom_m), pack(mom_v), rep_all, name="adamw_replicated")
    off = 0
    for n in REPLICATED:
        size = int(np.prod(params[n].shape))
        outs[n] = [a.reshape(-1)[off:off + size].reshape(params[n].shape) for a in res]
        off += size
    assert off == n_rep

    grad_x = dx[None]
    return (loss, grad_x, *[outs[n][0] for n in WEIGHTS], *[outs[n][1] for n in WEIGHTS],
            *[outs[n][2] for n in WEIGHTS], *[outs[n][3] for n in WEIGHTS])


def _from_flat_slots(flat, shapes):
    return _from_flat(flat, shapes)
```

```python
import functools

import numpy as np
import jax
import jax.numpy as jnp
from jax import lax
from jax.experimental import pallas as pl
from jax.experimental.pallas import tpu as pltpu

F32 = jnp.float32
BF16 = jnp.bfloat16
MESH_AXES = ("x", "y", "c")
N_DEV = 8
LANES = 128

QK_NOPE = 128
QK_ROPE = 64
HALF_ROPE = QK_ROPE // 2
V_HEAD = 128
QK_DIM = QK_NOPE + QK_ROPE
NA_HEAD_DIM = 128
GRID_W = 64
NA_KH = 8
NA_KW = 16
NA_SPAN_ROWS = 2 * NA_KH
NA_SPAN = NA_SPAN_ROWS * GRID_W
NA_PAD_LO = (NA_KH - 1) * GRID_W
NA_PAD_HI = NA_KH * GRID_W
ROPE_THETA = 10000.0
EPS = 1e-6
NEG = -1e30

ADAM_LR = 0.001
ADAM_B1 = 0.9
ADAM_B2 = 0.999
ADAM_EPS = 1e-08
ADAM_WD = 0.01
ADAM_STEP = 10

VMEM_LIMIT_V7X = 56 * 1024 * 1024

NN = (((1,), (0,)), ((), ()))
NT = (((1,), (1,)), ((), ()))
TN = (((0,), (0,)), ((), ()))


def _pick(dim, target, align=LANES):
    if dim <= target:
        return dim
    t = (target // align) * align
    while t >= align:
        if dim % t == 0:
            return t
        t -= align
    return dim


def _params(sem):
    return pltpu.CompilerParams(dimension_semantics=sem, vmem_limit_bytes=VMEM_LIMIT_V7X)


def mm(a, b, *, mode, name, out_dtypes=(F32,), epi=None, extras=(), tm=1024, tn=1024, tk=2048, exact=False):
    if mode == "nn":
        (m, k), (k2, n) = a.shape, b.shape
    elif mode == "nt":
        (m, k), (n, k2) = a.shape, b.shape
    else:
        (k, m), (k2, n) = a.shape, b.shape
    assert k == k2, (a.shape, b.shape, mode)
    tm, tn, tk = _pick(m, tm), _pick(n, tn), _pick(k, tk)
    nk = k // tk
    a_spec = pl.BlockSpec((tk, tm), lambda i, j, s: (s, i)) if mode == "tn" else pl.BlockSpec((tm, tk), lambda i, j, s: (i, s))
    b_spec = pl.BlockSpec((tn, tk), lambda i, j, s: (j, s)) if mode == "nt" else pl.BlockSpec((tk, tn), lambda i, j, s: (s, j))
    tile = pl.BlockSpec((tm, tn), lambda i, j, s: (i, j))
    dims = {"nn": NN, "nt": NT, "tn": TN}[mode]
    n_extra, n_out = len(extras), len(out_dtypes)

    def product(a_ref, b_ref):
        if exact:
            return lax.dot_general(a_ref[...], b_ref[...], dims, precision=lax.Precision.HIGHEST,
                                   preferred_element_type=F32)
        return lax.dot_general(a_ref[...].astype(BF16), b_ref[...].astype(BF16), dims, preferred_element_type=F32)

    def finish(r, extra_refs, out_refs):
        res = (r,) if epi is None else epi(r, *[e[...] for e in extra_refs])
        for o, v in zip(out_refs, res):
            o[...] = v.astype(o.dtype)

    def body_one_step(a_ref, b_ref, *rest):
        finish(product(a_ref, b_ref), rest[:n_extra], rest[n_extra:])

    def body(a_ref, b_ref, *rest):
        extra_refs, out_refs, acc = rest[:n_extra], rest[n_extra:n_extra + n_out], rest[-1]
        step = pl.program_id(2)

        @pl.when(step == 0)
        def _():
            acc[...] = product(a_ref, b_ref)

        @pl.when(step > 0)
        def _():
            acc[...] += product(a_ref, b_ref)

        @pl.when(step == nk - 1)
        def _():
            finish(acc[...], extra_refs, out_refs)

    outs = pl.pallas_call(
        body_one_step if nk == 1 else body, name=name, grid=(m // tm, n // tn, nk),
        in_specs=[a_spec, b_spec] + [tile] * n_extra,
        out_specs=[tile] * n_out,
        out_shape=[jax.ShapeDtypeStruct((m, n), d) for d in out_dtypes],
        scratch_shapes=[] if nk == 1 else [pltpu.VMEM((tm, tn), F32)],
        compiler_params=_params(("parallel", "parallel", "arbitrary")),
    )(a, b, *extras)
    return outs[0] if n_out == 1 else outs


def blockwise(fn, ins, outs, *, grid, name, sums=()):
    n_in, n_axes = len(ins), len(grid)

    def body(*refs):
        res = fn(*[r[...] for r in refs[:n_in]])
        first = functools.reduce(jnp.logical_and, [pl.program_id(ax) == 0 for ax in range(n_axes)])
        for idx, (o, v) in enumerate(zip(refs[n_in:], res)):
            if idx in sums:
                @pl.when(first)
                def _(o=o):
                    o[...] = jnp.zeros_like(o)

                o[...] += v.astype(o.dtype)
            else:
                o[...] = v.astype(o.dtype)

    sem = ("arbitrary" if sums else "parallel",) * n_axes
    res = pl.pallas_call(
        body, name=name, grid=grid,
        in_specs=[pl.BlockSpec(blk, imap) for _, blk, imap in ins],
        out_specs=[pl.BlockSpec(blk, imap) for _, _, blk, imap in outs],
        out_shape=[jax.ShapeDtypeStruct(shape, dt) for shape, dt, _, _ in outs],
        compiler_params=_params(sem),
    )(*[a for a, _, _ in ins])
    return res


def rows(arr, tr, width=None, col=0):
    width = arr.shape[1] if width is None else width
    return (arr, (tr, width), lambda i, col=col: (i, col))


def whole(arr):
    return (arr, arr.shape, lambda i: (0, 0))


def out_rows(n_rows, width, dtype, tr):
    return ((n_rows, width), dtype, (tr, width), lambda i: (i, 0))


def out_sum(shape, dtype=F32):
    return (shape, dtype, shape, lambda i: (0, 0))


def _rstd(x):
    return lax.rsqrt(jnp.mean(x * x, axis=-1, keepdims=True) + EPS)


def _colsum(v):
    return jnp.sum(v, axis=0, keepdims=True)


def rmsnorm_fwd(x, g, *, width=None, col=0, tr=256, name):
    n = x.shape[0]
    tr = _pick(n, tr, 8)
    width = x.shape[1] if width is None else width

    def fn(xv, gv):
        return ((xv * _rstd(xv)) * gv,)

    return blockwise(fn, [rows(x, tr, width, col), whole(g)], [out_rows(n, width, BF16, tr)],
                     grid=(n // tr,), name=name)[0]


def rmsnorm_bwd(x, g, dy, res=None, *, width=None, col=0, out_dtype=F32, tr=256, name):
    n = x.shape[0]
    tr = _pick(n, tr, 8)
    width = x.shape[1] if width is None else width

    def fn(xv, gv, dyv, *resv):
        dyv = dyv.astype(F32)
        r = _rstd(xv)
        xhat = xv * r
        dxhat = dyv * gv
        dx = r * (dxhat - xhat * jnp.mean(dxhat * xhat, axis=-1, keepdims=True))
        if resv:
            dx = dx + resv[0]
        return dx, _colsum(dyv * xhat)

    ins = [rows(x, tr, width, col), whole(g), rows(dy, tr)] + ([rows(res, tr)] if res is not None else [])
    return blockwise(fn, ins, [out_rows(n, width, out_dtype, tr), out_sum((1, width))],
                     grid=(n // tr,), name=name, sums=(1,))


def rope_pair(x1, x2, cos, sin, *, tr=512, name):
    n, w = x1.shape
    tr = _pick(n, tr, 8)

    def fn(a, b, c, s):
        a, b = a.astype(F32), b.astype(F32)
        return a * c - b * s, b * c + a * s

    return blockwise(fn, [rows(x1, tr), rows(x2, tr), rows(cos, tr), rows(sin, tr)],
                     [out_rows(n, w, BF16, tr), out_rows(n, w, BF16, tr)], grid=(n // tr,), name=name)


def rope_pair_headsum(d1, d2, cos, sin, *, tr=512, name):
    h, n, w = d1.shape
    tr = _pick(n, tr, 8)

    def fn(a, b, c, s):
        a, b = jnp.sum(a.astype(F32), axis=0), jnp.sum(b.astype(F32), axis=0)
        return a * c + b * s, b * c - a * s

    lead = lambda arr: (arr, (h, tr, w), lambda i: (0, i, 0))
    return blockwise(fn, [lead(d1), lead(d2), rows(cos, tr), rows(sin, tr)],
                     [out_rows(n, w, BF16, tr), out_rows(n, w, BF16, tr)], grid=(n // tr,), name=name)


def gate_fwd(proj, y_a, y_b, *, d, ga_col, gb_col, tr=256, name):
    n = proj.shape[0]
    tr = _pick(n, tr, 8)

    def fn(ga, gb, ya, yb):
        return (jax.nn.sigmoid(ga) * ya + jax.nn.sigmoid(gb) * yb,)

    return blockwise(fn, [rows(proj, tr, d, ga_col), rows(proj, tr, d, gb_col), rows(y_a, tr), rows(y_b, tr)],
                     [out_rows(n, d, BF16, tr)], grid=(n // tr,), name=name)[0]


def gate_bwd(proj, y_a, y_b, dmerged, *, d, ga_col, gb_col, tr=256, name):
    n = proj.shape[0]
    tr = _pick(n, tr, 8)

    def fn(ga, gb, ya, yb, dm):
        sa, sb = jax.nn.sigmoid(ga), jax.nn.sigmoid(gb)
        return dm * sa, dm * sb, dm * ya * (sa * (1.0 - sa)), dm * yb * (sb * (1.0 - sb))

    return blockwise(fn, [rows(proj, tr, d, ga_col), rows(proj, tr, d, gb_col), rows(y_a, tr), rows(y_b, tr),
                          rows(dmerged, tr)],
                     [out_rows(n, d, BF16, tr)] * 4, grid=(n // tr,), name=name)


def loss_head(x, g, target, *, tr=256, name):
    n, d = x.shape
    tr = _pick(n, tr, 8)

    def fn(xv, gv, tv):
        r = _rstd(xv)
        xhat = xv * r
        diff = xhat * gv - tv
        loss = 0.5 * jnp.sum(jnp.sum(diff * diff, axis=-1, keepdims=True) / d, axis=0, keepdims=True)
        dy = diff / d
        dxhat = dy * gv
        dx = r * (dxhat - xhat * jnp.mean(dxhat * xhat, axis=-1, keepdims=True))
        return dx, _colsum(dy * xhat), jnp.broadcast_to(loss, (8, LANES))

    return blockwise(fn, [rows(x, tr), whole(g), rows(target, tr)],
                     [out_rows(n, d, F32, tr), out_sum((1, d)), out_sum((8, LANES))],
                     grid=(n // tr,), name=name, sums=(1, 2))


def adamw(w, m, v, g_slots, *, tr=256, name):
    n, c = w.shape
    tr = _pick(n, tr, 8)
    c1 = 1.0 / (1.0 - ADAM_B1 ** ADAM_STEP)
    c2 = 1.0 / (1.0 - ADAM_B2 ** ADAM_STEP)

    def fn(wv, mv, vv, gs):
        g = gs[0].astype(F32)
        for dev in range(1, N_DEV):
            g = g + gs[dev].astype(F32)
        m_new = ADAM_B1 * mv + (1.0 - ADAM_B1) * g
        v_new = ADAM_B2 * vv + (1.0 - ADAM_B2) * (g * g)
        delta = -ADAM_LR * ((m_new * c1) / (jnp.sqrt(v_new * c2) + ADAM_EPS) + ADAM_WD * wv)
        return g, delta, m_new, v_new

    slots = (g_slots, (N_DEV, tr, c), lambda i: (0, i, 0))
    return blockwise(fn, [rows(w, tr), rows(m, tr), rows(v, tr), slots],
                     [out_rows(n, c, F32, tr)] * 4, grid=(n // tr,), name=name)


LOG2E = 1.4426950408889634
MLA_SUB = 512
MLA_FWD_BQ, MLA_BWD_BQ, MLA_BK = 1024, 1024, 1024


def mla_fwd(q, k, vt, *, name, ride=None):
    h, s, dq = q.shape
    bq, bk = _pick(s, MLA_FWD_BQ), _pick(s, MLA_BK)
    sub = min(MLA_SUB, bq)
    nk = s // bk
    scale = QK_DIM ** -0.5
    c2 = scale * LOG2E
    grid = (h, s // bq, nk)

    def body(q_ref, k_ref, vt_ref, *rest):
        if ride is None:
            o_ref, lse_ref, m_s, l_s, acc = rest
        else:
            x_ref, o_ref, lse_ref, g_ref, m_s, l_s, acc, *sems = rest
            first, middle, last = _grid_flags(grid)
            ride_start, ride_middle, ride_finish = ride.steps(x_ref, g_ref, sems)
            pl.when(first)(ride_start)
            pl.when(middle)(ride_middle)
        j = pl.program_id(2)

        @pl.when(j == 0)
        def _():
            m_s[...] = jnp.full_like(m_s, NEG)
            l_s[...] = jnp.zeros_like(l_s)
            acc[...] = jnp.zeros_like(acc)

        def scores(c0):
            return lax.dot_general(k_ref[...], q_ref[pl.ds(c0, sub), :], NT, preferred_element_type=F32)

        starts = list(range(0, bq, sub))
        st_next = scores(starts[0])
        for n, c0 in enumerate(starts):
            cols = pl.ds(c0, sub)
            st = st_next
            if n + 1 < len(starts):
                st_next = scores(starts[n + 1])
            m_prev = m_s[:, cols]
            m_new = jnp.maximum(m_prev, jnp.max(st, axis=0, keepdims=True))
            alpha = jnp.exp2((m_prev - m_new) * c2)
            pt = jnp.exp2((st - m_new) * c2)
            l_s[:, cols] = alpha * l_s[:, cols] + jnp.sum(pt, axis=0, keepdims=True)
            acc[:, cols] = alpha * acc[:, cols] + lax.dot_general(vt_ref[...], pt.astype(BF16), NN,
                                                                  preferred_element_type=F32)
            m_s[:, cols] = m_new

        @pl.when(j == nk - 1)
        def _():
            o_ref[...] = (acc[...] / l_s[...]).T
            lse_ref[...] = m_s[...] * scale + jnp.log(l_s[...])

        if ride is not None:
            pl.when(last)(ride_finish)

    in_specs = [pl.BlockSpec((None, bq, dq), lambda hh, i, j: (hh, i, 0)),
                pl.BlockSpec((None, bk, dq), lambda hh, i, j: (hh, j, 0)),
                pl.BlockSpec((None, V_HEAD, bk), lambda hh, i, j: (hh, 0, j))]
    out_specs = [pl.BlockSpec((bq, V_HEAD), lambda hh, i, j: (i, hh)),
                 pl.BlockSpec((None, 1, bq), lambda hh, i, j: (hh, 0, i))]
    out_shape = [jax.ShapeDtypeStruct((s, h * V_HEAD), F32), jax.ShapeDtypeStruct((h, 1, s), F32)]
    scratch = [pltpu.VMEM((1, bq), F32), pltpu.VMEM((1, bq), F32), pltpu.VMEM((V_HEAD, bq), F32)]
    args = (q, k, vt)
    sem = ("parallel", "parallel", "arbitrary")
    if ride is not None:
        in_specs, out_specs, out_shape, scratch, args = ride.extend(in_specs, out_specs, out_shape, scratch, args)
        sem = ("arbitrary",) * 3
    return pl.pallas_call(body, name=name, grid=grid, in_specs=in_specs, out_specs=out_specs, out_shape=out_shape,
                          scratch_shapes=scratch, compiler_params=_params(sem))(*args)


def mla_delta(do, o, h, *, tr=512, name):
    s = do.shape[0]
    tr = _pick(s, tr, 8)

    def fn(dov, ov):
        return (jnp.sum(dov.astype(F32) * ov, axis=-1, keepdims=True),)

    blk = lambda arr: (arr, (tr, V_HEAD), lambda hh, i: (i, hh))
    return blockwise(fn, [blk(do), blk(o)], [((h, s, 1), F32, (None, tr, 1), lambda hh, i: (hh, i, 0))],
                     grid=(h, s // tr), name=name)[0]


def mla_bwd(q, k, kt, kv, do, lse, delta, *, v_col0, name, ride=None):
    h, s, dq = q.shape
    bq, bk = _pick(s, MLA_BWD_BQ), _pick(s, MLA_BK)
    sub = min(MLA_SUB, bq)
    nq = s // bq
    scale = QK_DIM ** -0.5
    c2 = scale * LOG2E
    grid = (h, s // bk, nq)

    def body(q_ref, k_ref, kt_ref, v_ref, do_ref, lse_ref, delta_ref, *rest):
        if ride is None:
            dq_ref, dk_ref, dv_ref, dk_acc, dv_acc = rest
        else:
            x_ref, dq_ref, dk_ref, dv_ref, g_ref, dk_acc, dv_acc, *sems = rest
            first, _, last = _grid_flags(grid)
            ride_start, _, ride_finish = ride.steps(x_ref, g_ref, sems)
            pl.when(first)(ride_start)
        j, i = pl.program_id(1), pl.program_id(2)

        @pl.when(i == 0)
        def _():
            dk_acc[...] = jnp.zeros_like(dk_acc)
            dv_acc[...] = jnp.zeros_like(dv_acc)

        def scores(c0):
            cols = pl.ds(c0, sub)
            return (lax.dot_general(k_ref[...], q_ref[cols, :], NT, preferred_element_type=F32),
                    lax.dot_general(v_ref[...], do_ref[cols, :], NT, preferred_element_type=F32))

        starts = list(range(0, bq, sub))
        nxt = scores(starts[0])
        for n, c0 in enumerate(starts):
            cols = pl.ds(c0, sub)
            st, dpt = nxt
            if n + 1 < len(starts):
                nxt = scores(starts[n + 1])
            q_sub, do_sub = q_ref[cols, :], do_ref[cols, :]
            pt = jnp.exp2(st * c2 - lse_ref[:, cols] * LOG2E)
            ds_b = ((pt * (dpt - delta_ref[:, cols])) * scale).astype(BF16)
            dv_acc[...] += lax.dot_general(pt.astype(BF16), do_sub, NN, preferred_element_type=F32)
            dk_acc[...] += lax.dot_general(ds_b, q_sub, NN, preferred_element_type=F32)
            dq_t = lax.dot_general(kt_ref[...], ds_b, NN, preferred_element_type=F32)

            @pl.when(j == 0)
            def _():
                dq_ref[i, :, cols] = dq_t

            @pl.when(j > 0)
            def _():
                dq_ref[i, :, cols] += dq_t

        @pl.when(i == nq - 1)
        def _():
            dk_ref[...] = dk_acc[...]
            dv_ref[...] = dv_acc[...].astype(dv_ref.dtype)

        if ride is not None:
            pl.when(last)(ride_finish)

    in_specs = [pl.BlockSpec((None, bq, dq), lambda hh, j, i: (hh, i, 0)),
                pl.BlockSpec((None, bk, dq), lambda hh, j, i: (hh, j, 0)),
                pl.BlockSpec((None, dq, bk), lambda hh, j, i: (hh, 0, j)),
                pl.BlockSpec((bk, V_HEAD), lambda hh, j, i: (j, v_col0 + hh)),
                pl.BlockSpec((bq, V_HEAD), lambda hh, j, i: (i, hh)),
                pl.BlockSpec((None, 1, bq), lambda hh, j, i: (hh, 0, i)),
                pl.BlockSpec((None, 1, bq), lambda hh, j, i: (hh, 0, i))]
    out_specs = [pl.BlockSpec((None, nq, dq, bq), lambda hh, j, i: (hh, 0, 0, 0)),
                 pl.BlockSpec((None, bk, dq), lambda hh, j, i: (hh, j, 0)),
                 pl.BlockSpec((bk, V_HEAD), lambda hh, j, i: (j, hh))]
    out_shape = [jax.ShapeDtypeStruct((h, nq, dq, bq), F32), jax.ShapeDtypeStruct((h, s, dq), F32),
                 jax.ShapeDtypeStruct((s, h * V_HEAD), BF16)]
    scratch = [pltpu.VMEM((bk, dq), F32), pltpu.VMEM((bk, V_HEAD), F32)]
    args = (q, k, kt, kv, do, lse, delta)
    sem = ("parallel", "arbitrary", "arbitrary")
    if ride is not None:
        in_specs, out_specs, out_shape, scratch, args = ride.extend(in_specs, out_specs, out_shape, scratch, args)
        sem = ("arbitrary",) * 3
    return pl.pallas_call(body, name=name, grid=grid, in_specs=in_specs, out_specs=out_specs, out_shape=out_shape,
                          scratch_shapes=scratch, compiler_params=_params(sem))(*args)


def _na_probs(sc, bias, r, n_rows):
    sc = sc * (NA_HEAD_DIM ** -0.5) + bias
    lo = jnp.clip(r - NA_KH // 2, 0, n_rows - NA_KH) - r + (NA_KH - 1)
    dy = lax.broadcasted_iota(jnp.int32, (1, NA_SPAN), 1) // GRID_W
    sc = jnp.where((dy >= lo) & (dy < lo + NA_KH), sc, NEG)
    p = jnp.exp(sc - jnp.max(sc, axis=-1, keepdims=True))
    return p / jnp.sum(p, axis=-1, keepdims=True)


def na_fwd(proj, kp, vp, biasm, *, q_col0, rows_per_step=8, name):
    s = proj.shape[0]
    sp, hw = kp.shape
    h = hw // NA_HEAD_DIM
    n_rows = s // GRID_W
    rb = min(rows_per_step, n_rows)
    tq = rb * GRID_W

    def body(q_ref, k_ref, v_ref, b_ref, o_ref):
        i = pl.program_id(1)
        bias = b_ref[...]

        def span(rl):
            return pl.ds(pl.multiple_of((i * rb + rl) * GRID_W, GRID_W), NA_SPAN)

        def scores(rl):
            q_row = q_ref[pl.ds(rl * GRID_W, GRID_W), :].astype(BF16)
            return lax.dot_general(q_row, k_ref[span(rl), :], NT, preferred_element_type=F32)

        sc_next = scores(0)
        for rl in range(rb):
            sc = sc_next
            if rl + 1 < rb:
                sc_next = scores(rl + 1)
            p = _na_probs(sc, bias, i * rb + rl, n_rows)
            o_ref[pl.ds(rl * GRID_W, GRID_W), :] = lax.dot_general(
                p.astype(BF16), v_ref[span(rl), :], NN, preferred_element_type=F32)

    return pl.pallas_call(
        body, name=name, grid=(h, n_rows // rb),
        in_specs=[pl.BlockSpec((tq, NA_HEAD_DIM), lambda hh, i: (i, q_col0 + hh)),
                  pl.BlockSpec((sp, NA_HEAD_DIM), lambda hh, i: (0, hh)),
                  pl.BlockSpec((sp, NA_HEAD_DIM), lambda hh, i: (0, hh)),
                  pl.BlockSpec((None, GRID_W, NA_SPAN), lambda hh, i: (hh, 0, 0))],
        out_specs=pl.BlockSpec((tq, NA_HEAD_DIM), lambda hh, i: (i, hh)),
        out_shape=jax.ShapeDtypeStruct((s, hw), F32),
        compiler_params=_params(("parallel", "arbitrary")),
    )(proj, kp, vp, biasm)


def na_bwd(proj, kp, vp, biasm, do, *, q_col0, rows_per_step=8, name):
    s = proj.shape[0]
    sp, hw = kp.shape
    h = hw // NA_HEAD_DIM
    n_rows = s // GRID_W
    rb = min(rows_per_step, n_rows)
    tq = rb * GRID_W
    scale = NA_HEAD_DIM ** -0.5

    def body(q_ref, k_ref, v_ref, b_ref, do_ref, dq_ref, dk_ref, dv_ref, db_ref):
        i = pl.program_id(1)

        @pl.when(i == 0)
        def _():
            dk_ref[...] = jnp.zeros_like(dk_ref)
            dv_ref[...] = jnp.zeros_like(dv_ref)
            db_ref[...] = jnp.zeros_like(db_ref)

        bias = b_ref[...]

        def span(rl):
            return pl.ds(pl.multiple_of((i * rb + rl) * GRID_W, GRID_W), NA_SPAN)

        def query(rl):
            rows_ = pl.ds(rl * GRID_W, GRID_W)
            return q_ref[rows_, :].astype(BF16), do_ref[rows_, :]

        def scores(rl):
            q_row, do_row = query(rl)
            return (lax.dot_general(q_row, k_ref[span(rl), :], NT, preferred_element_type=F32),
                    lax.dot_general(do_row, v_ref[span(rl), :], NT, preferred_element_type=F32))

        nxt = scores(0)
        for rl in range(rb):
            sc, dp = nxt
            if rl + 1 < rb:
                nxt = scores(rl + 1)
            q_row, do_row = query(rl)
            p = _na_probs(sc, bias, i * rb + rl, n_rows)
            ds = p * (dp - jnp.sum(dp * p, axis=-1, keepdims=True))
            db_ref[...] += ds
            ds_b = (ds * scale).astype(BF16)
            dq_ref[pl.ds(rl * GRID_W, GRID_W), :] = lax.dot_general(
                ds_b, k_ref[span(rl), :], NN, preferred_element_type=F32).astype(dq_ref.dtype)
            dk_ref[span(rl), :] += lax.dot_general(ds_b, q_row, TN, preferred_element_type=F32)
            dv_ref[span(rl), :] += lax.dot_general(p.astype(BF16), do_row, TN, preferred_element_type=F32)

    return pl.pallas_call(
        body, name=name, grid=(h, n_rows // rb),
        in_specs=[pl.BlockSpec((tq, NA_HEAD_DIM), lambda hh, i: (i, q_col0 + hh)),
                  pl.BlockSpec((sp, NA_HEAD_DIM), lambda hh, i: (0, hh)),
                  pl.BlockSpec((sp, NA_HEAD_DIM), lambda hh, i: (0, hh)),
                  pl.BlockSpec((None, GRID_W, NA_SPAN), lambda hh, i: (hh, 0, 0)),
                  pl.BlockSpec((tq, NA_HEAD_DIM), lambda hh, i: (i, hh))],
        out_specs=[pl.BlockSpec((tq, NA_HEAD_DIM), lambda hh, i: (i, hh)),
                   pl.BlockSpec((sp, NA_HEAD_DIM), lambda hh, i: (0, hh)),
                   pl.BlockSpec((sp, NA_HEAD_DIM), lambda hh, i: (0, hh)),
                   pl.BlockSpec((None, GRID_W, NA_SPAN), lambda hh, i: (hh, 0, 0))],
        out_shape=[jax.ShapeDtypeStruct((s, hw), BF16), jax.ShapeDtypeStruct((sp, hw), F32),
                   jax.ShapeDtypeStruct((sp, hw), F32), jax.ShapeDtypeStruct((h, GRID_W, NA_SPAN), F32)],
        compiler_params=_params(("parallel", "arbitrary")),
    )(proj, kp, vp, biasm, do)


def _na_tables():
    qc = np.arange(GRID_W)[:, None]
    kc = np.arange(GRID_W)[None, :]
    col_start = np.clip(qc - NA_KW // 2, 0, GRID_W - NA_KW)
    col_ok = (kc >= col_start) & (kc < col_start + NA_KW)
    dx = np.clip(kc - qc, -(NA_KW - 1), NA_KW - 1) + (NA_KW - 1)
    return col_ok, dx


def na_bias_table(rpb_l):
    col_ok, dx = _na_tables()
    t = rpb_l[:, :, dx]
    t = jnp.where(col_ok[None, None], t, NEG)
    t = jnp.pad(t, ((0, 0), (0, 1), (0, 0), (0, 0)), constant_values=NEG)
    return t.transpose(0, 2, 1, 3).reshape(rpb_l.shape[0], GRID_W, NA_SPAN)


def na_bias_grad(dbias, *, name):
    h = dbias.shape[0]
    _, dx = _na_tables()
    n_dx = 2 * NA_KW - 1
    onehot = np.zeros((GRID_W * GRID_W, LANES), np.float32)
    onehot[np.arange(GRID_W * GRID_W), dx.reshape(-1)] = 1.0
    t = dbias.reshape(h, GRID_W, NA_SPAN_ROWS, GRID_W)[:, :, :2 * NA_KH - 1]
    t = t.transpose(0, 2, 1, 3).reshape(h * (2 * NA_KH - 1), GRID_W * GRID_W)
    t = jnp.pad(t, ((0, (-t.shape[0]) % 8), (0, 0)))
    out = mm(t, jnp.asarray(onehot), mode="nn", name=name, exact=True, tk=1024)
    return out[:h * (2 * NA_KH - 1), :n_dx].reshape(h, 2 * NA_KH - 1, n_dx)


def _flip(v, bit):
    return 1 - v if bit else v


def comm_scratch():
    return [pltpu.SemaphoreType.DMA((N_DEV - 1,)), pltpu.SemaphoreType.DMA((N_DEV - 1,)), pltpu.SemaphoreType.DMA]


def gather_plan(x_ref, out_ref, send_sems, recv_sems, local_sem):
    x, y, c = lax.axis_index("x"), lax.axis_index("y"), lax.axis_index("c")
    me, sibling = (x, y, c), (x, y, 1 - c)
    chips = [(1 - x, y), (x, 1 - y), (1 - x, 1 - y)]

    def slot(px, py, pc):
        return out_ref.at[4 * px + 2 * py + pc]

    def copy(k, blk, to, src=None):
        return pltpu.make_async_remote_copy(
            src_ref=slot(*blk) if src is None else src, dst_ref=slot(*blk),
            send_sem=send_sems.at[k], recv_sem=recv_sems.at[k],
            device_id=to, device_id_type=pl.DeviceIdType.MESH)

    def mine():
        return pltpu.make_async_copy(x_ref, slot(*me), local_sem)

    def first():
        return [copy(0, me, sibling, src=x_ref)] + [copy(1 + j, me, (*chip, c), src=x_ref)
                                                    for j, chip in enumerate(chips)]

    def passed():
        return [copy(4 + j, (*chip, c), sibling) for j, chip in enumerate(chips)]

    def start():
        mine().start()
        for cp in first():
            cp.start()

    def forward():
        for j, (chip, cp) in enumerate(zip(chips, passed())):
            copy(1 + j, (*chip, c), me).wait_recv()
            cp.start()

    def finish():
        copy(0, sibling, me).wait_recv()
        for j, chip in enumerate(chips):
            copy(4 + j, (*chip, 1 - c), me).wait_recv()
        for cp in first() + passed():
            cp.wait_send()
        mine().wait()

    return start, forward, finish


def exchange_plan(x_ref, out_ref, send_sems, recv_sems, local_sem):
    x, y, c = lax.axis_index("x"), lax.axis_index("y"), lax.axis_index("c")
    me = 4 * x + 2 * y + c

    def peer_of(k):
        peer = (_flip(x, k & 4), _flip(y, k & 2), _flip(c, k & 1))
        return peer, 4 * peer[0] + 2 * peer[1] + peer[2]

    def send(k):
        peer, theirs = peer_of(k)
        return pltpu.make_async_remote_copy(
            src_ref=x_ref.at[theirs], dst_ref=out_ref.at[me],
            send_sem=send_sems.at[k - 1], recv_sem=recv_sems.at[k - 1],
            device_id=peer, device_id_type=pl.DeviceIdType.MESH)

    def arrival(k):
        peer, theirs = peer_of(k)
        return pltpu.make_async_remote_copy(
            src_ref=x_ref.at[me], dst_ref=out_ref.at[theirs],
            send_sem=send_sems.at[k - 1], recv_sem=recv_sems.at[k - 1],
            device_id=peer, device_id_type=pl.DeviceIdType.MESH)

    def mine():
        return pltpu.make_async_copy(x_ref.at[me], out_ref.at[me], local_sem)

    def start():
        mine().start()
        for k in range(1, N_DEV):
            send(k).start()

    def finish():
        for k in range(1, N_DEV):
            arrival(k).wait_recv()
        for k in range(1, N_DEV):
            send(k).wait_send()
        mine().wait()

    return start, finish


def _comm_call(body, arr, out_shape, name):
    return pl.pallas_call(
        body, name=name, out_shape=jax.ShapeDtypeStruct(out_shape, arr.dtype),
        in_specs=[pl.BlockSpec(memory_space=pl.ANY)], out_specs=pl.BlockSpec(memory_space=pl.ANY),
        scratch_shapes=comm_scratch(),
    )(arr)


def all_gather(block, *, name):
    def body(*refs):
        for step in gather_plan(*refs):
            step()

    return _comm_call(body, block, (N_DEV,) + block.shape, name)


def all_to_all(slots, *, name):
    def body(*refs):
        for step in exchange_plan(*refs):
            step()

    return _comm_call(body, slots, slots.shape, name)


class Ride:
    def __init__(self, kind, arr):
        self.kind, self.arr = kind, arr
        self.out_shape = (N_DEV,) + arr.shape if kind == "gather" else arr.shape

    def extend(self, in_specs, out_specs, out_shape, scratch, args):
        any_spec = pl.BlockSpec(memory_space=pl.ANY)
        return (in_specs + [any_spec], out_specs + [any_spec],
                out_shape + [jax.ShapeDtypeStruct(self.out_shape, self.arr.dtype)], scratch + comm_scratch(),
                args + (self.arr,))

    def steps(self, x_ref, out_ref, sems):
        plan = (gather_plan if self.kind == "gather" else exchange_plan)(x_ref, out_ref, *sems)
        return plan[0], (plan[1] if len(plan) == 3 else None), plan[-1]


def _grid_flags(grid):
    ids = [pl.program_id(ax) for ax in range(len(grid))]
    inner_zero = functools.reduce(jnp.logical_and, [i == 0 for i in ids[1:]])
    first = jnp.logical_and(ids[0] == 0, inner_zero)
    middle = jnp.logical_and(ids[0] == grid[0] // 2, inner_zero)
    last = functools.reduce(jnp.logical_and, [i == g - 1 for i, g in zip(ids, grid)])
    return first, middle, last


SHARDED = ("w_in", "w_uq", "w_ukv", "w_o_mla", "w_o_na", "w_out", "w_ff1", "w_ff2")
ROW_SHARDED = ("w_out", "w_ff2")
REPLICATED = ("norm_mix", "norm_qa", "norm_kva", "rpb", "norm_mlp", "norm_final")
WEIGHTS = ("norm_mix", "w_in", "norm_qa", "w_uq", "norm_kva", "w_ukv", "rpb", "w_o_mla", "w_o_na", "w_out",
           "norm_mlp", "w_ff1", "w_ff2", "norm_final")


def _to_flat(shards):
    return jnp.concatenate([shards[n].reshape(-1, LANES) for n in SHARDED], axis=0)


def _from_flat(flat, shapes):
    out, r0 = {}, 0
    for n in SHARDED:
        r, c = shapes[n]
        nr = r * c // LANES
        out[n] = flat[:, r0:r0 + nr].reshape(N_DEV, r, c)
        r0 += nr
    return out


def _full(name, gathered):
    if name in ROW_SHARDED:
        return gathered.reshape(-1, gathered.shape[2])
    return gathered.transpose(1, 0, 2).reshape(gathered.shape[1], -1)


def _slots(name, full):
    if name in ROW_SHARDED:
        return full.reshape(N_DEV, -1, full.shape[1])
    return full.reshape(full.shape[0], N_DEV, -1).transpose(1, 0, 2)


class Dims:
    def __init__(self, x, w_in, norm_qa, norm_kva, rpb, w_o_mla, w_o_na, w_ff1):
        self.s, self.d = x.shape[1], x.shape[2]
        self.depth = w_in.shape[0]
        self.q_lora, self.kv_lora = norm_qa.shape[1], norm_kva.shape[1]
        self.mla_w, self.na_w = w_o_mla.shape[1], w_o_na.shape[1]
        self.mla_h, self.na_h = self.mla_w // V_HEAD, self.na_w // NA_HEAD_DIM
        self.d_ff = w_ff1.shape[2] * N_DEV
        assert rpb.shape[1] == self.na_h and self.s % GRID_W == 0 and self.s // GRID_W >= NA_SPAN_ROWS
        self.in_lo = self.q_lora + self.kv_lora
        self.main_w = self.in_lo + 3 * self.na_w + 2 * self.d
        assert self.q_lora == self.kv_lora and self.in_lo % self.na_w == 0 and self.in_lo % LANES == 0
        assert (self.in_lo + 3 * self.na_w) % self.d == 0
        self.q_col0 = self.in_lo // NA_HEAD_DIM
        self.k_off = self.in_lo + self.na_w
        self.v_off = self.in_lo + 2 * self.na_w
        self.ga_col = (self.in_lo + 3 * self.na_w) // self.d
        self.gb_col = self.ga_col + 1


def _split_w_in(dm, w):
    lo = dm.in_lo
    main = jnp.concatenate([w[:, :lo], w[:, lo + QK_ROPE:]], axis=1)
    kpe = jnp.pad(w[:, lo:lo + QK_ROPE], ((0, 0), (0, LANES - QK_ROPE)))
    return main, kpe


def _join_w_in(dm, main, kpe):
    lo = dm.in_lo
    return jnp.concatenate([main[:, :lo], kpe[:, :QK_ROPE], main[:, lo:]], axis=1)


def _split_heads(w, h, widths):
    r = w.shape[0]
    w3 = w.reshape(r, h, sum(widths))
    out, o = [], 0
    for wd in widths:
        out.append(w3[:, :, o:o + wd].reshape(r, h * wd))
        o += wd
    return out


def _join_heads(parts, h):
    r = parts[0].shape[0]
    return jnp.concatenate([p.reshape(r, h, -1) for p in parts], axis=2).reshape(r, -1)


UQ_WIDTHS = (QK_NOPE, HALF_ROPE, HALF_ROPE)
UKV_WIDTHS = (QK_NOPE, V_HEAD)


def _by_head(parts, h):
    s = parts[0].shape[0]
    return jnp.concatenate([p.reshape(s, h, -1) for p in parts], axis=2).transpose(1, 0, 2)


def _from_head(t, widths):
    h, s, _ = t.shape
    t = t.transpose(1, 0, 2)
    out, o = [], 0
    for wd in widths:
        out.append(t[:, :, o:o + wd].reshape(s, h * wd))
        o += wd
    return out


def layer_fwd(dm, lname, x, w, g, cos_q, sin_q, cos_k, sin_k, next_flat=None):
    s, h = dm.s, dm.mla_h
    u = rmsnorm_fwd(x, g["norm_mix"], name=f"{lname}_norm_mix")
    proj = mm(u, w["in_main"], mode="nn", name=f"{lname}_proj")
    kpe = mm(u, w["in_kpe"], mode="nn", name=f"{lname}_proj_kpe")
    qn = rmsnorm_fwd(proj, g["norm_qa"], width=dm.q_lora, col=0, name=f"{lname}_norm_qa")
    kvn = rmsnorm_fwd(proj, g["norm_kva"], width=dm.kv_lora, col=1, name=f"{lname}_norm_kva")
    q = mm(qn, w["uq"], mode="nn", name=f"{lname}_uq")
    kv = mm(kvn, w["ukv"], mode="nn", out_dtypes=(BF16,), name=f"{lname}_ukv")
    nope_w, half_w = h * QK_NOPE, h * HALF_ROPE
    q1, q2 = rope_pair(q[:, nope_w:nope_w + half_w], q[:, nope_w + half_w:], cos_q, sin_q, name=f"{lname}_rope_q")
    k1, k2 = rope_pair(kpe[:, :HALF_ROPE], kpe[:, HALF_ROPE:QK_ROPE], cos_k, sin_k, name=f"{lname}_rope_k")
    qh = _by_head([q[:, :nope_w].astype(BF16), q1, q2], h)
    kh = _by_head([kv[:, :nope_w], jnp.tile(k1, (1, h)), jnp.tile(k2, (1, h))], h)
    vt = kv[:, nope_w:].reshape(s, h, V_HEAD).transpose(1, 2, 0)
    ride = None if next_flat is None else Ride("gather", next_flat)
    o_a, lse, *gathered = mla_fwd(qh, kh, vt, name=f"{lname}_mla", ride=ride)
    y_a = mm(o_a, w["o_mla"], mode="nn", name=f"{lname}_o_mla")

    pad = ((NA_PAD_LO, NA_PAD_HI), (0, 0))
    kp = jnp.pad(proj[:, dm.k_off:dm.k_off + dm.na_w].astype(BF16), pad)
    vp = jnp.pad(proj[:, dm.v_off:dm.v_off + dm.na_w].astype(BF16), pad)
    biasm = na_bias_table(g["rpb"])
    o_b = na_fwd(proj, kp, vp, biasm, q_col0=dm.q_col0, name=f"{lname}_na")
    y_b = mm(o_b, w["o_na"], mode="nn", name=f"{lname}_o_na")

    merged = gate_fwd(proj, y_a, y_b, d=dm.d, ga_col=dm.ga_col, gb_col=dm.gb_col, name=f"{lname}_gate")
    x1 = mm(merged, w["out"], mode="nn", epi=lambda r, res: (r + res,), extras=(x,), name=f"{lname}_out")
    u2 = rmsnorm_fwd(x1, g["norm_mlp"], name=f"{lname}_norm_mlp")
    hid, act = mm(u2, w["ff1"], mode="nn", out_dtypes=(F32, BF16),
                  epi=lambda r: (r, jnp.square(jnp.maximum(r, 0.0))), name=f"{lname}_ff1")
    x2 = mm(act, w["ff2"], mode="nn", epi=lambda r, res: (r + res,), extras=(x1,), name=f"{lname}_ff2")
    saved = dict(x=x, u=u, proj=proj, qn=qn, kvn=kvn, kv=kv, qh=qh, kh=kh, o_a=o_a, lse=lse, y_a=y_a, kp=kp, vp=vp,
                 biasm=biasm, o_b=o_b, y_b=y_b, merged=merged, x1=x1, u2=u2, hid=hid, act=act)
    return x2, saved, (gathered[0] if gathered else None)


def layer_bwd(dm, lname, dx2, w, g, sv, cos_q, sin_q, cos_k, sin_k, slots=None):
    h = dm.mla_h
    gw, gr = {}, {}
    gw["ff2"] = mm(sv["act"], dx2, mode="tn", out_dtypes=(BF16,), name=f"{lname}_d_ff2")
    dh = mm(dx2, w["ff2"], mode="nt", out_dtypes=(BF16,), extras=(sv["hid"],),
            epi=lambda r, hv: (r * (2.0 * jnp.maximum(hv, 0.0)),), name=f"{lname}_d_act")
    gw["ff1"] = mm(sv["u2"], dh, mode="tn", out_dtypes=(BF16,), name=f"{lname}_d_ff1")
    du2 = mm(dh, w["ff1"], mode="nt", name=f"{lname}_d_u2")
    dx1, gr["norm_mlp"] = rmsnorm_bwd(sv["x1"], g["norm_mlp"], du2, dx2, name=f"{lname}_d_norm_mlp")
    gw["out"] = mm(sv["merged"], dx1, mode="tn", out_dtypes=(BF16,), name=f"{lname}_d_out")
    dmerged = mm(dx1, w["out"], mode="nt", name=f"{lname}_d_merged")
    dy_a, dy_b, dga, dgb = gate_bwd(sv["proj"], sv["y_a"], sv["y_b"], dmerged, d=dm.d, ga_col=dm.ga_col,
                                    gb_col=dm.gb_col, name=f"{lname}_d_gate")
    gw["o_na"] = mm(sv["o_b"], dy_b, mode="tn", out_dtypes=(BF16,), name=f"{lname}_d_o_na")
    do_b = mm(dy_b, w["o_na"], mode="nt", out_dtypes=(BF16,), name=f"{lname}_d_ob")
    dq_na, dkp, dvp, dbias = na_bwd(sv["proj"], sv["kp"], sv["vp"], sv["biasm"], do_b, q_col0=dm.q_col0,
                                    name=f"{lname}_d_na")
    gr["rpb"] = na_bias_grad(dbias, name=f"{lname}_d_rpb")
    dk_na = dkp[NA_PAD_LO:NA_PAD_LO + dm.s].astype(BF16)
    dv_na = dvp[NA_PAD_LO:NA_PAD_LO + dm.s].astype(BF16)
    gw["o_mla"] = mm(sv["o_a"], dy_a, mode="tn", out_dtypes=(BF16,), name=f"{lname}_d_o_mla")
    do_a = mm(dy_a, w["o_mla"], mode="nt", out_dtypes=(BF16,), name=f"{lname}_d_oa")
    delta = mla_delta(do_a, sv["o_a"], h, name=f"{lname}_d_mla_delta").reshape(h, 1, dm.s)
    ride = None if slots is None else Ride("exchange", slots)
    dqt, dkh, dv, *received = mla_bwd(sv["qh"], sv["kh"], sv["kh"].transpose(0, 2, 1), sv["kv"], do_a, sv["lse"],
                                      delta, v_col0=h, name=f"{lname}_d_mla", ride=ride)
    dqh = dqt.transpose(0, 1, 3, 2).reshape(h, dm.s, QK_DIM)
    dq_nope, dq1, dq2 = _from_head(dqh, UQ_WIDTHS)
    dq1, dq2 = rope_pair(dq1, dq2, cos_q, -sin_q, name=f"{lname}_d_rope_q")
    dq = jnp.concatenate([dq_nope.astype(BF16), dq1, dq2], axis=1)
    gw["uq"] = mm(sv["qn"], dq, mode="tn", out_dtypes=(BF16,), name=f"{lname}_d_uq")
    dqn = mm(dq, w["uq"], mode="nt", name=f"{lname}_d_qn")
    dk_nope = _from_head(dkh[:, :, :QK_NOPE], (QK_NOPE,))[0]
    dk1, dk2 = rope_pair_headsum(dkh[:, :, QK_NOPE:QK_NOPE + HALF_ROPE], dkh[:, :, QK_NOPE + HALF_ROPE:],
                                 cos_k, sin_k, name=f"{lname}_d_rope_k")
    dkv = jnp.concatenate([dk_nope.astype(BF16), dv], axis=1)
    gw["ukv"] = mm(sv["kvn"], dkv, mode="tn", out_dtypes=(BF16,), name=f"{lname}_d_ukv")
    dkvn = mm(dkv, w["ukv"], mode="nt", name=f"{lname}_d_kvn")
    dc_q, gr["norm_qa"] = rmsnorm_bwd(sv["proj"], g["norm_qa"], dqn, width=dm.q_lora, col=0, out_dtype=BF16,
                                      name=f"{lname}_d_norm_qa")
    dc_kv, gr["norm_kva"] = rmsnorm_bwd(sv["proj"], g["norm_kva"], dkvn, width=dm.kv_lora, col=1, out_dtype=BF16,
                                        name=f"{lname}_d_norm_kva")
    dproj = jnp.concatenate([dc_q, dc_kv, dq_na, dk_na, dv_na, dga, dgb], axis=1)
    dkpe = jnp.concatenate([dk1, dk2, jnp.zeros((dm.s, LANES - QK_ROPE), BF16)], axis=1)
    gw["in_main"] = mm(sv["u"], dproj, mode="tn", out_dtypes=(BF16,), name=f"{lname}_d_in")
    gw["in_kpe"] = mm(sv["u"], dkpe, mode="tn", out_dtypes=(BF16,), name=f"{lname}_d_in_kpe")
    du_k = mm(dkpe, w["in_kpe"], mode="nt", name=f"{lname}_d_u_kpe")
    du = mm(dproj, w["in_main"], mode="nt", epi=lambda r, res: (r + res,), extras=(du_k,), name=f"{lname}_d_u")
    dx, gr["norm_mix"] = rmsnorm_bwd(sv["x"], g["norm_mix"], du, dx1, name=f"{lname}_d_norm_mix")
    return dx, gw, gr, (received[0] if received else None)


def _kernel_weights(dm, full):
    main, kpe = _split_w_in(dm, full["w_in"])
    return dict(in_main=main, in_kpe=kpe,
                uq=jnp.concatenate(_split_heads(full["w_uq"], dm.mla_h, UQ_WIDTHS), axis=1),
                ukv=jnp.concatenate(_split_heads(full["w_ukv"], dm.mla_h, UKV_WIDTHS), axis=1),
                o_mla=full["w_o_mla"], o_na=full["w_o_na"], out=full["w_out"], ff1=full["w_ff1"], ff2=full["w_ff2"])


def _reference_order(dm, gw):
    h = dm.mla_h
    nope_w, half_w = h * QK_NOPE, h * HALF_ROPE
    uq = gw["uq"]
    ukv = gw["ukv"]
    return {"w_in": _join_w_in(dm, gw["in_main"], gw["in_kpe"]),
            "w_uq": _join_heads([uq[:, :nope_w], uq[:, nope_w:nope_w + half_w], uq[:, nope_w + half_w:]], h),
            "w_ukv": _join_heads([ukv[:, :nope_w], ukv[:, nope_w:]], h),
            "w_o_mla": gw["o_mla"], "w_o_na": gw["o_na"], "w_out": gw["out"], "w_ff1": gw["ff1"], "w_ff2": gw["ff2"]}


def _pack_replicated(dm, parts):
    flat = jnp.concatenate([parts[n].reshape(-1) for n in REPLICATED])
    n = flat.shape[0]
    rows_ = -(-n // (8 * LANES)) * 8
    return jnp.pad(flat, (0, rows_ * LANES - n)).reshape(rows_, LANES)


def kernel(x, norm_mix, w_in, norm_qa, w_uq, norm_kva, w_ukv, rpb, w_o_mla, w_o_na, w_out, norm_mlp, w_ff1, w_ff2, norm_final, loss_target, m_norm_mix, m_w_in, m_norm_qa, m_w_uq, m_norm_kva, m_w_ukv, m_rpb, m_w_o_mla, m_w_o_na, m_w_out, m_norm_mlp, m_w_ff1, m_w_ff2, m_norm_final, v_norm_mix, v_w_in, v_norm_qa, v_w_uq, v_norm_kva, v_w_ukv, v_rpb, v_w_o_mla, v_w_o_na, v_w_out, v_norm_mlp, v_w_ff1, v_w_ff2, v_norm_final):
    dm = Dims(x, w_in, norm_qa, norm_kva, rpb, w_o_mla, w_o_na, w_ff1)
    params = dict(norm_mix=norm_mix, w_in=w_in, norm_qa=norm_qa, w_uq=w_uq, norm_kva=norm_kva, w_ukv=w_ukv, rpb=rpb,
                  w_o_mla=w_o_mla, w_o_na=w_o_na, w_out=w_out, norm_mlp=norm_mlp, w_ff1=w_ff1, w_ff2=w_ff2,
                  norm_final=norm_final)
    mom_m = dict(norm_mix=m_norm_mix, w_in=m_w_in, norm_qa=m_norm_qa, w_uq=m_w_uq, norm_kva=m_norm_kva, w_ukv=m_w_ukv,
                 rpb=m_rpb, w_o_mla=m_w_o_mla, w_o_na=m_w_o_na, w_out=m_w_out, norm_mlp=m_norm_mlp, w_ff1=m_w_ff1,
                 w_ff2=m_w_ff2, norm_final=m_norm_final)
    mom_v = dict(norm_mix=v_norm_mix, w_in=v_w_in, norm_qa=v_norm_qa, w_uq=v_w_uq, norm_kva=v_norm_kva, w_ukv=v_w_ukv,
                 rpb=v_rpb, w_o_mla=v_w_o_mla, w_o_na=v_w_o_na, w_out=v_w_out, norm_mlp=v_norm_mlp, w_ff1=v_w_ff1,
                 w_ff2=v_w_ff2, norm_final=v_norm_final)
    depth, s, h = dm.depth, dm.s, dm.mla_h
    shard_shapes = {n: params[n].shape[1:] for n in SHARDED}

    pos = jnp.arange(s, dtype=F32)
    inv_freq = 1.0 / (ROPE_THETA ** (jnp.arange(0, QK_ROPE, 2, dtype=F32) / QK_ROPE))
    ang = pos[:, None] * inv_freq[None, :]
    cos_k, sin_k = jnp.cos(ang), jnp.sin(ang)
    cos_q, sin_q = jnp.tile(cos_k, (1, h)), jnp.tile(sin_k, (1, h))

    flats = [_to_flat({n: params[n][l].astype(BF16) for n in SHARDED}) for l in range(depth)]
    gains = [dict(norm_mix=norm_mix[l][None], norm_qa=norm_qa[l][None], norm_kva=norm_kva[l][None],
                  norm_mlp=norm_mlp[l][None], rpb=rpb[l]) for l in range(depth)]

    xl = x[0]
    saved, weights = [], []
    gathered = all_gather(flats[0], name="gather_weights")
    for l in range(depth):
        got = _from_flat(gathered, shard_shapes)
        weights.append(_kernel_weights(dm, {n: _full(n, got[n]) for n in SHARDED}))
        xl, sv, gathered = layer_fwd(dm, "fwd", xl, weights[l], gains[l], cos_q, sin_q, cos_k, sin_k,
                                     next_flat=flats[l + 1] if l + 1 < depth else None)
        saved.append(sv)
    dx, g_final, loss_part = loss_head(xl, norm_final[None], loss_target[0], name="loss_head")
    loss = lax.psum(loss_part[0, 0], MESH_AXES)

    rep = {n: [None] * depth for n in REPLICATED if n != "norm_final"}
    recv = [None] * depth
    slots = None
    for l in reversed(range(depth)):
        dx, gw, gr, received = layer_bwd(dm, "bwd", dx, weights[l], gains[l], saved[l], cos_q, sin_q, cos_k, sin_k,
                                         slots=slots)
        if received is not None:
            recv[l + 1] = _from_flat(received, shard_shapes)
        for n in gr:
            rep[n][l] = gr[n]
        full = _reference_order(dm, gw)
        slots = jnp.concatenate([_slots(n, full[n]).reshape(N_DEV, -1, LANES) for n in SHARDED], axis=1)
    recv[0] = _from_flat(all_to_all(slots, name="scatter_grads"), shard_shapes)
    rep_parts = {n: jnp.stack(rep[n]) for n in rep}
    rep_parts["norm_final"] = g_final
    rep_all = all_gather(_pack_replicated(dm, rep_parts), name="gather_small_grads")

    outs = {}
    for n in SHARDED:
        r, c = shard_shapes[n]
        g_slots = jnp.stack([recv[l][n] for l in range(depth)], axis=1).reshape(N_DEV, depth * r, c)
        flat2 = lambda a: a.reshape(depth * r, c)
        res = adamw(flat2(params[n]), flat2(mom_m[n]), flat2(mom_v[n]), g_slots, name=f"adamw_{n}")
        outs[n] = [a.reshape(params[n].shape) for a in res]
    n_rep = sum(int(np.prod(params[n].shape)) for n in REPLICATED)
    pack = lambda d: _pack_replicated(dm, d)
    res = adamw(pack(params), pack(mom_m), pack(mom_v), rep_all, name="adamw_replicated")
    off = 0
    for n in REPLICATED:
        size = int(np.prod(params[n].shape))
        outs[n] = [a.reshape(-1)[off:off + size].reshape(params[n].shape) for a in res]
        off += size
    assert off == n_rep

    grad_x = dx[None]
    return (loss, grad_x, *[outs[n][0] for n in WEIGHTS], *[outs[n][1] for n in WEIGHTS],
            *[outs[n][2] for n in WEIGHTS], *[outs[n][3] for n in WEIGHTS])
```

```python
import functools

import numpy as np
import jax
import jax.numpy as jnp
from jax import lax
from jax.experimental import pallas as pl
from jax.experimental.pallas import tpu as pltpu

F32 = jnp.float32
BF16 = jnp.bfloat16
MESH_AXES = ("x", "y", "c")
N_DEV = 8
LANES = 128

QK_NOPE = 128
QK_ROPE = 64
HALF_ROPE = QK_ROPE // 2
V_HEAD = 128
QK_DIM = QK_NOPE + QK_ROPE
NA_HEAD_DIM = 128
GRID_W = 64
NA_KH = 8
NA_KW = 16
NA_SPAN_ROWS = 2 * NA_KH
NA_SPAN = NA_SPAN_ROWS * GRID_W
NA_PAD_LO = (NA_KH - 1) * GRID_W
NA_PAD_HI = NA_KH * GRID_W
ROPE_THETA = 10000.0
EPS = 1e-6
NEG = -1e30

ADAM_LR = 0.001
ADAM_B1 = 0.9
ADAM_B2 = 0.999
ADAM_EPS = 1e-08
ADAM_WD = 0.01
ADAM_STEP = 10

VMEM_LIMIT_V7X = 56 * 1024 * 1024
ADAMW_BLOCK_BYTES = 24 * 1024 * 1024

NN = (((1,), (0,)), ((), ()))
NT = (((1,), (1,)), ((), ()))
TN = (((0,), (0,)), ((), ()))


def _pick(dim, target, align=LANES):
    if dim <= target:
        return dim
    t = (target // align) * align
    while t >= align:
        if dim % t == 0:
            return t
        t -= align
    return dim


def _params(sem):
    return pltpu.CompilerParams(dimension_semantics=sem, vmem_limit_bytes=VMEM_LIMIT_V7X)


def mm(a, b, *, mode, name, out_dtypes=(F32,), epi=None, extras=(), tm=1024, tn=1024, tk=2048, exact=False):
    if mode == "nn":
        (m, k), (k2, n) = a.shape, b.shape
    elif mode == "nt":
        (m, k), (n, k2) = a.shape, b.shape
    else:
        (k, m), (k2, n) = a.shape, b.shape
    assert k == k2, (a.shape, b.shape, mode)
    tm, tn, tk = _pick(m, tm), _pick(n, tn), _pick(k, tk)
    nk = k // tk
    a_spec = pl.BlockSpec((tk, tm), lambda i, j, s: (s, i)) if mode == "tn" else pl.BlockSpec((tm, tk), lambda i, j, s: (i, s))
    b_spec = pl.BlockSpec((tn, tk), lambda i, j, s: (j, s)) if mode == "nt" else pl.BlockSpec((tk, tn), lambda i, j, s: (s, j))
    tile = pl.BlockSpec((tm, tn), lambda i, j, s: (i, j))
    dims = {"nn": NN, "nt": NT, "tn": TN}[mode]
    n_extra, n_out = len(extras), len(out_dtypes)

    def product(a_ref, b_ref):
        if exact:
            return lax.dot_general(a_ref[...], b_ref[...], dims, precision=lax.Precision.HIGHEST,
                                   preferred_element_type=F32)
        return lax.dot_general(a_ref[...].astype(BF16), b_ref[...].astype(BF16), dims, preferred_element_type=F32)

    def finish(r, extra_refs, out_refs):
        res = (r,) if epi is None else epi(r, *[e[...] for e in extra_refs])
        for o, v in zip(out_refs, res):
            o[...] = v.astype(o.dtype)

    def body_one_step(a_ref, b_ref, *rest):
        finish(product(a_ref, b_ref), rest[:n_extra], rest[n_extra:])

    def body(a_ref, b_ref, *rest):
        extra_refs, out_refs, acc = rest[:n_extra], rest[n_extra:n_extra + n_out], rest[-1]
        step = pl.program_id(2)

        @pl.when(step == 0)
        def _():
            acc[...] = product(a_ref, b_ref)

        @pl.when(step > 0)
        def _():
            acc[...] += product(a_ref, b_ref)

        @pl.when(step == nk - 1)
        def _():
            finish(acc[...], extra_refs, out_refs)

    outs = pl.pallas_call(
        body_one_step if nk == 1 else body, name=name, grid=(m // tm, n // tn, nk),
        in_specs=[a_spec, b_spec] + [tile] * n_extra,
        out_specs=[tile] * n_out,
        out_shape=[jax.ShapeDtypeStruct((m, n), d) for d in out_dtypes],
        scratch_shapes=[] if nk == 1 else [pltpu.VMEM((tm, tn), F32)],
        compiler_params=_params(("parallel", "parallel", "arbitrary")),
    )(a, b, *extras)
    return outs[0] if n_out == 1 else outs


def blockwise(fn, ins, outs, *, grid, name, sums=()):
    n_in, n_axes = len(ins), len(grid)

    def body(*refs):
        res = fn(*[r[...] for r in refs[:n_in]])
        first = functools.reduce(jnp.logical_and, [pl.program_id(ax) == 0 for ax in range(n_axes)])
        for idx, (o, v) in enumerate(zip(refs[n_in:], res)):
            if idx in sums:
                @pl.when(first)
                def _(o=o):
                    o[...] = jnp.zeros_like(o)

                o[...] += v.astype(o.dtype)
            else:
                o[...] = v.astype(o.dtype)

    sem = ("arbitrary" if sums else "parallel",) * n_axes
    res = pl.pallas_call(
        body, name=name, grid=grid,
        in_specs=[pl.BlockSpec(blk, imap) for _, blk, imap in ins],
        out_specs=[pl.BlockSpec(blk, imap) for _, _, blk, imap in outs],
        out_shape=[jax.ShapeDtypeStruct(shape, dt) for shape, dt, _, _ in outs],
        compiler_params=_params(sem),
    )(*[a for a, _, _ in ins])
    return res


def rows(arr, tr, width=None, col=0):
    width = arr.shape[1] if width is None else width
    return (arr, (tr, width), lambda i, col=col: (i, col))


def whole(arr):
    return (arr, arr.shape, lambda i: (0, 0))


def out_rows(n_rows, width, dtype, tr):
    return ((n_rows, width), dtype, (tr, width), lambda i: (i, 0))


def out_sum(shape, dtype=F32):
    return (shape, dtype, shape, lambda i: (0, 0))


def _rstd(x):
    return lax.rsqrt(jnp.mean(x * x, axis=-1, keepdims=True) + EPS)


def _colsum(v):
    return jnp.sum(v, axis=0, keepdims=True)


def rmsnorm_fwd(x, g, *, width=None, col=0, tr=256, name):
    n = x.shape[0]
    tr = _pick(n, tr, 8)
    width = x.shape[1] if width is None else width

    def fn(xv, gv):
        return ((xv * _rstd(xv)) * gv,)

    return blockwise(fn, [rows(x, tr, width, col), whole(g)], [out_rows(n, width, BF16, tr)],
                     grid=(n // tr,), name=name)[0]


def rmsnorm_bwd(x, g, dy, res=None, *, width=None, col=0, out_dtype=F32, tr=256, name):
    n = x.shape[0]
    tr = _pick(n, tr, 8)
    width = x.shape[1] if width is None else width

    def fn(xv, gv, dyv, *resv):
        dyv = dyv.astype(F32)
        r = _rstd(xv)
        xhat = xv * r
        dxhat = dyv * gv
        dx = r * (dxhat - xhat * jnp.mean(dxhat * xhat, axis=-1, keepdims=True))
        if resv:
            dx = dx + resv[0]
        return dx, _colsum(dyv * xhat)

    ins = [rows(x, tr, width, col), whole(g), rows(dy, tr)] + ([rows(res, tr)] if res is not None else [])
    return blockwise(fn, ins, [out_rows(n, width, out_dtype, tr), out_sum((1, width))],
                     grid=(n // tr,), name=name, sums=(1,))


def rope_pair(x1, x2, cos, sin, *, tr=512, name):
    n, w = x1.shape
    tr = _pick(n, tr, 8)

    def fn(a, b, c, s):
        a, b = a.astype(F32), b.astype(F32)
        return a * c - b * s, b * c + a * s

    return blockwise(fn, [rows(x1, tr), rows(x2, tr), rows(cos, tr), rows(sin, tr)],
                     [out_rows(n, w, BF16, tr), out_rows(n, w, BF16, tr)], grid=(n // tr,), name=name)


def rope_pair_headsum(d1, d2, cos, sin, *, tr=512, name):
    h, n, w = d1.shape
    tr = _pick(n, tr, 8)

    def fn(a, b, c, s):
        a, b = jnp.sum(a.astype(F32), axis=0), jnp.sum(b.astype(F32), axis=0)
        return a * c + b * s, b * c - a * s

    lead = lambda arr: (arr, (h, tr, w), lambda i: (0, i, 0))
    return blockwise(fn, [lead(d1), lead(d2), rows(cos, tr), rows(sin, tr)],
                     [out_rows(n, w, BF16, tr), out_rows(n, w, BF16, tr)], grid=(n // tr,), name=name)


def gate_fwd(proj, y_a, y_b, *, d, ga_col, gb_col, tr=256, name):
    n = proj.shape[0]
    tr = _pick(n, tr, 8)

    def fn(ga, gb, ya, yb):
        return (jax.nn.sigmoid(ga) * ya + jax.nn.sigmoid(gb) * yb,)

    return blockwise(fn, [rows(proj, tr, d, ga_col), rows(proj, tr, d, gb_col), rows(y_a, tr), rows(y_b, tr)],
                     [out_rows(n, d, BF16, tr)], grid=(n // tr,), name=name)[0]


def gate_bwd(proj, y_a, y_b, dmerged, *, d, ga_col, gb_col, tr=256, name):
    n = proj.shape[0]
    tr = _pick(n, tr, 8)

    def fn(ga, gb, ya, yb, dm):
        sa, sb = jax.nn.sigmoid(ga), jax.nn.sigmoid(gb)
        return dm * sa, dm * sb, dm * ya * (sa * (1.0 - sa)), dm * yb * (sb * (1.0 - sb))

    return blockwise(fn, [rows(proj, tr, d, ga_col), rows(proj, tr, d, gb_col), rows(y_a, tr), rows(y_b, tr),
                          rows(dmerged, tr)],
                     [out_rows(n, d, BF16, tr)] * 4, grid=(n // tr,), name=name)


def loss_head(x, g, target, *, tr=256, name):
    n, d = x.shape
    tr = _pick(n, tr, 8)

    def fn(xv, gv, tv):
        r = _rstd(xv)
        xhat = xv * r
        diff = xhat * gv - tv
        loss = 0.5 * jnp.sum(jnp.sum(diff * diff, axis=-1, keepdims=True) / d, axis=0, keepdims=True)
        dy = diff / d
        dxhat = dy * gv
        dx = r * (dxhat - xhat * jnp.mean(dxhat * xhat, axis=-1, keepdims=True))
        return dx, _colsum(dy * xhat), jnp.broadcast_to(loss, (8, LANES))

    return blockwise(fn, [rows(x, tr), whole(g), rows(target, tr)],
                     [out_rows(n, d, F32, tr), out_sum((1, d)), out_sum((8, LANES))],
                     grid=(n // tr,), name=name, sums=(1, 2))


def _adamw_math(wv, mv, vv, gs):
    c1 = 1.0 / (1.0 - ADAM_B1 ** ADAM_STEP)
    c2 = 1.0 / (1.0 - ADAM_B2 ** ADAM_STEP)
    g = gs[0].astype(F32)
    for dev in range(1, N_DEV):
        g = g + gs[dev].astype(F32)
    m_new = ADAM_B1 * mv + (1.0 - ADAM_B1) * g
    v_new = ADAM_B2 * vv + (1.0 - ADAM_B2) * (g * g)
    delta = -ADAM_LR * ((m_new * c1) / (jnp.sqrt(v_new * c2) + ADAM_EPS) + ADAM_WD * wv)
    return g, delta, m_new, v_new


def adamw(w, m, v, g_slots, *, tr=256, name):
    n, c = w.shape
    tr = _pick(n, tr, 8)
    slots = (g_slots, (N_DEV, tr, c), lambda i: (0, i, 0))
    return blockwise(_adamw_math, [rows(w, tr), rows(m, tr), rows(v, tr), slots],
                     [out_rows(n, c, F32, tr)] * 4, grid=(n // tr,), name=name)


def adamw_layers(w, m, v, g_layers, *, name):
    depth, r, c = w.shape
    row_bytes = c * 2 * (depth * N_DEV * g_layers[0].dtype.itemsize + 7 * 4)
    tr = _pick(r, max(8, ADAMW_BLOCK_BYTES // row_bytes // 8 * 8), 8)

    def body(w_ref, m_ref, v_ref, *rest):
        g_refs, out_refs = rest[:depth], rest[depth:]
        for layer in range(depth):
            @pl.when(pl.program_id(0) == layer)
            def _(layer=layer):
                for o, val in zip(out_refs, _adamw_math(w_ref[...], m_ref[...], v_ref[...], g_refs[layer][...])):
                    o[...] = val

    blk = pl.BlockSpec((None, tr, c), lambda l, i: (l, i, 0))
    g_specs = [pl.BlockSpec((N_DEV, tr, c), lambda l, i, layer=layer: (0, jnp.where(l == layer, i, 0), 0))
               for layer in range(depth)]
    return pl.pallas_call(
        body, name=name, grid=(depth, r // tr), in_specs=[blk] * 3 + g_specs, out_specs=[blk] * 4,
        out_shape=[jax.ShapeDtypeStruct(w.shape, F32)] * 4, compiler_params=_params(("arbitrary", "arbitrary")),
    )(w, m, v, *g_layers)


LOG2E = 1.4426950408889634
MLA_SUB = 512
MLA_FWD_BQ, MLA_BWD_BQ, MLA_BK = 1024, 1024, 1024


def mla_fwd(q, k, vt, *, name, ride=None):
    h, s, dq = q.shape
    bq, bk = _pick(s, MLA_FWD_BQ), _pick(s, MLA_BK)
    sub = min(MLA_SUB, bq)
    nk = s // bk
    scale = QK_DIM ** -0.5
    c2 = scale * LOG2E
    grid = (h, s // bq, nk)

    def body(q_ref, k_ref, vt_ref, *rest):
        if ride is None:
            o_ref, lse_ref, m_s, l_s, acc = rest
        else:
            n = ride.n
            x_refs, (o_ref, lse_ref), g_refs = rest[:n], rest[n:n + 2], rest[n + 2:2 * n + 2]
            m_s, l_s, acc, *sems = rest[2 * n + 2:]
            first, middle, last = _grid_flags(grid)
            ride_start, ride_middle, ride_finish = ride.steps(x_refs, g_refs, sems)
            pl.when(first)(ride_start)
            pl.when(middle)(ride_middle)
        j = pl.program_id(2)

        @pl.when(j == 0)
        def _():
            m_s[...] = jnp.full_like(m_s, NEG)
            l_s[...] = jnp.zeros_like(l_s)
            acc[...] = jnp.zeros_like(acc)

        def scores(c0):
            return lax.dot_general(k_ref[...], q_ref[pl.ds(c0, sub), :], NT, preferred_element_type=F32)

        starts = list(range(0, bq, sub))
        st_next = scores(starts[0])
        for n, c0 in enumerate(starts):
            cols = pl.ds(c0, sub)
            st = st_next
            if n + 1 < len(starts):
                st_next = scores(starts[n + 1])
            m_prev = m_s[:, cols]
            m_new = jnp.maximum(m_prev, jnp.max(st, axis=0, keepdims=True))
            alpha = jnp.exp2((m_prev - m_new) * c2)
            pt = jnp.exp2((st - m_new) * c2)
            l_s[:, cols] = alpha * l_s[:, cols] + jnp.sum(pt, axis=0, keepdims=True)
            acc[:, cols] = alpha * acc[:, cols] + lax.dot_general(vt_ref[...], pt.astype(BF16), NN,
                                                                  preferred_element_type=F32)
            m_s[:, cols] = m_new

        @pl.when(j == nk - 1)
        def _():
            o_ref[...] = (acc[...] / l_s[...]).T
            lse_ref[...] = m_s[...] * scale + jnp.log(l_s[...])

        if ride is not None:
            pl.when(last)(ride_finish)

    in_specs = [pl.BlockSpec((None, bq, dq), lambda hh, i, j: (hh, i, 0)),
                pl.BlockSpec((None, bk, dq), lambda hh, i, j: (hh, j, 0)),
                pl.BlockSpec((None, V_HEAD, bk), lambda hh, i, j: (hh, 0, j))]
    out_specs = [pl.BlockSpec((bq, V_HEAD), lambda hh, i, j: (i, hh)),
                 pl.BlockSpec((None, 1, bq), lambda hh, i, j: (hh, 0, i))]
    out_shape = [jax.ShapeDtypeStruct((s, h * V_HEAD), F32), jax.ShapeDtypeStruct((h, 1, s), F32)]
    scratch = [pltpu.VMEM((1, bq), F32), pltpu.VMEM((1, bq), F32), pltpu.VMEM((V_HEAD, bq), F32)]
    args = (q, k, vt)
    sem = ("parallel", "parallel", "arbitrary")
    if ride is not None:
        in_specs, out_specs, out_shape, scratch, args = ride.extend(in_specs, out_specs, out_shape, scratch, args)
        sem = ("arbitrary",) * 3
    return pl.pallas_call(body, name=name, grid=grid, in_specs=in_specs, out_specs=out_specs, out_shape=out_shape,
                          scratch_shapes=scratch, compiler_params=_params(sem))(*args)


def mla_delta(do, o, h, *, tr=512, name):
    s = do.shape[0]
    tr = _pick(s, tr, 8)

    def fn(dov, ov):
        return (jnp.sum(dov.astype(F32) * ov, axis=-1, keepdims=True),)

    blk = lambda arr: (arr, (tr, V_HEAD), lambda hh, i: (i, hh))
    return blockwise(fn, [blk(do), blk(o)], [((h, s, 1), F32, (None, tr, 1), lambda hh, i: (hh, i, 0))],
                     grid=(h, s // tr), name=name)[0]


def mla_bwd(q, k, kt, kv, do, lse, delta, *, v_col0, name, ride=None):
    h, s, dq = q.shape
    bq, bk = _pick(s, MLA_BWD_BQ), _pick(s, MLA_BK)
    sub = min(MLA_SUB, bq)
    nq = s // bq
    scale = QK_DIM ** -0.5
    c2 = scale * LOG2E
    grid = (h, s // bk, nq)

    def body(q_ref, k_ref, kt_ref, v_ref, do_ref, lse_ref, delta_ref, *rest):
        if ride is None:
            dq_ref, dk_ref, dv_ref, dk_acc, dv_acc = rest
        else:
            n = ride.n
            x_refs, (dq_ref, dk_ref, dv_ref), g_refs = rest[:n], rest[n:n + 3], rest[n + 3:2 * n + 3]
            dk_acc, dv_acc, *sems = rest[2 * n + 3:]
            first, _, last = _grid_flags(grid)
            ride_start, _, ride_finish = ride.steps(x_refs, g_refs, sems)
            pl.when(first)(ride_start)
        j, i = pl.program_id(1), pl.program_id(2)

        @pl.when(i == 0)
        def _():
            dk_acc[...] = jnp.zeros_like(dk_acc)
            dv_acc[...] = jnp.zeros_like(dv_acc)

        def scores(c0):
            cols = pl.ds(c0, sub)
            return (lax.dot_general(k_ref[...], q_ref[cols, :], NT, preferred_element_type=F32),
                    lax.dot_general(v_ref[...], do_ref[cols, :], NT, preferred_element_type=F32))

        starts = list(range(0, bq, sub))
        nxt = scores(starts[0])
        for n, c0 in enumerate(starts):
            cols = pl.ds(c0, sub)
            st, dpt = nxt
            if n + 1 < len(starts):
                nxt = scores(starts[n + 1])
            q_sub, do_sub = q_ref[cols, :], do_ref[cols, :]
            pt = jnp.exp2(st * c2 - lse_ref[:, cols] * LOG2E)
            ds_b = ((pt * (dpt - delta_ref[:, cols])) * scale).astype(BF16)
            dv_acc[...] += lax.dot_general(pt.astype(BF16), do_sub, NN, preferred_element_type=F32)
            dk_acc[...] += lax.dot_general(ds_b, q_sub, NN, preferred_element_type=F32)
            dq_t = lax.dot_general(kt_ref[...], ds_b, NN, preferred_element_type=F32)

            @pl.when(j == 0)
            def _():
                dq_ref[i, :, cols] = dq_t

            @pl.when(j > 0)
            def _():
                dq_ref[i, :, cols] += dq_t

        @pl.when(i == nq - 1)
        def _():
            dk_ref[...] = dk_acc[...]
            dv_ref[...] = dv_acc[...].astype(dv_ref.dtype)

        if ride is not None:
            pl.when(last)(ride_finish)

    in_specs = [pl.BlockSpec((None, bq, dq), lambda hh, j, i: (hh, i, 0)),
                pl.BlockSpec((None, bk, dq), lambda hh, j, i: (hh, j, 0)),
                pl.BlockSpec((None, dq, bk), lambda hh, j, i: (hh, 0, j)),
                pl.BlockSpec((bk, V_HEAD), lambda hh, j, i: (j, v_col0 + hh)),
                pl.BlockSpec((bq, V_HEAD), lambda hh, j, i: (i, hh)),
                pl.BlockSpec((None, 1, bq), lambda hh, j, i: (hh, 0, i)),
                pl.BlockSpec((None, 1, bq), lambda hh, j, i: (hh, 0, i))]
    out_specs = [pl.BlockSpec((None, nq, dq, bq), lambda hh, j, i: (hh, 0, 0, 0)),
                 pl.BlockSpec((None, bk, dq), lambda hh, j, i: (hh, j, 0)),
                 pl.BlockSpec((bk, V_HEAD), lambda hh, j, i: (j, hh))]
    out_shape = [jax.ShapeDtypeStruct((h, nq, dq, bq), F32), jax.ShapeDtypeStruct((h, s, dq), F32),
                 jax.ShapeDtypeStruct((s, h * V_HEAD), BF16)]
    scratch = [pltpu.VMEM((bk, dq), F32), pltpu.VMEM((bk, V_HEAD), F32)]
    args = (q, k, kt, kv, do, lse, delta)
    sem = ("parallel", "arbitrary", "arbitrary")
    if ride is not None:
        in_specs, out_specs, out_shape, scratch, args = ride.extend(in_specs, out_specs, out_shape, scratch, args)
        sem = ("arbitrary",) * 3
    return pl.pallas_call(body, name=name, grid=grid, in_specs=in_specs, out_specs=out_specs, out_shape=out_shape,
                          scratch_shapes=scratch, compiler_params=_params(sem))(*args)


def _na_probs(sc, bias, r, n_rows):
    sc = sc * (NA_HEAD_DIM ** -0.5) + bias
    lo = jnp.clip(r - NA_KH // 2, 0, n_rows - NA_KH) - r + (NA_KH - 1)
    dy = lax.broadcasted_iota(jnp.int32, (1, NA_SPAN), 1) // GRID_W
    sc = jnp.where((dy >= lo) & (dy < lo + NA_KH), sc, NEG)
    p = jnp.exp(sc - jnp.max(sc, axis=-1, keepdims=True))
    return p / jnp.sum(p, axis=-1, keepdims=True)


def na_pad_cast(proj, *, col0, width, name):
    s = proj.shape[0]
    n_in, lo = s // GRID_W, NA_PAD_LO // GRID_W
    assert col0 % width == 0

    def body(x_ref, o_ref):
        i = pl.program_id(0)
        inside = jnp.logical_and(i >= lo, i < lo + n_in)
        o_ref[...] = jnp.where(inside, x_ref[...], 0.0).astype(o_ref.dtype)

    return pl.pallas_call(
        body, name=name, grid=(n_in + NA_SPAN_ROWS - 1,),
        in_specs=[pl.BlockSpec((GRID_W, width), lambda i: (jnp.clip(i - lo, 0, n_in - 1), col0 // width))],
        out_specs=pl.BlockSpec((GRID_W, width), lambda i: (i, 0)),
        out_shape=jax.ShapeDtypeStruct((s + NA_PAD_LO + NA_PAD_HI, width), BF16),
        compiler_params=_params(("parallel",)),
    )(proj)


def na_fwd(proj, kp, vp, biasm, *, q_col0, v_col0=0, rows_per_step=8, name):
    s = proj.shape[0]
    sp = kp.shape[0]
    h = biasm.shape[0]
    hw = h * NA_HEAD_DIM
    n_rows = s // GRID_W
    rb = min(rows_per_step, n_rows)
    tq = rb * GRID_W

    def body(q_ref, k_ref, v_ref, b_ref, o_ref):
        i = pl.program_id(1)
        bias = b_ref[...]

        def span(rl):
            return pl.ds(pl.multiple_of((i * rb + rl) * GRID_W, GRID_W), NA_SPAN)

        def scores(rl):
            q_row = q_ref[pl.ds(rl * GRID_W, GRID_W), :].astype(BF16)
            return lax.dot_general(q_row, k_ref[span(rl), :], NT, preferred_element_type=F32)

        sc_next = scores(0)
        for rl in range(rb):
            sc = sc_next
            if rl + 1 < rb:
                sc_next = scores(rl + 1)
            p = _na_probs(sc, bias, i * rb + rl, n_rows)
            o_ref[pl.ds(rl * GRID_W, GRID_W), :] = lax.dot_general(
                p.astype(BF16), v_ref[span(rl), :], NN, preferred_element_type=F32)

    return pl.pallas_call(
        body, name=name, grid=(h, n_rows // rb),
        in_specs=[pl.BlockSpec((tq, NA_HEAD_DIM), lambda hh, i: (i, q_col0 + hh)),
                  pl.BlockSpec((sp, NA_HEAD_DIM), lambda hh, i: (0, hh)),
                  pl.BlockSpec((sp, NA_HEAD_DIM), lambda hh, i: (0, v_col0 + hh)),
                  pl.BlockSpec((None, GRID_W, NA_SPAN), lambda hh, i: (hh, 0, 0))],
        out_specs=pl.BlockSpec((tq, NA_HEAD_DIM), lambda hh, i: (i, hh)),
        out_shape=jax.ShapeDtypeStruct((s, hw), F32),
        compiler_params=_params(("parallel", "arbitrary")),
    )(proj, kp, vp, biasm)


def na_bwd(proj, kp, vp, biasm, do, *, q_col0, v_col0=0, rows_per_step=8, name):
    s = proj.shape[0]
    sp = kp.shape[0]
    h = biasm.shape[0]
    hw = h * NA_HEAD_DIM
    n_rows = s // GRID_W
    rb = min(rows_per_step, n_rows)
    tq = rb * GRID_W
    scale = NA_HEAD_DIM ** -0.5

    def body(q_ref, k_ref, v_ref, b_ref, do_ref, dq_ref, dk_ref, dv_ref, db_ref):
        i = pl.program_id(1)

        @pl.when(i == 0)
        def _():
            dk_ref[...] = jnp.zeros_like(dk_ref)
            dv_ref[...] = jnp.zeros_like(dv_ref)
            db_ref[...] = jnp.zeros_like(db_ref)

        bias = b_ref[...]

        def span(rl):
            return pl.ds(pl.multiple_of((i * rb + rl) * GRID_W, GRID_W), NA_SPAN)

        def query(rl):
            rows_ = pl.ds(rl * GRID_W, GRID_W)
            return q_ref[rows_, :].astype(BF16), do_ref[rows_, :]

        def scores(rl):
            q_row, do_row = query(rl)
            return (lax.dot_general(q_row, k_ref[span(rl), :], NT, preferred_element_type=F32),
                    lax.dot_general(do_row, v_ref[span(rl), :], NT, preferred_element_type=F32))

        nxt = scores(0)
        for rl in range(rb):
            sc, dp = nxt
            if rl + 1 < rb:
                nxt = scores(rl + 1)
            q_row, do_row = query(rl)
            p = _na_probs(sc, bias, i * rb + rl, n_rows)
            ds = p * (dp - jnp.sum(dp * p, axis=-1, keepdims=True))
            db_ref[...] += ds
            ds_b = (ds * scale).astype(BF16)
            dq_ref[pl.ds(rl * GRID_W, GRID_W), :] = lax.dot_general(
                ds_b, k_ref[span(rl), :], NN, preferred_element_type=F32).astype(dq_ref.dtype)
            dk_ref[span(rl), :] += lax.dot_general(ds_b, q_row, TN, preferred_element_type=F32)
            dv_ref[span(rl), :] += lax.dot_general(p.astype(BF16), do_row, TN, preferred_element_type=F32)

    return pl.pallas_call(
        body, name=name, grid=(h, n_rows // rb),
        in_specs=[pl.BlockSpec((tq, NA_HEAD_DIM), lambda hh, i: (i, q_col0 + hh)),
                  pl.BlockSpec((sp, NA_HEAD_DIM), lambda hh, i: (0, hh)),
                  pl.BlockSpec((sp, NA_HEAD_DIM), lambda hh, i: (0, v_col0 + hh)),
                  pl.BlockSpec((None, GRID_W, NA_SPAN), lambda hh, i: (hh, 0, 0)),
                  pl.BlockSpec((tq, NA_HEAD_DIM), lambda hh, i: (i, hh))],
        out_specs=[pl.BlockSpec((tq, NA_HEAD_DIM), lambda hh, i: (i, hh)),
                   pl.BlockSpec((sp, NA_HEAD_DIM), lambda hh, i: (0, hh)),
                   pl.BlockSpec((sp, NA_HEAD_DIM), lambda hh, i: (0, hh)),
                   pl.BlockSpec((None, GRID_W, NA_SPAN), lambda hh, i: (hh, 0, 0))],
        out_shape=[jax.ShapeDtypeStruct((s, hw), BF16), jax.ShapeDtypeStruct((sp, hw), F32),
                   jax.ShapeDtypeStruct((sp, hw), F32), jax.ShapeDtypeStruct((h, GRID_W, NA_SPAN), F32)],
        compiler_params=_params(("parallel", "arbitrary")),
    )(proj, kp, vp, biasm, do)


def _na_tables():
    qc = np.arange(GRID_W)[:, None]
    kc = np.arange(GRID_W)[None, :]
    col_start = np.clip(qc - NA_KW // 2, 0, GRID_W - NA_KW)
    col_ok = (kc >= col_start) & (kc < col_start + NA_KW)
    dx = np.clip(kc - qc, -(NA_KW - 1), NA_KW - 1) + (NA_KW - 1)
    return col_ok, dx


def na_bias_table(rpb_l):
    col_ok, dx = _na_tables()
    t = rpb_l[:, :, dx]
    t = jnp.where(col_ok[None, None], t, NEG)
    t = jnp.pad(t, ((0, 0), (0, 1), (0, 0), (0, 0)), constant_values=NEG)
    return t.transpose(0, 2, 1, 3).reshape(rpb_l.shape[0], GRID_W, NA_SPAN)


def na_bias_grad(dbias, *, name):
    h = dbias.shape[0]
    _, dx = _na_tables()
    n_dx = 2 * NA_KW - 1
    onehot = np.zeros((GRID_W * GRID_W, LANES), np.float32)
    onehot[np.arange(GRID_W * GRID_W), dx.reshape(-1)] = 1.0
    t = dbias.reshape(h, GRID_W, NA_SPAN_ROWS, GRID_W)[:, :, :2 * NA_KH - 1]
    t = t.transpose(0, 2, 1, 3).reshape(h * (2 * NA_KH - 1), GRID_W * GRID_W)
    t = jnp.pad(t, ((0, (-t.shape[0]) % 8), (0, 0)))
    out = mm(t, jnp.asarray(onehot), mode="nn", name=name, exact=True, tk=1024)
    return out[:h * (2 * NA_KH - 1), :n_dx].reshape(h, 2 * NA_KH - 1, n_dx)


def _flip(v, bit):
    return 1 - v if bit else v


class Part:
    def __init__(self, shape, kind, dtype):
        self.r, self.c = shape
        self.kind, self.dtype = kind, dtype

    @property
    def whole_shape(self):
        return {"row": (N_DEV * self.r, self.c), "col": (self.r, N_DEV * self.c),
                "packed": (N_DEV, self.r, self.c)}[self.kind]

    @property
    def packed_shape(self):
        return (N_DEV, self.r, self.c)

    def shard_of(self, ref, j):
        if self.kind == "row":
            return ref.at[pl.ds(pl.multiple_of(j * self.r, 8), self.r), :]
        if self.kind == "col":
            return ref.at[:, pl.ds(pl.multiple_of(j * self.c, LANES), self.c)]
        return ref.at[j]


def comm_scratch(n_parts):
    n = n_parts * (N_DEV - 1)
    return [pltpu.SemaphoreType.DMA((n,)), pltpu.SemaphoreType.DMA((n,)), pltpu.SemaphoreType.DMA((n_parts,))]


def gather_plan(parts, x_refs, out_refs, send_sems, recv_sems, local_sems):
    x, y, c = lax.axis_index("x"), lax.axis_index("y"), lax.axis_index("c")
    me, sibling = (x, y, c), (x, y, 1 - c)
    chips = [(1 - x, y), (x, 1 - y), (1 - x, 1 - y)]

    def place(w, px, py, pc):
        return parts[w].shard_of(out_refs[w], 4 * px + 2 * py + pc)

    def copy(k, blk, to, own=False):
        return [pltpu.make_async_remote_copy(
            src_ref=x_refs[w] if own else place(w, *blk), dst_ref=place(w, *blk),
            send_sem=send_sems.at[w * (N_DEV - 1) + k], recv_sem=recv_sems.at[w * (N_DEV - 1) + k],
            device_id=to, device_id_type=pl.DeviceIdType.MESH) for w in range(len(parts))]

    def mine():
        return [pltpu.make_async_copy(x_refs[w], place(w, *me), local_sems.at[w]) for w in range(len(parts))]

    def first():
        return copy(0, me, sibling, own=True) + [cp for j, chip in enumerate(chips)
                                                 for cp in copy(1 + j, me, (*chip, c), own=True)]

    def passed(j):
        return copy(4 + j, (*chips[j], c), sibling)

    def start():
        for cp in mine() + first():
            cp.start()

    def forward():
        for j, chip in enumerate(chips):
            for cp in copy(1 + j, (*chip, c), me):
                cp.wait_recv()
            for cp in passed(j):
                cp.start()

    def finish():
        for cp in copy(0, sibling, me):
            cp.wait_recv()
        for j, chip in enumerate(chips):
            for cp in copy(4 + j, (*chip, 1 - c), me):
                cp.wait_recv()
        for cp in first() + [cp for j in range(len(chips)) for cp in passed(j)]:
            cp.wait_send()
        for cp in mine():
            cp.wait()

    return start, forward, finish


def exchange_plan(parts, x_refs, out_refs, send_sems, recv_sems, local_sems):
    x, y, c = lax.axis_index("x"), lax.axis_index("y"), lax.axis_index("c")
    me = 4 * x + 2 * y + c

    def peer_of(k):
        peer = (_flip(x, k & 4), _flip(y, k & 2), _flip(c, k & 1))
        return peer, 4 * peer[0] + 2 * peer[1] + peer[2]

    def copies(k, arriving):
        peer, theirs = peer_of(k)
        return [pltpu.make_async_remote_copy(
            src_ref=parts[w].shard_of(x_refs[w], me if arriving else theirs),
            dst_ref=out_refs[w].at[theirs if arriving else me],
            send_sem=send_sems.at[w * (N_DEV - 1) + k - 1], recv_sem=recv_sems.at[w * (N_DEV - 1) + k - 1],
            device_id=peer, device_id_type=pl.DeviceIdType.MESH) for w in range(len(parts))]

    def mine():
        return [pltpu.make_async_copy(parts[w].shard_of(x_refs[w], me), out_refs[w].at[me], local_sems.at[w])
                for w in range(len(parts))]

    def start():
        for cp in mine():
            cp.start()
        for k in range(1, N_DEV):
            for cp in copies(k, arriving=False):
                cp.start()

    def finish():
        for k in range(1, N_DEV):
            for cp in copies(k, arriving=True):
                cp.wait_recv()
        for k in range(1, N_DEV):
            for cp in copies(k, arriving=False):
                cp.wait_send()
        for cp in mine():
            cp.wait()

    return start, finish


class Collective:
    def __init__(self, kind, parts, arrays):
        self.kind, self.parts, self.arrays = kind, parts, list(arrays)
        self.n = len(parts)
        shapes = [p.whole_shape if kind == "gather" else p.packed_shape for p in parts]
        self.out_shape = [jax.ShapeDtypeStruct(s, p.dtype) for s, p in zip(shapes, parts)]

    def extend(self, in_specs, out_specs, out_shape, scratch, args):
        any_spec = pl.BlockSpec(memory_space=pl.ANY)
        return (in_specs + [any_spec] * self.n, out_specs + [any_spec] * self.n, out_shape + self.out_shape,
                scratch + comm_scratch(self.n), args + tuple(self.arrays))

    def steps(self, x_refs, out_refs, sems):
        plan = (gather_plan if self.kind == "gather" else exchange_plan)(self.parts, x_refs, out_refs, *sems)
        return plan[0], (plan[1] if len(plan) == 3 else None), plan[-1]

    def run(self, name):
        n = self.n

        def body(*refs):
            for step in self.steps(refs[:n], refs[n:2 * n], refs[2 * n:]):
                if step is not None:
                    step()

        in_specs, out_specs, out_shape, scratch, args = self.extend([], [], [], [], ())
        return pl.pallas_call(body, name=name, in_specs=in_specs, out_specs=out_specs, out_shape=out_shape,
                              scratch_shapes=scratch)(*args)


def _grid_flags(grid):
    ids = [pl.program_id(ax) for ax in range(len(grid))]
    inner_zero = functools.reduce(jnp.logical_and, [i == 0 for i in ids[1:]])
    first = jnp.logical_and(ids[0] == 0, inner_zero)
    middle = jnp.logical_and(ids[0] == grid[0] // 2, inner_zero)
    last = functools.reduce(jnp.logical_and, [i == g - 1 for i, g in zip(ids, grid)])
    return first, middle, last


SHARDED = ("w_in", "w_uq", "w_ukv", "w_o_mla", "w_o_na", "w_out", "w_ff1", "w_ff2")
ROW_SHARDED = ("w_out", "w_ff2")
REPLICATED = ("norm_mix", "norm_qa", "norm_kva", "rpb", "norm_mlp", "norm_final")
WEIGHTS = ("norm_mix", "w_in", "norm_qa", "w_uq", "norm_kva", "w_ukv", "rpb", "w_o_mla", "w_o_na", "w_out",
           "norm_mlp", "w_ff1", "w_ff2", "norm_final")


def part_of(name, shard_shape):
    r, c = shard_shape
    kind = "row" if name in ROW_SHARDED else ("col" if c % LANES == 0 else "packed")
    return Part((r, c), kind, BF16)


def _whole(part, gathered):
    return gathered.transpose(1, 0, 2).reshape(part.r, -1) if part.kind == "packed" else gathered


def _for_exchange(part, full):
    return full.reshape(part.r, N_DEV, part.c).transpose(1, 0, 2) if part.kind == "packed" else full


class Dims:
    def __init__(self, x, w_in, norm_qa, norm_kva, rpb, w_o_mla, w_o_na, w_ff1):
        self.s, self.d = x.shape[1], x.shape[2]
        self.depth = w_in.shape[0]
        self.q_lora, self.kv_lora = norm_qa.shape[1], norm_kva.shape[1]
        self.mla_w, self.na_w = w_o_mla.shape[1], w_o_na.shape[1]
        self.mla_h, self.na_h = self.mla_w // V_HEAD, self.na_w // NA_HEAD_DIM
        self.d_ff = w_ff1.shape[2] * N_DEV
        assert rpb.shape[1] == self.na_h and self.s % GRID_W == 0 and self.s // GRID_W >= NA_SPAN_ROWS
        self.in_lo = self.q_lora + self.kv_lora
        self.main_w = self.in_lo + 3 * self.na_w + 2 * self.d
        assert self.q_lora == self.kv_lora and self.in_lo % self.na_w == 0 and self.in_lo % LANES == 0
        assert (self.in_lo + 3 * self.na_w) % self.d == 0
        self.q_col0 = self.in_lo // NA_HEAD_DIM
        self.k_off = self.in_lo + self.na_w
        self.v_off = self.in_lo + 2 * self.na_w
        self.ga_col = (self.in_lo + 3 * self.na_w) // self.d
        self.gb_col = self.ga_col + 1


def _split_w_in(dm, w):
    lo = dm.in_lo
    main = jnp.concatenate([w[:, :lo], w[:, lo + QK_ROPE:]], axis=1)
    kpe = jnp.pad(w[:, lo:lo + QK_ROPE], ((0, 0), (0, LANES - QK_ROPE)))
    return main, kpe


def _join_w_in(dm, main, kpe):
    lo = dm.in_lo
    return jnp.concatenate([main[:, :lo], kpe[:, :QK_ROPE], main[:, lo:]], axis=1)


def _split_heads(w, h, widths):
    r = w.shape[0]
    w3 = w.reshape(r, h, sum(widths))
    out, o = [], 0
    for wd in widths:
        out.append(w3[:, :, o:o + wd].reshape(r, h * wd))
        o += wd
    return out


def _join_heads(parts, h):
    r = parts[0].shape[0]
    return jnp.concatenate([p.reshape(r, h, -1) for p in parts], axis=2).reshape(r, -1)


UQ_WIDTHS = (QK_NOPE, HALF_ROPE, HALF_ROPE)
UKV_WIDTHS = (QK_NOPE, V_HEAD)


def _by_head(parts, h):
    s = parts[0].shape[0]
    return jnp.concatenate([p.reshape(s, h, -1) for p in parts], axis=2).transpose(1, 0, 2)


def _from_head(t, widths):
    h, s, _ = t.shape
    t = t.transpose(1, 0, 2)
    out, o = [], 0
    for wd in widths:
        out.append(t[:, :, o:o + wd].reshape(s, h * wd))
        o += wd
    return out


def layer_fwd(dm, lname, x, w, g, cos_q, sin_q, cos_k, sin_k, ride=None):
    s, h = dm.s, dm.mla_h
    u = rmsnorm_fwd(x, g["norm_mix"], name=f"{lname}_norm_mix")
    proj = mm(u, w["in_main"], mode="nn", name=f"{lname}_proj")
    kpe = mm(u, w["in_kpe"], mode="nn", name=f"{lname}_proj_kpe")
    qn = rmsnorm_fwd(proj, g["norm_qa"], width=dm.q_lora, col=0, name=f"{lname}_norm_qa")
    kvn = rmsnorm_fwd(proj, g["norm_kva"], width=dm.kv_lora, col=1, name=f"{lname}_norm_kva")
    q = mm(qn, w["uq"], mode="nn", name=f"{lname}_uq")
    kv = mm(kvn, w["ukv"], mode="nn", out_dtypes=(BF16,), name=f"{lname}_ukv")
    nope_w, half_w = h * QK_NOPE, h * HALF_ROPE
    q1, q2 = rope_pair(q[:, nope_w:nope_w + half_w], q[:, nope_w + half_w:], cos_q, sin_q, name=f"{lname}_rope_q")
    k1, k2 = rope_pair(kpe[:, :HALF_ROPE], kpe[:, HALF_ROPE:QK_ROPE], cos_k, sin_k, name=f"{lname}_rope_k")
    qh = _by_head([q[:, :nope_w].astype(BF16), q1, q2], h)
    kh = _by_head([kv[:, :nope_w], jnp.tile(k1, (1, h)), jnp.tile(k2, (1, h))], h)
    vt = kv[:, nope_w:].reshape(s, h, V_HEAD).transpose(1, 2, 0)
    o_a, lse, *gathered = mla_fwd(qh, kh, vt, name=f"{lname}_mla", ride=ride)
    y_a = mm(o_a, w["o_mla"], mode="nn", name=f"{lname}_o_mla")

    kvp = na_pad_cast(proj, col0=dm.k_off, width=2 * dm.na_w, name=f"{lname}_na_pad")
    biasm = na_bias_table(g["rpb"])
    o_b = na_fwd(proj, kvp, kvp, biasm, q_col0=dm.q_col0, v_col0=dm.na_h, name=f"{lname}_na")
    y_b = mm(o_b, w["o_na"], mode="nn", name=f"{lname}_o_na")

    merged = gate_fwd(proj, y_a, y_b, d=dm.d, ga_col=dm.ga_col, gb_col=dm.gb_col, name=f"{lname}_gate")
    x1 = mm(merged, w["out"], mode="nn", epi=lambda r, res: (r + res,), extras=(x,), name=f"{lname}_out")
    u2 = rmsnorm_fwd(x1, g["norm_mlp"], name=f"{lname}_norm_mlp")
    hid, act = mm(u2, w["ff1"], mode="nn", out_dtypes=(F32, BF16),
                  epi=lambda r: (r, jnp.square(jnp.maximum(r, 0.0))), name=f"{lname}_ff1")
    x2 = mm(act, w["ff2"], mode="nn", epi=lambda r, res: (r + res,), extras=(x1,), name=f"{lname}_ff2")
    saved = dict(x=x, u=u, proj=proj, qn=qn, kvn=kvn, kv=kv, qh=qh, kh=kh, o_a=o_a, lse=lse, y_a=y_a, kvp=kvp,
                 biasm=biasm, o_b=o_b, y_b=y_b, merged=merged, x1=x1, u2=u2, hid=hid, act=act)
    return x2, saved, gathered


def layer_bwd(dm, lname, dx2, w, g, sv, cos_q, sin_q, cos_k, sin_k, ride=None):
    h = dm.mla_h
    gw, gr = {}, {}
    gw["ff2"] = mm(sv["act"], dx2, mode="tn", out_dtypes=(BF16,), name=f"{lname}_d_ff2")
    dh = mm(dx2, w["ff2"], mode="nt", out_dtypes=(BF16,), extras=(sv["hid"],),
            epi=lambda r, hv: (r * (2.0 * jnp.maximum(hv, 0.0)),), name=f"{lname}_d_act")
    gw["ff1"] = mm(sv["u2"], dh, mode="tn", out_dtypes=(BF16,), name=f"{lname}_d_ff1")
    du2 = mm(dh, w["ff1"], mode="nt", name=f"{lname}_d_u2")
    dx1, gr["norm_mlp"] = rmsnorm_bwd(sv["x1"], g["norm_mlp"], du2, dx2, name=f"{lname}_d_norm_mlp")
    gw["out"] = mm(sv["merged"], dx1, mode="tn", out_dtypes=(BF16,), name=f"{lname}_d_out")
    dmerged = mm(dx1, w["out"], mode="nt", name=f"{lname}_d_merged")
    dy_a, dy_b, dga, dgb = gate_bwd(sv["proj"], sv["y_a"], sv["y_b"], dmerged, d=dm.d, ga_col=dm.ga_col,
                                    gb_col=dm.gb_col, name=f"{lname}_d_gate")
    gw["o_na"] = mm(sv["o_b"], dy_b, mode="tn", out_dtypes=(BF16,), name=f"{lname}_d_o_na")
    do_b = mm(dy_b, w["o_na"], mode="nt", out_dtypes=(BF16,), name=f"{lname}_d_ob")
    dq_na, dkp, dvp, dbias = na_bwd(sv["proj"], sv["kvp"], sv["kvp"], sv["biasm"], do_b, q_col0=dm.q_col0,
                                    v_col0=dm.na_h, name=f"{lname}_d_na")
    gr["rpb"] = na_bias_grad(dbias, name=f"{lname}_d_rpb")
    dk_na = dkp[NA_PAD_LO:NA_PAD_LO + dm.s].astype(BF16)
    dv_na = dvp[NA_PAD_LO:NA_PAD_LO + dm.s].astype(BF16)
    gw["o_mla"] = mm(sv["o_a"], dy_a, mode="tn", out_dtypes=(BF16,), name=f"{lname}_d_o_mla")
    do_a = mm(dy_a, w["o_mla"], mode="nt", out_dtypes=(BF16,), name=f"{lname}_d_oa")
    delta = mla_delta(do_a, sv["o_a"], h, name=f"{lname}_d_mla_delta").reshape(h, 1, dm.s)
    dqt, dkh, dv, *received = mla_bwd(sv["qh"], sv["kh"], sv["kh"].transpose(0, 2, 1), sv["kv"], do_a, sv["lse"],
                                      delta, v_col0=h, name=f"{lname}_d_mla", ride=ride)
    dqh = dqt.transpose(0, 1, 3, 2).reshape(h, dm.s, QK_DIM)
    dq_nope, dq1, dq2 = _from_head(dqh, UQ_WIDTHS)
    dq1, dq2 = rope_pair(dq1, dq2, cos_q, -sin_q, name=f"{lname}_d_rope_q")
    dq = jnp.concatenate([dq_nope.astype(BF16), dq1, dq2], axis=1)
    gw["uq"] = mm(sv["qn"], dq, mode="tn", out_dtypes=(BF16,), name=f"{lname}_d_uq")
    dqn = mm(dq, w["uq"], mode="nt", name=f"{lname}_d_qn")
    dk_nope = _from_head(dkh[:, :, :QK_NOPE], (QK_NOPE,))[0]
    dk1, dk2 = rope_pair_headsum(dkh[:, :, QK_NOPE:QK_NOPE + HALF_ROPE], dkh[:, :, QK_NOPE + HALF_ROPE:],
                                 cos_k, sin_k, name=f"{lname}_d_rope_k")
    dkv = jnp.concatenate([dk_nope.astype(BF16), dv], axis=1)
    gw["ukv"] = mm(sv["kvn"], dkv, mode="tn", out_dtypes=(BF16,), name=f"{lname}_d_ukv")
    dkvn = mm(dkv, w["ukv"], mode="nt", name=f"{lname}_d_kvn")
    dc_q, gr["norm_qa"] = rmsnorm_bwd(sv["proj"], g["norm_qa"], dqn, width=dm.q_lora, col=0, out_dtype=BF16,
                                      name=f"{lname}_d_norm_qa")
    dc_kv, gr["norm_kva"] = rmsnorm_bwd(sv["proj"], g["norm_kva"], dkvn, width=dm.kv_lora, col=1, out_dtype=BF16,
                                        name=f"{lname}_d_norm_kva")
    dproj = jnp.concatenate([dc_q, dc_kv, dq_na, dk_na, dv_na, dga, dgb], axis=1)
    dkpe = jnp.concatenate([dk1, dk2, jnp.zeros((dm.s, LANES - QK_ROPE), BF16)], axis=1)
    gw["in_main"] = mm(sv["u"], dproj, mode="tn", out_dtypes=(BF16,), name=f"{lname}_d_in")
    gw["in_kpe"] = mm(sv["u"], dkpe, mode="tn", out_dtypes=(BF16,), name=f"{lname}_d_in_kpe")
    du_k = mm(dkpe, w["in_kpe"], mode="nt", name=f"{lname}_d_u_kpe")
    du = mm(dproj, w["in_main"], mode="nt", epi=lambda r, res: (r + res,), extras=(du_k,), name=f"{lname}_d_u")
    dx, gr["norm_mix"] = rmsnorm_bwd(sv["x"], g["norm_mix"], du, dx1, name=f"{lname}_d_norm_mix")
    return dx, gw, gr, received


def _kernel_weights(dm, full):
    main, kpe = _split_w_in(dm, full["w_in"])
    return dict(in_main=main, in_kpe=kpe,
                uq=jnp.concatenate(_split_heads(full["w_uq"], dm.mla_h, UQ_WIDTHS), axis=1),
                ukv=jnp.concatenate(_split_heads(full["w_ukv"], dm.mla_h, UKV_WIDTHS), axis=1),
                o_mla=full["w_o_mla"], o_na=full["w_o_na"], out=full["w_out"], ff1=full["w_ff1"], ff2=full["w_ff2"])


def _reference_order(dm, gw):
    h = dm.mla_h
    nope_w, half_w = h * QK_NOPE, h * HALF_ROPE
    uq = gw["uq"]
    ukv = gw["ukv"]
    return {"w_in": _join_w_in(dm, gw["in_main"], gw["in_kpe"]),
            "w_uq": _join_heads([uq[:, :nope_w], uq[:, nope_w:nope_w + half_w], uq[:, nope_w + half_w:]], h),
            "w_ukv": _join_heads([ukv[:, :nope_w], ukv[:, nope_w:]], h),
            "w_o_mla": gw["o_mla"], "w_o_na": gw["o_na"], "w_out": gw["out"], "w_ff1": gw["ff1"], "w_ff2": gw["ff2"]}


def _pack_replicated(dm, parts):
    flat = jnp.concatenate([parts[n].reshape(-1) for n in REPLICATED])
    n = flat.shape[0]
    rows_ = -(-n // (8 * LANES)) * 8
    return jnp.pad(flat, (0, rows_ * LANES - n)).reshape(rows_, LANES)


def kernel(x, norm_mix, w_in, norm_qa, w_uq, norm_kva, w_ukv, rpb, w_o_mla, w_o_na, w_out, norm_mlp, w_ff1, w_ff2, norm_final, loss_target, m_norm_mix, m_w_in, m_norm_qa, m_w_uq, m_norm_kva, m_w_ukv, m_rpb, m_w_o_mla, m_w_o_na, m_w_out, m_norm_mlp, m_w_ff1, m_w_ff2, m_norm_final, v_norm_mix, v_w_in, v_norm_qa, v_w_uq, v_norm_kva, v_w_ukv, v_rpb, v_w_o_mla, v_w_o_na, v_w_out, v_norm_mlp, v_w_ff1, v_w_ff2, v_norm_final):
    dm = Dims(x, w_in, norm_qa, norm_kva, rpb, w_o_mla, w_o_na, w_ff1)
    params = dict(norm_mix=norm_mix, w_in=w_in, norm_qa=norm_qa, w_uq=w_uq, norm_kva=norm_kva, w_ukv=w_ukv, rpb=rpb,
                  w_o_mla=w_o_mla, w_o_na=w_o_na, w_out=w_out, norm_mlp=norm_mlp, w_ff1=w_ff1, w_ff2=w_ff2,
                  norm_final=norm_final)
    mom_m = dict(norm_mix=m_norm_mix, w_in=m_w_in, norm_qa=m_norm_qa, w_uq=m_w_uq, norm_kva=m_norm_kva, w_ukv=m_w_ukv,
                 rpb=m_rpb, w_o_mla=m_w_o_mla, w_o_na=m_w_o_na, w_out=m_w_out, norm_mlp=m_norm_mlp, w_ff1=m_w_ff1,
                 w_ff2=m_w_ff2, norm_final=m_norm_final)
    mom_v = dict(norm_mix=v_norm_mix, w_in=v_w_in, norm_qa=v_norm_qa, w_uq=v_w_uq, norm_kva=v_norm_kva, w_ukv=v_w_ukv,
                 rpb=v_rpb, w_o_mla=v_w_o_mla, w_o_na=v_w_o_na, w_out=v_w_out, norm_mlp=v_norm_mlp, w_ff1=v_w_ff1,
                 w_ff2=v_w_ff2, norm_final=v_norm_final)
    depth, s, h = dm.depth, dm.s, dm.mla_h

    pos = jnp.arange(s, dtype=F32)
    inv_freq = 1.0 / (ROPE_THETA ** (jnp.arange(0, QK_ROPE, 2, dtype=F32) / QK_ROPE))
    ang = pos[:, None] * inv_freq[None, :]
    cos_k, sin_k = jnp.cos(ang), jnp.sin(ang)
    cos_q, sin_q = jnp.tile(cos_k, (1, h)), jnp.tile(sin_k, (1, h))

    parts = [part_of(n, params[n].shape[1:]) for n in SHARDED]
    shards = lambda l: [params[n][l].astype(BF16) for n in SHARDED]
    gains = [dict(norm_mix=norm_mix[l][None], norm_qa=norm_qa[l][None], norm_kva=norm_kva[l][None],
                  norm_mlp=norm_mlp[l][None], rpb=rpb[l]) for l in range(depth)]

    xl = x[0]
    saved, weights = [], []
    gathered = Collective("gather", parts, shards(0)).run("gather_weights")
    for l in range(depth):
        weights.append(_kernel_weights(dm, {n: _whole(p, a) for n, p, a in zip(SHARDED, parts, gathered)}))
        ride = Collective("gather", parts, shards(l + 1)) if l + 1 < depth else None
        xl, sv, gathered = layer_fwd(dm, "fwd", xl, weights[l], gains[l], cos_q, sin_q, cos_k, sin_k, ride=ride)
        saved.append(sv)
    dx, g_final, loss_part = loss_head(xl, norm_final[None], loss_target[0], name="loss_head")
    loss = lax.psum(loss_part[0, 0], MESH_AXES)

    rep = {n: [None] * depth for n in REPLICATED if n != "norm_final"}
    recv = [None] * depth
    ride = None
    for l in reversed(range(depth)):
        dx, gw, gr, received = layer_bwd(dm, "bwd", dx, weights[l], gains[l], saved[l], cos_q, sin_q, cos_k, sin_k,
                                         ride=ride)
        if received:
            recv[l + 1] = received
        for n in gr:
            rep[n][l] = gr[n]
        full = _reference_order(dm, gw)
        ride = Collective("exchange", parts, [_for_exchange(p, full[n]) for n, p in zip(SHARDED, parts)])
    recv[0] = ride.run("scatter_grads")
    rep_parts = {n: jnp.stack(rep[n]) for n in rep}
    rep_parts["norm_final"] = g_final
    packed = _pack_replicated(dm, rep_parts)
    rep_all = Collective("gather", [Part(packed.shape, "packed", F32)], [packed]).run("gather_small_grads")[0]

    outs = {}
    for w, n in enumerate(SHARDED):
        outs[n] = adamw_layers(params[n], mom_m[n], mom_v[n], [recv[l][w] for l in range(depth)], name=f"adamw_{n}")
    n_rep = sum(int(np.prod(params[n].shape)) for n in REPLICATED)
    pack = lambda d: _pack_replicated(dm, d)
    res = adamw(pack(params), pack(mom_m), pack(mom_v), rep_all, name="adamw_replicated")
    off = 0
    for n in REPLICATED:
        size = int(np.prod(params[n].shape))
        outs[n] = [a.reshape(-1)[off:off + size].reshape(params[n].shape) for a in res]
        off += size
    assert off == n_rep

    grad_x = dx[None]
    return (loss, grad_x, *[outs[n][0] for n in WEIGHTS], *[outs[n][1] for n in WEIGHTS],
            *[outs[n][2] for n in WEIGHTS], *[outs[n][3] for n in WEIGHTS])
```

```python
import functools

import numpy as np
import jax
import jax.numpy as jnp
from jax import lax
from jax.experimental import pallas as pl
from jax.experimental.pallas import tpu as pltpu

F32 = jnp.float32
BF16 = jnp.bfloat16
MESH_AXES = ("x", "y", "c")
N_DEV = 8
LANES = 128

QK_NOPE = 128
QK_ROPE = 64
HALF_ROPE = QK_ROPE // 2
V_HEAD = 128
QK_DIM = QK_NOPE + QK_ROPE
NA_HEAD_DIM = 128
GRID_W = 64
NA_KH = 8
NA_KW = 16
NA_SPAN_ROWS = 2 * NA_KH
NA_SPAN = NA_SPAN_ROWS * GRID_W
NA_PAD_LO = (NA_KH - 1) * GRID_W
NA_PAD_HI = NA_KH * GRID_W
ROPE_THETA = 10000.0
EPS = 1e-6
NEG = -1e30

ADAM_LR = 0.001
ADAM_B1 = 0.9
ADAM_B2 = 0.999
ADAM_EPS = 1e-08
ADAM_WD = 0.01
ADAM_STEP = 10

VMEM_LIMIT_V7X = 56 * 1024 * 1024
ADAMW_BLOCK_BYTES = 24 * 1024 * 1024

NN = (((1,), (0,)), ((), ()))
NT = (((1,), (1,)), ((), ()))
TN = (((0,), (0,)), ((), ()))


def _pick(dim, target, align=LANES):
    if dim <= target:
        return dim
    t = (target // align) * align
    while t >= align:
        if dim % t == 0:
            return t
        t -= align
    return dim


def _params(sem):
    return pltpu.CompilerParams(dimension_semantics=sem, vmem_limit_bytes=VMEM_LIMIT_V7X)


def mm(a, b, *, mode, name, out_dtypes=(F32,), epi=None, extras=(), tm=1024, tn=1024, tk=2048, exact=False):
    if mode == "nn":
        (m, k), (k2, n) = a.shape, b.shape
    elif mode == "nt":
        (m, k), (n, k2) = a.shape, b.shape
    else:
        (k, m), (k2, n) = a.shape, b.shape
    assert k == k2, (a.shape, b.shape, mode)
    tm, tn, tk = _pick(m, tm), _pick(n, tn), _pick(k, tk)
    nk = k // tk
    a_spec = pl.BlockSpec((tk, tm), lambda i, j, s: (s, i)) if mode == "tn" else pl.BlockSpec((tm, tk), lambda i, j, s: (i, s))
    b_spec = pl.BlockSpec((tn, tk), lambda i, j, s: (j, s)) if mode == "nt" else pl.BlockSpec((tk, tn), lambda i, j, s: (s, j))
    tile = pl.BlockSpec((tm, tn), lambda i, j, s: (i, j))
    dims = {"nn": NN, "nt": NT, "tn": TN}[mode]
    n_extra, n_out = len(extras), len(out_dtypes)

    def product(a_ref, b_ref):
        if exact:
            return lax.dot_general(a_ref[...], b_ref[...], dims, precision=lax.Precision.HIGHEST,
                                   preferred_element_type=F32)
        return lax.dot_general(a_ref[...].astype(BF16), b_ref[...].astype(BF16), dims, preferred_element_type=F32)

    def finish(r, extra_refs, out_refs):
        res = (r,) if epi is None else epi(r, *[e[...] for e in extra_refs])
        for o, v in zip(out_refs, res):
            o[...] = v.astype(o.dtype)

    def body_one_step(a_ref, b_ref, *rest):
        finish(product(a_ref, b_ref), rest[:n_extra], rest[n_extra:])

    def body(a_ref, b_ref, *rest):
        extra_refs, out_refs, acc = rest[:n_extra], rest[n_extra:n_extra + n_out], rest[-1]
        step = pl.program_id(2)

        @pl.when(step == 0)
        def _():
            acc[...] = product(a_ref, b_ref)

        @pl.when(step > 0)
        def _():
            acc[...] += product(a_ref, b_ref)

        @pl.when(step == nk - 1)
        def _():
            finish(acc[...], extra_refs, out_refs)

    outs = pl.pallas_call(
        body_one_step if nk == 1 else body, name=name, grid=(m // tm, n // tn, nk),
        in_specs=[a_spec, b_spec] + [tile] * n_extra,
        out_specs=[tile] * n_out,
        out_shape=[jax.ShapeDtypeStruct((m, n), d) for d in out_dtypes],
        scratch_shapes=[] if nk == 1 else [pltpu.VMEM((tm, tn), F32)],
        compiler_params=_params(("parallel", "parallel", "arbitrary")),
    )(a, b, *extras)
    return outs[0] if n_out == 1 else outs


def blockwise(fn, ins, outs, *, grid, name, sums=()):
    n_in, n_axes = len(ins), len(grid)

    def body(*refs):
        res = fn(*[r[...] for r in refs[:n_in]])
        first = functools.reduce(jnp.logical_and, [pl.program_id(ax) == 0 for ax in range(n_axes)])
        for idx, (o, v) in enumerate(zip(refs[n_in:], res)):
            if idx in sums:
                @pl.when(first)
                def _(o=o):
                    o[...] = jnp.zeros_like(o)

                o[...] += v.astype(o.dtype)
            else:
                o[...] = v.astype(o.dtype)

    sem = ("arbitrary" if sums else "parallel",) * n_axes
    res = pl.pallas_call(
        body, name=name, grid=grid,
        in_specs=[pl.BlockSpec(blk, imap) for _, blk, imap in ins],
        out_specs=[pl.BlockSpec(blk, imap) for _, _, blk, imap in outs],
        out_shape=[jax.ShapeDtypeStruct(shape, dt) for shape, dt, _, _ in outs],
        compiler_params=_params(sem),
    )(*[a for a, _, _ in ins])
    return res


def rows(arr, tr, width=None, col=0):
    width = arr.shape[1] if width is None else width
    return (arr, (tr, width), lambda i, col=col: (i, col))


def whole(arr):
    return (arr, arr.shape, lambda i: (0, 0))


def out_rows(n_rows, width, dtype, tr):
    return ((n_rows, width), dtype, (tr, width), lambda i: (i, 0))


def out_sum(shape, dtype=F32):
    return (shape, dtype, shape, lambda i: (0, 0))


def _rstd(x):
    return lax.rsqrt(jnp.mean(x * x, axis=-1, keepdims=True) + EPS)


def _colsum(v):
    return jnp.sum(v, axis=0, keepdims=True)


def rmsnorm_fwd(x, g, *, width=None, col=0, tr=256, name):
    n = x.shape[0]
    tr = _pick(n, tr, 8)
    width = x.shape[1] if width is None else width

    def fn(xv, gv):
        return ((xv * _rstd(xv)) * gv,)

    return blockwise(fn, [rows(x, tr, width, col), whole(g)], [out_rows(n, width, BF16, tr)],
                     grid=(n // tr,), name=name)[0]


def rmsnorm_bwd(x, g, dy, res=None, *, width=None, col=0, out_dtype=F32, tr=256, name):
    n = x.shape[0]
    tr = _pick(n, tr, 8)
    width = x.shape[1] if width is None else width

    def fn(xv, gv, dyv, *resv):
        dyv = dyv.astype(F32)
        r = _rstd(xv)
        xhat = xv * r
        dxhat = dyv * gv
        dx = r * (dxhat - xhat * jnp.mean(dxhat * xhat, axis=-1, keepdims=True))
        if resv:
            dx = dx + resv[0]
        return dx, _colsum(dyv * xhat)

    ins = [rows(x, tr, width, col), whole(g), rows(dy, tr)] + ([rows(res, tr)] if res is not None else [])
    return blockwise(fn, ins, [out_rows(n, width, out_dtype, tr), out_sum((1, width))],
                     grid=(n // tr,), name=name, sums=(1,))


def rope_pair(x1, x2, cos, sin, *, tr=512, name):
    n, w = x1.shape
    tr = _pick(n, tr, 8)

    def fn(a, b, c, s):
        a, b = a.astype(F32), b.astype(F32)
        return a * c - b * s, b * c + a * s

    return blockwise(fn, [rows(x1, tr), rows(x2, tr), rows(cos, tr), rows(sin, tr)],
                     [out_rows(n, w, BF16, tr), out_rows(n, w, BF16, tr)], grid=(n // tr,), name=name)


def rope_pair_headsum(d1, d2, cos, sin, *, tr=512, name):
    h, n, w = d1.shape
    tr = _pick(n, tr, 8)

    def fn(a, b, c, s):
        a, b = jnp.sum(a.astype(F32), axis=0), jnp.sum(b.astype(F32), axis=0)
        return a * c + b * s, b * c - a * s

    lead = lambda arr: (arr, (h, tr, w), lambda i: (0, i, 0))
    return blockwise(fn, [lead(d1), lead(d2), rows(cos, tr), rows(sin, tr)],
                     [out_rows(n, w, BF16, tr), out_rows(n, w, BF16, tr)], grid=(n // tr,), name=name)


def gate_fwd(proj, y_a, y_b, *, d, ga_col, gb_col, tr=256, name):
    n = proj.shape[0]
    tr = _pick(n, tr, 8)

    def fn(ga, gb, ya, yb):
        return (jax.nn.sigmoid(ga) * ya + jax.nn.sigmoid(gb) * yb,)

    return blockwise(fn, [rows(proj, tr, d, ga_col), rows(proj, tr, d, gb_col), rows(y_a, tr), rows(y_b, tr)],
                     [out_rows(n, d, BF16, tr)], grid=(n // tr,), name=name)[0]


def gate_bwd(proj, y_a, y_b, dmerged, *, d, ga_col, gb_col, tr=256, name):
    n = proj.shape[0]
    tr = _pick(n, tr, 8)

    def fn(ga, gb, ya, yb, dm):
        sa, sb = jax.nn.sigmoid(ga), jax.nn.sigmoid(gb)
        return dm * sa, dm * sb, dm * ya * (sa * (1.0 - sa)), dm * yb * (sb * (1.0 - sb))

    return blockwise(fn, [rows(proj, tr, d, ga_col), rows(proj, tr, d, gb_col), rows(y_a, tr), rows(y_b, tr),
                          rows(dmerged, tr)],
                     [out_rows(n, d, BF16, tr)] * 4, grid=(n // tr,), name=name)


def loss_head(x, g, target, *, tr=256, name):
    n, d = x.shape
    tr = _pick(n, tr, 8)

    def fn(xv, gv, tv):
        r = _rstd(xv)
        xhat = xv * r
        diff = xhat * gv - tv
        loss = 0.5 * jnp.sum(jnp.sum(diff * diff, axis=-1, keepdims=True) / d, axis=0, keepdims=True)
        dy = diff / d
        dxhat = dy * gv
        dx = r * (dxhat - xhat * jnp.mean(dxhat * xhat, axis=-1, keepdims=True))
        return dx, _colsum(dy * xhat), jnp.broadcast_to(loss, (8, LANES))

    return blockwise(fn, [rows(x, tr), whole(g), rows(target, tr)],
                     [out_rows(n, d, F32, tr), out_sum((1, d)), out_sum((8, LANES))],
                     grid=(n // tr,), name=name, sums=(1, 2))


def _adamw_math(wv, mv, vv, gs):
    c1 = 1.0 / (1.0 - ADAM_B1 ** ADAM_STEP)
    c2 = 1.0 / (1.0 - ADAM_B2 ** ADAM_STEP)
    g = gs[0].astype(F32)
    for dev in range(1, N_DEV):
        g = g + gs[dev].astype(F32)
    m_new = ADAM_B1 * mv + (1.0 - ADAM_B1) * g
    v_new = ADAM_B2 * vv + (1.0 - ADAM_B2) * (g * g)
    delta = -ADAM_LR * ((m_new * c1) / (jnp.sqrt(v_new * c2) + ADAM_EPS) + ADAM_WD * wv)
    return g, delta, m_new, v_new


def adamw(w, m, v, g_slots, *, tr=256, name):
    n, c = w.shape
    tr = _pick(n, tr, 8)
    slots = (g_slots, (N_DEV, tr, c), lambda i: (0, i, 0))
    return blockwise(_adamw_math, [rows(w, tr), rows(m, tr), rows(v, tr), slots],
                     [out_rows(n, c, F32, tr)] * 4, grid=(n // tr,), name=name)


def adamw_layers(w, m, v, g_layers, *, name):
    depth, r, c = w.shape
    row_bytes = c * 2 * (depth * N_DEV * g_layers[0].dtype.itemsize + 7 * 4)
    tr = _pick(r, max(8, ADAMW_BLOCK_BYTES // row_bytes // 8 * 8), 8)

    def body(w_ref, m_ref, v_ref, *rest):
        g_refs, out_refs = rest[:depth], rest[depth:]
        for layer in range(depth):
            @pl.when(pl.program_id(0) == layer)
            def _(layer=layer):
                for o, val in zip(out_refs, _adamw_math(w_ref[...], m_ref[...], v_ref[...], g_refs[layer][...])):
                    o[...] = val

    blk = pl.BlockSpec((None, tr, c), lambda l, i: (l, i, 0))
    g_specs = [pl.BlockSpec((N_DEV, tr, c), lambda l, i, layer=layer: (0, jnp.where(l == layer, i, 0), 0))
               for layer in range(depth)]
    return pl.pallas_call(
        body, name=name, grid=(depth, r // tr), in_specs=[blk] * 3 + g_specs, out_specs=[blk] * 4,
        out_shape=[jax.ShapeDtypeStruct(w.shape, F32)] * 4, compiler_params=_params(("arbitrary", "arbitrary")),
    )(w, m, v, *g_layers)


LOG2E = 1.4426950408889634
MLA_FWD_SUB, MLA_FWD_BQ = 512, 2048
MLA_BWD_SUB, MLA_BWD_BQ = 1024, 2048
MLA_BK = 2048


def mla_fwd(q, k, vt, *, name, ride=None):
    h, s, dq = q.shape
    bq, bk = _pick(s, MLA_FWD_BQ), _pick(s, MLA_BK)
    sub = min(MLA_FWD_SUB, bq)
    nk = s // bk
    scale = QK_DIM ** -0.5
    c2 = scale * LOG2E
    grid = (h, s // bq, nk)

    def body(q_ref, k_ref, vt_ref, *rest):
        if ride is None:
            o_ref, lse_ref, m_s, l_s, acc = rest
        else:
            n = ride.n
            x_refs, (o_ref, lse_ref), g_refs = rest[:n], rest[n:n + 2], rest[n + 2:2 * n + 2]
            m_s, l_s, acc, *sems = rest[2 * n + 2:]
            first, middle, last = _grid_flags(grid)
            ride_start, ride_middle, ride_finish = ride.steps(x_refs, g_refs, sems)
            pl.when(first)(ride_start)
            pl.when(middle)(ride_middle)
        j = pl.program_id(2)

        @pl.when(j == 0)
        def _():
            m_s[...] = jnp.full_like(m_s, NEG)
            l_s[...] = jnp.zeros_like(l_s)
            acc[...] = jnp.zeros_like(acc)

        def scores(c0):
            return lax.dot_general(k_ref[...], q_ref[pl.ds(c0, sub), :], NT, preferred_element_type=F32)

        starts = list(range(0, bq, sub))
        st_next = scores(starts[0])
        for n, c0 in enumerate(starts):
            cols = pl.ds(c0, sub)
            st = st_next
            if n + 1 < len(starts):
                st_next = scores(starts[n + 1])
            m_prev = m_s[:, cols]
            m_new = jnp.maximum(m_prev, jnp.max(st, axis=0, keepdims=True))
            alpha = jnp.exp2((m_prev - m_new) * c2)
            pt = jnp.exp2((st - m_new) * c2)
            l_s[:, cols] = alpha * l_s[:, cols] + jnp.sum(pt, axis=0, keepdims=True)
            acc[:, cols] = alpha * acc[:, cols] + lax.dot_general(vt_ref[...], pt.astype(BF16), NN,
                                                                  preferred_element_type=F32)
            m_s[:, cols] = m_new

        @pl.when(j == nk - 1)
        def _():
            o_ref[...] = (acc[...] / l_s[...]).T
            lse_ref[...] = m_s[...] * scale + jnp.log(l_s[...])

        if ride is not None:
            pl.when(last)(ride_finish)

    in_specs = [pl.BlockSpec((None, bq, dq), lambda hh, i, j: (hh, i, 0)),
                pl.BlockSpec((None, bk, dq), lambda hh, i, j: (hh, j, 0)),
                pl.BlockSpec((None, V_HEAD, bk), lambda hh, i, j: (hh, 0, j))]
    out_specs = [pl.BlockSpec((bq, V_HEAD), lambda hh, i, j: (i, hh)),
                 pl.BlockSpec((None, 1, bq), lambda hh, i, j: (hh, 0, i))]
    out_shape = [jax.ShapeDtypeStruct((s, h * V_HEAD), F32), jax.ShapeDtypeStruct((h, 1, s), F32)]
    scratch = [pltpu.VMEM((1, bq), F32), pltpu.VMEM((1, bq), F32), pltpu.VMEM((V_HEAD, bq), F32)]
    args = (q, k, vt)
    sem = ("parallel", "parallel", "arbitrary")
    if ride is not None:
        in_specs, out_specs, out_shape, scratch, args = ride.extend(in_specs, out_specs, out_shape, scratch, args)
        sem = ("arbitrary",) * 3
    return pl.pallas_call(body, name=name, grid=grid, in_specs=in_specs, out_specs=out_specs, out_shape=out_shape,
                          scratch_shapes=scratch, compiler_params=_params(sem))(*args)


def mla_delta(do, o, h, *, tr=512, name):
    s = do.shape[0]
    tr = _pick(s, tr, 8)

    def fn(dov, ov):
        return (jnp.sum(dov.astype(F32) * ov, axis=-1, keepdims=True),)

    blk = lambda arr: (arr, (tr, V_HEAD), lambda hh, i: (i, hh))
    return blockwise(fn, [blk(do), blk(o)], [((h, s, 1), F32, (None, tr, 1), lambda hh, i: (hh, i, 0))],
                     grid=(h, s // tr), name=name)[0]


def mla_bwd(q, k, kt, kv, do, lse, delta, *, v_col0, name, ride=None):
    h, s, dq = q.shape
    bq, bk = _pick(s, MLA_BWD_BQ), _pick(s, MLA_BK)
    sub = min(MLA_BWD_SUB, bq)
    nq = s // bq
    scale = QK_DIM ** -0.5
    c2 = scale * LOG2E
    grid = (h, s // bk, nq)

    def body(q_ref, k_ref, kt_ref, v_ref, do_ref, lse_ref, delta_ref, *rest):
        if ride is None:
            dq_ref, dk_ref, dv_ref, dk_acc, dv_acc = rest
        else:
            n = ride.n
            x_refs, (dq_ref, dk_ref, dv_ref), g_refs = rest[:n], rest[n:n + 3], rest[n + 3:2 * n + 3]
            dk_acc, dv_acc, *sems = rest[2 * n + 3:]
            first, _, last = _grid_flags(grid)
            ride_start, _, ride_finish = ride.steps(x_refs, g_refs, sems)
            pl.when(first)(ride_start)
        j, i = pl.program_id(1), pl.program_id(2)

        @pl.when(i == 0)
        def _():
            dk_acc[...] = jnp.zeros_like(dk_acc)
            dv_acc[...] = jnp.zeros_like(dv_acc)

        def scores(c0):
            cols = pl.ds(c0, sub)
            return (lax.dot_general(k_ref[...], q_ref[cols, :], NT, preferred_element_type=F32),
                    lax.dot_general(v_ref[...], do_ref[cols, :], NT, preferred_element_type=F32))

        starts = list(range(0, bq, sub))
        nxt = scores(starts[0])
        for n, c0 in enumerate(starts):
            cols = pl.ds(c0, sub)
            st, dpt = nxt
            if n + 1 < len(starts):
                nxt = scores(starts[n + 1])
            q_sub, do_sub = q_ref[cols, :], do_ref[cols, :]
            pt = jnp.exp2(st * c2 - lse_ref[:, cols] * LOG2E)
            ds_b = ((pt * (dpt - delta_ref[:, cols])) * scale).astype(BF16)
            dv_acc[...] += lax.dot_general(pt.astype(BF16), do_sub, NN, preferred_element_type=F32)
            dk_acc[...] += lax.dot_general(ds_b, q_sub, NN, preferred_element_type=F32)
            dq_t = lax.dot_general(kt_ref[...], ds_b, NN, preferred_element_type=F32)

            @pl.when(j == 0)
            def _():
                dq_ref[i, :, cols] = dq_t

            @pl.when(j > 0)
            def _():
                dq_ref[i, :, cols] += dq_t

        @pl.when(i == nq - 1)
        def _():
            dk_ref[...] = dk_acc[...]
            dv_ref[...] = dv_acc[...].astype(dv_ref.dtype)

        if ride is not None:
            pl.when(last)(ride_finish)

    in_specs = [pl.BlockSpec((None, bq, dq), lambda hh, j, i: (hh, i, 0)),
                pl.BlockSpec((None, bk, dq), lambda hh, j, i: (hh, j, 0)),
                pl.BlockSpec((None, dq, bk), lambda hh, j, i: (hh, 0, j)),
                pl.BlockSpec((bk, V_HEAD), lambda hh, j, i: (j, v_col0 + hh)),
                pl.BlockSpec((bq, V_HEAD), lambda hh, j, i: (i, hh)),
                pl.BlockSpec((None, 1, bq), lambda hh, j, i: (hh, 0, i)),
                pl.BlockSpec((None, 1, bq), lambda hh, j, i: (hh, 0, i))]
    out_specs = [pl.BlockSpec((None, nq, dq, bq), lambda hh, j, i: (hh, 0, 0, 0)),
                 pl.BlockSpec((None, bk, dq), lambda hh, j, i: (hh, j, 0)),
                 pl.BlockSpec((bk, V_HEAD), lambda hh, j, i: (j, hh))]
    out_shape = [jax.ShapeDtypeStruct((h, nq, dq, bq), F32), jax.ShapeDtypeStruct((h, s, dq), F32),
                 jax.ShapeDtypeStruct((s, h * V_HEAD), BF16)]
    scratch = [pltpu.VMEM((bk, dq), F32), pltpu.VMEM((bk, V_HEAD), F32)]
    args = (q, k, kt, kv, do, lse, delta)
    sem = ("parallel", "arbitrary", "arbitrary")
    if ride is not None:
        in_specs, out_specs, out_shape, scratch, args = ride.extend(in_specs, out_specs, out_shape, scratch, args)
        sem = ("arbitrary",) * 3
    return pl.pallas_call(body, name=name, grid=grid, in_specs=in_specs, out_specs=out_specs, out_shape=out_shape,
                          scratch_shapes=scratch, compiler_params=_params(sem))(*args)


def _na_probs(sc, bias, r, n_rows):
    sc = sc * (NA_HEAD_DIM ** -0.5) + bias
    lo = jnp.clip(r - NA_KH // 2, 0, n_rows - NA_KH) - r + (NA_KH - 1)
    dy = lax.broadcasted_iota(jnp.int32, (1, NA_SPAN), 1) // GRID_W
    sc = jnp.where((dy >= lo) & (dy < lo + NA_KH), sc, NEG)
    p = jnp.exp(sc - jnp.max(sc, axis=-1, keepdims=True))
    return p / jnp.sum(p, axis=-1, keepdims=True)


def na_pad_cast(proj, *, col0, width, name):
    s = proj.shape[0]
    n_in, lo = s // GRID_W, NA_PAD_LO // GRID_W
    assert col0 % width == 0

    def body(x_ref, o_ref):
        i = pl.program_id(0)
        inside = jnp.logical_and(i >= lo, i < lo + n_in)
        o_ref[...] = jnp.where(inside, x_ref[...], 0.0).astype(o_ref.dtype)

    return pl.pallas_call(
        body, name=name, grid=(n_in + NA_SPAN_ROWS - 1,),
        in_specs=[pl.BlockSpec((GRID_W, width), lambda i: (jnp.clip(i - lo, 0, n_in - 1), col0 // width))],
        out_specs=pl.BlockSpec((GRID_W, width), lambda i: (i, 0)),
        out_shape=jax.ShapeDtypeStruct((s + NA_PAD_LO + NA_PAD_HI, width), BF16),
        compiler_params=_params(("parallel",)),
    )(proj)


def na_fwd(proj, kp, vp, biasm, *, q_col0, v_col0=0, rows_per_step=8, name):
    s = proj.shape[0]
    sp = kp.shape[0]
    h = biasm.shape[0]
    hw = h * NA_HEAD_DIM
    n_rows = s // GRID_W
    rb = min(rows_per_step, n_rows)
    tq = rb * GRID_W

    def body(q_ref, k_ref, v_ref, b_ref, o_ref):
        i = pl.program_id(1)
        bias = b_ref[...]

        def span(rl):
            return pl.ds(pl.multiple_of((i * rb + rl) * GRID_W, GRID_W), NA_SPAN)

        def scores(rl):
            q_row = q_ref[pl.ds(rl * GRID_W, GRID_W), :].astype(BF16)
            return lax.dot_general(q_row, k_ref[span(rl), :], NT, preferred_element_type=F32)

        sc_next = scores(0)
        for rl in range(rb):
            sc = sc_next
            if rl + 1 < rb:
                sc_next = scores(rl + 1)
            p = _na_probs(sc, bias, i * rb + rl, n_rows)
            o_ref[pl.ds(rl * GRID_W, GRID_W), :] = lax.dot_general(
                p.astype(BF16), v_ref[span(rl), :], NN, preferred_element_type=F32)

    return pl.pallas_call(
        body, name=name, grid=(h, n_rows // rb),
        in_specs=[pl.BlockSpec((tq, NA_HEAD_DIM), lambda hh, i: (i, q_col0 + hh)),
                  pl.BlockSpec((sp, NA_HEAD_DIM), lambda hh, i: (0, hh)),
                  pl.BlockSpec((sp, NA_HEAD_DIM), lambda hh, i: (0, v_col0 + hh)),
                  pl.BlockSpec((None, GRID_W, NA_SPAN), lambda hh, i: (hh, 0, 0))],
        out_specs=pl.BlockSpec((tq, NA_HEAD_DIM), lambda hh, i: (i, hh)),
        out_shape=jax.ShapeDtypeStruct((s, hw), F32),
        compiler_params=_params(("parallel", "arbitrary")),
    )(proj, kp, vp, biasm)


def na_bwd(proj, kp, vp, biasm, do, *, q_col0, v_col0=0, rows_per_step=8, name):
    s = proj.shape[0]
    sp = kp.shape[0]
    h = biasm.shape[0]
    hw = h * NA_HEAD_DIM
    n_rows = s // GRID_W
    rb = min(rows_per_step, n_rows)
    tq = rb * GRID_W
    scale = NA_HEAD_DIM ** -0.5

    def body(q_ref, k_ref, v_ref, b_ref, do_ref, dq_ref, dk_ref, dv_ref, db_ref):
        i = pl.program_id(1)

        @pl.when(i == 0)
        def _():
            dk_ref[...] = jnp.zeros_like(dk_ref)
            dv_ref[...] = jnp.zeros_like(dv_ref)
            db_ref[...] = jnp.zeros_like(db_ref)

        bias = b_ref[...]

        def span(rl):
            return pl.ds(pl.multiple_of((i * rb + rl) * GRID_W, GRID_W), NA_SPAN)

        def query(rl):
            rows_ = pl.ds(rl * GRID_W, GRID_W)
            return q_ref[rows_, :].astype(BF16), do_ref[rows_, :]

        def scores(rl):
            q_row, do_row = query(rl)
            return (lax.dot_general(q_row, k_ref[span(rl), :], NT, preferred_element_type=F32),
                    lax.dot_general(do_row, v_ref[span(rl), :], NT, preferred_element_type=F32))

        nxt = scores(0)
        for rl in range(rb):
            sc, dp = nxt
            if rl + 1 < rb:
                nxt = scores(rl + 1)
            q_row, do_row = query(rl)
            p = _na_probs(sc, bias, i * rb + rl, n_rows)
            ds = p * (dp - jnp.sum(dp * p, axis=-1, keepdims=True))
            db_ref[...] += ds
            ds_b = (ds * scale).astype(BF16)
            dq_ref[pl.ds(rl * GRID_W, GRID_W), :] = lax.dot_general(
                ds_b, k_ref[span(rl), :], NN, preferred_element_type=F32).astype(dq_ref.dtype)
            dk_ref[span(rl), :] += lax.dot_general(ds_b, q_row, TN, preferred_element_type=F32)
            dv_ref[span(rl), :] += lax.dot_general(p.astype(BF16), do_row, TN, preferred_element_type=F32)

    return pl.pallas_call(
        body, name=name, grid=(h, n_rows // rb),
        in_specs=[pl.BlockSpec((tq, NA_HEAD_DIM), lambda hh, i: (i, q_col0 + hh)),
                  pl.BlockSpec((sp, NA_HEAD_DIM), lambda hh, i: (0, hh)),
                  pl.BlockSpec((sp, NA_HEAD_DIM), lambda hh, i: (0, v_col0 + hh)),
                  pl.BlockSpec((None, GRID_W, NA_SPAN), lambda hh, i: (hh, 0, 0)),
                  pl.BlockSpec((tq, NA_HEAD_DIM), lambda hh, i: (i, hh))],
        out_specs=[pl.BlockSpec((tq, NA_HEAD_DIM), lambda hh, i: (i, hh)),
                   pl.BlockSpec((sp, NA_HEAD_DIM), lambda hh, i: (0, hh)),
                   pl.BlockSpec((sp, NA_HEAD_DIM), lambda hh, i: (0, hh)),
                   pl.BlockSpec((None, GRID_W, NA_SPAN), lambda hh, i: (hh, 0, 0))],
        out_shape=[jax.ShapeDtypeStruct((s, hw), BF16), jax.ShapeDtypeStruct((sp, hw), F32),
                   jax.ShapeDtypeStruct((sp, hw), F32), jax.ShapeDtypeStruct((h, GRID_W, NA_SPAN), F32)],
        compiler_params=_params(("parallel", "arbitrary")),
    )(proj, kp, vp, biasm, do)


def _na_tables():
    qc = np.arange(GRID_W)[:, None]
    kc = np.arange(GRID_W)[None, :]
    col_start = np.clip(qc - NA_KW // 2, 0, GRID_W - NA_KW)
    col_ok = (kc >= col_start) & (kc < col_start + NA_KW)
    dx = np.clip(kc - qc, -(NA_KW - 1), NA_KW - 1) + (NA_KW - 1)
    return col_ok, dx


def na_bias_table(rpb_l):
    col_ok, dx = _na_tables()
    t = rpb_l[:, :, dx]
    t = jnp.where(col_ok[None, None], t, NEG)
    t = jnp.pad(t, ((0, 0), (0, 1), (0, 0), (0, 0)), constant_values=NEG)
    return t.transpose(0, 2, 1, 3).reshape(rpb_l.shape[0], GRID_W, NA_SPAN)


def na_bias_grad(dbias, *, name):
    h = dbias.shape[0]
    _, dx = _na_tables()
    n_dx = 2 * NA_KW - 1
    onehot = np.zeros((GRID_W * GRID_W, LANES), np.float32)
    onehot[np.arange(GRID_W * GRID_W), dx.reshape(-1)] = 1.0
    t = dbias.reshape(h, GRID_W, NA_SPAN_ROWS, GRID_W)[:, :, :2 * NA_KH - 1]
    t = t.transpose(0, 2, 1, 3).reshape(h * (2 * NA_KH - 1), GRID_W * GRID_W)
    t = jnp.pad(t, ((0, (-t.shape[0]) % 8), (0, 0)))
    out = mm(t, jnp.asarray(onehot), mode="nn", name=name, exact=True, tk=1024)
    return out[:h * (2 * NA_KH - 1), :n_dx].reshape(h, 2 * NA_KH - 1, n_dx)


def _flip(v, bit):
    return 1 - v if bit else v


class Part:
    def __init__(self, shape, kind, dtype):
        self.r, self.c = shape
        self.kind, self.dtype = kind, dtype

    @property
    def whole_shape(self):
        return {"row": (N_DEV * self.r, self.c), "col": (self.r, N_DEV * self.c),
                "packed": (N_DEV, self.r, self.c)}[self.kind]

    @property
    def packed_shape(self):
        return (N_DEV, self.r, self.c)

    def shard_of(self, ref, j):
        if self.kind == "row":
            return ref.at[pl.ds(pl.multiple_of(j * self.r, 8), self.r), :]
        if self.kind == "col":
            return ref.at[:, pl.ds(pl.multiple_of(j * self.c, LANES), self.c)]
        return ref.at[j]


def comm_scratch(n_parts):
    n = n_parts * (N_DEV - 1)
    return [pltpu.SemaphoreType.DMA((n,)), pltpu.SemaphoreType.DMA((n,)), pltpu.SemaphoreType.DMA((n_parts,))]


def gather_plan(parts, x_refs, out_refs, send_sems, recv_sems, local_sems):
    x, y, c = lax.axis_index("x"), lax.axis_index("y"), lax.axis_index("c")
    me, sibling = (x, y, c), (x, y, 1 - c)
    chips = [(1 - x, y), (x, 1 - y), (1 - x, 1 - y)]

    def place(w, px, py, pc):
        return parts[w].shard_of(out_refs[w], 4 * px + 2 * py + pc)

    def copy(k, blk, to, own=False):
        return [pltpu.make_async_remote_copy(
            src_ref=x_refs[w] if own else place(w, *blk), dst_ref=place(w, *blk),
            send_sem=send_sems.at[w * (N_DEV - 1) + k], recv_sem=recv_sems.at[w * (N_DEV - 1) + k],
            device_id=to, device_id_type=pl.DeviceIdType.MESH) for w in range(len(parts))]

    def mine():
        return [pltpu.make_async_copy(x_refs[w], place(w, *me), local_sems.at[w]) for w in range(len(parts))]

    def first():
        return copy(0, me, sibling, own=True) + [cp for j, chip in enumerate(chips)
                                                 for cp in copy(1 + j, me, (*chip, c), own=True)]

    def passed(j):
        return copy(4 + j, (*chips[j], c), sibling)

    def start():
        for cp in mine() + first():
            cp.start()

    def forward():
        for j, chip in enumerate(chips):
            for cp in copy(1 + j, (*chip, c), me):
                cp.wait_recv()
            for cp in passed(j):
                cp.start()

    def finish():
        for cp in copy(0, sibling, me):
            cp.wait_recv()
        for j, chip in enumerate(chips):
            for cp in copy(4 + j, (*chip, 1 - c), me):
                cp.wait_recv()
        for cp in first() + [cp for j in range(len(chips)) for cp in passed(j)]:
            cp.wait_send()
        for cp in mine():
            cp.wait()

    return start, forward, finish


def exchange_plan(parts, x_refs, out_refs, send_sems, recv_sems, local_sems):
    x, y, c = lax.axis_index("x"), lax.axis_index("y"), lax.axis_index("c")
    me = 4 * x + 2 * y + c

    def peer_of(k):
        peer = (_flip(x, k & 4), _flip(y, k & 2), _flip(c, k & 1))
        return peer, 4 * peer[0] + 2 * peer[1] + peer[2]

    def copies(k, arriving):
        peer, theirs = peer_of(k)
        return [pltpu.make_async_remote_copy(
            src_ref=parts[w].shard_of(x_refs[w], me if arriving else theirs),
            dst_ref=out_refs[w].at[theirs if arriving else me],
            send_sem=send_sems.at[w * (N_DEV - 1) + k - 1], recv_sem=recv_sems.at[w * (N_DEV - 1) + k - 1],
            device_id=peer, device_id_type=pl.DeviceIdType.MESH) for w in range(len(parts))]

    def mine():
        return [pltpu.make_async_copy(parts[w].shard_of(x_refs[w], me), out_refs[w].at[me], local_sems.at[w])
                for w in range(len(parts))]

    def start():
        for cp in mine():
            cp.start()
        for k in range(1, N_DEV):
            for cp in copies(k, arriving=False):
                cp.start()

    def finish():
        for k in range(1, N_DEV):
            for cp in copies(k, arriving=True):
                cp.wait_recv()
        for k in range(1, N_DEV):
            for cp in copies(k, arriving=False):
                cp.wait_send()
        for cp in mine():
            cp.wait()

    return start, finish


class Collective:
    def __init__(self, kind, parts, arrays):
        self.kind, self.parts, self.arrays = kind, parts, list(arrays)
        self.n = len(parts)
        shapes = [p.whole_shape if kind == "gather" else p.packed_shape for p in parts]
        self.out_shape = [jax.ShapeDtypeStruct(s, p.dtype) for s, p in zip(shapes, parts)]

    def extend(self, in_specs, out_specs, out_shape, scratch, args):
        any_spec = pl.BlockSpec(memory_space=pl.ANY)
        return (in_specs + [any_spec] * self.n, out_specs + [any_spec] * self.n, out_shape + self.out_shape,
                scratch + comm_scratch(self.n), args + tuple(self.arrays))

    def steps(self, x_refs, out_refs, sems):
        plan = (gather_plan if self.kind == "gather" else exchange_plan)(self.parts, x_refs, out_refs, *sems)
        return plan[0], (plan[1] if len(plan) == 3 else None), plan[-1]

    def run(self, name):
        n = self.n

        def body(*refs):
            for step in self.steps(refs[:n], refs[n:2 * n], refs[2 * n:]):
                if step is not None:
                    step()

        in_specs, out_specs, out_shape, scratch, args = self.extend([], [], [], [], ())
        return pl.pallas_call(body, name=name, in_specs=in_specs, out_specs=out_specs, out_shape=out_shape,
                              scratch_shapes=scratch)(*args)


def _grid_flags(grid):
    ids = [pl.program_id(ax) for ax in range(len(grid))]
    inner_zero = functools.reduce(jnp.logical_and, [i == 0 for i in ids[1:]])
    first = jnp.logical_and(ids[0] == 0, inner_zero)
    middle = jnp.logical_and(ids[0] == grid[0] // 2, inner_zero)
    last = functools.reduce(jnp.logical_and, [i == g - 1 for i, g in zip(ids, grid)])
    return first, middle, last


SHARDED = ("w_in", "w_uq", "w_ukv", "w_o_mla", "w_o_na", "w_out", "w_ff1", "w_ff2")
ROW_SHARDED = ("w_out", "w_ff2")
REPLICATED = ("norm_mix", "norm_qa", "norm_kva", "rpb", "norm_mlp", "norm_final")
WEIGHTS = ("norm_mix", "w_in", "norm_qa", "w_uq", "norm_kva", "w_ukv", "rpb", "w_o_mla", "w_o_na", "w_out",
           "norm_mlp", "w_ff1", "w_ff2", "norm_final")


def part_of(name, shard_shape):
    r, c = shard_shape
    kind = "row" if name in ROW_SHARDED else ("col" if c % LANES == 0 else "packed")
    return Part((r, c), kind, BF16)


def _whole(part, gathered):
    return gathered.transpose(1, 0, 2).reshape(part.r, -1) if part.kind == "packed" else gathered


def _for_exchange(part, full):
    return full.reshape(part.r, N_DEV, part.c).transpose(1, 0, 2) if part.kind == "packed" else full


class Dims:
    def __init__(self, x, w_in, norm_qa, norm_kva, rpb, w_o_mla, w_o_na, w_ff1):
        self.s, self.d = x.shape[1], x.shape[2]
        self.depth = w_in.shape[0]
        self.q_lora, self.kv_lora = norm_qa.shape[1], norm_kva.shape[1]
        self.mla_w, self.na_w = w_o_mla.shape[1], w_o_na.shape[1]
        self.mla_h, self.na_h = self.mla_w // V_HEAD, self.na_w // NA_HEAD_DIM
        self.d_ff = w_ff1.shape[2] * N_DEV
        assert rpb.shape[1] == self.na_h and self.s % GRID_W == 0 and self.s // GRID_W >= NA_SPAN_ROWS
        self.in_lo = self.q_lora + self.kv_lora
        self.main_w = self.in_lo + 3 * self.na_w + 2 * self.d
        assert self.q_lora == self.kv_lora and self.in_lo % self.na_w == 0 and self.in_lo % LANES == 0
        assert (self.in_lo + 3 * self.na_w) % self.d == 0
        self.q_col0 = self.in_lo // NA_HEAD_DIM
        self.k_off = self.in_lo + self.na_w
        self.v_off = self.in_lo + 2 * self.na_w
        self.ga_col = (self.in_lo + 3 * self.na_w) // self.d
        self.gb_col = self.ga_col + 1


def _split_w_in(dm, w):
    lo = dm.in_lo
    main = jnp.concatenate([w[:, :lo], w[:, lo + QK_ROPE:]], axis=1)
    kpe = jnp.pad(w[:, lo:lo + QK_ROPE], ((0, 0), (0, LANES - QK_ROPE)))
    return main, kpe


def _join_w_in(dm, main, kpe):
    lo = dm.in_lo
    return jnp.concatenate([main[:, :lo], kpe[:, :QK_ROPE], main[:, lo:]], axis=1)


def _split_heads(w, h, widths):
    r = w.shape[0]
    w3 = w.reshape(r, h, sum(widths))
    out, o = [], 0
    for wd in widths:
        out.append(w3[:, :, o:o + wd].reshape(r, h * wd))
        o += wd
    return out


def _join_heads(parts, h):
    r = parts[0].shape[0]
    return jnp.concatenate([p.reshape(r, h, -1) for p in parts], axis=2).reshape(r, -1)


UQ_WIDTHS = (QK_NOPE, HALF_ROPE, HALF_ROPE)
UKV_WIDTHS = (QK_NOPE, V_HEAD)


def _by_head(parts, h):
    s = parts[0].shape[0]
    return jnp.concatenate([p.reshape(s, h, -1) for p in parts], axis=2).transpose(1, 0, 2)


def _from_head(t, widths):
    h, s, _ = t.shape
    t = t.transpose(1, 0, 2)
    out, o = [], 0
    for wd in widths:
        out.append(t[:, :, o:o + wd].reshape(s, h * wd))
        o += wd
    return out


def layer_fwd(dm, lname, x, w, g, cos_q, sin_q, cos_k, sin_k, ride, late_weights):
    s, h = dm.s, dm.mla_h
    u = rmsnorm_fwd(x, g["norm_mix"], name=f"{lname}_norm_mix")
    proj = mm(u, w["in_main"], mode="nn", name=f"{lname}_proj")
    kpe = mm(u, w["in_kpe"], mode="nn", name=f"{lname}_proj_kpe")
    qn = rmsnorm_fwd(proj, g["norm_qa"], width=dm.q_lora, col=0, name=f"{lname}_norm_qa")
    kvn = rmsnorm_fwd(proj, g["norm_kva"], width=dm.kv_lora, col=1, name=f"{lname}_norm_kva")
    q = mm(qn, w["uq"], mode="nn", name=f"{lname}_uq")
    kv = mm(kvn, w["ukv"], mode="nn", out_dtypes=(BF16,), name=f"{lname}_ukv")
    nope_w, half_w = h * QK_NOPE, h * HALF_ROPE
    q1, q2 = rope_pair(q[:, nope_w:nope_w + half_w], q[:, nope_w + half_w:], cos_q, sin_q, name=f"{lname}_rope_q")
    k1, k2 = rope_pair(kpe[:, :HALF_ROPE], kpe[:, HALF_ROPE:QK_ROPE], cos_k, sin_k, name=f"{lname}_rope_k")
    qh = _by_head([q[:, :nope_w].astype(BF16), q1, q2], h)
    kh = _by_head([kv[:, :nope_w], jnp.tile(k1, (1, h)), jnp.tile(k2, (1, h))], h)
    vt = kv[:, nope_w:].reshape(s, h, V_HEAD).transpose(1, 2, 0)
    o_a, lse, *gathered = mla_fwd(qh, kh, vt, name=f"{lname}_mla", ride=ride)
    w = {**w, **late_weights(gathered)}
    y_a = mm(o_a, w["o_mla"], mode="nn", name=f"{lname}_o_mla")

    kvp = na_pad_cast(proj, col0=dm.k_off, width=2 * dm.na_w, name=f"{lname}_na_pad")
    biasm = na_bias_table(g["rpb"])
    o_b = na_fwd(proj, kvp, kvp, biasm, q_col0=dm.q_col0, v_col0=dm.na_h, name=f"{lname}_na")
    y_b = mm(o_b, w["o_na"], mode="nn", name=f"{lname}_o_na")

    merged = gate_fwd(proj, y_a, y_b, d=dm.d, ga_col=dm.ga_col, gb_col=dm.gb_col, name=f"{lname}_gate")
    x1 = mm(merged, w["out"], mode="nn", epi=lambda r, res: (r + res,), extras=(x,), name=f"{lname}_out")
    u2 = rmsnorm_fwd(x1, g["norm_mlp"], name=f"{lname}_norm_mlp")
    hid, act = mm(u2, w["ff1"], mode="nn", out_dtypes=(F32, BF16),
                  epi=lambda r: (r, jnp.square(jnp.maximum(r, 0.0))), name=f"{lname}_ff1")
    x2 = mm(act, w["ff2"], mode="nn", epi=lambda r, res: (r + res,), extras=(x1,), name=f"{lname}_ff2")
    saved = dict(x=x, u=u, proj=proj, qn=qn, kvn=kvn, kv=kv, qh=qh, kh=kh, o_a=o_a, lse=lse, y_a=y_a, kvp=kvp,
                 biasm=biasm, o_b=o_b, y_b=y_b, merged=merged, x1=x1, u2=u2, hid=hid, act=act)
    return x2, saved, w, gathered


def layer_bwd(dm, lname, dx2, w, g, sv, cos_q, sin_q, cos_k, sin_k, ride_of):
    h = dm.mla_h
    gw, gr = {}, {}
    gw["ff2"] = mm(sv["act"], dx2, mode="tn", out_dtypes=(BF16,), name=f"{lname}_d_ff2")
    dh = mm(dx2, w["ff2"], mode="nt", out_dtypes=(BF16,), extras=(sv["hid"],),
            epi=lambda r, hv: (r * (2.0 * jnp.maximum(hv, 0.0)),), name=f"{lname}_d_act")
    gw["ff1"] = mm(sv["u2"], dh, mode="tn", out_dtypes=(BF16,), name=f"{lname}_d_ff1")
    du2 = mm(dh, w["ff1"], mode="nt", name=f"{lname}_d_u2")
    dx1, gr["norm_mlp"] = rmsnorm_bwd(sv["x1"], g["norm_mlp"], du2, dx2, name=f"{lname}_d_norm_mlp")
    gw["out"] = mm(sv["merged"], dx1, mode="tn", out_dtypes=(BF16,), name=f"{lname}_d_out")
    dmerged = mm(dx1, w["out"], mode="nt", name=f"{lname}_d_merged")
    dy_a, dy_b, dga, dgb = gate_bwd(sv["proj"], sv["y_a"], sv["y_b"], dmerged, d=dm.d, ga_col=dm.ga_col,
                                    gb_col=dm.gb_col, name=f"{lname}_d_gate")
    gw["o_na"] = mm(sv["o_b"], dy_b, mode="tn", out_dtypes=(BF16,), name=f"{lname}_d_o_na")
    do_b = mm(dy_b, w["o_na"], mode="nt", out_dtypes=(BF16,), name=f"{lname}_d_ob")
    dq_na, dkp, dvp, dbias = na_bwd(sv["proj"], sv["kvp"], sv["kvp"], sv["biasm"], do_b, q_col0=dm.q_col0,
                                    v_col0=dm.na_h, name=f"{lname}_d_na")
    gr["rpb"] = na_bias_grad(dbias, name=f"{lname}_d_rpb")
    dk_na = dkp[NA_PAD_LO:NA_PAD_LO + dm.s].astype(BF16)
    dv_na = dvp[NA_PAD_LO:NA_PAD_LO + dm.s].astype(BF16)
    gw["o_mla"] = mm(sv["o_a"], dy_a, mode="tn", out_dtypes=(BF16,), name=f"{lname}_d_o_mla")
    do_a = mm(dy_a, w["o_mla"], mode="nt", out_dtypes=(BF16,), name=f"{lname}_d_oa")
    delta = mla_delta(do_a, sv["o_a"], h, name=f"{lname}_d_mla_delta").reshape(h, 1, dm.s)
    dqt, dkh, dv, *received = mla_bwd(sv["qh"], sv["kh"], sv["kh"].transpose(0, 2, 1), sv["kv"], do_a, sv["lse"],
                                      delta, v_col0=h, name=f"{lname}_d_mla", ride=ride_of(gw))
    dqh = dqt.transpose(0, 1, 3, 2).reshape(h, dm.s, QK_DIM)
    dq_nope, dq1, dq2 = _from_head(dqh, UQ_WIDTHS)
    dq1, dq2 = rope_pair(dq1, dq2, cos_q, -sin_q, name=f"{lname}_d_rope_q")
    dq = jnp.concatenate([dq_nope.astype(BF16), dq1, dq2], axis=1)
    gw["uq"] = mm(sv["qn"], dq, mode="tn", out_dtypes=(BF16,), name=f"{lname}_d_uq")
    dqn = mm(dq, w["uq"], mode="nt", name=f"{lname}_d_qn")
    dk_nope = _from_head(dkh[:, :, :QK_NOPE], (QK_NOPE,))[0]
    dk1, dk2 = rope_pair_headsum(dkh[:, :, QK_NOPE:QK_NOPE + HALF_ROPE], dkh[:, :, QK_NOPE + HALF_ROPE:],
                                 cos_k, sin_k, name=f"{lname}_d_rope_k")
    dkv = jnp.concatenate([dk_nope.astype(BF16), dv], axis=1)
    gw["ukv"] = mm(sv["kvn"], dkv, mode="tn", out_dtypes=(BF16,), name=f"{lname}_d_ukv")
    dkvn = mm(dkv, w["ukv"], mode="nt", name=f"{lname}_d_kvn")
    dc_q, gr["norm_qa"] = rmsnorm_bwd(sv["proj"], g["norm_qa"], dqn, width=dm.q_lora, col=0, out_dtype=BF16,
                                      name=f"{lname}_d_norm_qa")
    dc_kv, gr["norm_kva"] = rmsnorm_bwd(sv["proj"], g["norm_kva"], dkvn, width=dm.kv_lora, col=1, out_dtype=BF16,
                                        name=f"{lname}_d_norm_kva")
    dproj = jnp.concatenate([dc_q, dc_kv, dq_na, dk_na, dv_na, dga, dgb], axis=1)
    dkpe = jnp.concatenate([dk1, dk2, jnp.zeros((dm.s, LANES - QK_ROPE), BF16)], axis=1)
    gw["in_main"] = mm(sv["u"], dproj, mode="tn", out_dtypes=(BF16,), name=f"{lname}_d_in")
    gw["in_kpe"] = mm(sv["u"], dkpe, mode="tn", out_dtypes=(BF16,), name=f"{lname}_d_in_kpe")
    du_k = mm(dkpe, w["in_kpe"], mode="nt", name=f"{lname}_d_u_kpe")
    du = mm(dproj, w["in_main"], mode="nt", epi=lambda r, res: (r + res,), extras=(du_k,), name=f"{lname}_d_u")
    dx, gr["norm_mix"] = rmsnorm_bwd(sv["x"], g["norm_mix"], du, dx1, name=f"{lname}_d_norm_mix")
    return dx, gw, gr, received


EARLY = ("w_in", "w_uq", "w_ukv")
LATE = ("w_o_mla", "w_o_na", "w_out", "w_ff1", "w_ff2")
LATE_KEYS = ("o_mla", "o_na", "out", "ff1", "ff2")


def _early_weights(dm, full):
    main, kpe = _split_w_in(dm, full["w_in"])
    return dict(in_main=main, in_kpe=kpe,
                uq=jnp.concatenate(_split_heads(full["w_uq"], dm.mla_h, UQ_WIDTHS), axis=1),
                ukv=jnp.concatenate(_split_heads(full["w_ukv"], dm.mla_h, UKV_WIDTHS), axis=1))


def _early_grads(dm, gw):
    h = dm.mla_h
    nope_w, half_w = h * QK_NOPE, h * HALF_ROPE
    uq = gw["uq"]
    ukv = gw["ukv"]
    return {"w_in": _join_w_in(dm, gw["in_main"], gw["in_kpe"]),
            "w_uq": _join_heads([uq[:, :nope_w], uq[:, nope_w:nope_w + half_w], uq[:, nope_w + half_w:]], h),
            "w_ukv": _join_heads([ukv[:, :nope_w], ukv[:, nope_w:]], h)}


def _pack_replicated(dm, parts):
    flat = jnp.concatenate([parts[n].reshape(-1) for n in REPLICATED])
    n = flat.shape[0]
    rows_ = -(-n // (8 * LANES)) * 8
    return jnp.pad(flat, (0, rows_ * LANES - n)).reshape(rows_, LANES)


def kernel(x, norm_mix, w_in, norm_qa, w_uq, norm_kva, w_ukv, rpb, w_o_mla, w_o_na, w_out, norm_mlp, w_ff1, w_ff2, norm_final, loss_target, m_norm_mix, m_w_in, m_norm_qa, m_w_uq, m_norm_kva, m_w_ukv, m_rpb, m_w_o_mla, m_w_o_na, m_w_out, m_norm_mlp, m_w_ff1, m_w_ff2, m_norm_final, v_norm_mix, v_w_in, v_norm_qa, v_w_uq, v_norm_kva, v_w_ukv, v_rpb, v_w_o_mla, v_w_o_na, v_w_out, v_norm_mlp, v_w_ff1, v_w_ff2, v_norm_final):
    dm = Dims(x, w_in, norm_qa, norm_kva, rpb, w_o_mla, w_o_na, w_ff1)
    params = dict(norm_mix=norm_mix, w_in=w_in, norm_qa=norm_qa, w_uq=w_uq, norm_kva=norm_kva, w_ukv=w_ukv, rpb=rpb,
                  w_o_mla=w_o_mla, w_o_na=w_o_na, w_out=w_out, norm_mlp=norm_mlp, w_ff1=w_ff1, w_ff2=w_ff2,
                  norm_final=norm_final)
    mom_m = dict(norm_mix=m_norm_mix, w_in=m_w_in, norm_qa=m_norm_qa, w_uq=m_w_uq, norm_kva=m_norm_kva, w_ukv=m_w_ukv,
                 rpb=m_rpb, w_o_mla=m_w_o_mla, w_o_na=m_w_o_na, w_out=m_w_out, norm_mlp=m_norm_mlp, w_ff1=m_w_ff1,
                 w_ff2=m_w_ff2, norm_final=m_norm_final)
    mom_v = dict(norm_mix=v_norm_mix, w_in=v_w_in, norm_qa=v_norm_qa, w_uq=v_w_uq, norm_kva=v_norm_kva, w_ukv=v_w_ukv,
                 rpb=v_rpb, w_o_mla=v_w_o_mla, w_o_na=v_w_o_na, w_out=v_w_out, norm_mlp=v_norm_mlp, w_ff1=v_w_ff1,
                 w_ff2=v_w_ff2, norm_final=v_norm_final)
    depth, s, h = dm.depth, dm.s, dm.mla_h

    pos = jnp.arange(s, dtype=F32)
    inv_freq = 1.0 / (ROPE_THETA ** (jnp.arange(0, QK_ROPE, 2, dtype=F32) / QK_ROPE))
    ang = pos[:, None] * inv_freq[None, :]
    cos_k, sin_k = jnp.cos(ang), jnp.sin(ang)
    cos_q, sin_q = jnp.tile(cos_k, (1, h)), jnp.tile(sin_k, (1, h))

    part = {n: part_of(n, params[n].shape[1:]) for n in SHARDED}
    shard = lambda n, l: params[n][l].astype(BF16)
    whole = lambda names, arrays: {n: _whole(part[n], a) for n, a in zip(names, arrays)}
    gains = [dict(norm_mix=norm_mix[l][None], norm_qa=norm_qa[l][None], norm_kva=norm_kva[l][None],
                  norm_mlp=norm_mlp[l][None], rpb=rpb[l]) for l in range(depth)]

    xl = x[0]
    saved, weights = [], []
    early = Collective("gather", [part[n] for n in EARLY], [shard(n, 0) for n in EARLY]).run("gather_weights")
    for l in range(depth):
        nxt = EARLY if l + 1 < depth else ()
        ride = Collective("gather", [part[n] for n in LATE + nxt],
                          [shard(n, l) for n in LATE] + [shard(n, l + 1) for n in nxt])
        late_weights = lambda got: dict(zip(LATE_KEYS, whole(LATE, got[:len(LATE)]).values()))
        xl, sv, w, got = layer_fwd(dm, "fwd", xl, _early_weights(dm, whole(EARLY, early)), gains[l], cos_q, sin_q,
                                   cos_k, sin_k, ride, late_weights)
        early = got[len(LATE):]
        saved.append(sv)
        weights.append(w)
    dx, g_final, loss_part = loss_head(xl, norm_final[None], loss_target[0], name="loss_head")
    loss = lax.psum(loss_part[0, 0], MESH_AXES)

    rep = {n: [None] * depth for n in REPLICATED if n != "norm_final"}
    recv = [{} for _ in range(depth)]
    pending = {}
    for l in reversed(range(depth)):
        def ride_of(gw, pending=pending):
            ready = {**dict(zip(LATE, [gw[k] for k in LATE_KEYS])), **pending}
            return Collective("exchange", [part[n] for n in ready], [_for_exchange(part[n], a) for n, a in ready.items()])

        dx, gw, gr, received = layer_bwd(dm, "bwd", dx, weights[l], gains[l], saved[l], cos_q, sin_q, cos_k, sin_k,
                                         ride_of)
        recv[l].update(zip(LATE, received[:len(LATE)]))
        if pending:
            recv[l + 1].update(zip(EARLY, received[len(LATE):]))
        for n in gr:
            rep[n][l] = gr[n]
        pending = _early_grads(dm, gw)
    last = Collective("exchange", [part[n] for n in EARLY], [_for_exchange(part[n], pending[n]) for n in EARLY])
    recv[0].update(zip(EARLY, last.run("scatter_grads")))
    rep_parts = {n: jnp.stack(rep[n]) for n in rep}
    rep_parts["norm_final"] = g_final
    packed = _pack_replicated(dm, rep_parts)
    rep_all = Collective("gather", [Part(packed.shape, "packed", F32)], [packed]).run("gather_small_grads")[0]

    outs = {}
    for n in SHARDED:
        outs[n] = adamw_layers(params[n], mom_m[n], mom_v[n], [recv[l][n] for l in range(depth)], name=f"adamw_{n}")
    n_rep = sum(int(np.prod(params[n].shape)) for n in REPLICATED)
    pack = lambda d: _pack_replicated(dm, d)
    res = adamw(pack(params), pack(mom_m), pack(mom_v), rep_all, name="adamw_replicated")
    off = 0
    for n in REPLICATED:
        size = int(np.prod(params[n].shape))
        outs[n] = [a.reshape(-1)[off:off + size].reshape(params[n].shape) for a in res]
        off += size
    assert off == n_rep

    grad_x = dx[None]
    return (loss, grad_x, *[outs[n][0] for n in WEIGHTS], *[outs[n][1] for n in WEIGHTS],
            *[outs[n][2] for n in WEIGHTS], *[outs[n][3] for n in WEIGHTS])
```

```python
import functools

import numpy as np
import jax
import jax.numpy as jnp
from jax import lax
from jax.experimental import pallas as pl
from jax.experimental.pallas import tpu as pltpu

F32 = jnp.float32
BF16 = jnp.bfloat16
MESH_AXES = ("x", "y", "c")
N_DEV = 8
LANES = 128

QK_NOPE = 128
QK_ROPE = 64
HALF_ROPE = QK_ROPE // 2
V_HEAD = 128
QK_DIM = QK_NOPE + QK_ROPE
NA_HEAD_DIM = 128
GRID_W = 64
NA_KH = 8
NA_KW = 16
NA_SPAN_ROWS = 2 * NA_KH
NA_SPAN = NA_SPAN_ROWS * GRID_W
NA_PAD = NA_KH * GRID_W
NA_SPAN_OFF = NA_PAD - (NA_KH - 1) * GRID_W
NA_FWD_AHEAD, NA_BWD_AHEAD = 3, 1
ROPE_THETA = 10000.0
EPS = 1e-6
NEG = -1e30

ADAM_LR = 0.001
ADAM_B1 = 0.9
ADAM_B2 = 0.999
ADAM_EPS = 1e-08
ADAM_WD = 0.01
ADAM_STEP = 10

VMEM_LIMIT_V7X = 56 * 1024 * 1024
ADAMW_BLOCK_BYTES = 24 * 1024 * 1024

NN = (((1,), (0,)), ((), ()))
NT = (((1,), (1,)), ((), ()))
TN = (((0,), (0,)), ((), ()))


def _pick(dim, target, align=LANES):
    if dim <= target:
        return dim
    t = (target // align) * align
    while t >= align:
        if dim % t == 0:
            return t
        t -= align
    return dim


def _params(sem):
    return pltpu.CompilerParams(dimension_semantics=sem, vmem_limit_bytes=VMEM_LIMIT_V7X)


def mm(a, b, *, mode, name, out_dtypes=(F32,), epi=None, extras=(), tm=1024, tn=1024, tk=2048, exact=False):
    if mode == "nn":
        (m, k), (k2, n) = a.shape, b.shape
    elif mode == "nt":
        (m, k), (n, k2) = a.shape, b.shape
    else:
        (k, m), (k2, n) = a.shape, b.shape
    assert k == k2, (a.shape, b.shape, mode)
    tm, tn, tk = _pick(m, tm), _pick(n, tn), _pick(k, tk)
    nk = k // tk
    a_spec = pl.BlockSpec((tk, tm), lambda i, j, s: (s, i)) if mode == "tn" else pl.BlockSpec((tm, tk), lambda i, j, s: (i, s))
    b_spec = pl.BlockSpec((tn, tk), lambda i, j, s: (j, s)) if mode == "nt" else pl.BlockSpec((tk, tn), lambda i, j, s: (s, j))
    tile = pl.BlockSpec((tm, tn), lambda i, j, s: (i, j))
    dims = {"nn": NN, "nt": NT, "tn": TN}[mode]
    n_extra, n_out = len(extras), len(out_dtypes)

    def product(a_ref, b_ref):
        if exact:
            return lax.dot_general(a_ref[...], b_ref[...], dims, precision=lax.Precision.HIGHEST,
                                   preferred_element_type=F32)
        return lax.dot_general(a_ref[...].astype(BF16), b_ref[...].astype(BF16), dims, preferred_element_type=F32)

    def finish(r, extra_refs, out_refs):
        res = (r,) if epi is None else epi(r, *[e[...] for e in extra_refs])
        for o, v in zip(out_refs, res):
            o[...] = v.astype(o.dtype)

    def body_one_step(a_ref, b_ref, *rest):
        finish(product(a_ref, b_ref), rest[:n_extra], rest[n_extra:])

    def body(a_ref, b_ref, *rest):
        extra_refs, out_refs, acc = rest[:n_extra], rest[n_extra:n_extra + n_out], rest[-1]
        step = pl.program_id(2)

        @pl.when(step == 0)
        def _():
            acc[...] = product(a_ref, b_ref)

        @pl.when(step > 0)
        def _():
            acc[...] += product(a_ref, b_ref)

        @pl.when(step == nk - 1)
        def _():
            finish(acc[...], extra_refs, out_refs)

    outs = pl.pallas_call(
        body_one_step if nk == 1 else body, name=name, grid=(m // tm, n // tn, nk),
        in_specs=[a_spec, b_spec] + [tile] * n_extra,
        out_specs=[tile] * n_out,
        out_shape=[jax.ShapeDtypeStruct((m, n), d) for d in out_dtypes],
        scratch_shapes=[] if nk == 1 else [pltpu.VMEM((tm, tn), F32)],
        compiler_params=_params(("parallel", "parallel", "arbitrary")),
    )(a, b, *extras)
    return outs[0] if n_out == 1 else outs


def blockwise(fn, ins, outs, *, grid, name, sums=()):
    n_in, n_axes = len(ins), len(grid)

    def body(*refs):
        res = fn(*[r[...] for r in refs[:n_in]])
        first = functools.reduce(jnp.logical_and, [pl.program_id(ax) == 0 for ax in range(n_axes)])
        for idx, (o, v) in enumerate(zip(refs[n_in:], res)):
            if idx in sums:
                @pl.when(first)
                def _(o=o):
                    o[...] = jnp.zeros_like(o)

                o[...] += v.astype(o.dtype)
            else:
                o[...] = v.astype(o.dtype)

    sem = ("arbitrary" if sums else "parallel",) * n_axes
    res = pl.pallas_call(
        body, name=name, grid=grid,
        in_specs=[pl.BlockSpec(blk, imap) for _, blk, imap in ins],
        out_specs=[pl.BlockSpec(blk, imap) for _, _, blk, imap in outs],
        out_shape=[jax.ShapeDtypeStruct(shape, dt) for shape, dt, _, _ in outs],
        compiler_params=_params(sem),
    )(*[a for a, _, _ in ins])
    return res


def rows(arr, tr, width=None, col=0):
    width = arr.shape[1] if width is None else width
    return (arr, (tr, width), lambda i, col=col: (i, col))


def whole(arr):
    return (arr, arr.shape, lambda i: (0, 0))


def out_rows(n_rows, width, dtype, tr):
    return ((n_rows, width), dtype, (tr, width), lambda i: (i, 0))


def out_sum(shape, dtype=F32):
    return (shape, dtype, shape, lambda i: (0, 0))


def _rstd(x):
    return lax.rsqrt(jnp.mean(x * x, axis=-1, keepdims=True) + EPS)


def _colsum(v):
    return jnp.sum(v, axis=0, keepdims=True)


def rmsnorm_fwd(x, g, *, width=None, col=0, tr=256, name):
    n = x.shape[0]
    tr = _pick(n, tr, 8)
    width = x.shape[1] if width is None else width

    def fn(xv, gv):
        return ((xv * _rstd(xv)) * gv,)

    return blockwise(fn, [rows(x, tr, width, col), whole(g)], [out_rows(n, width, BF16, tr)],
                     grid=(n // tr,), name=name)[0]


def rmsnorm_bwd(x, g, dy, res=None, *, width=None, col=0, out_dtype=F32, tr=256, name):
    n = x.shape[0]
    tr = _pick(n, tr, 8)
    width = x.shape[1] if width is None else width

    def fn(xv, gv, dyv, *resv):
        dyv = dyv.astype(F32)
        r = _rstd(xv)
        xhat = xv * r
        dxhat = dyv * gv
        dx = r * (dxhat - xhat * jnp.mean(dxhat * xhat, axis=-1, keepdims=True))
        if resv:
            dx = dx + resv[0]
        return dx, _colsum(dyv * xhat)

    ins = [rows(x, tr, width, col), whole(g), rows(dy, tr)] + ([rows(res, tr)] if res is not None else [])
    return blockwise(fn, ins, [out_rows(n, width, out_dtype, tr), out_sum((1, width))],
                     grid=(n // tr,), name=name, sums=(1,))


def rope_pair(x1, x2, cos, sin, *, tr=512, name):
    n, w = x1.shape
    tr = _pick(n, tr, 8)

    def fn(a, b, c, s):
        a, b = a.astype(F32), b.astype(F32)
        return a * c - b * s, b * c + a * s

    return blockwise(fn, [rows(x1, tr), rows(x2, tr), rows(cos, tr), rows(sin, tr)],
                     [out_rows(n, w, BF16, tr), out_rows(n, w, BF16, tr)], grid=(n // tr,), name=name)


def rope_pair_headsum(d1, d2, cos, sin, *, tr=512, name):
    h, n, w = d1.shape
    tr = _pick(n, tr, 8)

    def fn(a, b, c, s):
        a, b = jnp.sum(a.astype(F32), axis=0), jnp.sum(b.astype(F32), axis=0)
        return a * c + b * s, b * c - a * s

    lead = lambda arr: (arr, (h, tr, w), lambda i: (0, i, 0))
    return blockwise(fn, [lead(d1), lead(d2), rows(cos, tr), rows(sin, tr)],
                     [out_rows(n, w, BF16, tr), out_rows(n, w, BF16, tr)], grid=(n // tr,), name=name)


def gate_fwd(proj, y_a, y_b, *, d, ga_col, gb_col, tr=256, name):
    n = proj.shape[0]
    tr = _pick(n, tr, 8)

    def fn(ga, gb, ya, yb):
        return (jax.nn.sigmoid(ga) * ya + jax.nn.sigmoid(gb) * yb,)

    return blockwise(fn, [rows(proj, tr, d, ga_col), rows(proj, tr, d, gb_col), rows(y_a, tr), rows(y_b, tr)],
                     [out_rows(n, d, BF16, tr)], grid=(n // tr,), name=name)[0]


def gate_bwd(proj, y_a, y_b, dmerged, *, d, ga_col, gb_col, tr=256, name):
    n = proj.shape[0]
    tr = _pick(n, tr, 8)

    def fn(ga, gb, ya, yb, dm):
        sa, sb = jax.nn.sigmoid(ga), jax.nn.sigmoid(gb)
        return dm * sa, dm * sb, dm * ya * (sa * (1.0 - sa)), dm * yb * (sb * (1.0 - sb))

    return blockwise(fn, [rows(proj, tr, d, ga_col), rows(proj, tr, d, gb_col), rows(y_a, tr), rows(y_b, tr),
                          rows(dmerged, tr)],
                     [out_rows(n, d, BF16, tr)] * 4, grid=(n // tr,), name=name)


def loss_head(x, g, target, *, tr=256, name):
    n, d = x.shape
    tr = _pick(n, tr, 8)

    def fn(xv, gv, tv):
        r = _rstd(xv)
        xhat = xv * r
        diff = xhat * gv - tv
        loss = 0.5 * jnp.sum(jnp.sum(diff * diff, axis=-1, keepdims=True) / d, axis=0, keepdims=True)
        dy = diff / d
        dxhat = dy * gv
        dx = r * (dxhat - xhat * jnp.mean(dxhat * xhat, axis=-1, keepdims=True))
        return dx, _colsum(dy * xhat), jnp.broadcast_to(loss, (8, LANES))

    return blockwise(fn, [rows(x, tr), whole(g), rows(target, tr)],
                     [out_rows(n, d, F32, tr), out_sum((1, d)), out_sum((8, LANES))],
                     grid=(n // tr,), name=name, sums=(1, 2))


def _adamw_math(wv, mv, vv, gs):
    c1 = 1.0 / (1.0 - ADAM_B1 ** ADAM_STEP)
    c2 = 1.0 / (1.0 - ADAM_B2 ** ADAM_STEP)
    g = gs[0].astype(F32)
    for dev in range(1, N_DEV):
        g = g + gs[dev].astype(F32)
    m_new = ADAM_B1 * mv + (1.0 - ADAM_B1) * g
    v_new = ADAM_B2 * vv + (1.0 - ADAM_B2) * (g * g)
    delta = -ADAM_LR * ((m_new * c1) / (jnp.sqrt(v_new * c2) + ADAM_EPS) + ADAM_WD * wv)
    return g, delta, m_new, v_new


def adamw(w, m, v, g_slots, *, tr=256, name):
    n, c = w.shape
    tr = _pick(n, tr, 8)
    slots = (g_slots, (N_DEV, tr, c), lambda i: (0, i, 0))
    return blockwise(_adamw_math, [rows(w, tr), rows(m, tr), rows(v, tr), slots],
                     [out_rows(n, c, F32, tr)] * 4, grid=(n // tr,), name=name)


def adamw_layers(w, m, v, g_layers, *, name):
    depth, r, c = w.shape
    row_bytes = c * 2 * (depth * N_DEV * g_layers[0].dtype.itemsize + 7 * 4)
    tr = _pick(r, max(8, ADAMW_BLOCK_BYTES // row_bytes // 8 * 8), 8)

    def body(w_ref, m_ref, v_ref, *rest):
        g_refs, out_refs = rest[:depth], rest[depth:]
        for layer in range(depth):
            @pl.when(pl.program_id(0) == layer)
            def _(layer=layer):
                for o, val in zip(out_refs, _adamw_math(w_ref[...], m_ref[...], v_ref[...], g_refs[layer][...])):
                    o[...] = val

    blk = pl.BlockSpec((None, tr, c), lambda l, i: (l, i, 0))
    g_specs = [pl.BlockSpec((N_DEV, tr, c), lambda l, i, layer=layer: (0, jnp.where(l == layer, i, 0), 0))
               for layer in range(depth)]
    return pl.pallas_call(
        body, name=name, grid=(depth, r // tr), in_specs=[blk] * 3 + g_specs, out_specs=[blk] * 4,
        out_shape=[jax.ShapeDtypeStruct(w.shape, F32)] * 4, compiler_params=_params(("arbitrary", "arbitrary")),
    )(w, m, v, *g_layers)


LOG2E = 1.4426950408889634
MLA_FWD_SUB, MLA_FWD_BQ = 512, 2048
MLA_BWD_SUB, MLA_BWD_BQ = 1024, 2048
MLA_BK = 2048


def mla_fwd(q, k, vt, *, name, ride=None):
    h, s, dq = q.shape
    bq, bk = _pick(s, MLA_FWD_BQ), _pick(s, MLA_BK)
    sub = min(MLA_FWD_SUB, bq)
    nk = s // bk
    scale = QK_DIM ** -0.5
    c2 = scale * LOG2E
    grid = (h, s // bq, nk)

    def body(q_ref, k_ref, vt_ref, *rest):
        if ride is None:
            o_ref, lse_ref, m_s, l_s, acc = rest
        else:
            n = ride.n
            x_refs, (o_ref, lse_ref), g_refs = rest[:n], rest[n:n + 2], rest[n + 2:2 * n + 2]
            m_s, l_s, acc, *sems = rest[2 * n + 2:]
            first, middle, last = _grid_flags(grid)
            ride_start, ride_middle, ride_finish = ride.steps(x_refs, g_refs, sems)
            pl.when(first)(ride_start)
            pl.when(middle)(ride_middle)
        j = pl.program_id(2)

        @pl.when(j == 0)
        def _():
            m_s[...] = jnp.full_like(m_s, NEG)
            l_s[...] = jnp.zeros_like(l_s)
            acc[...] = jnp.zeros_like(acc)

        def scores(c0):
            return lax.dot_general(k_ref[...], q_ref[pl.ds(c0, sub), :], NT, preferred_element_type=F32)

        starts = list(range(0, bq, sub))
        st_next = scores(starts[0])
        for n, c0 in enumerate(starts):
            cols = pl.ds(c0, sub)
            st = st_next
            if n + 1 < len(starts):
                st_next = scores(starts[n + 1])
            m_prev = m_s[:, cols]
            m_new = jnp.maximum(m_prev, jnp.max(st, axis=0, keepdims=True))
            alpha = jnp.exp2((m_prev - m_new) * c2)
            pt = jnp.exp2((st - m_new) * c2)
            l_s[:, cols] = alpha * l_s[:, cols] + jnp.sum(pt, axis=0, keepdims=True)
            acc[:, cols] = alpha * acc[:, cols] + lax.dot_general(vt_ref[...], pt.astype(BF16), NN,
                                                                  preferred_element_type=F32)
            m_s[:, cols] = m_new

        @pl.when(j == nk - 1)
        def _():
            o_ref[...] = (acc[...] / l_s[...]).T
            lse_ref[...] = m_s[...] * scale + jnp.log(l_s[...])

        if ride is not None:
            pl.when(last)(ride_finish)

    in_specs = [pl.BlockSpec((None, bq, dq), lambda hh, i, j: (hh, i, 0)),
                pl.BlockSpec((None, bk, dq), lambda hh, i, j: (hh, j, 0)),
                pl.BlockSpec((None, V_HEAD, bk), lambda hh, i, j: (hh, 0, j))]
    out_specs = [pl.BlockSpec((bq, V_HEAD), lambda hh, i, j: (i, hh)),
                 pl.BlockSpec((None, 1, bq), lambda hh, i, j: (hh, 0, i))]
    out_shape = [jax.ShapeDtypeStruct((s, h * V_HEAD), F32), jax.ShapeDtypeStruct((h, 1, s), F32)]
    scratch = [pltpu.VMEM((1, bq), F32), pltpu.VMEM((1, bq), F32), pltpu.VMEM((V_HEAD, bq), F32)]
    args = (q, k, vt)
    sem = ("parallel", "parallel", "arbitrary")
    if ride is not None:
        in_specs, out_specs, out_shape, scratch, args = ride.extend(in_specs, out_specs, out_shape, scratch, args)
        sem = ("arbitrary",) * 3
    return pl.pallas_call(body, name=name, grid=grid, in_specs=in_specs, out_specs=out_specs, out_shape=out_shape,
                          scratch_shapes=scratch, compiler_params=_params(sem))(*args)


def mla_delta(do, o, h, *, tr=512, name):
    s = do.shape[0]
    tr = _pick(s, tr, 8)

    def fn(dov, ov):
        return (jnp.sum(dov.astype(F32) * ov, axis=-1, keepdims=True),)

    blk = lambda arr: (arr, (tr, V_HEAD), lambda hh, i: (i, hh))
    return blockwise(fn, [blk(do), blk(o)], [((h, s, 1), F32, (None, tr, 1), lambda hh, i: (hh, i, 0))],
                     grid=(h, s // tr), name=name)[0]


def mla_bwd(q, k, kt, kv, do, lse, delta, *, v_col0, name, ride=None):
    h, s, dq = q.shape
    bq, bk = _pick(s, MLA_BWD_BQ), _pick(s, MLA_BK)
    sub = min(MLA_BWD_SUB, bq)
    nq = s // bq
    scale = QK_DIM ** -0.5
    c2 = scale * LOG2E
    grid = (h, s // bk, nq)

    def body(q_ref, k_ref, kt_ref, v_ref, do_ref, lse_ref, delta_ref, *rest):
        if ride is None:
            dq_ref, dk_ref, dv_ref, dk_acc, dv_acc = rest
        else:
            n = ride.n
            x_refs, (dq_ref, dk_ref, dv_ref), g_refs = rest[:n], rest[n:n + 3], rest[n + 3:2 * n + 3]
            dk_acc, dv_acc, *sems = rest[2 * n + 3:]
            first, _, last = _grid_flags(grid)
            ride_start, _, ride_finish = ride.steps(x_refs, g_refs, sems)
            pl.when(first)(ride_start)
        j, i = pl.program_id(1), pl.program_id(2)

        @pl.when(i == 0)
        def _():
            dk_acc[...] = jnp.zeros_like(dk_acc)
            dv_acc[...] = jnp.zeros_like(dv_acc)

        def scores(c0):
            cols = pl.ds(c0, sub)
            return (lax.dot_general(k_ref[...], q_ref[cols, :], NT, preferred_element_type=F32),
                    lax.dot_general(v_ref[...], do_ref[cols, :], NT, preferred_element_type=F32))

        starts = list(range(0, bq, sub))
        nxt = scores(starts[0])
        for n, c0 in enumerate(starts):
            cols = pl.ds(c0, sub)
            st, dpt = nxt
            if n + 1 < len(starts):
                nxt = scores(starts[n + 1])
            q_sub, do_sub = q_ref[cols, :], do_ref[cols, :]
            pt = jnp.exp2(st * c2 - lse_ref[:, cols] * LOG2E)
            ds_b = ((pt * (dpt - delta_ref[:, cols])) * scale).astype(BF16)
            dv_acc[...] += lax.dot_general(pt.astype(BF16), do_sub, NN, preferred_element_type=F32)
            dk_acc[...] += lax.dot_general(ds_b, q_sub, NN, preferred_element_type=F32)
            dq_t = lax.dot_general(kt_ref[...], ds_b, NN, preferred_element_type=F32)

            @pl.when(j == 0)
            def _():
                dq_ref[i, :, cols] = dq_t

            @pl.when(j > 0)
            def _():
                dq_ref[i, :, cols] += dq_t

        @pl.when(i == nq - 1)
        def _():
            dk_ref[...] = dk_acc[...]
            dv_ref[...] = dv_acc[...].astype(dv_ref.dtype)

        if ride is not None:
            pl.when(last)(ride_finish)

    in_specs = [pl.BlockSpec((None, bq, dq), lambda hh, j, i: (hh, i, 0)),
                pl.BlockSpec((None, bk, dq), lambda hh, j, i: (hh, j, 0)),
                pl.BlockSpec((None, dq, bk), lambda hh, j, i: (hh, 0, j)),
                pl.BlockSpec((bk, V_HEAD), lambda hh, j, i: (j, v_col0 + hh)),
                pl.BlockSpec((bq, V_HEAD), lambda hh, j, i: (i, hh)),
                pl.BlockSpec((None, 1, bq), lambda hh, j, i: (hh, 0, i)),
                pl.BlockSpec((None, 1, bq), lambda hh, j, i: (hh, 0, i))]
    out_specs = [pl.BlockSpec((None, nq, dq, bq), lambda hh, j, i: (hh, 0, 0, 0)),
                 pl.BlockSpec((None, bk, dq), lambda hh, j, i: (hh, j, 0)),
                 pl.BlockSpec((bk, V_HEAD), lambda hh, j, i: (j, hh))]
    out_shape = [jax.ShapeDtypeStruct((h, nq, dq, bq), F32), jax.ShapeDtypeStruct((h, s, dq), F32),
                 jax.ShapeDtypeStruct((s, h * V_HEAD), BF16)]
    scratch = [pltpu.VMEM((bk, dq), F32), pltpu.VMEM((bk, V_HEAD), F32)]
    args = (q, k, kt, kv, do, lse, delta)
    sem = ("parallel", "arbitrary", "arbitrary")
    if ride is not None:
        in_specs, out_specs, out_shape, scratch, args = ride.extend(in_specs, out_specs, out_shape, scratch, args)
        sem = ("arbitrary",) * 3
    return pl.pallas_call(body, name=name, grid=grid, in_specs=in_specs, out_specs=out_specs, out_shape=out_shape,
                          scratch_shapes=scratch, compiler_params=_params(sem))(*args)


def _na_probs(sc, bias, r, n_rows):
    sc = sc * (NA_HEAD_DIM ** -0.5) + bias
    lo = jnp.clip(r - NA_KH // 2, 0, n_rows - NA_KH) - r + (NA_KH - 1)
    dy = lax.broadcasted_iota(jnp.int32, (1, NA_SPAN), 1) // GRID_W
    sc = jnp.where((dy >= lo) & (dy < lo + NA_KH), sc, NEG)
    p = jnp.exp(sc - jnp.max(sc, axis=-1, keepdims=True))
    return p / jnp.sum(p, axis=-1, keepdims=True)


def na_pad_cast(proj, *, col0, width, name):
    s = proj.shape[0]
    n_in = s // NA_PAD
    assert col0 % width == 0 and s % NA_PAD == 0

    def body(x_ref, o_ref):
        i = pl.program_id(0)
        inside = jnp.logical_and(i >= 1, i <= n_in)
        o_ref[...] = jnp.where(inside, x_ref[...], 0.0).astype(o_ref.dtype)

    return pl.pallas_call(
        body, name=name, grid=(n_in + 2,),
        in_specs=[pl.BlockSpec((NA_PAD, width), lambda i: (jnp.clip(i - 1, 0, n_in - 1), col0 // width))],
        out_specs=pl.BlockSpec((NA_PAD, width), lambda i: (i, 0)),
        out_shape=jax.ShapeDtypeStruct((s + 2 * NA_PAD, width), BF16),
        compiler_params=_params(("parallel",)),
    )(proj)


def na_fwd(proj, kp, vp, biasm, *, q_col0, v_col0=0, rows_per_step=8, name):
    s = proj.shape[0]
    sp = kp.shape[0]
    h = biasm.shape[0]
    hw = h * NA_HEAD_DIM
    n_rows = s // GRID_W
    rb = min(rows_per_step, n_rows)
    tq = rb * GRID_W

    def body(q_ref, k_ref, v_ref, b_ref, o_ref):
        i = pl.program_id(1)
        bias = b_ref[...]

        def span(rl):
            return pl.ds(pl.multiple_of((i * rb + rl) * GRID_W + NA_SPAN_OFF, GRID_W), NA_SPAN)

        def scores(rl):
            q_row = q_ref[pl.ds(rl * GRID_W, GRID_W), :].astype(BF16)
            return lax.dot_general(q_row, k_ref[span(rl), :], NT, preferred_element_type=F32)

        ahead = [scores(rl) for rl in range(min(NA_FWD_AHEAD, rb))]
        for rl in range(rb):
            sc = ahead.pop(0)
            if rl + NA_FWD_AHEAD < rb:
                ahead.append(scores(rl + NA_FWD_AHEAD))
            p = _na_probs(sc, bias, i * rb + rl, n_rows)
            o_ref[pl.ds(rl * GRID_W, GRID_W), :] = lax.dot_general(
                p.astype(BF16), v_ref[span(rl), :], NN, preferred_element_type=F32)

    return pl.pallas_call(
        body, name=name, grid=(h, n_rows // rb),
        in_specs=[pl.BlockSpec((tq, NA_HEAD_DIM), lambda hh, i: (i, q_col0 + hh)),
                  pl.BlockSpec((sp, NA_HEAD_DIM), lambda hh, i: (0, hh)),
                  pl.BlockSpec((sp, NA_HEAD_DIM), lambda hh, i: (0, v_col0 + hh)),
                  pl.BlockSpec((None, GRID_W, NA_SPAN), lambda hh, i: (hh, 0, 0))],
        out_specs=pl.BlockSpec((tq, NA_HEAD_DIM), lambda hh, i: (i, hh)),
        out_shape=jax.ShapeDtypeStruct((s, hw), F32),
        compiler_params=_params(("parallel", "arbitrary")),
    )(proj, kp, vp, biasm)


def na_bwd(proj, kp, vp, biasm, do, *, q_col0, v_col0=0, rows_per_step=8, name):
    s = proj.shape[0]
    sp = kp.shape[0]
    h = biasm.shape[0]
    hw = h * NA_HEAD_DIM
    n_rows = s // GRID_W
    rb = min(rows_per_step, n_rows)
    tq = rb * GRID_W
    scale = NA_HEAD_DIM ** -0.5

    def body(q_ref, k_ref, v_ref, b_ref, do_ref, dq_ref, dk_ref, dv_ref, db_ref):
        i = pl.program_id(1)

        @pl.when(i == 0)
        def _():
            dk_ref[...] = jnp.zeros_like(dk_ref)
            dv_ref[...] = jnp.zeros_like(dv_ref)
            db_ref[...] = jnp.zeros_like(db_ref)

        bias = b_ref[...]

        def span(rl):
            return pl.ds(pl.multiple_of((i * rb + rl) * GRID_W + NA_SPAN_OFF, GRID_W), NA_SPAN)

        def query(rl):
            rows_ = pl.ds(rl * GRID_W, GRID_W)
            return q_ref[rows_, :].astype(BF16), do_ref[rows_, :]

        def scores(rl):
            q_row, do_row = query(rl)
            return (lax.dot_general(q_row, k_ref[span(rl), :], NT, preferred_element_type=F32),
                    lax.dot_general(do_row, v_ref[span(rl), :], NT, preferred_element_type=F32))

        ahead = [scores(rl) for rl in range(min(NA_BWD_AHEAD, rb))]
        for rl in range(rb):
            sc, dp = ahead.pop(0)
            if rl + NA_BWD_AHEAD < rb:
                ahead.append(scores(rl + NA_BWD_AHEAD))
            q_row, do_row = query(rl)
            p = _na_probs(sc, bias, i * rb + rl, n_rows)
            ds = p * (dp - jnp.sum(dp * p, axis=-1, keepdims=True))
            db_ref[...] += ds
            ds_b = (ds * scale).astype(BF16)
            dq_ref[pl.ds(rl * GRID_W, GRID_W), :] = lax.dot_general(
                ds_b, k_ref[span(rl), :], NN, preferred_element_type=F32).astype(dq_ref.dtype)
            dk_ref[span(rl), :] += lax.dot_general(ds_b, q_row, TN, preferred_element_type=F32)
            dv_ref[span(rl), :] += lax.dot_general(p.astype(BF16), do_row, TN, preferred_element_type=F32)

    return pl.pallas_call(
        body, name=name, grid=(h, n_rows // rb),
        in_specs=[pl.BlockSpec((tq, NA_HEAD_DIM), lambda hh, i: (i, q_col0 + hh)),
                  pl.BlockSpec((sp, NA_HEAD_DIM), lambda hh, i: (0, hh)),
                  pl.BlockSpec((sp, NA_HEAD_DIM), lambda hh, i: (0, v_col0 + hh)),
                  pl.BlockSpec((None, GRID_W, NA_SPAN), lambda hh, i: (hh, 0, 0)),
                  pl.BlockSpec((tq, NA_HEAD_DIM), lambda hh, i: (i, hh))],
        out_specs=[pl.BlockSpec((tq, NA_HEAD_DIM), lambda hh, i: (i, hh)),
                   pl.BlockSpec((sp, NA_HEAD_DIM), lambda hh, i: (0, hh)),
                   pl.BlockSpec((sp, NA_HEAD_DIM), lambda hh, i: (0, hh)),
                   pl.BlockSpec((None, GRID_W, NA_SPAN), lambda hh, i: (hh, 0, 0))],
        out_shape=[jax.ShapeDtypeStruct((s, hw), BF16), jax.ShapeDtypeStruct((sp, hw), F32),
                   jax.ShapeDtypeStruct((sp, hw), F32), jax.ShapeDtypeStruct((h, GRID_W, NA_SPAN), F32)],
        compiler_params=_params(("parallel", "arbitrary")),
    )(proj, kp, vp, biasm, do)


def _na_tables():
    qc = np.arange(GRID_W)[:, None]
    kc = np.arange(GRID_W)[None, :]
    col_start = np.clip(qc - NA_KW // 2, 0, GRID_W - NA_KW)
    col_ok = (kc >= col_start) & (kc < col_start + NA_KW)
    dx = np.clip(kc - qc, -(NA_KW - 1), NA_KW - 1) + (NA_KW - 1)
    return col_ok, dx


def na_bias_table(rpb_l):
    col_ok, dx = _na_tables()
    t = rpb_l[:, :, dx]
    t = jnp.where(col_ok[None, None], t, NEG)
    t = jnp.pad(t, ((0, 0), (0, 1), (0, 0), (0, 0)), constant_values=NEG)
    return t.transpose(0, 2, 1, 3).reshape(rpb_l.shape[0], GRID_W, NA_SPAN)


def na_bias_grad(dbias, *, name):
    h = dbias.shape[0]
    _, dx = _na_tables()
    n_dx = 2 * NA_KW - 1
    onehot = np.zeros((GRID_W * GRID_W, LANES), np.float32)
    onehot[np.arange(GRID_W * GRID_W), dx.reshape(-1)] = 1.0
    t = dbias.reshape(h, GRID_W, NA_SPAN_ROWS, GRID_W)[:, :, :2 * NA_KH - 1]
    t = t.transpose(0, 2, 1, 3).reshape(h * (2 * NA_KH - 1), GRID_W * GRID_W)
    t = jnp.pad(t, ((0, (-t.shape[0]) % 8), (0, 0)))
    out = mm(t, jnp.asarray(onehot), mode="nn", name=name, exact=True, tk=1024)
    return out[:h * (2 * NA_KH - 1), :n_dx].reshape(h, 2 * NA_KH - 1, n_dx)


def _flip(v, bit):
    return 1 - v if bit else v


class Part:
    def __init__(self, shape, kind, dtype):
        self.r, self.c = shape
        self.kind, self.dtype = kind, dtype

    @property
    def whole_shape(self):
        return {"row": (N_DEV * self.r, self.c), "col": (self.r, N_DEV * self.c),
                "packed": (N_DEV, self.r, self.c)}[self.kind]

    @property
    def packed_shape(self):
        return (N_DEV, self.r, self.c)

    def shard_of(self, ref, j):
        if self.kind == "row":
            return ref.at[pl.ds(pl.multiple_of(j * self.r, 8), self.r), :]
        if self.kind == "col":
            return ref.at[:, pl.ds(pl.multiple_of(j * self.c, LANES), self.c)]
        return ref.at[j]


def comm_scratch(n_parts):
    n = n_parts * (N_DEV - 1)
    return [pltpu.SemaphoreType.DMA((n,)), pltpu.SemaphoreType.DMA((n,)), pltpu.SemaphoreType.DMA((n_parts,))]


def gather_plan(parts, x_refs, out_refs, send_sems, recv_sems, local_sems):
    x, y, c = lax.axis_index("x"), lax.axis_index("y"), lax.axis_index("c")
    me, sibling = (x, y, c), (x, y, 1 - c)
    chips = [(1 - x, y), (x, 1 - y), (1 - x, 1 - y)]

    def place(w, px, py, pc):
        return parts[w].shard_of(out_refs[w], 4 * px + 2 * py + pc)

    def copy(k, blk, to, own=False):
        return [pltpu.make_async_remote_copy(
            src_ref=x_refs[w] if own else place(w, *blk), dst_ref=place(w, *blk),
            send_sem=send_sems.at[w * (N_DEV - 1) + k], recv_sem=recv_sems.at[w * (N_DEV - 1) + k],
            device_id=to, device_id_type=pl.DeviceIdType.MESH) for w in range(len(parts))]

    def mine():
        return [pltpu.make_async_copy(x_refs[w], place(w, *me), local_sems.at[w]) for w in range(len(parts))]

    def first():
        return copy(0, me, sibling, own=True) + [cp for j, chip in enumerate(chips)
                                                 for cp in copy(1 + j, me, (*chip, c), own=True)]

    def passed(j):
        return copy(4 + j, (*chips[j], c), sibling)

    def start():
        for cp in mine() + first():
            cp.start()

    def forward():
        for j, chip in enumerate(chips):
            for cp in copy(1 + j, (*chip, c), me):
                cp.wait_recv()
            for cp in passed(j):
                cp.start()

    def finish():
        for cp in copy(0, sibling, me):
            cp.wait_recv()
        for j, chip in enumerate(chips):
            for cp in copy(4 + j, (*chip, 1 - c), me):
                cp.wait_recv()
        for cp in first() + [cp for j in range(len(chips)) for cp in passed(j)]:
            cp.wait_send()
        for cp in mine():
            cp.wait()

    return start, forward, finish


def exchange_plan(parts, x_refs, out_refs, send_sems, recv_sems, local_sems):
    x, y, c = lax.axis_index("x"), lax.axis_index("y"), lax.axis_index("c")
    me = 4 * x + 2 * y + c

    def peer_of(k):
        peer = (_flip(x, k & 4), _flip(y, k & 2), _flip(c, k & 1))
        return peer, 4 * peer[0] + 2 * peer[1] + peer[2]

    def copies(k, arriving):
        peer, theirs = peer_of(k)
        return [pltpu.make_async_remote_copy(
            src_ref=parts[w].shard_of(x_refs[w], me if arriving else theirs),
            dst_ref=out_refs[w].at[theirs if arriving else me],
            send_sem=send_sems.at[w * (N_DEV - 1) + k - 1], recv_sem=recv_sems.at[w * (N_DEV - 1) + k - 1],
            device_id=peer, device_id_type=pl.DeviceIdType.MESH) for w in range(len(parts))]

    def mine():
        return [pltpu.make_async_copy(parts[w].shard_of(x_refs[w], me), out_refs[w].at[me], local_sems.at[w])
                for w in range(len(parts))]

    def start():
        for cp in mine():
            cp.start()
        for k in range(1, N_DEV):
            for cp in copies(k, arriving=False):
                cp.start()

    def finish():
        for k in range(1, N_DEV):
            for cp in copies(k, arriving=True):
                cp.wait_recv()
        for k in range(1, N_DEV):
            for cp in copies(k, arriving=False):
                cp.wait_send()
        for cp in mine():
            cp.wait()

    return start, finish


class Collective:
    def __init__(self, kind, parts, arrays):
        self.kind, self.parts, self.arrays = kind, parts, list(arrays)
        self.n = len(parts)
        shapes = [p.whole_shape if kind == "gather" else p.packed_shape for p in parts]
        self.out_shape = [jax.ShapeDtypeStruct(s, p.dtype) for s, p in zip(shapes, parts)]

    def extend(self, in_specs, out_specs, out_shape, scratch, args):
        any_spec = pl.BlockSpec(memory_space=pl.ANY)
        return (in_specs + [any_spec] * self.n, out_specs + [any_spec] * self.n, out_shape + self.out_shape,
                scratch + comm_scratch(self.n), args + tuple(self.arrays))

    def steps(self, x_refs, out_refs, sems):
        plan = (gather_plan if self.kind == "gather" else exchange_plan)(self.parts, x_refs, out_refs, *sems)
        return plan[0], (plan[1] if len(plan) == 3 else None), plan[-1]

    def run(self, name):
        n = self.n

        def body(*refs):
            for step in self.steps(refs[:n], refs[n:2 * n], refs[2 * n:]):
                if step is not None:
                    step()

        in_specs, out_specs, out_shape, scratch, args = self.extend([], [], [], [], ())
        return pl.pallas_call(body, name=name, in_specs=in_specs, out_specs=out_specs, out_shape=out_shape,
                              scratch_shapes=scratch)(*args)


def _grid_flags(grid):
    ids = [pl.program_id(ax) for ax in range(len(grid))]
    inner_zero = functools.reduce(jnp.logical_and, [i == 0 for i in ids[1:]])
    first = jnp.logical_and(ids[0] == 0, inner_zero)
    middle = jnp.logical_and(ids[0] == grid[0] - 1, inner_zero)
    last = functools.reduce(jnp.logical_and, [i == g - 1 for i, g in zip(ids, grid)])
    return first, middle, last


SHARDED = ("w_in", "w_uq", "w_ukv", "w_o_mla", "w_o_na", "w_out", "w_ff1", "w_ff2")
ROW_SHARDED = ("w_out", "w_ff2")
TRANSPOSED = ("w_in",)
REPLICATED = ("norm_mix", "norm_qa", "norm_kva", "rpb", "norm_mlp", "norm_final")
WEIGHTS = ("norm_mix", "w_in", "norm_qa", "w_uq", "norm_kva", "w_ukv", "rpb", "w_o_mla", "w_o_na", "w_out",
           "norm_mlp", "w_ff1", "w_ff2", "norm_final")


def part_of(name, shard_shape):
    r, c = shard_shape
    kind = "row" if name in ROW_SHARDED + TRANSPOSED else ("col" if c % LANES == 0 else "packed")
    return Part((r, c), kind, BF16)


def _whole(part, gathered):
    return gathered.transpose(1, 0, 2).reshape(part.r, -1) if part.kind == "packed" else gathered


def _for_exchange(part, full):
    return full.reshape(part.r, N_DEV, part.c).transpose(1, 0, 2) if part.kind == "packed" else full


class Dims:
    def __init__(self, x, w_in, norm_qa, norm_kva, rpb, w_o_mla, w_o_na, w_ff1):
        self.s, self.d = x.shape[1], x.shape[2]
        self.depth = w_in.shape[0]
        self.q_lora, self.kv_lora = norm_qa.shape[1], norm_kva.shape[1]
        self.mla_w, self.na_w = w_o_mla.shape[1], w_o_na.shape[1]
        self.mla_h, self.na_h = self.mla_w // V_HEAD, self.na_w // NA_HEAD_DIM
        self.d_ff = w_ff1.shape[2] * N_DEV
        assert rpb.shape[1] == self.na_h and self.s % GRID_W == 0 and self.s // GRID_W >= NA_SPAN_ROWS
        self.in_lo = self.q_lora + self.kv_lora
        self.main_w = self.in_lo + 3 * self.na_w + 2 * self.d
        assert self.q_lora == self.kv_lora and self.in_lo % self.na_w == 0 and self.in_lo % LANES == 0
        assert (self.in_lo + 3 * self.na_w) % self.d == 0
        self.q_col0 = self.in_lo // NA_HEAD_DIM
        self.k_off = self.in_lo + self.na_w
        self.v_off = self.in_lo + 2 * self.na_w
        self.ga_col = (self.in_lo + 3 * self.na_w) // self.d
        self.gb_col = self.ga_col + 1


def _split_w_in(dm, wt):
    lo = dm.in_lo
    main = jnp.concatenate([wt[:lo], wt[lo + QK_ROPE:]], axis=0)
    kpe = jnp.pad(wt[lo:lo + QK_ROPE], ((0, LANES - QK_ROPE), (0, 0)))
    return main, kpe


def _join_w_in(dm, main, kpe):
    lo = dm.in_lo
    return jnp.concatenate([main[:lo], kpe[:QK_ROPE], main[lo:]], axis=0)


def _split_heads(w, h, widths):
    r = w.shape[0]
    w3 = w.reshape(r, h, sum(widths))
    out, o = [], 0
    for wd in widths:
        out.append(w3[:, :, o:o + wd].reshape(r, h * wd))
        o += wd
    return out


def _join_heads(parts, h):
    r = parts[0].shape[0]
    return jnp.concatenate([p.reshape(r, h, -1) for p in parts], axis=2).reshape(r, -1)


UQ_WIDTHS = (QK_NOPE, HALF_ROPE, HALF_ROPE)
UKV_WIDTHS = (QK_NOPE, V_HEAD)


def _by_head(parts, h):
    s = parts[0].shape[0]
    return jnp.concatenate([p.reshape(s, h, -1) for p in parts], axis=2).transpose(1, 0, 2)


def _from_head(t, widths):
    h, s, _ = t.shape
    t = t.transpose(1, 0, 2)
    out, o = [], 0
    for wd in widths:
        out.append(t[:, :, o:o + wd].reshape(s, h * wd))
        o += wd
    return out


def layer_fwd(dm, lname, x, w, g, cos_q, sin_q, cos_k, sin_k, ride, late_weights):
    s, h = dm.s, dm.mla_h
    u = rmsnorm_fwd(x, g["norm_mix"], name=f"{lname}_norm_mix")
    proj = mm(u, w["in_main"], mode="nt", name=f"{lname}_proj")
    kpe = mm(u, w["in_kpe"], mode="nt", name=f"{lname}_proj_kpe")
    qn = rmsnorm_fwd(proj, g["norm_qa"], width=dm.q_lora, col=0, name=f"{lname}_norm_qa")
    kvn = rmsnorm_fwd(proj, g["norm_kva"], width=dm.kv_lora, col=1, name=f"{lname}_norm_kva")
    q = mm(qn, w["uq"], mode="nn", name=f"{lname}_uq")
    kv = mm(kvn, w["ukv"], mode="nn", out_dtypes=(BF16,), name=f"{lname}_ukv")
    nope_w, half_w = h * QK_NOPE, h * HALF_ROPE
    q1, q2 = rope_pair(q[:, nope_w:nope_w + half_w], q[:, nope_w + half_w:], cos_q, sin_q, name=f"{lname}_rope_q")
    k1, k2 = rope_pair(kpe[:, :HALF_ROPE], kpe[:, HALF_ROPE:QK_ROPE], cos_k, sin_k, name=f"{lname}_rope_k")
    qh = _by_head([q[:, :nope_w].astype(BF16), q1, q2], h)
    kh = _by_head([kv[:, :nope_w], jnp.tile(k1, (1, h)), jnp.tile(k2, (1, h))], h)
    vt = kv[:, nope_w:].reshape(s, h, V_HEAD).transpose(1, 2, 0)
    o_a, lse, *gathered = mla_fwd(qh, kh, vt, name=f"{lname}_mla", ride=ride)
    w = {**w, **late_weights(gathered)}
    y_a = mm(o_a, w["o_mla"], mode="nn", name=f"{lname}_o_mla")

    kvp = na_pad_cast(proj, col0=dm.k_off, width=2 * dm.na_w, name=f"{lname}_na_pad")
    biasm = na_bias_table(g["rpb"])
    o_b = na_fwd(proj, kvp, kvp, biasm, q_col0=dm.q_col0, v_col0=dm.na_h, name=f"{lname}_na")
    y_b = mm(o_b, w["o_na"], mode="nn", name=f"{lname}_o_na")

    merged = gate_fwd(proj, y_a, y_b, d=dm.d, ga_col=dm.ga_col, gb_col=dm.gb_col, name=f"{lname}_gate")
    x1 = mm(merged, w["out"], mode="nn", epi=lambda r, res: (r + res,), extras=(x,), name=f"{lname}_out")
    u2 = rmsnorm_fwd(x1, g["norm_mlp"], name=f"{lname}_norm_mlp")
    hid, act = mm(u2, w["ff1"], mode="nn", out_dtypes=(F32, BF16),
                  epi=lambda r: (r, jnp.square(jnp.maximum(r, 0.0))), name=f"{lname}_ff1")
    x2 = mm(act, w["ff2"], mode="nn", epi=lambda r, res: (r + res,), extras=(x1,), name=f"{lname}_ff2")
    saved = dict(x=x, u=u, proj=proj, qn=qn, kvn=kvn, kv=kv, qh=qh, kh=kh, o_a=o_a, lse=lse, y_a=y_a, kvp=kvp,
                 biasm=biasm, o_b=o_b, y_b=y_b, merged=merged, x1=x1, u2=u2, hid=hid, act=act)
    return x2, saved, w, gathered


def layer_bwd(dm, lname, dx2, w, g, sv, cos_q, sin_q, cos_k, sin_k, ride_of):
    h = dm.mla_h
    gw, gr = {}, {}
    gw["ff2"] = mm(sv["act"], dx2, mode="tn", out_dtypes=(BF16,), name=f"{lname}_d_ff2")
    dh = mm(dx2, w["ff2"], mode="nt", out_dtypes=(BF16,), extras=(sv["hid"],),
            epi=lambda r, hv: (r * (2.0 * jnp.maximum(hv, 0.0)),), name=f"{lname}_d_act")
    gw["ff1"] = mm(sv["u2"], dh, mode="tn", out_dtypes=(BF16,), name=f"{lname}_d_ff1")
    du2 = mm(dh, w["ff1"], mode="nt", name=f"{lname}_d_u2")
    dx1, gr["norm_mlp"] = rmsnorm_bwd(sv["x1"], g["norm_mlp"], du2, dx2, name=f"{lname}_d_norm_mlp")
    gw["out"] = mm(sv["merged"], dx1, mode="tn", out_dtypes=(BF16,), name=f"{lname}_d_out")
    dmerged = mm(dx1, w["out"], mode="nt", name=f"{lname}_d_merged")
    dy_a, dy_b, dga, dgb = gate_bwd(sv["proj"], sv["y_a"], sv["y_b"], dmerged, d=dm.d, ga_col=dm.ga_col,
                                    gb_col=dm.gb_col, name=f"{lname}_d_gate")
    gw["o_na"] = mm(sv["o_b"], dy_b, mode="tn", out_dtypes=(BF16,), name=f"{lname}_d_o_na")
    do_b = mm(dy_b, w["o_na"], mode="nt", out_dtypes=(BF16,), name=f"{lname}_d_ob")
    dq_na, dkp, dvp, dbias = na_bwd(sv["proj"], sv["kvp"], sv["kvp"], sv["biasm"], do_b, q_col0=dm.q_col0,
                                    v_col0=dm.na_h, name=f"{lname}_d_na")
    gr["rpb"] = na_bias_grad(dbias, name=f"{lname}_d_rpb")
    dk_na = dkp[NA_PAD:NA_PAD + dm.s].astype(BF16)
    dv_na = dvp[NA_PAD:NA_PAD + dm.s].astype(BF16)
    gw["o_mla"] = mm(sv["o_a"], dy_a, mode="tn", out_dtypes=(BF16,), name=f"{lname}_d_o_mla")
    do_a = mm(dy_a, w["o_mla"], mode="nt", out_dtypes=(BF16,), name=f"{lname}_d_oa")
    delta = mla_delta(do_a, sv["o_a"], h, name=f"{lname}_d_mla_delta").reshape(h, 1, dm.s)
    dqt, dkh, dv, *received = mla_bwd(sv["qh"], sv["kh"], sv["kh"].transpose(0, 2, 1), sv["kv"], do_a, sv["lse"],
                                      delta, v_col0=h, name=f"{lname}_d_mla", ride=ride_of(gw))
    dqh = dqt.transpose(0, 1, 3, 2).reshape(h, dm.s, QK_DIM)
    dq_nope, dq1, dq2 = _from_head(dqh, UQ_WIDTHS)
    dq1, dq2 = rope_pair(dq1, dq2, cos_q, -sin_q, name=f"{lname}_d_rope_q")
    dq = jnp.concatenate([dq_nope.astype(BF16), dq1, dq2], axis=1)
    gw["uq"] = mm(sv["qn"], dq, mode="tn", out_dtypes=(BF16,), name=f"{lname}_d_uq")
    dqn = mm(dq, w["uq"], mode="nt", name=f"{lname}_d_qn")
    dk_nope = _from_head(dkh[:, :, :QK_NOPE], (QK_NOPE,))[0]
    dk1, dk2 = rope_pair_headsum(dkh[:, :, QK_NOPE:QK_NOPE + HALF_ROPE], dkh[:, :, QK_NOPE + HALF_ROPE:],
                                 cos_k, sin_k, name=f"{lname}_d_rope_k")
    dkv = jnp.concatenate([dk_nope.astype(BF16), dv], axis=1)
    gw["ukv"] = mm(sv["kvn"], dkv, mode="tn", out_dtypes=(BF16,), name=f"{lname}_d_ukv")
    dkvn = mm(dkv, w["ukv"], mode="nt", name=f"{lname}_d_kvn")
    dc_q, gr["norm_qa"] = rmsnorm_bwd(sv["proj"], g["norm_qa"], dqn, width=dm.q_lora, col=0, out_dtype=BF16,
                                      name=f"{lname}_d_norm_qa")
    dc_kv, gr["norm_kva"] = rmsnorm_bwd(sv["proj"], g["norm_kva"], dkvn, width=dm.kv_lora, col=1, out_dtype=BF16,
                                        name=f"{lname}_d_norm_kva")
    dproj = jnp.concatenate([dc_q, dc_kv, dq_na, dk_na, dv_na, dga, dgb], axis=1)
    dkpe = jnp.concatenate([dk1, dk2, jnp.zeros((dm.s, LANES - QK_ROPE), BF16)], axis=1)
    gw["in_main"] = mm(dproj, sv["u"], mode="tn", out_dtypes=(BF16,), name=f"{lname}_d_in")
    gw["in_kpe"] = mm(dkpe, sv["u"], mode="tn", out_dtypes=(BF16,), name=f"{lname}_d_in_kpe")
    du_k = mm(dkpe, w["in_kpe"], mode="nn", name=f"{lname}_d_u_kpe")
    du = mm(dproj, w["in_main"], mode="nn", epi=lambda r, res: (r + res,), extras=(du_k,), name=f"{lname}_d_u")
    dx, gr["norm_mix"] = rmsnorm_bwd(sv["x"], g["norm_mix"], du, dx1, name=f"{lname}_d_norm_mix")
    return dx, gw, gr, received


EARLY = ("w_in", "w_uq", "w_ukv")
LATE = ("w_o_mla", "w_o_na", "w_out", "w_ff1", "w_ff2")
LATE_KEYS = ("o_mla", "o_na", "out", "ff1", "ff2")


def _early_weights(dm, full):
    main, kpe = _split_w_in(dm, full["w_in"])
    return dict(in_main=main, in_kpe=kpe,
                uq=jnp.concatenate(_split_heads(full["w_uq"], dm.mla_h, UQ_WIDTHS), axis=1),
                ukv=jnp.concatenate(_split_heads(full["w_ukv"], dm.mla_h, UKV_WIDTHS), axis=1))


def _early_grads(dm, gw):
    h = dm.mla_h
    nope_w, half_w = h * QK_NOPE, h * HALF_ROPE
    uq = gw["uq"]
    ukv = gw["ukv"]
    return {"w_in": _join_w_in(dm, gw["in_main"], gw["in_kpe"]),
            "w_uq": _join_heads([uq[:, :nope_w], uq[:, nope_w:nope_w + half_w], uq[:, nope_w + half_w:]], h),
            "w_ukv": _join_heads([ukv[:, :nope_w], ukv[:, nope_w:]], h)}


def _pack_replicated(dm, parts):
    flat = jnp.concatenate([parts[n].reshape(-1) for n in REPLICATED])
    n = flat.shape[0]
    rows_ = -(-n // (8 * LANES)) * 8
    return jnp.pad(flat, (0, rows_ * LANES - n)).reshape(rows_, LANES)


def kernel(x, norm_mix, w_in, norm_qa, w_uq, norm_kva, w_ukv, rpb, w_o_mla, w_o_na, w_out, norm_mlp, w_ff1, w_ff2, norm_final, loss_target, m_norm_mix, m_w_in, m_norm_qa, m_w_uq, m_norm_kva, m_w_ukv, m_rpb, m_w_o_mla, m_w_o_na, m_w_out, m_norm_mlp, m_w_ff1, m_w_ff2, m_norm_final, v_norm_mix, v_w_in, v_norm_qa, v_w_uq, v_norm_kva, v_w_ukv, v_rpb, v_w_o_mla, v_w_o_na, v_w_out, v_norm_mlp, v_w_ff1, v_w_ff2, v_norm_final):
    dm = Dims(x, w_in, norm_qa, norm_kva, rpb, w_o_mla, w_o_na, w_ff1)
    params = dict(norm_mix=norm_mix, w_in=w_in, norm_qa=norm_qa, w_uq=w_uq, norm_kva=norm_kva, w_ukv=w_ukv, rpb=rpb,
                  w_o_mla=w_o_mla, w_o_na=w_o_na, w_out=w_out, norm_mlp=norm_mlp, w_ff1=w_ff1, w_ff2=w_ff2,
                  norm_final=norm_final)
    mom_m = dict(norm_mix=m_norm_mix, w_in=m_w_in, norm_qa=m_norm_qa, w_uq=m_w_uq, norm_kva=m_norm_kva, w_ukv=m_w_ukv,
                 rpb=m_rpb, w_o_mla=m_w_o_mla, w_o_na=m_w_o_na, w_out=m_w_out, norm_mlp=m_norm_mlp, w_ff1=m_w_ff1,
                 w_ff2=m_w_ff2, norm_final=m_norm_final)
    mom_v = dict(norm_mix=v_norm_mix, w_in=v_w_in, norm_qa=v_norm_qa, w_uq=v_w_uq, norm_kva=v_norm_kva, w_ukv=v_w_ukv,
                 rpb=v_rpb, w_o_mla=v_w_o_mla, w_o_na=v_w_o_na, w_out=v_w_out, norm_mlp=v_norm_mlp, w_ff1=v_w_ff1,
                 w_ff2=v_w_ff2, norm_final=v_norm_final)
    depth, s, h = dm.depth, dm.s, dm.mla_h

    pos = jnp.arange(s, dtype=F32)
    inv_freq = 1.0 / (ROPE_THETA ** (jnp.arange(0, QK_ROPE, 2, dtype=F32) / QK_ROPE))
    ang = pos[:, None] * inv_freq[None, :]
    cos_k, sin_k = jnp.cos(ang), jnp.sin(ang)
    cos_q, sin_q = jnp.tile(cos_k, (1, h)), jnp.tile(sin_k, (1, h))

    held = lambda d, n: d[n].transpose(0, 2, 1) if n in TRANSPOSED else d[n]
    part = {n: part_of(n, held(params, n).shape[1:]) for n in SHARDED}
    shard = lambda n, l: held(params, n)[l].astype(BF16)
    whole = lambda names, arrays: {n: _whole(part[n], a) for n, a in zip(names, arrays)}
    gains = [dict(norm_mix=norm_mix[l][None], norm_qa=norm_qa[l][None], norm_kva=norm_kva[l][None],
                  norm_mlp=norm_mlp[l][None], rpb=rpb[l]) for l in range(depth)]

    xl = x[0]
    saved, weights = [], []
    early = Collective("gather", [part[n] for n in EARLY], [shard(n, 0) for n in EARLY]).run("gather_weights")
    for l in range(depth):
        nxt = EARLY if l + 1 < depth else ()
        ride = Collective("gather", [part[n] for n in LATE + nxt],
                          [shard(n, l) for n in LATE] + [shard(n, l + 1) for n in nxt])
        late_weights = lambda got: dict(zip(LATE_KEYS, whole(LATE, got[:len(LATE)]).values()))
        xl, sv, w, got = layer_fwd(dm, "fwd", xl, _early_weights(dm, whole(EARLY, early)), gains[l], cos_q, sin_q,
                                   cos_k, sin_k, ride, late_weights)
        early = got[len(LATE):]
        saved.append(sv)
        weights.append(w)
    dx, g_final, loss_part = loss_head(xl, norm_final[None], loss_target[0], name="loss_head")
    loss = lax.psum(loss_part[0, 0], MESH_AXES)

    rep = {n: [None] * depth for n in REPLICATED if n != "norm_final"}
    recv = [{} for _ in range(depth)]
    pending = {}
    for l in reversed(range(depth)):
        def ride_of(gw, pending=pending):
            ready = {**dict(zip(LATE, [gw[k] for k in LATE_KEYS])), **pending}
            return Collective("exchange", [part[n] for n in ready], [_for_exchange(part[n], a) for n, a in ready.items()])

        dx, gw, gr, received = layer_bwd(dm, "bwd", dx, weights[l], gains[l], saved[l], cos_q, sin_q, cos_k, sin_k,
                                         ride_of)
        recv[l].update(zip(LATE, received[:len(LATE)]))
        if pending:
            recv[l + 1].update(zip(EARLY, received[len(LATE):]))
        for n in gr:
            rep[n][l] = gr[n]
        pending = _early_grads(dm, gw)
    last = Collective("exchange", [part[n] for n in EARLY], [_for_exchange(part[n], pending[n]) for n in EARLY])
    recv[0].update(zip(EARLY, last.run("scatter_grads")))
    rep_parts = {n: jnp.stack(rep[n]) for n in rep}
    rep_parts["norm_final"] = g_final
    packed = _pack_replicated(dm, rep_parts)
    rep_all = Collective("gather", [Part(packed.shape, "packed", F32)], [packed]).run("gather_small_grads")[0]

    outs = {}
    for n in SHARDED:
        res = adamw_layers(held(params, n), held(mom_m, n), held(mom_v, n), [recv[l][n] for l in range(depth)],
                           name=f"adamw_{n}")
        outs[n] = [a.transpose(0, 2, 1) for a in res] if n in TRANSPOSED else res
    n_rep = sum(int(np.prod(params[n].shape)) for n in REPLICATED)
    pack = lambda d: _pack_replicated(dm, d)
    res = adamw(pack(params), pack(mom_m), pack(mom_v), rep_all, name="adamw_replicated")
    off = 0
    for n in REPLICATED:
        size = int(np.prod(params[n].shape))
        outs[n] = [a.reshape(-1)[off:off + size].reshape(params[n].shape) for a in res]
        off += size
    assert off == n_rep

    grad_x = dx[None]
    return (loss, grad_x, *[outs[n][0] for n in WEIGHTS], *[outs[n][1] for n in WEIGHTS],
            *[outs[n][2] for n in WEIGHTS], *[outs[n][3] for n in WEIGHTS])
```

```python
import functools

import numpy as np
import jax
import jax.numpy as jnp
from jax import lax
from jax.experimental import pallas as pl
from jax.experimental.pallas import tpu as pltpu

F32 = jnp.float32
BF16 = jnp.bfloat16
MESH_AXES = ("x", "y", "c")
N_DEV = 8
LANES = 128

QK_NOPE = 128
QK_ROPE = 64
HALF_ROPE = QK_ROPE // 2
V_HEAD = 128
QK_DIM = QK_NOPE + QK_ROPE
NA_HEAD_DIM = 128
GRID_W = 64
NA_KH = 8
NA_KW = 16
NA_WIN = NA_KH * GRID_W
ROPE_THETA = 10000.0
EPS = 1e-6
NEG = -1e30

ADAM_LR = 0.001
ADAM_B1 = 0.9
ADAM_B2 = 0.999
ADAM_EPS = 1e-08
ADAM_WD = 0.01
ADAM_STEP = 10

VMEM_LIMIT_V7X = 56 * 1024 * 1024
ADAMW_BLOCK_BYTES = 24 * 1024 * 1024

NN = (((1,), (0,)), ((), ()))
NT = (((1,), (1,)), ((), ()))
TN = (((0,), (0,)), ((), ()))


def _pick(dim, target, align=LANES):
    if dim <= target:
        return dim
    t = (target // align) * align
    while t >= align:
        if dim % t == 0:
            return t
        t -= align
    return dim


def _params(sem):
    return pltpu.CompilerParams(dimension_semantics=sem, vmem_limit_bytes=VMEM_LIMIT_V7X)


def mm(a, b, *, mode, name, out_dtypes=(F32,), epi=None, extras=(), tm=1024, tn=1024, tk=2048, exact=False):
    if mode == "nn":
        (m, k), (k2, n) = a.shape, b.shape
    elif mode == "nt":
        (m, k), (n, k2) = a.shape, b.shape
    else:
        (k, m), (k2, n) = a.shape, b.shape
    assert k == k2, (a.shape, b.shape, mode)
    tm, tn, tk = _pick(m, tm), _pick(n, tn), _pick(k, tk)
    nk = k // tk
    a_spec = pl.BlockSpec((tk, tm), lambda i, j, s: (s, i)) if mode == "tn" else pl.BlockSpec((tm, tk), lambda i, j, s: (i, s))
    b_spec = pl.BlockSpec((tn, tk), lambda i, j, s: (j, s)) if mode == "nt" else pl.BlockSpec((tk, tn), lambda i, j, s: (s, j))
    tile = pl.BlockSpec((tm, tn), lambda i, j, s: (i, j))
    dims = {"nn": NN, "nt": NT, "tn": TN}[mode]
    n_extra, n_out = len(extras), len(out_dtypes)

    def product(a_ref, b_ref):
        if exact:
            return lax.dot_general(a_ref[...], b_ref[...], dims, precision=lax.Precision.HIGHEST,
                                   preferred_element_type=F32)
        return lax.dot_general(a_ref[...].astype(BF16), b_ref[...].astype(BF16), dims, preferred_element_type=F32)

    def finish(r, extra_refs, out_refs):
        res = (r,) if epi is None else epi(r, *[e[...] for e in extra_refs])
        for o, v in zip(out_refs, res):
            o[...] = v.astype(o.dtype)

    def body_one_step(a_ref, b_ref, *rest):
        finish(product(a_ref, b_ref), rest[:n_extra], rest[n_extra:])

    def body(a_ref, b_ref, *rest):
        extra_refs, out_refs, acc = rest[:n_extra], rest[n_extra:n_extra + n_out], rest[-1]
        step = pl.program_id(2)

        @pl.when(step == 0)
        def _():
            acc[...] = product(a_ref, b_ref)

        @pl.when(step > 0)
        def _():
            acc[...] += product(a_ref, b_ref)

        @pl.when(step == nk - 1)
        def _():
            finish(acc[...], extra_refs, out_refs)

    outs = pl.pallas_call(
        body_one_step if nk == 1 else body, name=name, grid=(m // tm, n // tn, nk),
        in_specs=[a_spec, b_spec] + [tile] * n_extra,
        out_specs=[tile] * n_out,
        out_shape=[jax.ShapeDtypeStruct((m, n), d) for d in out_dtypes],
        scratch_shapes=[] if nk == 1 else [pltpu.VMEM((tm, tn), F32)],
        compiler_params=_params(("parallel", "parallel", "arbitrary")),
    )(a, b, *extras)
    return outs[0] if n_out == 1 else outs


def blockwise(fn, ins, outs, *, grid, name, sums=()):
    n_in, n_axes = len(ins), len(grid)

    def body(*refs):
        res = fn(*[r[...] for r in refs[:n_in]])
        first = functools.reduce(jnp.logical_and, [pl.program_id(ax) == 0 for ax in range(n_axes)])
        for idx, (o, v) in enumerate(zip(refs[n_in:], res)):
            if idx in sums:
                @pl.when(first)
                def _(o=o):
                    o[...] = jnp.zeros_like(o)

                o[...] += v.astype(o.dtype)
            else:
                o[...] = v.astype(o.dtype)

    sem = ("arbitrary" if sums else "parallel",) * n_axes
    res = pl.pallas_call(
        body, name=name, grid=grid,
        in_specs=[pl.BlockSpec(blk, imap) for _, blk, imap in ins],
        out_specs=[pl.BlockSpec(blk, imap) for _, _, blk, imap in outs],
        out_shape=[jax.ShapeDtypeStruct(shape, dt) for shape, dt, _, _ in outs],
        compiler_params=_params(sem),
    )(*[a for a, _, _ in ins])
    return res


def rows(arr, tr, width=None, col=0):
    width = arr.shape[1] if width is None else width
    return (arr, (tr, width), lambda i, col=col: (i, col))


def whole(arr):
    return (arr, arr.shape, lambda i: (0, 0))


def out_rows(n_rows, width, dtype, tr):
    return ((n_rows, width), dtype, (tr, width), lambda i: (i, 0))


def out_sum(shape, dtype=F32):
    return (shape, dtype, shape, lambda i: (0, 0))


def _rstd(x):
    return lax.rsqrt(jnp.mean(x * x, axis=-1, keepdims=True) + EPS)


def _colsum(v):
    return jnp.sum(v, axis=0, keepdims=True)


def rmsnorm_fwd(x, g, *, width=None, col=0, tr=256, name):
    n = x.shape[0]
    tr = _pick(n, tr, 8)
    width = x.shape[1] if width is None else width

    def fn(xv, gv):
        return ((xv * _rstd(xv)) * gv,)

    return blockwise(fn, [rows(x, tr, width, col), whole(g)], [out_rows(n, width, BF16, tr)],
                     grid=(n // tr,), name=name)[0]


def rmsnorm_bwd(x, g, dy, res=None, *, width=None, col=0, out_dtype=F32, tr=256, name):
    n = x.shape[0]
    tr = _pick(n, tr, 8)
    width = x.shape[1] if width is None else width

    def fn(xv, gv, dyv, *resv):
        dyv = dyv.astype(F32)
        r = _rstd(xv)
        xhat = xv * r
        dxhat = dyv * gv
        dx = r * (dxhat - xhat * jnp.mean(dxhat * xhat, axis=-1, keepdims=True))
        if resv:
            dx = dx + resv[0]
        return dx, _colsum(dyv * xhat)

    ins = [rows(x, tr, width, col), whole(g), rows(dy, tr)] + ([rows(res, tr)] if res is not None else [])
    return blockwise(fn, ins, [out_rows(n, width, out_dtype, tr), out_sum((1, width))],
                     grid=(n // tr,), name=name, sums=(1,))


def rope_pair(x1, x2, cos, sin, *, tr=512, name):
    n, w = x1.shape
    tr = _pick(n, tr, 8)

    def fn(a, b, c, s):
        a, b = a.astype(F32), b.astype(F32)
        return a * c - b * s, b * c + a * s

    return blockwise(fn, [rows(x1, tr), rows(x2, tr), rows(cos, tr), rows(sin, tr)],
                     [out_rows(n, w, BF16, tr), out_rows(n, w, BF16, tr)], grid=(n // tr,), name=name)


def rope_pair_headsum(d1, d2, cos, sin, *, tr=512, name):
    h, n, w = d1.shape
    tr = _pick(n, tr, 8)

    def fn(a, b, c, s):
        a, b = jnp.sum(a.astype(F32), axis=0), jnp.sum(b.astype(F32), axis=0)
        return a * c + b * s, b * c - a * s

    lead = lambda arr: (arr, (h, tr, w), lambda i: (0, i, 0))
    return blockwise(fn, [lead(d1), lead(d2), rows(cos, tr), rows(sin, tr)],
                     [out_rows(n, w, BF16, tr), out_rows(n, w, BF16, tr)], grid=(n // tr,), name=name)


def gate_fwd(proj, y_a, y_b, *, d, ga_col, gb_col, tr=256, name):
    n = proj.shape[0]
    tr = _pick(n, tr, 8)

    def fn(ga, gb, ya, yb):
        return (jax.nn.sigmoid(ga) * ya + jax.nn.sigmoid(gb) * yb,)

    return blockwise(fn, [rows(proj, tr, d, ga_col), rows(proj, tr, d, gb_col), rows(y_a, tr), rows(y_b, tr)],
                     [out_rows(n, d, BF16, tr)], grid=(n // tr,), name=name)[0]


def gate_bwd(proj, y_a, y_b, dmerged, *, d, ga_col, gb_col, tr=256, name):
    n = proj.shape[0]
    tr = _pick(n, tr, 8)

    def fn(ga, gb, ya, yb, dm):
        sa, sb = jax.nn.sigmoid(ga), jax.nn.sigmoid(gb)
        return dm * sa, dm * sb, dm * ya * (sa * (1.0 - sa)), dm * yb * (sb * (1.0 - sb))

    return blockwise(fn, [rows(proj, tr, d, ga_col), rows(proj, tr, d, gb_col), rows(y_a, tr), rows(y_b, tr),
                          rows(dmerged, tr)],
                     [out_rows(n, d, BF16, tr)] * 4, grid=(n // tr,), name=name)


def loss_head(x, g, target, *, tr=256, name):
    n, d = x.shape
    tr = _pick(n, tr, 8)

    def fn(xv, gv, tv):
        r = _rstd(xv)
        xhat = xv * r
        diff = xhat * gv - tv
        loss = 0.5 * jnp.sum(jnp.sum(diff * diff, axis=-1, keepdims=True) / d, axis=0, keepdims=True)
        dy = diff / d
        dxhat = dy * gv
        dx = r * (dxhat - xhat * jnp.mean(dxhat * xhat, axis=-1, keepdims=True))
        return dx, _colsum(dy * xhat), jnp.broadcast_to(loss, (8, LANES))

    return blockwise(fn, [rows(x, tr), whole(g), rows(target, tr)],
                     [out_rows(n, d, F32, tr), out_sum((1, d)), out_sum((8, LANES))],
                     grid=(n // tr,), name=name, sums=(1, 2))


def _adamw_math(wv, mv, vv, gs):
    c1 = 1.0 / (1.0 - ADAM_B1 ** ADAM_STEP)
    c2 = 1.0 / (1.0 - ADAM_B2 ** ADAM_STEP)
    g = gs[0].astype(F32)
    for dev in range(1, N_DEV):
        g = g + gs[dev].astype(F32)
    m_new = ADAM_B1 * mv + (1.0 - ADAM_B1) * g
    v_new = ADAM_B2 * vv + (1.0 - ADAM_B2) * (g * g)
    delta = -ADAM_LR * ((m_new * c1) / (jnp.sqrt(v_new * c2) + ADAM_EPS) + ADAM_WD * wv)
    return g, delta, m_new, v_new


def adamw(w, m, v, g_slots, *, tr=256, name):
    n, c = w.shape
    tr = _pick(n, tr, 8)
    slots = (g_slots, (N_DEV, tr, c), lambda i: (0, i, 0))
    return blockwise(_adamw_math, [rows(w, tr), rows(m, tr), rows(v, tr), slots],
                     [out_rows(n, c, F32, tr)] * 4, grid=(n // tr,), name=name)


def adamw_layers(w, m, v, g_layers, *, name):
    depth, r, c = w.shape
    row_bytes = c * 2 * (depth * N_DEV * g_layers[0].dtype.itemsize + 7 * 4)
    tr = _pick(r, max(8, ADAMW_BLOCK_BYTES // row_bytes // 8 * 8), 8)

    def body(w_ref, m_ref, v_ref, *rest):
        g_refs, out_refs = rest[:depth], rest[depth:]
        for layer in range(depth):
            @pl.when(pl.program_id(0) == layer)
            def _(layer=layer):
                for o, val in zip(out_refs, _adamw_math(w_ref[...], m_ref[...], v_ref[...], g_refs[layer][...])):
                    o[...] = val

    blk = pl.BlockSpec((None, tr, c), lambda l, i: (l, i, 0))
    g_specs = [pl.BlockSpec((N_DEV, tr, c), lambda l, i, layer=layer: (0, jnp.where(l == layer, i, 0), 0))
               for layer in range(depth)]
    return pl.pallas_call(
        body, name=name, grid=(depth, r // tr), in_specs=[blk] * 3 + g_specs, out_specs=[blk] * 4,
        out_shape=[jax.ShapeDtypeStruct(w.shape, F32)] * 4, compiler_params=_params(("arbitrary", "arbitrary")),
    )(w, m, v, *g_layers)


LOG2E = 1.4426950408889634
MLA_FWD_SUB, MLA_FWD_BQ = 512, 2048
MLA_BWD_SUB, MLA_BWD_BQ = 1024, 2048
MLA_BK = 2048


def mla_fwd(q, k, vt, *, name, ride=None):
    h, s, dq = q.shape
    bq, bk = _pick(s, MLA_FWD_BQ), _pick(s, MLA_BK)
    sub = min(MLA_FWD_SUB, bq)
    nk = s // bk
    scale = QK_DIM ** -0.5
    c2 = scale * LOG2E
    grid = (h, s // bq, nk)

    def body(q_ref, k_ref, vt_ref, *rest):
        if ride is None:
            o_ref, lse_ref, m_s, l_s, acc = rest
        else:
            n = ride.n
            x_refs, (o_ref, lse_ref), g_refs = rest[:n], rest[n:n + 2], rest[n + 2:2 * n + 2]
            m_s, l_s, acc, *sems = rest[2 * n + 2:]
            first, middle, last = _grid_flags(grid)
            ride_start, ride_middle, ride_finish = ride.steps(x_refs, g_refs, sems)
            pl.when(first)(ride_start)
            pl.when(middle)(ride_middle)
        j = pl.program_id(2)

        @pl.when(j == 0)
        def _():
            m_s[...] = jnp.full_like(m_s, NEG)
            l_s[...] = jnp.zeros_like(l_s)
            acc[...] = jnp.zeros_like(acc)

        def scores(c0):
            return lax.dot_general(k_ref[...], q_ref[pl.ds(c0, sub), :], NT, preferred_element_type=F32)

        starts = list(range(0, bq, sub))
        st_next = scores(starts[0])
        for n, c0 in enumerate(starts):
            cols = pl.ds(c0, sub)
            st = st_next
            if n + 1 < len(starts):
                st_next = scores(starts[n + 1])
            m_prev = m_s[:, cols]
            m_new = jnp.maximum(m_prev, jnp.max(st, axis=0, keepdims=True))
            alpha = jnp.exp2((m_prev - m_new) * c2)
            pt = jnp.exp2((st - m_new) * c2)
            l_s[:, cols] = alpha * l_s[:, cols] + jnp.sum(pt, axis=0, keepdims=True)
            acc[:, cols] = alpha * acc[:, cols] + lax.dot_general(vt_ref[...], pt.astype(BF16), NN,
                                                                  preferred_element_type=F32)
            m_s[:, cols] = m_new

        @pl.when(j == nk - 1)
        def _():
            o_ref[...] = (acc[...] / l_s[...]).T
            lse_ref[...] = m_s[...] * scale + jnp.log(l_s[...])

        if ride is not None:
            pl.when(last)(ride_finish)

    in_specs = [pl.BlockSpec((None, bq, dq), lambda hh, i, j: (hh, i, 0)),
                pl.BlockSpec((None, bk, dq), lambda hh, i, j: (hh, j, 0)),
                pl.BlockSpec((None, V_HEAD, bk), lambda hh, i, j: (hh, 0, j))]
    out_specs = [pl.BlockSpec((bq, V_HEAD), lambda hh, i, j: (i, hh)),
                 pl.BlockSpec((None, 1, bq), lambda hh, i, j: (hh, 0, i))]
    out_shape = [jax.ShapeDtypeStruct((s, h * V_HEAD), F32), jax.ShapeDtypeStruct((h, 1, s), F32)]
    scratch = [pltpu.VMEM((1, bq), F32), pltpu.VMEM((1, bq), F32), pltpu.VMEM((V_HEAD, bq), F32)]
    args = (q, k, vt)
    sem = ("parallel", "parallel", "arbitrary")
    if ride is not None:
        in_specs, out_specs, out_shape, scratch, args = ride.extend(in_specs, out_specs, out_shape, scratch, args)
        sem = ("arbitrary",) * 3
    return pl.pallas_call(body, name=name, grid=grid, in_specs=in_specs, out_specs=out_specs, out_shape=out_shape,
                          scratch_shapes=scratch, compiler_params=_params(sem))(*args)


def mla_delta(do, o, h, *, tr=512, name):
    s = do.shape[0]
    tr = _pick(s, tr, 8)

    def fn(dov, ov):
        return (jnp.sum(dov.astype(F32) * ov, axis=-1, keepdims=True),)

    blk = lambda arr: (arr, (tr, V_HEAD), lambda hh, i: (i, hh))
    return blockwise(fn, [blk(do), blk(o)], [((h, s, 1), F32, (None, tr, 1), lambda hh, i: (hh, i, 0))],
                     grid=(h, s // tr), name=name)[0]


def mla_bwd(q, k, kt, kv, do, lse, delta, *, v_col0, name, ride=None):
    h, s, dq = q.shape
    bq, bk = _pick(s, MLA_BWD_BQ), _pick(s, MLA_BK)
    sub = min(MLA_BWD_SUB, bq)
    nq = s // bq
    scale = QK_DIM ** -0.5
    c2 = scale * LOG2E
    grid = (h, s // bk, nq)

    def body(q_ref, k_ref, kt_ref, v_ref, do_ref, lse_ref, delta_ref, *rest):
        if ride is None:
            dq_ref, dk_ref, dv_ref, dk_acc, dv_acc = rest
        else:
            n = ride.n
            x_refs, (dq_ref, dk_ref, dv_ref), g_refs = rest[:n], rest[n:n + 3], rest[n + 3:2 * n + 3]
            dk_acc, dv_acc, *sems = rest[2 * n + 3:]
            first, _, last = _grid_flags(grid)
            ride_start, _, ride_finish = ride.steps(x_refs, g_refs, sems)
            pl.when(first)(ride_start)
        j, i = pl.program_id(1), pl.program_id(2)

        @pl.when(i == 0)
        def _():
            dk_acc[...] = jnp.zeros_like(dk_acc)
            dv_acc[...] = jnp.zeros_like(dv_acc)

        def scores(c0):
            cols = pl.ds(c0, sub)
            return (lax.dot_general(k_ref[...], q_ref[cols, :], NT, preferred_element_type=F32),
                    lax.dot_general(v_ref[...], do_ref[cols, :], NT, preferred_element_type=F32))

        starts = list(range(0, bq, sub))
        nxt = scores(starts[0])
        for n, c0 in enumerate(starts):
            cols = pl.ds(c0, sub)
            st, dpt = nxt
            if n + 1 < len(starts):
                nxt = scores(starts[n + 1])
            q_sub, do_sub = q_ref[cols, :], do_ref[cols, :]
            pt = jnp.exp2(st * c2 - lse_ref[:, cols] * LOG2E)
            ds_b = ((pt * (dpt - delta_ref[:, cols])) * scale).astype(BF16)
            dv_acc[...] += lax.dot_general(pt.astype(BF16), do_sub, NN, preferred_element_type=F32)
            dk_acc[...] += lax.dot_general(ds_b, q_sub, NN, preferred_element_type=F32)
            dq_t = lax.dot_general(kt_ref[...], ds_b, NN, preferred_element_type=F32)

            @pl.when(j == 0)
            def _():
                dq_ref[i, :, cols] = dq_t

            @pl.when(j > 0)
            def _():
                dq_ref[i, :, cols] += dq_t

        @pl.when(i == nq - 1)
        def _():
            dk_ref[...] = dk_acc[...]
            dv_ref[...] = dv_acc[...].astype(dv_ref.dtype)

        if ride is not None:
            pl.when(last)(ride_finish)

    in_specs = [pl.BlockSpec((None, bq, dq), lambda hh, j, i: (hh, i, 0)),
                pl.BlockSpec((None, bk, dq), lambda hh, j, i: (hh, j, 0)),
                pl.BlockSpec((None, dq, bk), lambda hh, j, i: (hh, 0, j)),
                pl.BlockSpec((bk, V_HEAD), lambda hh, j, i: (j, v_col0 + hh)),
                pl.BlockSpec((bq, V_HEAD), lambda hh, j, i: (i, hh)),
                pl.BlockSpec((None, 1, bq), lambda hh, j, i: (hh, 0, i)),
                pl.BlockSpec((None, 1, bq), lambda hh, j, i: (hh, 0, i))]
    out_specs = [pl.BlockSpec((None, nq, dq, bq), lambda hh, j, i: (hh, 0, 0, 0)),
                 pl.BlockSpec((None, bk, dq), lambda hh, j, i: (hh, j, 0)),
                 pl.BlockSpec((bk, V_HEAD), lambda hh, j, i: (j, hh))]
    out_shape = [jax.ShapeDtypeStruct((h, nq, dq, bq), F32), jax.ShapeDtypeStruct((h, s, dq), F32),
                 jax.ShapeDtypeStruct((s, h * V_HEAD), BF16)]
    scratch = [pltpu.VMEM((bk, dq), F32), pltpu.VMEM((bk, V_HEAD), F32)]
    args = (q, k, kt, kv, do, lse, delta)
    sem = ("parallel", "arbitrary", "arbitrary")
    if ride is not None:
        in_specs, out_specs, out_shape, scratch, args = ride.extend(in_specs, out_specs, out_shape, scratch, args)
        sem = ("arbitrary",) * 3
    return pl.pallas_call(body, name=name, grid=grid, in_specs=in_specs, out_specs=out_specs, out_shape=out_shape,
                          scratch_shapes=scratch, compiler_params=_params(sem))(*args)


def _na_window(r, n_rows):
    row_start = jnp.clip(r - NA_KH // 2, 0, n_rows - NA_KH)
    return row_start, row_start - r + (NA_KH - 1)


def _na_probs(sc, bias):
    sc = sc * (NA_HEAD_DIM ** -0.5) + bias
    p = jnp.exp(sc - jnp.max(sc, axis=-1, keepdims=True))
    return p / jnp.sum(p, axis=-1, keepdims=True)


def na_fwd(proj, tables, *, q_col0, k_col0, v_col0, rows_per_step=16, name):
    s = proj.shape[0]
    h = tables.shape[0]
    n_rows = s // GRID_W
    rb = min(rows_per_step, n_rows)
    tq = rb * GRID_W

    def body(q_ref, k_ref, v_ref, b_ref, o_ref):
        i = pl.program_id(1)

        def window(rl):
            row_start, which = _na_window(i * rb + rl, n_rows)
            return pl.ds(pl.multiple_of(row_start * GRID_W, GRID_W), NA_WIN), which

        def scores(rl):
            q_row = q_ref[pl.ds(rl * GRID_W, GRID_W), :].astype(BF16)
            return lax.dot_general(q_row, k_ref[window(rl)[0], :].astype(BF16), NT, preferred_element_type=F32)

        ahead = [scores(rl) for rl in range(rb)]
        for rl in range(rb):
            sc = ahead[rl]
            keys, which = window(rl)
            p = _na_probs(sc, b_ref[which])
            o_ref[pl.ds(rl * GRID_W, GRID_W), :] = lax.dot_general(
                p.astype(BF16), v_ref[keys, :].astype(BF16), NN, preferred_element_type=F32)

    head = lambda col0: pl.BlockSpec((s, NA_HEAD_DIM), lambda hh, i: (0, col0 + hh))
    return pl.pallas_call(
        body, name=name, grid=(h, n_rows // rb),
        in_specs=[pl.BlockSpec((tq, NA_HEAD_DIM), lambda hh, i: (i, q_col0 + hh)), head(k_col0), head(v_col0),
                  pl.BlockSpec((None, NA_KH, GRID_W, NA_WIN), lambda hh, i: (hh, 0, 0, 0))],
        out_specs=pl.BlockSpec((tq, NA_HEAD_DIM), lambda hh, i: (i, hh)),
        out_shape=jax.ShapeDtypeStruct((s, h * NA_HEAD_DIM), F32),
        compiler_params=_params(("parallel", "arbitrary")),
    )(proj, proj, proj, tables)


def na_bwd(proj, tables, do, *, q_col0, k_col0, v_col0, rows_per_step=16, name):
    s = proj.shape[0]
    h = tables.shape[0]
    n_rows = s // GRID_W
    rb = min(rows_per_step, n_rows)
    tq = rb * GRID_W
    scale = NA_HEAD_DIM ** -0.5

    def body(q_ref, k_ref, v_ref, b_ref, do_ref, dq_ref, dk_ref, dv_ref, db_ref):
        i = pl.program_id(1)

        @pl.when(i == 0)
        def _():
            dk_ref[...] = jnp.zeros_like(dk_ref)
            dv_ref[...] = jnp.zeros_like(dv_ref)
            db_ref[...] = jnp.zeros_like(db_ref)

        def window(rl):
            row_start, which = _na_window(i * rb + rl, n_rows)
            return pl.ds(pl.multiple_of(row_start * GRID_W, GRID_W), NA_WIN), which

        def query(rl):
            rows_ = pl.ds(rl * GRID_W, GRID_W)
            return q_ref[rows_, :].astype(BF16), do_ref[rows_, :]

        def scores(rl):
            q_row, do_row = query(rl)
            keys = window(rl)[0]
            return (lax.dot_general(q_row, k_ref[keys, :].astype(BF16), NT, preferred_element_type=F32),
                    lax.dot_general(do_row, v_ref[keys, :].astype(BF16), NT, preferred_element_type=F32))

        ahead = [scores(rl) for rl in range(rb)]
        for rl in range(rb):
            sc, dp = ahead[rl]
            q_row, do_row = query(rl)
            keys, which = window(rl)
            p = _na_probs(sc, b_ref[which])
            ds = p * (dp - jnp.sum(dp * p, axis=-1, keepdims=True))
            db_ref[which] += ds
            ds_b = (ds * scale).astype(BF16)
            dq_ref[pl.ds(rl * GRID_W, GRID_W), :] = lax.dot_general(
                ds_b, k_ref[keys, :].astype(BF16), NN, preferred_element_type=F32).astype(dq_ref.dtype)
            dk_ref[keys, :] += lax.dot_general(ds_b, q_row, TN, preferred_element_type=F32)
            dv_ref[keys, :] += lax.dot_general(p.astype(BF16), do_row, TN, preferred_element_type=F32)

    head = lambda col0: pl.BlockSpec((s, NA_HEAD_DIM), lambda hh, i: (0, col0 + hh))
    rows_of = lambda col0: pl.BlockSpec((tq, NA_HEAD_DIM), lambda hh, i: (i, col0 + hh))
    table = pl.BlockSpec((None, NA_KH, GRID_W, NA_WIN), lambda hh, i: (hh, 0, 0, 0))
    hw = h * NA_HEAD_DIM
    return pl.pallas_call(
        body, name=name, grid=(h, n_rows // rb),
        in_specs=[rows_of(q_col0), head(k_col0), head(v_col0), table, rows_of(0)],
        out_specs=[rows_of(0), head(0), head(0), table],
        out_shape=[jax.ShapeDtypeStruct((s, hw), BF16), jax.ShapeDtypeStruct((s, hw), F32),
                   jax.ShapeDtypeStruct((s, hw), F32), jax.ShapeDtypeStruct(tables.shape, F32)],
        compiler_params=_params(("parallel", "arbitrary")),
    )(proj, proj, proj, tables, do)


def _na_tables():
    qc = np.arange(GRID_W)[:, None]
    kc = np.arange(GRID_W)[None, :]
    col_start = np.clip(qc - NA_KW // 2, 0, GRID_W - NA_KW)
    col_ok = (kc >= col_start) & (kc < col_start + NA_KW)
    dx = np.clip(kc - qc, -(NA_KW - 1), NA_KW - 1) + (NA_KW - 1)
    return col_ok, dx


def na_bias_tables(rpb_l):
    col_ok, dx = _na_tables()
    t = jnp.where(col_ok[None, None], rpb_l[:, :, dx], NEG)
    t = t.transpose(0, 2, 1, 3)
    return jnp.stack([t[:, :, w:w + NA_KH].reshape(t.shape[0], GRID_W, NA_WIN) for w in range(NA_KH)], axis=1)


def na_bias_grad(dtables, *, name):
    h = dtables.shape[0]
    _, dx = _na_tables()
    n_dy, n_dx = 2 * NA_KH - 1, 2 * NA_KW - 1
    d5 = dtables.reshape(h, NA_KH, GRID_W, NA_KH, GRID_W)
    t = sum(jnp.pad(d5[:, w], ((0, 0), (0, 0), (w, n_dy - NA_KH - w), (0, 0))) for w in range(NA_KH))
    onehot = np.zeros((GRID_W * GRID_W, LANES), np.float32)
    onehot[np.arange(GRID_W * GRID_W), dx.reshape(-1)] = 1.0
    t = t.transpose(0, 2, 1, 3).reshape(h * n_dy, GRID_W * GRID_W)
    t = jnp.pad(t, ((0, (-t.shape[0]) % 8), (0, 0)))
    out = mm(t, jnp.asarray(onehot), mode="nn", name=name, exact=True, tk=1024)
    return out[:h * n_dy, :n_dx].reshape(h, n_dy, n_dx)


def _flip(v, bit):
    return 1 - v if bit else v


class Part:
    def __init__(self, shape, kind, dtype):
        self.r, self.c = shape
        self.kind, self.dtype = kind, dtype

    @property
    def whole_shape(self):
        return {"row": (N_DEV * self.r, self.c), "col": (self.r, N_DEV * self.c),
                "packed": (N_DEV, self.r, self.c)}[self.kind]

    @property
    def packed_shape(self):
        return (N_DEV, self.r, self.c)

    def shard_of(self, ref, j):
        if self.kind == "row":
            return ref.at[pl.ds(pl.multiple_of(j * self.r, 8), self.r), :]
        if self.kind == "col":
            return ref.at[:, pl.ds(pl.multiple_of(j * self.c, LANES), self.c)]
        return ref.at[j]


def comm_scratch(n_parts):
    n = n_parts * (N_DEV - 1)
    return [pltpu.SemaphoreType.DMA((n,)), pltpu.SemaphoreType.DMA((n,)), pltpu.SemaphoreType.DMA((n_parts,))]


def gather_plan(parts, x_refs, out_refs, send_sems, recv_sems, local_sems):
    x, y, c = lax.axis_index("x"), lax.axis_index("y"), lax.axis_index("c")
    me, sibling = (x, y, c), (x, y, 1 - c)
    chips = [(1 - x, y), (x, 1 - y), (1 - x, 1 - y)]

    def place(w, px, py, pc):
        return parts[w].shard_of(out_refs[w], 4 * px + 2 * py + pc)

    def copy(k, blk, to, own=False):
        return [pltpu.make_async_remote_copy(
            src_ref=x_refs[w] if own else place(w, *blk), dst_ref=place(w, *blk),
            send_sem=send_sems.at[w * (N_DEV - 1) + k], recv_sem=recv_sems.at[w * (N_DEV - 1) + k],
            device_id=to, device_id_type=pl.DeviceIdType.MESH) for w in range(len(parts))]

    def mine():
        return [pltpu.make_async_copy(x_refs[w], place(w, *me), local_sems.at[w]) for w in range(len(parts))]

    def first():
        return copy(0, me, sibling, own=True) + [cp for j, chip in enumerate(chips)
                                                 for cp in copy(1 + j, me, (*chip, c), own=True)]

    def passed(j):
        return copy(4 + j, (*chips[j], c), sibling)

    def start():
        for cp in mine() + first():
            cp.start()

    def forward():
        for j, chip in enumerate(chips):
            for cp in copy(1 + j, (*chip, c), me):
                cp.wait_recv()
            for cp in passed(j):
                cp.start()

    def finish():
        for cp in copy(0, sibling, me):
            cp.wait_recv()
        for j, chip in enumerate(chips):
            for cp in copy(4 + j, (*chip, 1 - c), me):
                cp.wait_recv()
        for cp in first() + [cp for j in range(len(chips)) for cp in passed(j)]:
            cp.wait_send()
        for cp in mine():
            cp.wait()

    return start, forward, finish


def exchange_plan(parts, x_refs, out_refs, send_sems, recv_sems, local_sems):
    x, y, c = lax.axis_index("x"), lax.axis_index("y"), lax.axis_index("c")
    me = 4 * x + 2 * y + c

    def peer_of(k):
        peer = (_flip(x, k & 4), _flip(y, k & 2), _flip(c, k & 1))
        return peer, 4 * peer[0] + 2 * peer[1] + peer[2]

    def copies(k, arriving):
        peer, theirs = peer_of(k)
        return [pltpu.make_async_remote_copy(
            src_ref=parts[w].shard_of(x_refs[w], me if arriving else theirs),
            dst_ref=out_refs[w].at[theirs if arriving else me],
            send_sem=send_sems.at[w * (N_DEV - 1) + k - 1], recv_sem=recv_sems.at[w * (N_DEV - 1) + k - 1],
            device_id=peer, device_id_type=pl.DeviceIdType.MESH) for w in range(len(parts))]

    def mine():
        return [pltpu.make_async_copy(parts[w].shard_of(x_refs[w], me), out_refs[w].at[me], local_sems.at[w])
                for w in range(len(parts))]

    def start():
        for cp in mine():
            cp.start()
        for k in range(1, N_DEV):
            for cp in copies(k, arriving=False):
                cp.start()

    def finish():
        for k in range(1, N_DEV):
            for cp in copies(k, arriving=True):
                cp.wait_recv()
        for k in range(1, N_DEV):
            for cp in copies(k, arriving=False):
                cp.wait_send()
        for cp in mine():
            cp.wait()

    return start, finish


class Collective:
    def __init__(self, kind, parts, arrays):
        self.kind, self.parts, self.arrays = kind, parts, list(arrays)
        self.n = len(parts)
        shapes = [p.whole_shape if kind == "gather" else p.packed_shape for p in parts]
        self.out_shape = [jax.ShapeDtypeStruct(s, p.dtype) for s, p in zip(shapes, parts)]

    def extend(self, in_specs, out_specs, out_shape, scratch, args):
        any_spec = pl.BlockSpec(memory_space=pl.ANY)
        return (in_specs + [any_spec] * self.n, out_specs + [any_spec] * self.n, out_shape + self.out_shape,
                scratch + comm_scratch(self.n), args + tuple(self.arrays))

    def steps(self, x_refs, out_refs, sems):
        plan = (gather_plan if self.kind == "gather" else exchange_plan)(self.parts, x_refs, out_refs, *sems)
        return plan[0], (plan[1] if len(plan) == 3 else None), plan[-1]

    def run(self, name):
        n = self.n

        def body(*refs):
            for step in self.steps(refs[:n], refs[n:2 * n], refs[2 * n:]):
                if step is not None:
                    step()

        in_specs, out_specs, out_shape, scratch, args = self.extend([], [], [], [], ())
        return pl.pallas_call(body, name=name, in_specs=in_specs, out_specs=out_specs, out_shape=out_shape,
                              scratch_shapes=scratch)(*args)


def _grid_flags(grid):
    ids = [pl.program_id(ax) for ax in range(len(grid))]
    inner_zero = functools.reduce(jnp.logical_and, [i == 0 for i in ids[1:]])
    first = jnp.logical_and(ids[0] == 0, inner_zero)
    middle = jnp.logical_and(ids[0] == grid[0] - 1, inner_zero)
    last = functools.reduce(jnp.logical_and, [i == g - 1 for i, g in zip(ids, grid)])
    return first, middle, last


SHARDED = ("w_in", "w_uq", "w_ukv", "w_o_mla", "w_o_na", "w_out", "w_ff1", "w_ff2")
ROW_SHARDED = ("w_out", "w_ff2")
TRANSPOSED = ("w_in",)
REPLICATED = ("norm_mix", "norm_qa", "norm_kva", "rpb", "norm_mlp", "norm_final")
WEIGHTS = ("norm_mix", "w_in", "norm_qa", "w_uq", "norm_kva", "w_ukv", "rpb", "w_o_mla", "w_o_na", "w_out",
           "norm_mlp", "w_ff1", "w_ff2", "norm_final")


def part_of(name, shard_shape):
    r, c = shard_shape
    kind = "row" if name in ROW_SHARDED + TRANSPOSED else ("col" if c % LANES == 0 else "packed")
    return Part((r, c), kind, BF16)


def _whole(part, gathered):
    return gathered.transpose(1, 0, 2).reshape(part.r, -1) if part.kind == "packed" else gathered


def _for_exchange(part, full):
    return full.reshape(part.r, N_DEV, part.c).transpose(1, 0, 2) if part.kind == "packed" else full


class Dims:
    def __init__(self, x, w_in, norm_qa, norm_kva, rpb, w_o_mla, w_o_na, w_ff1):
        self.s, self.d = x.shape[1], x.shape[2]
        self.depth = w_in.shape[0]
        self.q_lora, self.kv_lora = norm_qa.shape[1], norm_kva.shape[1]
        self.mla_w, self.na_w = w_o_mla.shape[1], w_o_na.shape[1]
        self.mla_h, self.na_h = self.mla_w // V_HEAD, self.na_w // NA_HEAD_DIM
        self.d_ff = w_ff1.shape[2] * N_DEV
        assert rpb.shape[1] == self.na_h and self.s % GRID_W == 0 and self.s // GRID_W >= NA_KH
        self.in_lo = self.q_lora + self.kv_lora
        self.main_w = self.in_lo + 3 * self.na_w + 2 * self.d
        assert self.q_lora == self.kv_lora and self.in_lo % self.na_w == 0 and self.in_lo % LANES == 0
        assert (self.in_lo + 3 * self.na_w) % self.d == 0
        self.q_col0 = self.in_lo // NA_HEAD_DIM
        self.k_off = self.in_lo + self.na_w
        self.v_off = self.in_lo + 2 * self.na_w
        self.ga_col = (self.in_lo + 3 * self.na_w) // self.d
        self.gb_col = self.ga_col + 1


def _split_w_in(dm, wt):
    lo = dm.in_lo
    main = jnp.concatenate([wt[:lo], wt[lo + QK_ROPE:]], axis=0)
    kpe = jnp.pad(wt[lo:lo + QK_ROPE], ((0, LANES - QK_ROPE), (0, 0)))
    return main, kpe


def _join_w_in(dm, main, kpe):
    lo = dm.in_lo
    return jnp.concatenate([main[:lo], kpe[:QK_ROPE], main[lo:]], axis=0)


def _split_heads(w, h, widths):
    r = w.shape[0]
    w3 = w.reshape(r, h, sum(widths))
    out, o = [], 0
    for wd in widths:
        out.append(w3[:, :, o:o + wd].reshape(r, h * wd))
        o += wd
    return out


def _join_heads(parts, h):
    r = parts[0].shape[0]
    return jnp.concatenate([p.reshape(r, h, -1) for p in parts], axis=2).reshape(r, -1)


UQ_WIDTHS = (QK_NOPE, HALF_ROPE, HALF_ROPE)
UKV_WIDTHS = (QK_NOPE, V_HEAD)


def _by_head(parts, h):
    s = parts[0].shape[0]
    return jnp.concatenate([p.reshape(s, h, -1) for p in parts], axis=2).transpose(1, 0, 2)


def _from_head(t, widths):
    h, s, _ = t.shape
    t = t.transpose(1, 0, 2)
    out, o = [], 0
    for wd in widths:
        out.append(t[:, :, o:o + wd].reshape(s, h * wd))
        o += wd
    return out


def layer_fwd(dm, lname, x, w, g, cos_q, sin_q, cos_k, sin_k, ride, late_weights):
    s, h = dm.s, dm.mla_h
    u = rmsnorm_fwd(x, g["norm_mix"], name=f"{lname}_norm_mix")
    proj = mm(u, w["in_main"], mode="nt", name=f"{lname}_proj")
    kpe = mm(u, w["in_kpe"], mode="nt", name=f"{lname}_proj_kpe")
    qn = rmsnorm_fwd(proj, g["norm_qa"], width=dm.q_lora, col=0, name=f"{lname}_norm_qa")
    kvn = rmsnorm_fwd(proj, g["norm_kva"], width=dm.kv_lora, col=1, name=f"{lname}_norm_kva")
    q = mm(qn, w["uq"], mode="nn", name=f"{lname}_uq")
    kv = mm(kvn, w["ukv"], mode="nn", out_dtypes=(BF16,), name=f"{lname}_ukv")
    nope_w, half_w = h * QK_NOPE, h * HALF_ROPE
    q1, q2 = rope_pair(q[:, nope_w:nope_w + half_w], q[:, nope_w + half_w:], cos_q, sin_q, name=f"{lname}_rope_q")
    k1, k2 = rope_pair(kpe[:, :HALF_ROPE], kpe[:, HALF_ROPE:QK_ROPE], cos_k, sin_k, name=f"{lname}_rope_k")
    qh = _by_head([q[:, :nope_w].astype(BF16), q1, q2], h)
    kh = _by_head([kv[:, :nope_w], jnp.tile(k1, (1, h)), jnp.tile(k2, (1, h))], h)
    vt = kv[:, nope_w:].reshape(s, h, V_HEAD).transpose(1, 2, 0)
    o_a, lse, *gathered = mla_fwd(qh, kh, vt, name=f"{lname}_mla", ride=ride)
    w = {**w, **late_weights(gathered)}
    y_a = mm(o_a, w["o_mla"], mode="nn", name=f"{lname}_o_mla")

    tables = na_bias_tables(g["rpb"])
    o_b = na_fwd(proj, tables, q_col0=dm.q_col0, k_col0=dm.k_off // NA_HEAD_DIM, v_col0=dm.v_off // NA_HEAD_DIM,
                 name=f"{lname}_na")
    y_b = mm(o_b, w["o_na"], mode="nn", name=f"{lname}_o_na")

    merged = gate_fwd(proj, y_a, y_b, d=dm.d, ga_col=dm.ga_col, gb_col=dm.gb_col, name=f"{lname}_gate")
    x1 = mm(merged, w["out"], mode="nn", epi=lambda r, res: (r + res,), extras=(x,), name=f"{lname}_out")
    u2 = rmsnorm_fwd(x1, g["norm_mlp"], name=f"{lname}_norm_mlp")
    hid, act = mm(u2, w["ff1"], mode="nn", out_dtypes=(F32, BF16),
                  epi=lambda r: (r, jnp.square(jnp.maximum(r, 0.0))), name=f"{lname}_ff1")
    x2 = mm(act, w["ff2"], mode="nn", epi=lambda r, res: (r + res,), extras=(x1,), name=f"{lname}_ff2")
    saved = dict(x=x, u=u, proj=proj, qn=qn, kvn=kvn, kv=kv, qh=qh, kh=kh, o_a=o_a, lse=lse, y_a=y_a,
                 tables=tables, o_b=o_b, y_b=y_b, merged=merged, x1=x1, u2=u2, hid=hid, act=act)
    return x2, saved, w, gathered


def layer_bwd(dm, lname, dx2, w, g, sv, cos_q, sin_q, cos_k, sin_k, ride_of):
    h = dm.mla_h
    gw, gr = {}, {}
    gw["ff2"] = mm(sv["act"], dx2, mode="tn", out_dtypes=(BF16,), name=f"{lname}_d_ff2")
    dh = mm(dx2, w["ff2"], mode="nt", out_dtypes=(BF16,), extras=(sv["hid"],),
            epi=lambda r, hv: (r * (2.0 * jnp.maximum(hv, 0.0)),), name=f"{lname}_d_act")
    gw["ff1"] = mm(sv["u2"], dh, mode="tn", out_dtypes=(BF16,), name=f"{lname}_d_ff1")
    du2 = mm(dh, w["ff1"], mode="nt", name=f"{lname}_d_u2")
    dx1, gr["norm_mlp"] = rmsnorm_bwd(sv["x1"], g["norm_mlp"], du2, dx2, name=f"{lname}_d_norm_mlp")
    gw["out"] = mm(sv["merged"], dx1, mode="tn", out_dtypes=(BF16,), name=f"{lname}_d_out")
    dmerged = mm(dx1, w["out"], mode="nt", name=f"{lname}_d_merged")
    dy_a, dy_b, dga, dgb = gate_bwd(sv["proj"], sv["y_a"], sv["y_b"], dmerged, d=dm.d, ga_col=dm.ga_col,
                                    gb_col=dm.gb_col, name=f"{lname}_d_gate")
    gw["o_na"] = mm(sv["o_b"], dy_b, mode="tn", out_dtypes=(BF16,), name=f"{lname}_d_o_na")
    do_b = mm(dy_b, w["o_na"], mode="nt", out_dtypes=(BF16,), name=f"{lname}_d_ob")
    dq_na, dk_na, dv_na, dtables = na_bwd(sv["proj"], sv["tables"], do_b, q_col0=dm.q_col0,
                                          k_col0=dm.k_off // NA_HEAD_DIM, v_col0=dm.v_off // NA_HEAD_DIM,
                                          name=f"{lname}_d_na")
    gr["rpb"] = na_bias_grad(dtables, name=f"{lname}_d_rpb")
    gw["o_mla"] = mm(sv["o_a"], dy_a, mode="tn", out_dtypes=(BF16,), name=f"{lname}_d_o_mla")
    do_a = mm(dy_a, w["o_mla"], mode="nt", out_dtypes=(BF16,), name=f"{lname}_d_oa")
    delta = mla_delta(do_a, sv["o_a"], h, name=f"{lname}_d_mla_delta").reshape(h, 1, dm.s)
    dqt, dkh, dv, *received = mla_bwd(sv["qh"], sv["kh"], sv["kh"].transpose(0, 2, 1), sv["kv"], do_a, sv["lse"],
                                      delta, v_col0=h, name=f"{lname}_d_mla", ride=ride_of(gw))
    dqh = dqt.transpose(0, 1, 3, 2).reshape(h, dm.s, QK_DIM)
    dq_nope, dq1, dq2 = _from_head(dqh, UQ_WIDTHS)
    dq1, dq2 = rope_pair(dq1, dq2, cos_q, -sin_q, name=f"{lname}_d_rope_q")
    dq = jnp.concatenate([dq_nope.astype(BF16), dq1, dq2], axis=1)
    gw["uq"] = mm(sv["qn"], dq, mode="tn", out_dtypes=(BF16,), name=f"{lname}_d_uq")
    dqn = mm(dq, w["uq"], mode="nt", name=f"{lname}_d_qn")
    dk_nope = _from_head(dkh[:, :, :QK_NOPE], (QK_NOPE,))[0]
    dk1, dk2 = rope_pair_headsum(dkh[:, :, QK_NOPE:QK_NOPE + HALF_ROPE], dkh[:, :, QK_NOPE + HALF_ROPE:],
                                 cos_k, sin_k, name=f"{lname}_d_rope_k")
    dkv = jnp.concatenate([dk_nope.astype(BF16), dv], axis=1)
    gw["ukv"] = mm(sv["kvn"], dkv, mode="tn", out_dtypes=(BF16,), name=f"{lname}_d_ukv")
    dkvn = mm(dkv, w["ukv"], mode="nt", name=f"{lname}_d_kvn")
    dc_q, gr["norm_qa"] = rmsnorm_bwd(sv["proj"], g["norm_qa"], dqn, width=dm.q_lora, col=0, out_dtype=BF16,
                                      name=f"{lname}_d_norm_qa")
    dc_kv, gr["norm_kva"] = rmsnorm_bwd(sv["proj"], g["norm_kva"], dkvn, width=dm.kv_lora, col=1, out_dtype=BF16,
                                        name=f"{lname}_d_norm_kva")
    dproj = jnp.concatenate([dc_q, dc_kv, dq_na, dk_na.astype(BF16), dv_na.astype(BF16), dga, dgb], axis=1)
    dkpe = jnp.concatenate([dk1, dk2, jnp.zeros((dm.s, LANES - QK_ROPE), BF16)], axis=1)
    gw["in_main"] = mm(dproj, sv["u"], mode="tn", out_dtypes=(BF16,), name=f"{lname}_d_in")
    gw["in_kpe"] = mm(dkpe, sv["u"], mode="tn", out_dtypes=(BF16,), name=f"{lname}_d_in_kpe")
    du_k = mm(dkpe, w["in_kpe"], mode="nn", name=f"{lname}_d_u_kpe")
    du = mm(dproj, w["in_main"], mode="nn", epi=lambda r, res: (r + res,), extras=(du_k,), name=f"{lname}_d_u")
    dx, gr["norm_mix"] = rmsnorm_bwd(sv["x"], g["norm_mix"], du, dx1, name=f"{lname}_d_norm_mix")
    return dx, gw, gr, received


EARLY = ("w_in", "w_uq", "w_ukv")
LATE = ("w_o_mla", "w_o_na", "w_out", "w_ff1", "w_ff2")
LATE_KEYS = ("o_mla", "o_na", "out", "ff1", "ff2")


def _early_weights(dm, full):
    main, kpe = _split_w_in(dm, full["w_in"])
    return dict(in_main=main, in_kpe=kpe,
                uq=jnp.concatenate(_split_heads(full["w_uq"], dm.mla_h, UQ_WIDTHS), axis=1),
                ukv=jnp.concatenate(_split_heads(full["w_ukv"], dm.mla_h, UKV_WIDTHS), axis=1))


def _early_grads(dm, gw):
    h = dm.mla_h
    nope_w, half_w = h * QK_NOPE, h * HALF_ROPE
    uq = gw["uq"]
    ukv = gw["ukv"]
    return {"w_in": _join_w_in(dm, gw["in_main"], gw["in_kpe"]),
            "w_uq": _join_heads([uq[:, :nope_w], uq[:, nope_w:nope_w + half_w], uq[:, nope_w + half_w:]], h),
            "w_ukv": _join_heads([ukv[:, :nope_w], ukv[:, nope_w:]], h)}


def _pack_replicated(dm, parts):
    flat = jnp.concatenate([parts[n].reshape(-1) for n in REPLICATED])
    n = flat.shape[0]
    rows_ = -(-n // (8 * LANES)) * 8
    return jnp.pad(flat, (0, rows_ * LANES - n)).reshape(rows_, LANES)


def kernel(x, norm_mix, w_in, norm_qa, w_uq, norm_kva, w_ukv, rpb, w_o_mla, w_o_na, w_out, norm_mlp, w_ff1, w_ff2, norm_final, loss_target, m_norm_mix, m_w_in, m_norm_qa, m_w_uq, m_norm_kva, m_w_ukv, m_rpb, m_w_o_mla, m_w_o_na, m_w_out, m_norm_mlp, m_w_ff1, m_w_ff2, m_norm_final, v_norm_mix, v_w_in, v_norm_qa, v_w_uq, v_norm_kva, v_w_ukv, v_rpb, v_w_o_mla, v_w_o_na, v_w_out, v_norm_mlp, v_w_ff1, v_w_ff2, v_norm_final):
    dm = Dims(x, w_in, norm_qa, norm_kva, rpb, w_o_mla, w_o_na, w_ff1)
    params = dict(norm_mix=norm_mix, w_in=w_in, norm_qa=norm_qa, w_uq=w_uq, norm_kva=norm_kva, w_ukv=w_ukv, rpb=rpb,
                  w_o_mla=w_o_mla, w_o_na=w_o_na, w_out=w_out, norm_mlp=norm_mlp, w_ff1=w_ff1, w_ff2=w_ff2,
                  norm_final=norm_final)
    mom_m = dict(norm_mix=m_norm_mix, w_in=m_w_in, norm_qa=m_norm_qa, w_uq=m_w_uq, norm_kva=m_norm_kva, w_ukv=m_w_ukv,
                 rpb=m_rpb, w_o_mla=m_w_o_mla, w_o_na=m_w_o_na, w_out=m_w_out, norm_mlp=m_norm_mlp, w_ff1=m_w_ff1,
                 w_ff2=m_w_ff2, norm_final=m_norm_final)
    mom_v = dict(norm_mix=v_norm_mix, w_in=v_w_in, norm_qa=v_norm_qa, w_uq=v_w_uq, norm_kva=v_norm_kva, w_ukv=v_w_ukv,
                 rpb=v_rpb, w_o_mla=v_w_o_mla, w_o_na=v_w_o_na, w_out=v_w_out, norm_mlp=v_norm_mlp, w_ff1=v_w_ff1,
                 w_ff2=v_w_ff2, norm_final=v_norm_final)
    depth, s, h = dm.depth, dm.s, dm.mla_h

    pos = jnp.arange(s, dtype=F32)
    inv_freq = 1.0 / (ROPE_THETA ** (jnp.arange(0, QK_ROPE, 2, dtype=F32) / QK_ROPE))
    ang = pos[:, None] * inv_freq[None, :]
    cos_k, sin_k = jnp.cos(ang), jnp.sin(ang)
    cos_q, sin_q = jnp.tile(cos_k, (1, h)), jnp.tile(sin_k, (1, h))

    held = lambda d, n: d[n].transpose(0, 2, 1) if n in TRANSPOSED else d[n]
    part = {n: part_of(n, held(params, n).shape[1:]) for n in SHARDED}
    shard = lambda n, l: held(params, n)[l].astype(BF16)
    whole = lambda names, arrays: {n: _whole(part[n], a) for n, a in zip(names, arrays)}
    gains = [dict(norm_mix=norm_mix[l][None], norm_qa=norm_qa[l][None], norm_kva=norm_kva[l][None],
                  norm_mlp=norm_mlp[l][None], rpb=rpb[l]) for l in range(depth)]

    xl = x[0]
    saved, weights = [], []
    early = Collective("gather", [part[n] for n in EARLY], [shard(n, 0) for n in EARLY]).run("gather_weights")
    for l in range(depth):
        nxt = EARLY if l + 1 < depth else ()
        ride = Collective("gather", [part[n] for n in LATE + nxt],
                          [shard(n, l) for n in LATE] + [shard(n, l + 1) for n in nxt])
        late_weights = lambda got: dict(zip(LATE_KEYS, whole(LATE, got[:len(LATE)]).values()))
        xl, sv, w, got = layer_fwd(dm, "fwd", xl, _early_weights(dm, whole(EARLY, early)), gains[l], cos_q, sin_q,
                                   cos_k, sin_k, ride, late_weights)
        early = got[len(LATE):]
        saved.append(sv)
        weights.append(w)
    dx, g_final, loss_part = loss_head(xl, norm_final[None], loss_target[0], name="loss_head")
    loss = lax.psum(loss_part[0, 0], MESH_AXES)

    rep = {n: [None] * depth for n in REPLICATED if n != "norm_final"}
    recv = [{} for _ in range(depth)]
    pending = {}
    for l in reversed(range(depth)):
        def ride_of(gw, pending=pending):
            ready = {**dict(zip(LATE, [gw[k] for k in LATE_KEYS])), **pending}
            return Collective("exchange", [part[n] for n in ready], [_for_exchange(part[n], a) for n, a in ready.items()])

        dx, gw, gr, received = layer_bwd(dm, "bwd", dx, weights[l], gains[l], saved[l], cos_q, sin_q, cos_k, sin_k,
                                         ride_of)
        recv[l].update(zip(LATE, received[:len(LATE)]))
        if pending:
            recv[l + 1].update(zip(EARLY, received[len(LATE):]))
        for n in gr:
            rep[n][l] = gr[n]
        pending = _early_grads(dm, gw)
    last = Collective("exchange", [part[n] for n in EARLY], [_for_exchange(part[n], pending[n]) for n in EARLY])
    recv[0].update(zip(EARLY, last.run("scatter_grads")))
    rep_parts = {n: jnp.stack(rep[n]) for n in rep}
    rep_parts["norm_final"] = g_final
    packed = _pack_replicated(dm, rep_parts)
    rep_all = Collective("gather", [Part(packed.shape, "packed", F32)], [packed]).run("gather_small_grads")[0]

    outs = {}
    for n in SHARDED:
        res = adamw_layers(held(params, n), held(mom_m, n), held(mom_v, n), [recv[l][n] for l in range(depth)],
                           name=f"adamw_{n}")
        outs[n] = [a.transpose(0, 2, 1) for a in res] if n in TRANSPOSED else res
    n_rep = sum(int(np.prod(params[n].shape)) for n in REPLICATED)
    pack = lambda d: _pack_replicated(dm, d)
    res = adamw(pack(params), pack(mom_m), pack(mom_v), rep_all, name="adamw_replicated")
    off = 0
    for n in REPLICATED:
        size = int(np.prod(params[n].shape))
        outs[n] = [a.reshape(-1)[off:off + size].reshape(params[n].shape) for a in res]
        off += size
    assert off == n_rep

    grad_x = dx[None]
    return (loss, grad_x, *[outs[n][0] for n in WEIGHTS], *[outs[n][1] for n in WEIGHTS],
            *[outs[n][2] for n in WEIGHTS], *[outs[n][3] for n in WEIGHTS])
```

```python
import functools

import numpy as np
import jax
import jax.numpy as jnp
from jax import lax
from jax.experimental import pallas as pl
from jax.experimental.pallas import tpu as pltpu

F32 = jnp.float32
BF16 = jnp.bfloat16
MESH_AXES = ("x", "y", "c")
N_DEV = 8
LANES = 128

QK_NOPE = 128
QK_ROPE = 64
HALF_ROPE = QK_ROPE // 2
V_HEAD = 128
QK_DIM = QK_NOPE + QK_ROPE
NA_HEAD_DIM = 128
GRID_W = 64
NA_KH = 8
NA_KW = 16
NA_WIN = NA_KH * GRID_W
ROPE_THETA = 10000.0
EPS = 1e-6
NEG = -1e30

ADAM_LR = 0.001
ADAM_B1 = 0.9
ADAM_B2 = 0.999
ADAM_EPS = 1e-08
ADAM_WD = 0.01
ADAM_STEP = 10

VMEM_LIMIT_V7X = 56 * 1024 * 1024
ADAMW_BLOCK_BYTES = 24 * 1024 * 1024

NN = (((1,), (0,)), ((), ()))
NT = (((1,), (1,)), ((), ()))
TN = (((0,), (0,)), ((), ()))


def _pick(dim, target, align=LANES):
    if dim <= target:
        return dim
    t = (target // align) * align
    while t >= align:
        if dim % t == 0:
            return t
        t -= align
    return dim


def _params(sem):
    return pltpu.CompilerParams(dimension_semantics=sem, vmem_limit_bytes=VMEM_LIMIT_V7X)


def mm(a, b, *, mode, name, out_dtypes=(F32,), epi=None, extras=(), tm=1024, tn=1024, tk=2048, exact=False):
    if mode == "nn":
        (m, k), (k2, n) = a.shape, b.shape
    elif mode == "nt":
        (m, k), (n, k2) = a.shape, b.shape
    else:
        (k, m), (k2, n) = a.shape, b.shape
    assert k == k2, (a.shape, b.shape, mode)
    tm, tn, tk = _pick(m, tm), _pick(n, tn), _pick(k, tk)
    nk = k // tk
    a_spec = pl.BlockSpec((tk, tm), lambda i, j, s: (s, i)) if mode == "tn" else pl.BlockSpec((tm, tk), lambda i, j, s: (i, s))
    b_spec = pl.BlockSpec((tn, tk), lambda i, j, s: (j, s)) if mode == "nt" else pl.BlockSpec((tk, tn), lambda i, j, s: (s, j))
    tile = pl.BlockSpec((tm, tn), lambda i, j, s: (i, j))
    dims = {"nn": NN, "nt": NT, "tn": TN}[mode]
    n_extra, n_out = len(extras), len(out_dtypes)

    def product(a_ref, b_ref):
        if exact:
            return lax.dot_general(a_ref[...], b_ref[...], dims, precision=lax.Precision.HIGHEST,
                                   preferred_element_type=F32)
        return lax.dot_general(a_ref[...].astype(BF16), b_ref[...].astype(BF16), dims, preferred_element_type=F32)

    def finish(r, extra_refs, out_refs):
        res = (r,) if epi is None else epi(r, *[e[...] for e in extra_refs])
        for o, v in zip(out_refs, res):
            o[...] = v.astype(o.dtype)

    def body_one_step(a_ref, b_ref, *rest):
        finish(product(a_ref, b_ref), rest[:n_extra], rest[n_extra:])

    def body(a_ref, b_ref, *rest):
        extra_refs, out_refs, acc = rest[:n_extra], rest[n_extra:n_extra + n_out], rest[-1]
        step = pl.program_id(2)

        @pl.when(step == 0)
        def _():
            acc[...] = product(a_ref, b_ref)

        @pl.when(step > 0)
        def _():
            acc[...] += product(a_ref, b_ref)

        @pl.when(step == nk - 1)
        def _():
            finish(acc[...], extra_refs, out_refs)

    outs = pl.pallas_call(
        body_one_step if nk == 1 else body, name=name, grid=(m // tm, n // tn, nk),
        in_specs=[a_spec, b_spec] + [tile] * n_extra,
        out_specs=[tile] * n_out,
        out_shape=[jax.ShapeDtypeStruct((m, n), d) for d in out_dtypes],
        scratch_shapes=[] if nk == 1 else [pltpu.VMEM((tm, tn), F32)],
        compiler_params=_params(("parallel", "parallel", "arbitrary")),
    )(a, b, *extras)
    return outs[0] if n_out == 1 else outs


def blockwise(fn, ins, outs, *, grid, name, sums=()):
    n_in, n_axes = len(ins), len(grid)

    def body(*refs):
        res = fn(*[r[...] for r in refs[:n_in]])
        first = functools.reduce(jnp.logical_and, [pl.program_id(ax) == 0 for ax in range(n_axes)])
        for idx, (o, v) in enumerate(zip(refs[n_in:], res)):
            if idx in sums:
                @pl.when(first)
                def _(o=o):
                    o[...] = jnp.zeros_like(o)

                o[...] += v.astype(o.dtype)
            else:
                o[...] = v.astype(o.dtype)

    sem = ("arbitrary" if sums else "parallel",) * n_axes
    res = pl.pallas_call(
        body, name=name, grid=grid,
        in_specs=[pl.BlockSpec(blk, imap) for _, blk, imap in ins],
        out_specs=[pl.BlockSpec(blk, imap) for _, _, blk, imap in outs],
        out_shape=[jax.ShapeDtypeStruct(shape, dt) for shape, dt, _, _ in outs],
        compiler_params=_params(sem),
    )(*[a for a, _, _ in ins])
    return res


def rows(arr, tr, width=None, col=0):
    width = arr.shape[1] if width is None else width
    return (arr, (tr, width), lambda i, col=col: (i, col))


def whole(arr):
    return (arr, arr.shape, lambda i: (0, 0))


def out_rows(n_rows, width, dtype, tr):
    return ((n_rows, width), dtype, (tr, width), lambda i: (i, 0))


def out_sum(shape, dtype=F32):
    return (shape, dtype, shape, lambda i: (0, 0))


def _rstd(x):
    return lax.rsqrt(jnp.mean(x * x, axis=-1, keepdims=True) + EPS)


def _colsum(v):
    return jnp.sum(v, axis=0, keepdims=True)


def rmsnorm_fwd(x, g, *, width=None, col=0, tr=256, name):
    n = x.shape[0]
    tr = _pick(n, tr, 8)
    width = x.shape[1] if width is None else width

    def fn(xv, gv):
        return ((xv * _rstd(xv)) * gv,)

    return blockwise(fn, [rows(x, tr, width, col), whole(g)], [out_rows(n, width, BF16, tr)],
                     grid=(n // tr,), name=name)[0]


def rmsnorm_bwd(x, g, dy, res=None, *, width=None, col=0, out_dtype=F32, tr=256, name):
    n = x.shape[0]
    tr = _pick(n, tr, 8)
    width = x.shape[1] if width is None else width

    def fn(xv, gv, dyv, *resv):
        dyv = dyv.astype(F32)
        r = _rstd(xv)
        xhat = xv * r
        dxhat = dyv * gv
        dx = r * (dxhat - xhat * jnp.mean(dxhat * xhat, axis=-1, keepdims=True))
        if resv:
            dx = dx + resv[0]
        return dx, _colsum(dyv * xhat)

    ins = [rows(x, tr, width, col), whole(g), rows(dy, tr)] + ([rows(res, tr)] if res is not None else [])
    return blockwise(fn, ins, [out_rows(n, width, out_dtype, tr), out_sum((1, width))],
                     grid=(n // tr,), name=name, sums=(1,))


def rope_pair(x1, x2, cos, sin, *, tr=512, name):
    n, w = x1.shape
    tr = _pick(n, tr, 8)

    def fn(a, b, c, s):
        a, b = a.astype(F32), b.astype(F32)
        return a * c - b * s, b * c + a * s

    return blockwise(fn, [rows(x1, tr), rows(x2, tr), rows(cos, tr), rows(sin, tr)],
                     [out_rows(n, w, BF16, tr), out_rows(n, w, BF16, tr)], grid=(n // tr,), name=name)


def rope_pair_headsum(d1, d2, cos, sin, *, tr=512, name):
    h, n, w = d1.shape
    tr = _pick(n, tr, 8)

    def fn(a, b, c, s):
        a, b = jnp.sum(a.astype(F32), axis=0), jnp.sum(b.astype(F32), axis=0)
        return a * c + b * s, b * c - a * s

    lead = lambda arr: (arr, (h, tr, w), lambda i: (0, i, 0))
    return blockwise(fn, [lead(d1), lead(d2), rows(cos, tr), rows(sin, tr)],
                     [out_rows(n, w, BF16, tr), out_rows(n, w, BF16, tr)], grid=(n // tr,), name=name)


def gate_fwd(proj, y_a, y_b, *, d, ga_col, gb_col, tr=256, name):
    n = proj.shape[0]
    tr = _pick(n, tr, 8)

    def fn(ga, gb, ya, yb):
        return (jax.nn.sigmoid(ga) * ya + jax.nn.sigmoid(gb) * yb,)

    return blockwise(fn, [rows(proj, tr, d, ga_col), rows(proj, tr, d, gb_col), rows(y_a, tr), rows(y_b, tr)],
                     [out_rows(n, d, BF16, tr)], grid=(n // tr,), name=name)[0]


def gate_bwd(proj, y_a, y_b, dmerged, *, d, ga_col, gb_col, tr=256, name):
    n = proj.shape[0]
    tr = _pick(n, tr, 8)

    def fn(ga, gb, ya, yb, dm):
        sa, sb = jax.nn.sigmoid(ga), jax.nn.sigmoid(gb)
        return dm * sa, dm * sb, dm * ya * (sa * (1.0 - sa)), dm * yb * (sb * (1.0 - sb))

    return blockwise(fn, [rows(proj, tr, d, ga_col), rows(proj, tr, d, gb_col), rows(y_a, tr), rows(y_b, tr),
                          rows(dmerged, tr)],
                     [out_rows(n, d, BF16, tr)] * 4, grid=(n // tr,), name=name)


def loss_head(x, g, target, *, tr=256, name):
    n, d = x.shape
    tr = _pick(n, tr, 8)

    def fn(xv, gv, tv):
        r = _rstd(xv)
        xhat = xv * r
        diff = xhat * gv - tv
        loss = 0.5 * jnp.sum(jnp.sum(diff * diff, axis=-1, keepdims=True) / d, axis=0, keepdims=True)
        dy = diff / d
        dxhat = dy * gv
        dx = r * (dxhat - xhat * jnp.mean(dxhat * xhat, axis=-1, keepdims=True))
        return dx, _colsum(dy * xhat), jnp.broadcast_to(loss, (8, LANES))

    return blockwise(fn, [rows(x, tr), whole(g), rows(target, tr)],
                     [out_rows(n, d, F32, tr), out_sum((1, d)), out_sum((8, LANES))],
                     grid=(n // tr,), name=name, sums=(1, 2))


def _adamw_math(wv, mv, vv, gs):
    c1 = 1.0 / (1.0 - ADAM_B1 ** ADAM_STEP)
    c2 = 1.0 / (1.0 - ADAM_B2 ** ADAM_STEP)
    g = gs[0].astype(F32)
    for dev in range(1, N_DEV):
        g = g + gs[dev].astype(F32)
    m_new = ADAM_B1 * mv + (1.0 - ADAM_B1) * g
    v_new = ADAM_B2 * vv + (1.0 - ADAM_B2) * (g * g)
    delta = -ADAM_LR * ((m_new * c1) / (jnp.sqrt(v_new * c2) + ADAM_EPS) + ADAM_WD * wv)
    return g, delta, m_new, v_new


def adamw(w, m, v, g_slots, *, tr=256, name):
    n, c = w.shape
    tr = _pick(n, tr, 8)
    slots = (g_slots, (N_DEV, tr, c), lambda i: (0, i, 0))
    return blockwise(_adamw_math, [rows(w, tr), rows(m, tr), rows(v, tr), slots],
                     [out_rows(n, c, F32, tr)] * 4, grid=(n // tr,), name=name)


def adamw_layers(w, m, v, g_layers, *, name):
    depth, r, c = w.shape
    row_bytes = c * 2 * (depth * N_DEV * g_layers[0].dtype.itemsize + 7 * 4)
    tr = _pick(r, max(8, ADAMW_BLOCK_BYTES // row_bytes // 8 * 8), 8)

    def body(w_ref, m_ref, v_ref, *rest):
        g_refs, out_refs = rest[:depth], rest[depth:]
        for layer in range(depth):
            @pl.when(pl.program_id(0) == layer)
            def _(layer=layer):
                for o, val in zip(out_refs, _adamw_math(w_ref[...], m_ref[...], v_ref[...], g_refs[layer][...])):
                    o[...] = val

    blk = pl.BlockSpec((None, tr, c), lambda l, i: (l, i, 0))
    g_specs = [pl.BlockSpec((N_DEV, tr, c), lambda l, i, layer=layer: (0, jnp.where(l == layer, i, 0), 0))
               for layer in range(depth)]
    return pl.pallas_call(
        body, name=name, grid=(depth, r // tr), in_specs=[blk] * 3 + g_specs, out_specs=[blk] * 4,
        out_shape=[jax.ShapeDtypeStruct(w.shape, F32)] * 4, compiler_params=_params(("arbitrary", "arbitrary")),
    )(w, m, v, *g_layers)


LOG2E = 1.4426950408889634
MLA_FWD_SUB, MLA_FWD_BQ = 512, 2048
MLA_BWD_SUB, MLA_BWD_BQ = 1024, 2048
MLA_FWD_BK, MLA_BWD_BK = 4096, 2048


def mla_fwd(q, k, vt, *, name, ride=None):
    h, s, dq = q.shape
    bq, bk = _pick(s, MLA_FWD_BQ), _pick(s, MLA_FWD_BK)
    sub = min(MLA_FWD_SUB, bq)
    nk = s // bk
    scale = QK_DIM ** -0.5
    c2 = scale * LOG2E
    grid = (h, s // bq, nk)

    def body(q_ref, k_ref, vt_ref, *rest):
        if ride is None:
            o_ref, lse_ref, m_s, l_s, acc = rest
        else:
            n = ride.n
            x_refs, (o_ref, lse_ref), g_refs = rest[:n], rest[n:n + 2], rest[n + 2:2 * n + 2]
            m_s, l_s, acc, *sems = rest[2 * n + 2:]
            first, middle, last = _grid_flags(grid)
            ride_start, ride_middle, ride_finish = ride.steps(x_refs, g_refs, sems)
            pl.when(first)(ride_start)
            pl.when(middle)(ride_middle)
        j = pl.program_id(2)

        @pl.when(j == 0)
        def _():
            m_s[...] = jnp.full_like(m_s, NEG)
            l_s[...] = jnp.zeros_like(l_s)
            acc[...] = jnp.zeros_like(acc)

        def scores(c0):
            return lax.dot_general(k_ref[...], q_ref[pl.ds(c0, sub), :], NT, preferred_element_type=F32)

        starts = list(range(0, bq, sub))
        st_next = scores(starts[0])
        for n, c0 in enumerate(starts):
            cols = pl.ds(c0, sub)
            st = st_next
            if n + 1 < len(starts):
                st_next = scores(starts[n + 1])
            m_prev = m_s[:, cols]
            m_new = jnp.maximum(m_prev, jnp.max(st, axis=0, keepdims=True))
            alpha = jnp.exp2((m_prev - m_new) * c2)
            pt = jnp.exp2((st - m_new) * c2)
            l_s[:, cols] = alpha * l_s[:, cols] + jnp.sum(pt, axis=0, keepdims=True)
            acc[:, cols] = alpha * acc[:, cols] + lax.dot_general(vt_ref[...], pt.astype(BF16), NN,
                                                                  preferred_element_type=F32)
            m_s[:, cols] = m_new

        @pl.when(j == nk - 1)
        def _():
            o_ref[...] = (acc[...] / l_s[...]).T
            lse_ref[...] = m_s[...] * scale + jnp.log(l_s[...])

        if ride is not None:
            pl.when(last)(ride_finish)

    in_specs = [pl.BlockSpec((None, bq, dq), lambda hh, i, j: (hh, i, 0)),
                pl.BlockSpec((None, bk, dq), lambda hh, i, j: (hh, j, 0)),
                pl.BlockSpec((None, V_HEAD, bk), lambda hh, i, j: (hh, 0, j))]
    out_specs = [pl.BlockSpec((bq, V_HEAD), lambda hh, i, j: (i, hh)),
                 pl.BlockSpec((None, 1, bq), lambda hh, i, j: (hh, 0, i))]
    out_shape = [jax.ShapeDtypeStruct((s, h * V_HEAD), F32), jax.ShapeDtypeStruct((h, 1, s), F32)]
    scratch = [pltpu.VMEM((1, bq), F32), pltpu.VMEM((1, bq), F32), pltpu.VMEM((V_HEAD, bq), F32)]
    args = (q, k, vt)
    sem = ("parallel", "parallel", "arbitrary")
    if ride is not None:
        in_specs, out_specs, out_shape, scratch, args = ride.extend(in_specs, out_specs, out_shape, scratch, args)
        sem = ("arbitrary",) * 3
    return pl.pallas_call(body, name=name, grid=grid, in_specs=in_specs, out_specs=out_specs, out_shape=out_shape,
                          scratch_shapes=scratch, compiler_params=_params(sem))(*args)


def mla_delta(do, o, h, *, tr=4096, name):
    s = do.shape[0]
    tr = _pick(s, tr, 8)

    def fn(dov, ov):
        return (jnp.sum(dov.astype(F32) * ov, axis=-1, keepdims=True),)

    blk = lambda arr: (arr, (tr, V_HEAD), lambda hh, i: (i, hh))
    return blockwise(fn, [blk(do), blk(o)], [((h, s, 1), F32, (None, tr, 1), lambda hh, i: (hh, i, 0))],
                     grid=(h, s // tr), name=name)[0]


def mla_bwd(q, k, kt, kv, do, lse, delta, *, v_col0, name, ride=None):
    h, s, dq = q.shape
    bq, bk = _pick(s, MLA_BWD_BQ), _pick(s, MLA_BWD_BK)
    sub = min(MLA_BWD_SUB, bq)
    nq = s // bq
    scale = QK_DIM ** -0.5
    c2 = scale * LOG2E
    grid = (h, s // bk, nq)

    def body(q_ref, k_ref, kt_ref, v_ref, do_ref, lse_ref, delta_ref, *rest):
        if ride is None:
            dq_ref, dk_ref, dv_ref, dk_acc, dv_acc = rest
        else:
            n = ride.n
            x_refs, (dq_ref, dk_ref, dv_ref), g_refs = rest[:n], rest[n:n + 3], rest[n + 3:2 * n + 3]
            dk_acc, dv_acc, *sems = rest[2 * n + 3:]
            first, _, last = _grid_flags(grid)
            ride_start, _, ride_finish = ride.steps(x_refs, g_refs, sems)
            pl.when(first)(ride_start)
        j, i = pl.program_id(1), pl.program_id(2)

        @pl.when(i == 0)
        def _():
            dk_acc[...] = jnp.zeros_like(dk_acc)
            dv_acc[...] = jnp.zeros_like(dv_acc)

        def scores(c0):
            cols = pl.ds(c0, sub)
            return (lax.dot_general(k_ref[...], q_ref[cols, :], NT, preferred_element_type=F32),
                    lax.dot_general(v_ref[...], do_ref[cols, :], NT, preferred_element_type=F32))

        starts = list(range(0, bq, sub))
        nxt = scores(starts[0])
        for n, c0 in enumerate(starts):
            cols = pl.ds(c0, sub)
            st, dpt = nxt
            if n + 1 < len(starts):
                nxt = scores(starts[n + 1])
            q_sub, do_sub = q_ref[cols, :], do_ref[cols, :]
            pt = jnp.exp2(st * c2 - lse_ref[:, cols] * LOG2E)
            ds_b = ((pt * (dpt - delta_ref[:, cols])) * scale).astype(BF16)
            dv_acc[...] += lax.dot_general(pt.astype(BF16), do_sub, NN, preferred_element_type=F32)
            dk_acc[...] += lax.dot_general(ds_b, q_sub, NN, preferred_element_type=F32)
            dq_t = lax.dot_general(kt_ref[...], ds_b, NN, preferred_element_type=F32)

            @pl.when(j == 0)
            def _():
                dq_ref[i, :, cols] = dq_t

            @pl.when(j > 0)
            def _():
                dq_ref[i, :, cols] += dq_t

        @pl.when(i == nq - 1)
        def _():
            dk_ref[...] = dk_acc[...]
            dv_ref[...] = dv_acc[...].astype(dv_ref.dtype)

        if ride is not None:
            pl.when(last)(ride_finish)

    in_specs = [pl.BlockSpec((None, bq, dq), lambda hh, j, i: (hh, i, 0)),
                pl.BlockSpec((None, bk, dq), lambda hh, j, i: (hh, j, 0)),
                pl.BlockSpec((None, dq, bk), lambda hh, j, i: (hh, 0, j)),
                pl.BlockSpec((bk, V_HEAD), lambda hh, j, i: (j, v_col0 + hh)),
                pl.BlockSpec((bq, V_HEAD), lambda hh, j, i: (i, hh)),
                pl.BlockSpec((None, 1, bq), lambda hh, j, i: (hh, 0, i)),
                pl.BlockSpec((None, 1, bq), lambda hh, j, i: (hh, 0, i))]
    out_specs = [pl.BlockSpec((None, nq, dq, bq), lambda hh, j, i: (hh, 0, 0, 0)),
                 pl.BlockSpec((None, bk, dq), lambda hh, j, i: (hh, j, 0)),
                 pl.BlockSpec((bk, V_HEAD), lambda hh, j, i: (j, hh))]
    out_shape = [jax.ShapeDtypeStruct((h, nq, dq, bq), F32), jax.ShapeDtypeStruct((h, s, dq), F32),
                 jax.ShapeDtypeStruct((s, h * V_HEAD), BF16)]
    scratch = [pltpu.VMEM((bk, dq), F32), pltpu.VMEM((bk, V_HEAD), F32)]
    args = (q, k, kt, kv, do, lse, delta)
    sem = ("parallel", "arbitrary", "arbitrary")
    if ride is not None:
        in_specs, out_specs, out_shape, scratch, args = ride.extend(in_specs, out_specs, out_shape, scratch, args)
        sem = ("arbitrary",) * 3
    return pl.pallas_call(body, name=name, grid=grid, in_specs=in_specs, out_specs=out_specs, out_shape=out_shape,
                          scratch_shapes=scratch, compiler_params=_params(sem))(*args)


def _na_window(r, n_rows):
    row_start = jnp.clip(r - NA_KH // 2, 0, n_rows - NA_KH)
    return row_start, row_start - r + (NA_KH - 1)


def _na_probs(sc, bias):
    sc = sc * (NA_HEAD_DIM ** -0.5) + bias
    p = jnp.exp(sc - jnp.max(sc, axis=-1, keepdims=True))
    return p / jnp.sum(p, axis=-1, keepdims=True)


def na_fwd(proj, tables, *, q_col0, k_col0, v_col0, rows_per_step=16, name):
    s = proj.shape[0]
    h = tables.shape[0]
    n_rows = s // GRID_W
    rb = min(rows_per_step, n_rows)
    tq = rb * GRID_W

    def body(q_ref, k_ref, v_ref, b_ref, o_ref):
        i = pl.program_id(1)

        def window(rl):
            row_start, which = _na_window(i * rb + rl, n_rows)
            return pl.ds(pl.multiple_of(row_start * GRID_W, GRID_W), NA_WIN), which

        def scores(rl):
            q_row = q_ref[pl.ds(rl * GRID_W, GRID_W), :].astype(BF16)
            return lax.dot_general(q_row, k_ref[window(rl)[0], :].astype(BF16), NT, preferred_element_type=F32)

        ahead = [scores(rl) for rl in range(rb)]
        for rl in range(rb):
            sc = ahead[rl]
            keys, which = window(rl)
            p = _na_probs(sc, b_ref[which])
            o_ref[pl.ds(rl * GRID_W, GRID_W), :] = lax.dot_general(
                p.astype(BF16), v_ref[keys, :].astype(BF16), NN, preferred_element_type=F32)

    head = lambda col0: pl.BlockSpec((s, NA_HEAD_DIM), lambda hh, i: (0, col0 + hh))
    return pl.pallas_call(
        body, name=name, grid=(h, n_rows // rb),
        in_specs=[pl.BlockSpec((tq, NA_HEAD_DIM), lambda hh, i: (i, q_col0 + hh)), head(k_col0), head(v_col0),
                  pl.BlockSpec((None, NA_KH, GRID_W, NA_WIN), lambda hh, i: (hh, 0, 0, 0))],
        out_specs=pl.BlockSpec((tq, NA_HEAD_DIM), lambda hh, i: (i, hh)),
        out_shape=jax.ShapeDtypeStruct((s, h * NA_HEAD_DIM), F32),
        compiler_params=_params(("parallel", "arbitrary")),
    )(proj, proj, proj, tables)


def na_bwd(proj, tables, do, *, q_col0, k_col0, v_col0, rows_per_step=16, name):
    s = proj.shape[0]
    h = tables.shape[0]
    n_rows = s // GRID_W
    rb = min(rows_per_step, n_rows)
    tq = rb * GRID_W
    scale = NA_HEAD_DIM ** -0.5

    def body(q_ref, k_ref, v_ref, b_ref, do_ref, dq_ref, dk_ref, dv_ref, db_ref):
        i = pl.program_id(1)

        @pl.when(i == 0)
        def _():
            dk_ref[...] = jnp.zeros_like(dk_ref)
            dv_ref[...] = jnp.zeros_like(dv_ref)
            db_ref[...] = jnp.zeros_like(db_ref)

        def window(rl):
            row_start, which = _na_window(i * rb + rl, n_rows)
            return pl.ds(pl.multiple_of(row_start * GRID_W, GRID_W), NA_WIN), which

        def query(rl):
            rows_ = pl.ds(rl * GRID_W, GRID_W)
            return q_ref[rows_, :].astype(BF16), do_ref[rows_, :]

        def scores(rl):
            q_row, do_row = query(rl)
            keys = window(rl)[0]
            return (lax.dot_general(q_row, k_ref[keys, :].astype(BF16), NT, preferred_element_type=F32),
                    lax.dot_general(do_row, v_ref[keys, :].astype(BF16), NT, preferred_element_type=F32))

        ahead = [scores(rl) for rl in range(rb)]
        for rl in range(rb):
            sc, dp = ahead[rl]
            q_row, do_row = query(rl)
            keys, which = window(rl)
            p = _na_probs(sc, b_ref[which])
            ds = p * (dp - jnp.sum(dp * p, axis=-1, keepdims=True))
            db_ref[which] += ds
            ds_b = (ds * scale).astype(BF16)
            dq_ref[pl.ds(rl * GRID_W, GRID_W), :] = lax.dot_general(
                ds_b, k_ref[keys, :].astype(BF16), NN, preferred_element_type=F32).astype(dq_ref.dtype)
            dk_ref[keys, :] += lax.dot_general(ds_b, q_row, TN, preferred_element_type=F32)
            dv_ref[keys, :] += lax.dot_general(p.astype(BF16), do_row, TN, preferred_element_type=F32)

    head = lambda col0: pl.BlockSpec((s, NA_HEAD_DIM), lambda hh, i: (0, col0 + hh))
    rows_of = lambda col0: pl.BlockSpec((tq, NA_HEAD_DIM), lambda hh, i: (i, col0 + hh))
    table = pl.BlockSpec((None, NA_KH, GRID_W, NA_WIN), lambda hh, i: (hh, 0, 0, 0))
    hw = h * NA_HEAD_DIM
    return pl.pallas_call(
        body, name=name, grid=(h, n_rows // rb),
        in_specs=[rows_of(q_col0), head(k_col0), head(v_col0), table, rows_of(0)],
        out_specs=[rows_of(0), head(0), head(0), table],
        out_shape=[jax.ShapeDtypeStruct((s, hw), BF16), jax.ShapeDtypeStruct((s, hw), F32),
                   jax.ShapeDtypeStruct((s, hw), F32), jax.ShapeDtypeStruct(tables.shape, F32)],
        compiler_params=_params(("parallel", "arbitrary")),
    )(proj, proj, proj, tables, do)


def _na_tables():
    qc = np.arange(GRID_W)[:, None]
    kc = np.arange(GRID_W)[None, :]
    col_start = np.clip(qc - NA_KW // 2, 0, GRID_W - NA_KW)
    col_ok = (kc >= col_start) & (kc < col_start + NA_KW)
    dx = np.clip(kc - qc, -(NA_KW - 1), NA_KW - 1) + (NA_KW - 1)
    return col_ok, dx


def na_bias_tables(rpb_l):
    col_ok, _ = _na_tables()
    side = GRID_W - NA_KW
    padded = jnp.pad(rpb_l, ((0, 0), (0, 0), (side, side)))
    t = jnp.stack([padded[:, :, GRID_W - 1 - qc:2 * GRID_W - 1 - qc] for qc in range(GRID_W)], axis=2)
    t = jnp.where(col_ok[None, None], t, NEG)
    t = t.transpose(0, 2, 1, 3)
    return jnp.stack([t[:, :, w:w + NA_KH].reshape(t.shape[0], GRID_W, NA_WIN) for w in range(NA_KH)], axis=1)


def na_bias_grad(dtables, *, name):
    h = dtables.shape[0]
    _, dx = _na_tables()
    n_dy, n_dx = 2 * NA_KH - 1, 2 * NA_KW - 1
    d5 = dtables.reshape(h, NA_KH, GRID_W, NA_KH, GRID_W)
    t = sum(jnp.pad(d5[:, w], ((0, 0), (0, 0), (w, n_dy - NA_KH - w), (0, 0))) for w in range(NA_KH))
    onehot = np.zeros((GRID_W * GRID_W, LANES), np.float32)
    onehot[np.arange(GRID_W * GRID_W), dx.reshape(-1)] = 1.0
    t = t.transpose(0, 2, 1, 3).reshape(h * n_dy, GRID_W * GRID_W)
    t = jnp.pad(t, ((0, (-t.shape[0]) % 8), (0, 0)))
    out = mm(t, jnp.asarray(onehot), mode="nn", name=name, exact=True, tk=1024)
    return out[:h * n_dy, :n_dx].reshape(h, n_dy, n_dx)


def _flip(v, bit):
    return 1 - v if bit else v


class Part:
    def __init__(self, shape, kind, dtype):
        self.r, self.c = shape
        self.kind, self.dtype = kind, dtype

    @property
    def whole_shape(self):
        return {"row": (N_DEV * self.r, self.c), "col": (self.r, N_DEV * self.c),
                "packed": (N_DEV, self.r, self.c)}[self.kind]

    @property
    def packed_shape(self):
        return (N_DEV, self.r, self.c)

    def shard_of(self, ref, j):
        if self.kind == "row":
            return ref.at[pl.ds(pl.multiple_of(j * self.r, 8), self.r), :]
        if self.kind == "col":
            return ref.at[:, pl.ds(pl.multiple_of(j * self.c, LANES), self.c)]
        return ref.at[j]


def comm_scratch(n_parts):
    n = n_parts * (N_DEV - 1)
    return [pltpu.SemaphoreType.DMA((n,)), pltpu.SemaphoreType.DMA((n,)), pltpu.SemaphoreType.DMA((n_parts,))]


def gather_plan(parts, x_refs, out_refs, send_sems, recv_sems, local_sems):
    x, y, c = lax.axis_index("x"), lax.axis_index("y"), lax.axis_index("c")
    me, sibling = (x, y, c), (x, y, 1 - c)
    chips = [(1 - x, y), (x, 1 - y), (1 - x, 1 - y)]

    def place(w, px, py, pc):
        return parts[w].shard_of(out_refs[w], 4 * px + 2 * py + pc)

    def copy(k, blk, to, own=False):
        return [pltpu.make_async_remote_copy(
            src_ref=x_refs[w] if own else place(w, *blk), dst_ref=place(w, *blk),
            send_sem=send_sems.at[w * (N_DEV - 1) + k], recv_sem=recv_sems.at[w * (N_DEV - 1) + k],
            device_id=to, device_id_type=pl.DeviceIdType.MESH) for w in range(len(parts))]

    def mine():
        return [pltpu.make_async_copy(x_refs[w], place(w, *me), local_sems.at[w]) for w in range(len(parts))]

    def first():
        return copy(0, me, sibling, own=True) + [cp for j, chip in enumerate(chips)
                                                 for cp in copy(1 + j, me, (*chip, c), own=True)]

    def passed(j):
        return copy(4 + j, (*chips[j], c), sibling)

    def start():
        for cp in mine() + first():
            cp.start()

    def forward():
        for j, chip in enumerate(chips):
            for cp in copy(1 + j, (*chip, c), me):
                cp.wait_recv()
            for cp in passed(j):
                cp.start()

    def finish():
        for cp in copy(0, sibling, me):
            cp.wait_recv()
        for j, chip in enumerate(chips):
            for cp in copy(4 + j, (*chip, 1 - c), me):
                cp.wait_recv()
        for cp in first() + [cp for j in range(len(chips)) for cp in passed(j)]:
            cp.wait_send()
        for cp in mine():
            cp.wait()

    return start, forward, finish


def exchange_plan(parts, x_refs, out_refs, send_sems, recv_sems, local_sems):
    x, y, c = lax.axis_index("x"), lax.axis_index("y"), lax.axis_index("c")
    me = 4 * x + 2 * y + c

    def peer_of(k):
        peer = (_flip(x, k & 4), _flip(y, k & 2), _flip(c, k & 1))
        return peer, 4 * peer[0] + 2 * peer[1] + peer[2]

    def copies(k, arriving):
        peer, theirs = peer_of(k)
        return [pltpu.make_async_remote_copy(
            src_ref=parts[w].shard_of(x_refs[w], me if arriving else theirs),
            dst_ref=out_refs[w].at[theirs if arriving else me],
            send_sem=send_sems.at[w * (N_DEV - 1) + k - 1], recv_sem=recv_sems.at[w * (N_DEV - 1) + k - 1],
            device_id=peer, device_id_type=pl.DeviceIdType.MESH) for w in range(len(parts))]

    def mine():
        return [pltpu.make_async_copy(parts[w].shard_of(x_refs[w], me), out_refs[w].at[me], local_sems.at[w])
                for w in range(len(parts))]

    def start():
        for cp in mine():
            cp.start()
        for k in range(1, N_DEV):
            for cp in copies(k, arriving=False):
                cp.start()

    def finish():
        for k in range(1, N_DEV):
            for cp in copies(k, arriving=True):
                cp.wait_recv()
        for k in range(1, N_DEV):
            for cp in copies(k, arriving=False):
                cp.wait_send()
        for cp in mine():
            cp.wait()

    return start, finish


class Collective:
    def __init__(self, kind, parts, arrays):
        self.kind, self.parts, self.arrays = kind, parts, list(arrays)
        self.n = len(parts)
        shapes = [p.whole_shape if kind == "gather" else p.packed_shape for p in parts]
        self.out_shape = [jax.ShapeDtypeStruct(s, p.dtype) for s, p in zip(shapes, parts)]

    def extend(self, in_specs, out_specs, out_shape, scratch, args):
        any_spec = pl.BlockSpec(memory_space=pl.ANY)
        return (in_specs + [any_spec] * self.n, out_specs + [any_spec] * self.n, out_shape + self.out_shape,
                scratch + comm_scratch(self.n), args + tuple(self.arrays))

    def steps(self, x_refs, out_refs, sems):
        plan = (gather_plan if self.kind == "gather" else exchange_plan)(self.parts, x_refs, out_refs, *sems)
        return plan[0], (plan[1] if len(plan) == 3 else None), plan[-1]

    def run(self, name):
        n = self.n

        def body(*refs):
            for step in self.steps(refs[:n], refs[n:2 * n], refs[2 * n:]):
                if step is not None:
                    step()

        in_specs, out_specs, out_shape, scratch, args = self.extend([], [], [], [], ())
        return pl.pallas_call(body, name=name, in_specs=in_specs, out_specs=out_specs, out_shape=out_shape,
                              scratch_shapes=scratch)(*args)


def _grid_flags(grid):
    ids = [pl.program_id(ax) for ax in range(len(grid))]
    inner_zero = functools.reduce(jnp.logical_and, [i == 0 for i in ids[1:]])
    first = jnp.logical_and(ids[0] == 0, inner_zero)
    middle = jnp.logical_and(ids[0] == grid[0] - 1, inner_zero)
    last = functools.reduce(jnp.logical_and, [i == g - 1 for i, g in zip(ids, grid)])
    return first, middle, last


SHARDED = ("w_in", "w_uq", "w_ukv", "w_o_mla", "w_o_na", "w_out", "w_ff1", "w_ff2")
ROW_SHARDED = ("w_out", "w_ff2")
TRANSPOSED = ("w_in",)
REPLICATED = ("norm_mix", "norm_qa", "norm_kva", "rpb", "norm_mlp", "norm_final")
WEIGHTS = ("norm_mix", "w_in", "norm_qa", "w_uq", "norm_kva", "w_ukv", "rpb", "w_o_mla", "w_o_na", "w_out",
           "norm_mlp", "w_ff1", "w_ff2", "norm_final")


def part_of(name, shard_shape):
    r, c = shard_shape
    kind = "row" if name in ROW_SHARDED + TRANSPOSED else ("col" if c % LANES == 0 else "packed")
    return Part((r, c), kind, BF16)


def _whole(part, gathered):
    return gathered.transpose(1, 0, 2).reshape(part.r, -1) if part.kind == "packed" else gathered


def _for_exchange(part, full):
    return full.reshape(part.r, N_DEV, part.c).transpose(1, 0, 2) if part.kind == "packed" else full


class Dims:
    def __init__(self, x, w_in, norm_qa, norm_kva, rpb, w_o_mla, w_o_na, w_ff1):
        self.s, self.d = x.shape[1], x.shape[2]
        self.depth = w_in.shape[0]
        self.q_lora, self.kv_lora = norm_qa.shape[1], norm_kva.shape[1]
        self.mla_w, self.na_w = w_o_mla.shape[1], w_o_na.shape[1]
        self.mla_h, self.na_h = self.mla_w // V_HEAD, self.na_w // NA_HEAD_DIM
        self.d_ff = w_ff1.shape[2] * N_DEV
        assert rpb.shape[1] == self.na_h and self.s % GRID_W == 0 and self.s // GRID_W >= NA_KH
        self.in_lo = self.q_lora + self.kv_lora
        self.main_w = self.in_lo + 3 * self.na_w + 2 * self.d
        assert self.q_lora == self.kv_lora and self.in_lo % self.na_w == 0 and self.in_lo % LANES == 0
        assert (self.in_lo + 3 * self.na_w) % self.d == 0
        self.q_col0 = self.in_lo // NA_HEAD_DIM
        self.k_off = self.in_lo + self.na_w
        self.v_off = self.in_lo + 2 * self.na_w
        self.ga_col = (self.in_lo + 3 * self.na_w) // self.d
        self.gb_col = self.ga_col + 1


def _split_w_in(dm, wt):
    lo = dm.in_lo
    main = jnp.concatenate([wt[:lo], wt[lo + QK_ROPE:]], axis=0)
    kpe = jnp.pad(wt[lo:lo + QK_ROPE], ((0, LANES - QK_ROPE), (0, 0)))
    return main, kpe


def _join_w_in(dm, main, kpe):
    lo = dm.in_lo
    return jnp.concatenate([main[:lo], kpe[:QK_ROPE], main[lo:]], axis=0)


def _split_heads(w, h, widths):
    r = w.shape[0]
    w3 = w.reshape(r, h, sum(widths))
    out, o = [], 0
    for wd in widths:
        out.append(w3[:, :, o:o + wd].reshape(r, h * wd))
        o += wd
    return out


def _join_heads(parts, h):
    r = parts[0].shape[0]
    return jnp.concatenate([p.reshape(r, h, -1) for p in parts], axis=2).reshape(r, -1)


UQ_WIDTHS = (QK_NOPE, HALF_ROPE, HALF_ROPE)
UKV_WIDTHS = (QK_NOPE, V_HEAD)


def _by_head(parts, h):
    s = parts[0].shape[0]
    return jnp.concatenate([p.reshape(s, h, -1) for p in parts], axis=2).transpose(1, 0, 2)


def _from_head(t, widths):
    h, s, _ = t.shape
    t = t.transpose(1, 0, 2)
    out, o = [], 0
    for wd in widths:
        out.append(t[:, :, o:o + wd].reshape(s, h * wd))
        o += wd
    return out


def layer_fwd(dm, lname, x, w, g, cos_q, sin_q, cos_k, sin_k, ride, late_weights):
    s, h = dm.s, dm.mla_h
    u = rmsnorm_fwd(x, g["norm_mix"], name=f"{lname}_norm_mix")
    proj = mm(u, w["in_main"], mode="nt", name=f"{lname}_proj")
    kpe = mm(u, w["in_kpe"], mode="nt", name=f"{lname}_proj_kpe")
    qn = rmsnorm_fwd(proj, g["norm_qa"], width=dm.q_lora, col=0, name=f"{lname}_norm_qa")
    kvn = rmsnorm_fwd(proj, g["norm_kva"], width=dm.kv_lora, col=1, name=f"{lname}_norm_kva")
    q = mm(qn, w["uq"], mode="nn", name=f"{lname}_uq")
    kv = mm(kvn, w["ukv"], mode="nn", out_dtypes=(BF16,), name=f"{lname}_ukv")
    nope_w, half_w = h * QK_NOPE, h * HALF_ROPE
    q1, q2 = rope_pair(q[:, nope_w:nope_w + half_w], q[:, nope_w + half_w:], cos_q, sin_q, name=f"{lname}_rope_q")
    k1, k2 = rope_pair(kpe[:, :HALF_ROPE], kpe[:, HALF_ROPE:QK_ROPE], cos_k, sin_k, name=f"{lname}_rope_k")
    qh = _by_head([q[:, :nope_w].astype(BF16), q1, q2], h)
    kh = _by_head([kv[:, :nope_w], jnp.tile(k1, (1, h)), jnp.tile(k2, (1, h))], h)
    vt = kv[:, nope_w:].reshape(s, h, V_HEAD).transpose(1, 2, 0)
    o_a, lse, *gathered = mla_fwd(qh, kh, vt, name=f"{lname}_mla", ride=ride)
    w = {**w, **late_weights(gathered)}
    y_a = mm(o_a, w["o_mla"], mode="nn", name=f"{lname}_o_mla")

    tables = na_bias_tables(g["rpb"])
    o_b = na_fwd(proj, tables, q_col0=dm.q_col0, k_col0=dm.k_off // NA_HEAD_DIM, v_col0=dm.v_off // NA_HEAD_DIM,
                 name=f"{lname}_na")
    y_b = mm(o_b, w["o_na"], mode="nn", name=f"{lname}_o_na")

    merged = gate_fwd(proj, y_a, y_b, d=dm.d, ga_col=dm.ga_col, gb_col=dm.gb_col, name=f"{lname}_gate")
    x1 = mm(merged, w["out"], mode="nn", epi=lambda r, res: (r + res,), extras=(x,), name=f"{lname}_out")
    u2 = rmsnorm_fwd(x1, g["norm_mlp"], name=f"{lname}_norm_mlp")
    hid, act = mm(u2, w["ff1"], mode="nn", out_dtypes=(F32, BF16),
                  epi=lambda r: (r, jnp.square(jnp.maximum(r, 0.0))), name=f"{lname}_ff1")
    x2 = mm(act, w["ff2"], mode="nn", epi=lambda r, res: (r + res,), extras=(x1,), name=f"{lname}_ff2")
    saved = dict(x=x, u=u, proj=proj, qn=qn, kvn=kvn, kv=kv, qh=qh, kh=kh, o_a=o_a, lse=lse, y_a=y_a,
                 tables=tables, o_b=o_b, y_b=y_b, merged=merged, x1=x1, u2=u2, hid=hid, act=act)
    return x2, saved, w, gathered


def layer_bwd(dm, lname, dx2, w, g, sv, cos_q, sin_q, cos_k, sin_k, ride_of):
    h = dm.mla_h
    gw, gr = {}, {}
    gw["ff2"] = mm(sv["act"], dx2, mode="tn", out_dtypes=(BF16,), name=f"{lname}_d_ff2")
    dh = mm(dx2, w["ff2"], mode="nt", out_dtypes=(BF16,), extras=(sv["hid"],),
            epi=lambda r, hv: (r * (2.0 * jnp.maximum(hv, 0.0)),), name=f"{lname}_d_act")
    gw["ff1"] = mm(sv["u2"], dh, mode="tn", out_dtypes=(BF16,), name=f"{lname}_d_ff1")
    du2 = mm(dh, w["ff1"], mode="nt", name=f"{lname}_d_u2")
    dx1, gr["norm_mlp"] = rmsnorm_bwd(sv["x1"], g["norm_mlp"], du2, dx2, name=f"{lname}_d_norm_mlp")
    gw["out"] = mm(sv["merged"], dx1, mode="tn", out_dtypes=(BF16,), name=f"{lname}_d_out")
    dmerged = mm(dx1, w["out"], mode="nt", name=f"{lname}_d_merged")
    dy_a, dy_b, dga, dgb = gate_bwd(sv["proj"], sv["y_a"], sv["y_b"], dmerged, d=dm.d, ga_col=dm.ga_col,
                                    gb_col=dm.gb_col, name=f"{lname}_d_gate")
    gw["o_na"] = mm(sv["o_b"], dy_b, mode="tn", out_dtypes=(BF16,), name=f"{lname}_d_o_na")
    do_b = mm(dy_b, w["o_na"], mode="nt", out_dtypes=(BF16,), name=f"{lname}_d_ob")
    dq_na, dk_na, dv_na, dtables = na_bwd(sv["proj"], sv["tables"], do_b, q_col0=dm.q_col0,
                                          k_col0=dm.k_off // NA_HEAD_DIM, v_col0=dm.v_off // NA_HEAD_DIM,
                                          name=f"{lname}_d_na")
    gr["rpb"] = na_bias_grad(dtables, name=f"{lname}_d_rpb")
    gw["o_mla"] = mm(sv["o_a"], dy_a, mode="tn", out_dtypes=(BF16,), name=f"{lname}_d_o_mla")
    do_a = mm(dy_a, w["o_mla"], mode="nt", out_dtypes=(BF16,), name=f"{lname}_d_oa")
    delta = mla_delta(do_a, sv["o_a"], h, name=f"{lname}_d_mla_delta").reshape(h, 1, dm.s)
    dqt, dkh, dv, *received = mla_bwd(sv["qh"], sv["kh"], sv["kh"].transpose(0, 2, 1), sv["kv"], do_a, sv["lse"],
                                      delta, v_col0=h, name=f"{lname}_d_mla", ride=ride_of(gw))
    dqh = dqt.transpose(0, 1, 3, 2).reshape(h, dm.s, QK_DIM)
    dq_nope, dq1, dq2 = _from_head(dqh, UQ_WIDTHS)
    dq1, dq2 = rope_pair(dq1, dq2, cos_q, -sin_q, name=f"{lname}_d_rope_q")
    dq = jnp.concatenate([dq_nope.astype(BF16), dq1, dq2], axis=1)
    gw["uq"] = mm(sv["qn"], dq, mode="tn", out_dtypes=(BF16,), name=f"{lname}_d_uq")
    dqn = mm(dq, w["uq"], mode="nt", name=f"{lname}_d_qn")
    dk_nope = _from_head(dkh[:, :, :QK_NOPE], (QK_NOPE,))[0]
    dk1, dk2 = rope_pair_headsum(dkh[:, :, QK_NOPE:QK_NOPE + HALF_ROPE], dkh[:, :, QK_NOPE + HALF_ROPE:],
                                 cos_k, sin_k, name=f"{lname}_d_rope_k")
    dkv = jnp.concatenate([dk_nope.astype(BF16), dv], axis=1)
    gw["ukv"] = mm(sv["kvn"], dkv, mode="tn", out_dtypes=(BF16,), name=f"{lname}_d_ukv")
    dkvn = mm(dkv, w["ukv"], mode="nt", name=f"{lname}_d_kvn")
    dc_q, gr["norm_qa"] = rmsnorm_bwd(sv["proj"], g["norm_qa"], dqn, width=dm.q_lora, col=0, out_dtype=BF16,
                                      name=f"{lname}_d_norm_qa")
    dc_kv, gr["norm_kva"] = rmsnorm_bwd(sv["proj"], g["norm_kva"], dkvn, width=dm.kv_lora, col=1, out_dtype=BF16,
                                        name=f"{lname}_d_norm_kva")
    dproj = jnp.concatenate([dc_q, dc_kv, dq_na, dk_na.astype(BF16), dv_na.astype(BF16), dga, dgb], axis=1)
    dkpe = jnp.concatenate([dk1, dk2, jnp.zeros((dm.s, LANES - QK_ROPE), BF16)], axis=1)
    gw["in_main"] = mm(dproj, sv["u"], mode="tn", out_dtypes=(BF16,), name=f"{lname}_d_in")
    gw["in_kpe"] = mm(dkpe, sv["u"], mode="tn", out_dtypes=(BF16,), name=f"{lname}_d_in_kpe")
    du_k = mm(dkpe, w["in_kpe"], mode="nn", name=f"{lname}_d_u_kpe")
    du = mm(dproj, w["in_main"], mode="nn", epi=lambda r, res: (r + res,), extras=(du_k,), name=f"{lname}_d_u")
    dx, gr["norm_mix"] = rmsnorm_bwd(sv["x"], g["norm_mix"], du, dx1, name=f"{lname}_d_norm_mix")
    return dx, gw, gr, received


EARLY = ("w_in", "w_uq", "w_ukv")
LATE = ("w_o_mla", "w_o_na", "w_out", "w_ff1", "w_ff2")
LATE_KEYS = ("o_mla", "o_na", "out", "ff1", "ff2")


def _early_weights(dm, full):
    main, kpe = _split_w_in(dm, full["w_in"])
    return dict(in_main=main, in_kpe=kpe,
                uq=jnp.concatenate(_split_heads(full["w_uq"], dm.mla_h, UQ_WIDTHS), axis=1),
                ukv=jnp.concatenate(_split_heads(full["w_ukv"], dm.mla_h, UKV_WIDTHS), axis=1))


def _early_grads(dm, gw):
    h = dm.mla_h
    nope_w, half_w = h * QK_NOPE, h * HALF_ROPE
    uq = gw["uq"]
    ukv = gw["ukv"]
    return {"w_in": _join_w_in(dm, gw["in_main"], gw["in_kpe"]),
            "w_uq": _join_heads([uq[:, :nope_w], uq[:, nope_w:nope_w + half_w], uq[:, nope_w + half_w:]], h),
            "w_ukv": _join_heads([ukv[:, :nope_w], ukv[:, nope_w:]], h)}


def _pack_replicated(dm, parts):
    flat = jnp.concatenate([parts[n].reshape(-1) for n in REPLICATED])
    n = flat.shape[0]
    rows_ = -(-n // (8 * LANES)) * 8
    return jnp.pad(flat, (0, rows_ * LANES - n)).reshape(rows_, LANES)


def kernel(x, norm_mix, w_in, norm_qa, w_uq, norm_kva, w_ukv, rpb, w_o_mla, w_o_na, w_out, norm_mlp, w_ff1, w_ff2, norm_final, loss_target, m_norm_mix, m_w_in, m_norm_qa, m_w_uq, m_norm_kva, m_w_ukv, m_rpb, m_w_o_mla, m_w_o_na, m_w_out, m_norm_mlp, m_w_ff1, m_w_ff2, m_norm_final, v_norm_mix, v_w_in, v_norm_qa, v_w_uq, v_norm_kva, v_w_ukv, v_rpb, v_w_o_mla, v_w_o_na, v_w_out, v_norm_mlp, v_w_ff1, v_w_ff2, v_norm_final):
    dm = Dims(x, w_in, norm_qa, norm_kva, rpb, w_o_mla, w_o_na, w_ff1)
    params = dict(norm_mix=norm_mix, w_in=w_in, norm_qa=norm_qa, w_uq=w_uq, norm_kva=norm_kva, w_ukv=w_ukv, rpb=rpb,
                  w_o_mla=w_o_mla, w_o_na=w_o_na, w_out=w_out, norm_mlp=norm_mlp, w_ff1=w_ff1, w_ff2=w_ff2,
                  norm_final=norm_final)
    mom_m = dict(norm_mix=m_norm_mix, w_in=m_w_in, norm_qa=m_norm_qa, w_uq=m_w_uq, norm_kva=m_norm_kva, w_ukv=m_w_ukv,
                 rpb=m_rpb, w_o_mla=m_w_o_mla, w_o_na=m_w_o_na, w_out=m_w_out, norm_mlp=m_norm_mlp, w_ff1=m_w_ff1,
                 w_ff2=m_w_ff2, norm_final=m_norm_final)
    mom_v = dict(norm_mix=v_norm_mix, w_in=v_w_in, norm_qa=v_norm_qa, w_uq=v_w_uq, norm_kva=v_norm_kva, w_ukv=v_w_ukv,
                 rpb=v_rpb, w_o_mla=v_w_o_mla, w_o_na=v_w_o_na, w_out=v_w_out, norm_mlp=v_norm_mlp, w_ff1=v_w_ff1,
                 w_ff2=v_w_ff2, norm_final=v_norm_final)
    depth, s, h = dm.depth, dm.s, dm.mla_h

    pos = jnp.arange(s, dtype=F32)
    inv_freq = 1.0 / (ROPE_THETA ** (jnp.arange(0, QK_ROPE, 2, dtype=F32) / QK_ROPE))
    ang = pos[:, None] * inv_freq[None, :]
    cos_k, sin_k = jnp.cos(ang), jnp.sin(ang)
    cos_q, sin_q = jnp.tile(cos_k, (1, h)), jnp.tile(sin_k, (1, h))

    held = lambda d, n: d[n].transpose(0, 2, 1) if n in TRANSPOSED else d[n]
    part = {n: part_of(n, held(params, n).shape[1:]) for n in SHARDED}
    shard = lambda n, l: held(params, n)[l].astype(BF16)
    whole = lambda names, arrays: {n: _whole(part[n], a) for n, a in zip(names, arrays)}
    gains = [dict(norm_mix=norm_mix[l][None], norm_qa=norm_qa[l][None], norm_kva=norm_kva[l][None],
                  norm_mlp=norm_mlp[l][None], rpb=rpb[l]) for l in range(depth)]

    xl = x[0]
    saved, weights = [], []
    early = Collective("gather", [part[n] for n in EARLY], [shard(n, 0) for n in EARLY]).run("gather_weights")
    for l in range(depth):
        nxt = EARLY if l + 1 < depth else ()
        ride = Collective("gather", [part[n] for n in LATE + nxt],
                          [shard(n, l) for n in LATE] + [shard(n, l + 1) for n in nxt])
        late_weights = lambda got: dict(zip(LATE_KEYS, whole(LATE, got[:len(LATE)]).values()))
        xl, sv, w, got = layer_fwd(dm, "fwd", xl, _early_weights(dm, whole(EARLY, early)), gains[l], cos_q, sin_q,
                                   cos_k, sin_k, ride, late_weights)
        early = got[len(LATE):]
        saved.append(sv)
        weights.append(w)
    dx, g_final, loss_part = loss_head(xl, norm_final[None], loss_target[0], name="loss_head")
    loss = lax.psum(loss_part[0, 0], MESH_AXES)

    rep = {n: [None] * depth for n in REPLICATED if n != "norm_final"}
    recv = [{} for _ in range(depth)]
    pending = {}
    for l in reversed(range(depth)):
        def ride_of(gw, pending=pending):
            ready = {**dict(zip(LATE, [gw[k] for k in LATE_KEYS])), **pending}
            return Collective("exchange", [part[n] for n in ready], [_for_exchange(part[n], a) for n, a in ready.items()])

        dx, gw, gr, received = layer_bwd(dm, "bwd", dx, weights[l], gains[l], saved[l], cos_q, sin_q, cos_k, sin_k,
                                         ride_of)
        recv[l].update(zip(LATE, received[:len(LATE)]))
        if pending:
            recv[l + 1].update(zip(EARLY, received[len(LATE):]))
        for n in gr:
            rep[n][l] = gr[n]
        pending = _early_grads(dm, gw)
    last = Collective("exchange", [part[n] for n in EARLY], [_for_exchange(part[n], pending[n]) for n in EARLY])
    recv[0].update(zip(EARLY, last.run("scatter_grads")))
    rep_parts = {n: jnp.stack(rep[n]) for n in rep}
    rep_parts["norm_final"] = g_final
    packed = _pack_replicated(dm, rep_parts)
    rep_all = Collective("gather", [Part(packed.shape, "packed", F32)], [packed]).run("gather_small_grads")[0]

    outs = {}
    for n in SHARDED:
        res = adamw_layers(held(params, n), held(mom_m, n), held(mom_v, n), [recv[l][n] for l in range(depth)],
                           name=f"adamw_{n}")
        outs[n] = [a.transpose(0, 2, 1) for a in res] if n in TRANSPOSED else res
    n_rep = sum(int(np.prod(params[n].shape)) for n in REPLICATED)
    pack = lambda d: _pack_replicated(dm, d)
    res = adamw(pack(params), pack(mom_m), pack(mom_v), rep_all, name="adamw_replicated")
    off = 0
    for n in REPLICATED:
        size = int(np.prod(params[n].shape))
        outs[n] = [a.reshape(-1)[off:off + size].reshape(params[n].shape) for a in res]
        off += size
    assert off == n_rep

    grad_x = dx[None]
    return (loss, grad_x, *[outs[n][0] for n in WEIGHTS], *[outs[n][1] for n in WEIGHTS],
            *[outs[n][2] for n in WEIGHTS], *[outs[n][3] for n in WEIGHTS])
```

```python
import functools

import numpy as np
import jax
import jax.numpy as jnp
from jax import lax
from jax.experimental import pallas as pl
from jax.experimental.pallas import tpu as pltpu

F32 = jnp.float32
BF16 = jnp.bfloat16
MESH_AXES = ("x", "y", "c")
N_DEV = 8
LANES = 128

QK_NOPE = 128
QK_ROPE = 64
HALF_ROPE = QK_ROPE // 2
V_HEAD = 128
QK_DIM = QK_NOPE + QK_ROPE
NA_HEAD_DIM = 128
GRID_W = 64
NA_KH = 8
NA_KW = 16
NA_WIN = NA_KH * GRID_W
ROPE_THETA = 10000.0
EPS = 1e-6
NEG = -1e30

ADAM_LR = 0.001
ADAM_B1 = 0.9
ADAM_B2 = 0.999
ADAM_EPS = 1e-08
ADAM_WD = 0.01
ADAM_STEP = 10

VMEM_LIMIT_V7X = 56 * 1024 * 1024
ADAMW_BLOCK_BYTES = 24 * 1024 * 1024

NN = (((1,), (0,)), ((), ()))
NT = (((1,), (1,)), ((), ()))
TN = (((0,), (0,)), ((), ()))


def _pick(dim, target, align=LANES):
    if dim <= target:
        return dim
    t = (target // align) * align
    while t >= align:
        if dim % t == 0:
            return t
        t -= align
    return dim


def _params(sem):
    return pltpu.CompilerParams(dimension_semantics=sem, vmem_limit_bytes=VMEM_LIMIT_V7X)


def mm(a, b, *, mode, name, out_dtypes=(F32,), epi=None, extras=(), tm=1024, tn=1024, tk=2048, exact=False):
    if mode == "nn":
        (m, k), (k2, n) = a.shape, b.shape
    elif mode == "nt":
        (m, k), (n, k2) = a.shape, b.shape
    else:
        (k, m), (k2, n) = a.shape, b.shape
    assert k == k2, (a.shape, b.shape, mode)
    tm, tn, tk = _pick(m, tm), _pick(n, tn), _pick(k, tk)
    nk = k // tk
    a_spec = pl.BlockSpec((tk, tm), lambda i, j, s: (s, i)) if mode == "tn" else pl.BlockSpec((tm, tk), lambda i, j, s: (i, s))
    b_spec = pl.BlockSpec((tn, tk), lambda i, j, s: (j, s)) if mode == "nt" else pl.BlockSpec((tk, tn), lambda i, j, s: (s, j))
    tile = pl.BlockSpec((tm, tn), lambda i, j, s: (i, j))
    dims = {"nn": NN, "nt": NT, "tn": TN}[mode]
    n_extra, n_out = len(extras), len(out_dtypes)

    def product(a_ref, b_ref):
        if exact:
            return lax.dot_general(a_ref[...], b_ref[...], dims, precision=lax.Precision.HIGHEST,
                                   preferred_element_type=F32)
        return lax.dot_general(a_ref[...].astype(BF16), b_ref[...].astype(BF16), dims, preferred_element_type=F32)

    def finish(r, extra_refs, out_refs):
        res = (r,) if epi is None else epi(r, *[e[...] for e in extra_refs])
        for o, v in zip(out_refs, res):
            o[...] = v.astype(o.dtype)

    def body_one_step(a_ref, b_ref, *rest):
        finish(product(a_ref, b_ref), rest[:n_extra], rest[n_extra:])

    def body(a_ref, b_ref, *rest):
        extra_refs, out_refs, acc = rest[:n_extra], rest[n_extra:n_extra + n_out], rest[-1]
        step = pl.program_id(2)

        @pl.when(step == 0)
        def _():
            acc[...] = product(a_ref, b_ref)

        @pl.when(step > 0)
        def _():
            acc[...] += product(a_ref, b_ref)

        @pl.when(step == nk - 1)
        def _():
            finish(acc[...], extra_refs, out_refs)

    outs = pl.pallas_call(
        body_one_step if nk == 1 else body, name=name, grid=(m // tm, n // tn, nk),
        in_specs=[a_spec, b_spec] + [tile] * n_extra,
        out_specs=[tile] * n_out,
        out_shape=[jax.ShapeDtypeStruct((m, n), d) for d in out_dtypes],
        scratch_shapes=[] if nk == 1 else [pltpu.VMEM((tm, tn), F32)],
        compiler_params=_params(("parallel", "parallel", "arbitrary")),
    )(a, b, *extras)
    return outs[0] if n_out == 1 else outs


def blockwise(fn, ins, outs, *, grid, name, sums=()):
    n_in, n_axes = len(ins), len(grid)

    def body(*refs):
        res = fn(*[r[...] for r in refs[:n_in]])
        first = functools.reduce(jnp.logical_and, [pl.program_id(ax) == 0 for ax in range(n_axes)])
        for idx, (o, v) in enumerate(zip(refs[n_in:], res)):
            if idx in sums:
                @pl.when(first)
                def _(o=o):
                    o[...] = jnp.zeros_like(o)

                o[...] += v.astype(o.dtype)
            else:
                o[...] = v.astype(o.dtype)

    sem = ("arbitrary" if sums else "parallel",) * n_axes
    res = pl.pallas_call(
        body, name=name, grid=grid,
        in_specs=[pl.BlockSpec(blk, imap) for _, blk, imap in ins],
        out_specs=[pl.BlockSpec(blk, imap) for _, _, blk, imap in outs],
        out_shape=[jax.ShapeDtypeStruct(shape, dt) for shape, dt, _, _ in outs],
        compiler_params=_params(sem),
    )(*[a for a, _, _ in ins])
    return res


def rows(arr, tr, width=None, col=0):
    width = arr.shape[1] if width is None else width
    return (arr, (tr, width), lambda i, col=col: (i, col))


def whole(arr):
    return (arr, arr.shape, lambda i: (0, 0))


def out_rows(n_rows, width, dtype, tr):
    return ((n_rows, width), dtype, (tr, width), lambda i: (i, 0))


def out_sum(shape, dtype=F32):
    return (shape, dtype, shape, lambda i: (0, 0))


def _rstd(x):
    return lax.rsqrt(jnp.mean(x * x, axis=-1, keepdims=True) + EPS)


def _colsum(v):
    return jnp.sum(v, axis=0, keepdims=True)


def rmsnorm_fwd(x, g, *, width=None, col=0, tr=256, name):
    n = x.shape[0]
    tr = _pick(n, tr, 8)
    width = x.shape[1] if width is None else width

    def fn(xv, gv):
        return ((xv * _rstd(xv)) * gv,)

    return blockwise(fn, [rows(x, tr, width, col), whole(g)], [out_rows(n, width, BF16, tr)],
                     grid=(n // tr,), name=name)[0]


def rmsnorm_bwd(x, g, dy, res=None, *, width=None, col=0, out_dtype=F32, tr=256, name):
    n = x.shape[0]
    tr = _pick(n, tr, 8)
    width = x.shape[1] if width is None else width

    def fn(xv, gv, dyv, *resv):
        dyv = dyv.astype(F32)
        r = _rstd(xv)
        xhat = xv * r
        dxhat = dyv * gv
        dx = r * (dxhat - xhat * jnp.mean(dxhat * xhat, axis=-1, keepdims=True))
        if resv:
            dx = dx + resv[0]
        return dx, _colsum(dyv * xhat)

    ins = [rows(x, tr, width, col), whole(g), rows(dy, tr)] + ([rows(res, tr)] if res is not None else [])
    return blockwise(fn, ins, [out_rows(n, width, out_dtype, tr), out_sum((1, width))],
                     grid=(n // tr,), name=name, sums=(1,))


def rope_pair(x1, x2, cos, sin, *, tr=512, name):
    n, w = x1.shape
    tr = _pick(n, tr, 8)

    def fn(a, b, c, s):
        a, b = a.astype(F32), b.astype(F32)
        return a * c - b * s, b * c + a * s

    return blockwise(fn, [rows(x1, tr), rows(x2, tr), rows(cos, tr), rows(sin, tr)],
                     [out_rows(n, w, BF16, tr), out_rows(n, w, BF16, tr)], grid=(n // tr,), name=name)


def rope_pair_headsum(d1, d2, cos, sin, *, tr=512, name):
    h, n, w = d1.shape
    tr = _pick(n, tr, 8)

    def fn(a, b, c, s):
        a, b = jnp.sum(a.astype(F32), axis=0), jnp.sum(b.astype(F32), axis=0)
        return a * c + b * s, b * c - a * s

    lead = lambda arr: (arr, (h, tr, w), lambda i: (0, i, 0))
    return blockwise(fn, [lead(d1), lead(d2), rows(cos, tr), rows(sin, tr)],
                     [out_rows(n, w, BF16, tr), out_rows(n, w, BF16, tr)], grid=(n // tr,), name=name)


def gate_fwd(proj, y_a, y_b, *, d, ga_col, gb_col, tr=256, name):
    n = proj.shape[0]
    tr = _pick(n, tr, 8)

    def fn(ga, gb, ya, yb):
        return (jax.nn.sigmoid(ga) * ya + jax.nn.sigmoid(gb) * yb,)

    return blockwise(fn, [rows(proj, tr, d, ga_col), rows(proj, tr, d, gb_col), rows(y_a, tr), rows(y_b, tr)],
                     [out_rows(n, d, BF16, tr)], grid=(n // tr,), name=name)[0]


def gate_bwd(proj, y_a, y_b, dmerged, *, d, ga_col, gb_col, tr=256, name):
    n = proj.shape[0]
    tr = _pick(n, tr, 8)

    def fn(ga, gb, ya, yb, dm):
        sa, sb = jax.nn.sigmoid(ga), jax.nn.sigmoid(gb)
        return dm * sa, dm * sb, dm * ya * (sa * (1.0 - sa)), dm * yb * (sb * (1.0 - sb))

    return blockwise(fn, [rows(proj, tr, d, ga_col), rows(proj, tr, d, gb_col), rows(y_a, tr), rows(y_b, tr),
                          rows(dmerged, tr)],
                     [out_rows(n, d, BF16, tr)] * 4, grid=(n // tr,), name=name)


def loss_head(x, g, target, *, tr=256, name):
    n, d = x.shape
    tr = _pick(n, tr, 8)

    def fn(xv, gv, tv):
        r = _rstd(xv)
        xhat = xv * r
        diff = xhat * gv - tv
        loss = 0.5 * jnp.sum(jnp.sum(diff * diff, axis=-1, keepdims=True) / d, axis=0, keepdims=True)
        dy = diff / d
        dxhat = dy * gv
        dx = r * (dxhat - xhat * jnp.mean(dxhat * xhat, axis=-1, keepdims=True))
        return dx, _colsum(dy * xhat), jnp.broadcast_to(loss, (8, LANES))

    return blockwise(fn, [rows(x, tr), whole(g), rows(target, tr)],
                     [out_rows(n, d, F32, tr), out_sum((1, d)), out_sum((8, LANES))],
                     grid=(n // tr,), name=name, sums=(1, 2))


def _adamw_math(wv, mv, vv, gs):
    c1 = 1.0 / (1.0 - ADAM_B1 ** ADAM_STEP)
    c2 = 1.0 / (1.0 - ADAM_B2 ** ADAM_STEP)
    g = gs[0].astype(F32)
    for dev in range(1, N_DEV):
        g = g + gs[dev].astype(F32)
    m_new = ADAM_B1 * mv + (1.0 - ADAM_B1) * g
    v_new = ADAM_B2 * vv + (1.0 - ADAM_B2) * (g * g)
    delta = -ADAM_LR * ((m_new * c1) / (jnp.sqrt(v_new * c2) + ADAM_EPS) + ADAM_WD * wv)
    return g, delta, m_new, v_new


def adamw(w, m, v, g_slots, *, tr=256, name):
    n, c = w.shape
    tr = _pick(n, tr, 8)
    slots = (g_slots, (N_DEV, tr, c), lambda i: (0, i, 0))
    return blockwise(_adamw_math, [rows(w, tr), rows(m, tr), rows(v, tr), slots],
                     [out_rows(n, c, F32, tr)] * 4, grid=(n // tr,), name=name)


def adamw_layers(w, m, v, g_layers, *, name):
    depth, r, c = w.shape
    row_bytes = c * 2 * (depth * N_DEV * g_layers[0].dtype.itemsize + 7 * 4)
    tr = _pick(r, max(8, ADAMW_BLOCK_BYTES // row_bytes // 8 * 8), 8)

    def body(w_ref, m_ref, v_ref, *rest):
        g_refs, out_refs = rest[:depth], rest[depth:]
        for layer in range(depth):
            @pl.when(pl.program_id(0) == layer)
            def _(layer=layer):
                for o, val in zip(out_refs, _adamw_math(w_ref[...], m_ref[...], v_ref[...], g_refs[layer][...])):
                    o[...] = val

    blk = pl.BlockSpec((None, tr, c), lambda l, i: (l, i, 0))
    g_specs = [pl.BlockSpec((N_DEV, tr, c), lambda l, i, layer=layer: (0, jnp.where(l == layer, i, 0), 0))
               for layer in range(depth)]
    return pl.pallas_call(
        body, name=name, grid=(depth, r // tr), in_specs=[blk] * 3 + g_specs, out_specs=[blk] * 4,
        out_shape=[jax.ShapeDtypeStruct(w.shape, F32)] * 4, compiler_params=_params(("arbitrary", "arbitrary")),
    )(w, m, v, *g_layers)


LOG2E = 1.4426950408889634
MLA_FWD_SUB, MLA_FWD_BQ = 512, 4096
MLA_BWD_SUB, MLA_BWD_BQ = 1024, 2048
MLA_FWD_BK, MLA_BWD_BK = 4096, 2048


def mla_fwd(q, k, vt, *, name, ride=None):
    h, s, dq = q.shape
    bq, bk = _pick(s, MLA_FWD_BQ), _pick(s, MLA_FWD_BK)
    sub = min(MLA_FWD_SUB, bq)
    nk = s // bk
    scale = QK_DIM ** -0.5
    c2 = scale * LOG2E
    grid = (h, s // bq, nk)

    def body(q_ref, k_ref, vt_ref, *rest):
        if ride is None:
            o_ref, lse_ref, m_s, l_s, acc = rest
        else:
            n = ride.n
            x_refs, (o_ref, lse_ref), g_refs = rest[:n], rest[n:n + 2], rest[n + 2:2 * n + 2]
            m_s, l_s, acc, *sems = rest[2 * n + 2:]
            first, middle, last = _grid_flags(grid)
            ride_start, ride_middle, ride_finish = ride.steps(x_refs, g_refs, sems)
            pl.when(first)(ride_start)
            pl.when(middle)(ride_middle)
        j = pl.program_id(2)

        @pl.when(j == 0)
        def _():
            m_s[...] = jnp.full_like(m_s, NEG)
            l_s[...] = jnp.zeros_like(l_s)
            acc[...] = jnp.zeros_like(acc)

        def scores(c0):
            return lax.dot_general(k_ref[...], q_ref[pl.ds(c0, sub), :], NT, preferred_element_type=F32)

        starts = list(range(0, bq, sub))
        st_next = scores(starts[0])
        for n, c0 in enumerate(starts):
            cols = pl.ds(c0, sub)
            st = st_next
            if n + 1 < len(starts):
                st_next = scores(starts[n + 1])
            m_prev = m_s[:, cols]
            m_new = jnp.maximum(m_prev, jnp.max(st, axis=0, keepdims=True))
            alpha = jnp.exp2((m_prev - m_new) * c2)
            pt = jnp.exp2((st - m_new) * c2)
            l_s[:, cols] = alpha * l_s[:, cols] + jnp.sum(pt, axis=0, keepdims=True)
            acc[:, cols] = alpha * acc[:, cols] + lax.dot_general(vt_ref[...], pt.astype(BF16), NN,
                                                                  preferred_element_type=F32)
            m_s[:, cols] = m_new

        @pl.when(j == nk - 1)
        def _():
            o_ref[...] = (acc[...] / l_s[...]).T
            lse_ref[...] = m_s[...] * scale + jnp.log(l_s[...])

        if ride is not None:
            pl.when(last)(ride_finish)

    in_specs = [pl.BlockSpec((None, bq, dq), lambda hh, i, j: (hh, i, 0)),
                pl.BlockSpec((None, bk, dq), lambda hh, i, j: (hh, j, 0)),
                pl.BlockSpec((None, V_HEAD, bk), lambda hh, i, j: (hh, 0, j))]
    out_specs = [pl.BlockSpec((bq, V_HEAD), lambda hh, i, j: (i, hh)),
                 pl.BlockSpec((None, 1, bq), lambda hh, i, j: (hh, 0, i))]
    out_shape = [jax.ShapeDtypeStruct((s, h * V_HEAD), F32), jax.ShapeDtypeStruct((h, 1, s), F32)]
    scratch = [pltpu.VMEM((1, bq), F32), pltpu.VMEM((1, bq), F32), pltpu.VMEM((V_HEAD, bq), F32)]
    args = (q, k, vt)
    sem = ("parallel", "parallel", "arbitrary")
    if ride is not None:
        in_specs, out_specs, out_shape, scratch, args = ride.extend(in_specs, out_specs, out_shape, scratch, args)
        sem = ("arbitrary",) * 3
    return pl.pallas_call(body, name=name, grid=grid, in_specs=in_specs, out_specs=out_specs, out_shape=out_shape,
                          scratch_shapes=scratch, compiler_params=_params(sem))(*args)


def mla_delta(do, o, h, *, tr=4096, name):
    s = do.shape[0]
    tr = _pick(s, tr, 8)

    def fn(dov, ov):
        return (jnp.sum(dov.astype(F32) * ov, axis=-1, keepdims=True),)

    blk = lambda arr: (arr, (tr, V_HEAD), lambda hh, i: (i, hh))
    return blockwise(fn, [blk(do), blk(o)], [((h, s, 1), F32, (None, tr, 1), lambda hh, i: (hh, i, 0))],
                     grid=(h, s // tr), name=name)[0]


def mla_bwd(q, k, kt, kv, do, lse, delta, *, v_col0, name, ride=None):
    h, s, dq = q.shape
    bq, bk = _pick(s, MLA_BWD_BQ), _pick(s, MLA_BWD_BK)
    sub = min(MLA_BWD_SUB, bq)
    nq = s // bq
    scale = QK_DIM ** -0.5
    c2 = scale * LOG2E
    grid = (h, s // bk, nq)

    def body(q_ref, k_ref, kt_ref, v_ref, do_ref, lse_ref, delta_ref, *rest):
        if ride is None:
            dq_ref, dk_ref, dv_ref, dk_acc, dv_acc = rest
        else:
            n = ride.n
            x_refs, (dq_ref, dk_ref, dv_ref), g_refs = rest[:n], rest[n:n + 3], rest[n + 3:2 * n + 3]
            dk_acc, dv_acc, *sems = rest[2 * n + 3:]
            first, _, last = _grid_flags(grid)
            ride_start, _, ride_finish = ride.steps(x_refs, g_refs, sems)
            pl.when(first)(ride_start)
        j, i = pl.program_id(1), pl.program_id(2)

        @pl.when(i == 0)
        def _():
            dk_acc[...] = jnp.zeros_like(dk_acc)
            dv_acc[...] = jnp.zeros_like(dv_acc)

        def scores(c0):
            cols = pl.ds(c0, sub)
            return (lax.dot_general(k_ref[...], q_ref[cols, :], NT, preferred_element_type=F32),
                    lax.dot_general(v_ref[...], do_ref[cols, :], NT, preferred_element_type=F32))

        starts = list(range(0, bq, sub))
        nxt = scores(starts[0])
        for n, c0 in enumerate(starts):
            cols = pl.ds(c0, sub)
            st, dpt = nxt
            if n + 1 < len(starts):
                nxt = scores(starts[n + 1])
            q_sub, do_sub = q_ref[cols, :], do_ref[cols, :]
            pt = jnp.exp2(st * c2 - lse_ref[:, cols] * LOG2E)
            ds_b = ((pt * (dpt - delta_ref[:, cols])) * scale).astype(BF16)
            dv_acc[...] += lax.dot_general(pt.astype(BF16), do_sub, NN, preferred_element_type=F32)
            dk_acc[...] += lax.dot_general(ds_b, q_sub, NN, preferred_element_type=F32)
            dq_t = lax.dot_general(kt_ref[...], ds_b, NN, preferred_element_type=F32)

            @pl.when(j == 0)
            def _():
                dq_ref[i, :, cols] = dq_t

            @pl.when(j > 0)
            def _():
                dq_ref[i, :, cols] += dq_t

        @pl.when(i == nq - 1)
        def _():
            dk_ref[...] = dk_acc[...]
            dv_ref[...] = dv_acc[...].astype(dv_ref.dtype)

        if ride is not None:
            pl.when(last)(ride_finish)

    in_specs = [pl.BlockSpec((None, bq, dq), lambda hh, j, i: (hh, i, 0)),
                pl.BlockSpec((None, bk, dq), lambda hh, j, i: (hh, j, 0)),
                pl.BlockSpec((None, dq, bk), lambda hh, j, i: (hh, 0, j)),
                pl.BlockSpec((bk, V_HEAD), lambda hh, j, i: (j, v_col0 + hh)),
                pl.BlockSpec((bq, V_HEAD), lambda hh, j, i: (i, hh)),
                pl.BlockSpec((None, 1, bq), lambda hh, j, i: (hh, 0, i)),
                pl.BlockSpec((None, 1, bq), lambda hh, j, i: (hh, 0, i))]
    out_specs = [pl.BlockSpec((None, nq, dq, bq), lambda hh, j, i: (hh, 0, 0, 0)),
                 pl.BlockSpec((None, bk, dq), lambda hh, j, i: (hh, j, 0)),
                 pl.BlockSpec((bk, V_HEAD), lambda hh, j, i: (j, hh))]
    out_shape = [jax.ShapeDtypeStruct((h, nq, dq, bq), F32), jax.ShapeDtypeStruct((h, s, dq), F32),
                 jax.ShapeDtypeStruct((s, h * V_HEAD), BF16)]
    scratch = [pltpu.VMEM((bk, dq), F32), pltpu.VMEM((bk, V_HEAD), F32)]
    args = (q, k, kt, kv, do, lse, delta)
    sem = ("parallel", "arbitrary", "arbitrary")
    if ride is not None:
        in_specs, out_specs, out_shape, scratch, args = ride.extend(in_specs, out_specs, out_shape, scratch, args)
        sem = ("arbitrary",) * 3
    return pl.pallas_call(body, name=name, grid=grid, in_specs=in_specs, out_specs=out_specs, out_shape=out_shape,
                          scratch_shapes=scratch, compiler_params=_params(sem))(*args)


def _na_window(r, n_rows):
    row_start = jnp.clip(r - NA_KH // 2, 0, n_rows - NA_KH)
    return row_start, row_start - r + (NA_KH - 1)


def _na_probs(sc, bias):
    sc = sc * (NA_HEAD_DIM ** -0.5) + bias
    p = jnp.exp(sc - jnp.max(sc, axis=-1, keepdims=True))
    return p / jnp.sum(p, axis=-1, keepdims=True)


def na_fwd(proj, tables, *, q_col0, k_col0, v_col0, rows_per_step=32, name):
    s = proj.shape[0]
    h = tables.shape[0]
    n_rows = s // GRID_W
    rb = min(rows_per_step, n_rows)
    tq = rb * GRID_W

    def body(q_ref, k_ref, v_ref, b_ref, o_ref):
        i = pl.program_id(1)

        def window(rl):
            row_start, which = _na_window(i * rb + rl, n_rows)
            return pl.ds(pl.multiple_of(row_start * GRID_W, GRID_W), NA_WIN), which

        def scores(rl):
            q_row = q_ref[pl.ds(rl * GRID_W, GRID_W), :].astype(BF16)
            return lax.dot_general(q_row, k_ref[window(rl)[0], :].astype(BF16), NT, preferred_element_type=F32)

        ahead = [scores(rl) for rl in range(rb)]
        for rl in range(rb):
            sc = ahead[rl]
            keys, which = window(rl)
            p = _na_probs(sc, b_ref[which])
            o_ref[pl.ds(rl * GRID_W, GRID_W), :] = lax.dot_general(
                p.astype(BF16), v_ref[keys, :].astype(BF16), NN, preferred_element_type=F32)

    head = lambda col0: pl.BlockSpec((s, NA_HEAD_DIM), lambda hh, i: (0, col0 + hh))
    return pl.pallas_call(
        body, name=name, grid=(h, n_rows // rb),
        in_specs=[pl.BlockSpec((tq, NA_HEAD_DIM), lambda hh, i: (i, q_col0 + hh)), head(k_col0), head(v_col0),
                  pl.BlockSpec((None, NA_KH, GRID_W, NA_WIN), lambda hh, i: (hh, 0, 0, 0))],
        out_specs=pl.BlockSpec((tq, NA_HEAD_DIM), lambda hh, i: (i, hh)),
        out_shape=jax.ShapeDtypeStruct((s, h * NA_HEAD_DIM), F32),
        compiler_params=_params(("parallel", "arbitrary")),
    )(proj, proj, proj, tables)


def na_bwd(proj, tables, do, *, q_col0, k_col0, v_col0, rows_per_step=16, name):
    s = proj.shape[0]
    h = tables.shape[0]
    n_rows = s // GRID_W
    rb = min(rows_per_step, n_rows)
    tq = rb * GRID_W
    scale = NA_HEAD_DIM ** -0.5

    def body(q_ref, k_ref, v_ref, b_ref, do_ref, dq_ref, dk_ref, dv_ref, db_ref):
        i = pl.program_id(1)

        @pl.when(i == 0)
        def _():
            dk_ref[...] = jnp.zeros_like(dk_ref)
            dv_ref[...] = jnp.zeros_like(dv_ref)
            db_ref[...] = jnp.zeros_like(db_ref)

        def window(rl):
            row_start, which = _na_window(i * rb + rl, n_rows)
            return pl.ds(pl.multiple_of(row_start * GRID_W, GRID_W), NA_WIN), which

        def query(rl):
            rows_ = pl.ds(rl * GRID_W, GRID_W)
            return q_ref[rows_, :].astype(BF16), do_ref[rows_, :]

        def scores(rl):
            q_row, do_row = query(rl)
            keys = window(rl)[0]
            return (lax.dot_general(q_row, k_ref[keys, :].astype(BF16), NT, preferred_element_type=F32),
                    lax.dot_general(do_row, v_ref[keys, :].astype(BF16), NT, preferred_element_type=F32))

        ahead = [scores(rl) for rl in range(rb)]
        for rl in range(rb):
            sc, dp = ahead[rl]
            q_row, do_row = query(rl)
            keys, which = window(rl)
            p = _na_probs(sc, b_ref[which])
            ds = p * (dp - jnp.sum(dp * p, axis=-1, keepdims=True))
            db_ref[which] += ds
            ds_b = (ds * scale).astype(BF16)
            dq_ref[pl.ds(rl * GRID_W, GRID_W), :] = lax.dot_general(
                ds_b, k_ref[keys, :].astype(BF16), NN, preferred_element_type=F32).astype(dq_ref.dtype)
            dk_ref[keys, :] += lax.dot_general(ds_b, q_row, TN, preferred_element_type=F32)
            dv_ref[keys, :] += lax.dot_general(p.astype(BF16), do_row, TN, preferred_element_type=F32)

    head = lambda col0: pl.BlockSpec((s, NA_HEAD_DIM), lambda hh, i: (0, col0 + hh))
    rows_of = lambda col0: pl.BlockSpec((tq, NA_HEAD_DIM), lambda hh, i: (i, col0 + hh))
    table = pl.BlockSpec((None, NA_KH, GRID_W, NA_WIN), lambda hh, i: (hh, 0, 0, 0))
    hw = h * NA_HEAD_DIM
    return pl.pallas_call(
        body, name=name, grid=(h, n_rows // rb),
        in_specs=[rows_of(q_col0), head(k_col0), head(v_col0), table, rows_of(0)],
        out_specs=[rows_of(0), head(0), head(0), table],
        out_shape=[jax.ShapeDtypeStruct((s, hw), BF16), jax.ShapeDtypeStruct((s, hw), F32),
                   jax.ShapeDtypeStruct((s, hw), F32), jax.ShapeDtypeStruct(tables.shape, F32)],
        compiler_params=_params(("parallel", "arbitrary")),
    )(proj, proj, proj, tables, do)


def _na_tables():
    qc = np.arange(GRID_W)[:, None]
    kc = np.arange(GRID_W)[None, :]
    col_start = np.clip(qc - NA_KW // 2, 0, GRID_W - NA_KW)
    col_ok = (kc >= col_start) & (kc < col_start + NA_KW)
    dx = np.clip(kc - qc, -(NA_KW - 1), NA_KW - 1) + (NA_KW - 1)
    return col_ok, dx


def na_bias_tables(rpb_l):
    col_ok, _ = _na_tables()
    side = GRID_W - NA_KW
    padded = jnp.pad(rpb_l, ((0, 0), (0, 0), (side, side)))
    t = jnp.stack([padded[:, :, GRID_W - 1 - qc:2 * GRID_W - 1 - qc] for qc in range(GRID_W)], axis=2)
    t = jnp.where(col_ok[None, None], t, NEG)
    t = t.transpose(0, 2, 1, 3)
    return jnp.stack([t[:, :, w:w + NA_KH].reshape(t.shape[0], GRID_W, NA_WIN) for w in range(NA_KH)], axis=1)


def na_bias_grad(dtables, *, name):
    h = dtables.shape[0]
    _, dx = _na_tables()
    n_dy, n_dx = 2 * NA_KH - 1, 2 * NA_KW - 1
    d5 = dtables.reshape(h, NA_KH, GRID_W, NA_KH, GRID_W)
    t = sum(jnp.pad(d5[:, w], ((0, 0), (0, 0), (w, n_dy - NA_KH - w), (0, 0))) for w in range(NA_KH))
    onehot = np.zeros((GRID_W * GRID_W, LANES), np.float32)
    onehot[np.arange(GRID_W * GRID_W), dx.reshape(-1)] = 1.0
    t = t.transpose(0, 2, 1, 3).reshape(h * n_dy, GRID_W * GRID_W)
    t = jnp.pad(t, ((0, (-t.shape[0]) % 8), (0, 0)))
    out = mm(t, jnp.asarray(onehot), mode="nn", name=name, exact=True, tk=1024)
    return out[:h * n_dy, :n_dx].reshape(h, n_dy, n_dx)


def _flip(v, bit):
    return 1 - v if bit else v


class Part:
    def __init__(self, shape, kind, dtype):
        self.r, self.c = shape
        self.kind, self.dtype = kind, dtype

    @property
    def whole_shape(self):
        return {"row": (N_DEV * self.r, self.c), "col": (self.r, N_DEV * self.c),
                "packed": (N_DEV, self.r, self.c)}[self.kind]

    @property
    def packed_shape(self):
        return (N_DEV, self.r, self.c)

    def shard_of(self, ref, j):
        if self.kind == "row":
            return ref.at[pl.ds(pl.multiple_of(j * self.r, 8), self.r), :]
        if self.kind == "col":
            return ref.at[:, pl.ds(pl.multiple_of(j * self.c, LANES), self.c)]
        return ref.at[j]


def comm_scratch(n_parts):
    n = n_parts * (N_DEV - 1)
    return [pltpu.SemaphoreType.DMA((n,)), pltpu.SemaphoreType.DMA((n,)), pltpu.SemaphoreType.DMA((n_parts,))]


def gather_plan(parts, x_refs, out_refs, send_sems, recv_sems, local_sems):
    x, y, c = lax.axis_index("x"), lax.axis_index("y"), lax.axis_index("c")
    me, sibling = (x, y, c), (x, y, 1 - c)
    chips = [(1 - x, y), (x, 1 - y), (1 - x, 1 - y)]

    def place(w, px, py, pc):
        return parts[w].shard_of(out_refs[w], 4 * px + 2 * py + pc)

    def copy(k, blk, to, own=False):
        return [pltpu.make_async_remote_copy(
            src_ref=x_refs[w] if own else place(w, *blk), dst_ref=place(w, *blk),
            send_sem=send_sems.at[w * (N_DEV - 1) + k], recv_sem=recv_sems.at[w * (N_DEV - 1) + k],
            device_id=to, device_id_type=pl.DeviceIdType.MESH) for w in range(len(parts))]

    def mine():
        return [pltpu.make_async_copy(x_refs[w], place(w, *me), local_sems.at[w]) for w in range(len(parts))]

    def first():
        return copy(0, me, sibling, own=True) + [cp for j, chip in enumerate(chips)
                                                 for cp in copy(1 + j, me, (*chip, c), own=True)]

    def passed(j):
        return copy(4 + j, (*chips[j], c), sibling)

    def start():
        for cp in mine() + first():
            cp.start()

    def forward():
        for j, chip in enumerate(chips):
            for cp in copy(1 + j, (*chip, c), me):
                cp.wait_recv()
            for cp in passed(j):
                cp.start()

    def finish():
        for cp in copy(0, sibling, me):
            cp.wait_recv()
        for j, chip in enumerate(chips):
            for cp in copy(4 + j, (*chip, 1 - c), me):
                cp.wait_recv()
        for cp in first() + [cp for j in range(len(chips)) for cp in passed(j)]:
            cp.wait_send()
        for cp in mine():
            cp.wait()

    return start, forward, finish


def exchange_plan(parts, x_refs, out_refs, send_sems, recv_sems, local_sems):
    x, y, c = lax.axis_index("x"), lax.axis_index("y"), lax.axis_index("c")
    me = 4 * x + 2 * y + c

    def peer_of(k):
        peer = (_flip(x, k & 4), _flip(y, k & 2), _flip(c, k & 1))
        return peer, 4 * peer[0] + 2 * peer[1] + peer[2]

    def copies(k, arriving):
        peer, theirs = peer_of(k)
        return [pltpu.make_async_remote_copy(
            src_ref=parts[w].shard_of(x_refs[w], me if arriving else theirs),
            dst_ref=out_refs[w].at[theirs if arriving else me],
            send_sem=send_sems.at[w * (N_DEV - 1) + k - 1], recv_sem=recv_sems.at[w * (N_DEV - 1) + k - 1],
            device_id=peer, device_id_type=pl.DeviceIdType.MESH) for w in range(len(parts))]

    def mine():
        return [pltpu.make_async_copy(parts[w].shard_of(x_refs[w], me), out_refs[w].at[me], local_sems.at[w])
                for w in range(len(parts))]

    def start():
        for cp in mine():
            cp.start()
        for k in range(1, N_DEV):
            for cp in copies(k, arriving=False):
                cp.start()

    def finish():
        for k in range(1, N_DEV):
            for cp in copies(k, arriving=True):
                cp.wait_recv()
        for k in range(1, N_DEV):
            for cp in copies(k, arriving=False):
                cp.wait_send()
        for cp in mine():
            cp.wait()

    return start, finish


class Collective:
    def __init__(self, kind, parts, arrays):
        self.kind, self.parts, self.arrays = kind, parts, list(arrays)
        self.n = len(parts)
        shapes = [p.whole_shape if kind == "gather" else p.packed_shape for p in parts]
        self.out_shape = [jax.ShapeDtypeStruct(s, p.dtype) for s, p in zip(shapes, parts)]

    def extend(self, in_specs, out_specs, out_shape, scratch, args):
        any_spec = pl.BlockSpec(memory_space=pl.ANY)
        return (in_specs + [any_spec] * self.n, out_specs + [any_spec] * self.n, out_shape + self.out_shape,
                scratch + comm_scratch(self.n), args + tuple(self.arrays))

    def steps(self, x_refs, out_refs, sems):
        plan = (gather_plan if self.kind == "gather" else exchange_plan)(self.parts, x_refs, out_refs, *sems)
        return plan[0], (plan[1] if len(plan) == 3 else None), plan[-1]

    def run(self, name):
        n = self.n

        def body(*refs):
            for step in self.steps(refs[:n], refs[n:2 * n], refs[2 * n:]):
                if step is not None:
                    step()

        in_specs, out_specs, out_shape, scratch, args = self.extend([], [], [], [], ())
        return pl.pallas_call(body, name=name, in_specs=in_specs, out_specs=out_specs, out_shape=out_shape,
                              scratch_shapes=scratch)(*args)


def _grid_flags(grid):
    ids = [pl.program_id(ax) for ax in range(len(grid))]
    inner_zero = functools.reduce(jnp.logical_and, [i == 0 for i in ids[1:]])
    first = jnp.logical_and(ids[0] == 0, inner_zero)
    middle = jnp.logical_and(ids[0] == grid[0] - 1, inner_zero)
    last = functools.reduce(jnp.logical_and, [i == g - 1 for i, g in zip(ids, grid)])
    return first, middle, last


SHARDED = ("w_in", "w_uq", "w_ukv", "w_o_mla", "w_o_na", "w_out", "w_ff1", "w_ff2")
ROW_SHARDED = ("w_out", "w_ff2")
TRANSPOSED = ("w_in",)
REPLICATED = ("norm_mix", "norm_qa", "norm_kva", "rpb", "norm_mlp", "norm_final")
WEIGHTS = ("norm_mix", "w_in", "norm_qa", "w_uq", "norm_kva", "w_ukv", "rpb", "w_o_mla", "w_o_na", "w_out",
           "norm_mlp", "w_ff1", "w_ff2", "norm_final")


def part_of(name, shard_shape):
    r, c = shard_shape
    kind = "row" if name in ROW_SHARDED + TRANSPOSED else ("col" if c % LANES == 0 else "packed")
    return Part((r, c), kind, BF16)


def _whole(part, gathered):
    return gathered.transpose(1, 0, 2).reshape(part.r, -1) if part.kind == "packed" else gathered


def _for_exchange(part, full):
    return full.reshape(part.r, N_DEV, part.c).transpose(1, 0, 2) if part.kind == "packed" else full


class Dims:
    def __init__(self, x, w_in, norm_qa, norm_kva, rpb, w_o_mla, w_o_na, w_ff1):
        self.s, self.d = x.shape[1], x.shape[2]
        self.depth = w_in.shape[0]
        self.q_lora, self.kv_lora = norm_qa.shape[1], norm_kva.shape[1]
        self.mla_w, self.na_w = w_o_mla.shape[1], w_o_na.shape[1]
        self.mla_h, self.na_h = self.mla_w // V_HEAD, self.na_w // NA_HEAD_DIM
        self.d_ff = w_ff1.shape[2] * N_DEV
        assert rpb.shape[1] == self.na_h and self.s % GRID_W == 0 and self.s // GRID_W >= NA_KH
        self.in_lo = self.q_lora + self.kv_lora
        self.main_w = self.in_lo + 3 * self.na_w + 2 * self.d
        assert self.q_lora == self.kv_lora and self.in_lo % self.na_w == 0 and self.in_lo % LANES == 0
        assert (self.in_lo + 3 * self.na_w) % self.d == 0
        self.q_col0 = self.in_lo // NA_HEAD_DIM
        self.k_off = self.in_lo + self.na_w
        self.v_off = self.in_lo + 2 * self.na_w
        self.ga_col = (self.in_lo + 3 * self.na_w) // self.d
        self.gb_col = self.ga_col + 1


def _split_w_in(dm, wt):
    lo = dm.in_lo
    main = jnp.concatenate([wt[:lo], wt[lo + QK_ROPE:]], axis=0)
    kpe = jnp.pad(wt[lo:lo + QK_ROPE], ((0, LANES - QK_ROPE), (0, 0)))
    return main, kpe


def _join_w_in(dm, main, kpe):
    lo = dm.in_lo
    return jnp.concatenate([main[:lo], kpe[:QK_ROPE], main[lo:]], axis=0)


def _split_heads(w, h, widths):
    r = w.shape[0]
    w3 = w.reshape(r, h, sum(widths))
    out, o = [], 0
    for wd in widths:
        out.append(w3[:, :, o:o + wd].reshape(r, h * wd))
        o += wd
    return out


def _join_heads(parts, h):
    r = parts[0].shape[0]
    return jnp.concatenate([p.reshape(r, h, -1) for p in parts], axis=2).reshape(r, -1)


UQ_WIDTHS = (QK_NOPE, HALF_ROPE, HALF_ROPE)
UKV_WIDTHS = (QK_NOPE, V_HEAD)


def _by_head(parts, h):
    s = parts[0].shape[0]
    return jnp.concatenate([p.reshape(s, h, -1) for p in parts], axis=2).transpose(1, 0, 2)


def _from_head(t, widths):
    h, s, _ = t.shape
    t = t.transpose(1, 0, 2)
    out, o = [], 0
    for wd in widths:
        out.append(t[:, :, o:o + wd].reshape(s, h * wd))
        o += wd
    return out


def layer_fwd(dm, lname, x, w, g, cos_q, sin_q, cos_k, sin_k, ride, late_weights):
    s, h = dm.s, dm.mla_h
    u = rmsnorm_fwd(x, g["norm_mix"], name=f"{lname}_norm_mix")
    proj = mm(u, w["in_main"], mode="nt", name=f"{lname}_proj")
    kpe = mm(u, w["in_kpe"], mode="nt", name=f"{lname}_proj_kpe")
    qn = rmsnorm_fwd(proj, g["norm_qa"], width=dm.q_lora, col=0, name=f"{lname}_norm_qa")
    kvn = rmsnorm_fwd(proj, g["norm_kva"], width=dm.kv_lora, col=1, name=f"{lname}_norm_kva")
    q = mm(qn, w["uq"], mode="nn", name=f"{lname}_uq")
    kv = mm(kvn, w["ukv"], mode="nn", out_dtypes=(BF16,), name=f"{lname}_ukv")
    nope_w, half_w = h * QK_NOPE, h * HALF_ROPE
    q1, q2 = rope_pair(q[:, nope_w:nope_w + half_w], q[:, nope_w + half_w:], cos_q, sin_q, name=f"{lname}_rope_q")
    k1, k2 = rope_pair(kpe[:, :HALF_ROPE], kpe[:, HALF_ROPE:QK_ROPE], cos_k, sin_k, name=f"{lname}_rope_k")
    qh = _by_head([q[:, :nope_w].astype(BF16), q1, q2], h)
    kh = _by_head([kv[:, :nope_w], jnp.tile(k1, (1, h)), jnp.tile(k2, (1, h))], h)
    vt = kv[:, nope_w:].reshape(s, h, V_HEAD).transpose(1, 2, 0)
    o_a, lse, *gathered = mla_fwd(qh, kh, vt, name=f"{lname}_mla", ride=ride)
    w = {**w, **late_weights(gathered)}
    y_a = mm(o_a, w["o_mla"], mode="nn", name=f"{lname}_o_mla")

    tables = na_bias_tables(g["rpb"])
    o_b = na_fwd(proj, tables, q_col0=dm.q_col0, k_col0=dm.k_off // NA_HEAD_DIM, v_col0=dm.v_off // NA_HEAD_DIM,
                 name=f"{lname}_na")
    y_b = mm(o_b, w["o_na"], mode="nn", name=f"{lname}_o_na")

    merged = gate_fwd(proj, y_a, y_b, d=dm.d, ga_col=dm.ga_col, gb_col=dm.gb_col, name=f"{lname}_gate")
    x1 = mm(merged, w["out"], mode="nn", epi=lambda r, res: (r + res,), extras=(x,), name=f"{lname}_out")
    u2 = rmsnorm_fwd(x1, g["norm_mlp"], name=f"{lname}_norm_mlp")
    hid, act = mm(u2, w["ff1"], mode="nn", out_dtypes=(F32, BF16),
                  epi=lambda r: (r, jnp.square(jnp.maximum(r, 0.0))), name=f"{lname}_ff1")
    x2 = mm(act, w["ff2"], mode="nn", epi=lambda r, res: (r + res,), extras=(x1,), name=f"{lname}_ff2")
    saved = dict(x=x, u=u, proj=proj, qn=qn, kvn=kvn, kv=kv, qh=qh, kh=kh, o_a=o_a, lse=lse, y_a=y_a,
                 tables=tables, o_b=o_b, y_b=y_b, merged=merged, x1=x1, u2=u2, hid=hid, act=act)
    return x2, saved, w, gathered


def layer_bwd(dm, lname, dx2, w, g, sv, cos_q, sin_q, cos_k, sin_k, ride_of):
    h = dm.mla_h
    gw, gr = {}, {}
    gw["ff2"] = mm(sv["act"], dx2, mode="tn", out_dtypes=(BF16,), name=f"{lname}_d_ff2")
    dh = mm(dx2, w["ff2"], mode="nt", out_dtypes=(BF16,), extras=(sv["hid"],),
            epi=lambda r, hv: (r * (2.0 * jnp.maximum(hv, 0.0)),), name=f"{lname}_d_act")
    gw["ff1"] = mm(sv["u2"], dh, mode="tn", out_dtypes=(BF16,), name=f"{lname}_d_ff1")
    du2 = mm(dh, w["ff1"], mode="nt", name=f"{lname}_d_u2")
    dx1, gr["norm_mlp"] = rmsnorm_bwd(sv["x1"], g["norm_mlp"], du2, dx2, name=f"{lname}_d_norm_mlp")
    gw["out"] = mm(sv["merged"], dx1, mode="tn", out_dtypes=(BF16,), name=f"{lname}_d_out")
    dmerged = mm(dx1, w["out"], mode="nt", name=f"{lname}_d_merged")
    dy_a, dy_b, dga, dgb = gate_bwd(sv["proj"], sv["y_a"], sv["y_b"], dmerged, d=dm.d, ga_col=dm.ga_col,
                                    gb_col=dm.gb_col, name=f"{lname}_d_gate")
    gw["o_na"] = mm(sv["o_b"], dy_b, mode="tn", out_dtypes=(BF16,), name=f"{lname}_d_o_na")
    do_b = mm(dy_b, w["o_na"], mode="nt", out_dtypes=(BF16,), name=f"{lname}_d_ob")
    dq_na, dk_na, dv_na, dtables = na_bwd(sv["proj"], sv["tables"], do_b, q_col0=dm.q_col0,
                                          k_col0=dm.k_off // NA_HEAD_DIM, v_col0=dm.v_off // NA_HEAD_DIM,
                                          name=f"{lname}_d_na")
    gr["rpb"] = na_bias_grad(dtables, name=f"{lname}_d_rpb")
    gw["o_mla"] = mm(sv["o_a"], dy_a, mode="tn", out_dtypes=(BF16,), name=f"{lname}_d_o_mla")
    do_a = mm(dy_a, w["o_mla"], mode="nt", out_dtypes=(BF16,), name=f"{lname}_d_oa")
    delta = mla_delta(do_a, sv["o_a"], h, name=f"{lname}_d_mla_delta").reshape(h, 1, dm.s)
    dqt, dkh, dv, *received = mla_bwd(sv["qh"], sv["kh"], sv["kh"].transpose(0, 2, 1), sv["kv"], do_a, sv["lse"],
                                      delta, v_col0=h, name=f"{lname}_d_mla", ride=ride_of(gw))
    dqh = dqt.transpose(0, 1, 3, 2).reshape(h, dm.s, QK_DIM)
    dq_nope, dq1, dq2 = _from_head(dqh, UQ_WIDTHS)
    dq1, dq2 = rope_pair(dq1, dq2, cos_q, -sin_q, name=f"{lname}_d_rope_q")
    dq = jnp.concatenate([dq_nope.astype(BF16), dq1, dq2], axis=1)
    gw["uq"] = mm(sv["qn"], dq, mode="tn", out_dtypes=(BF16,), name=f"{lname}_d_uq")
    dqn = mm(dq, w["uq"], mode="nt", name=f"{lname}_d_qn")
    dk_nope = _from_head(dkh[:, :, :QK_NOPE], (QK_NOPE,))[0]
    dk1, dk2 = rope_pair_headsum(dkh[:, :, QK_NOPE:QK_NOPE + HALF_ROPE], dkh[:, :, QK_NOPE + HALF_ROPE:],
                                 cos_k, sin_k, name=f"{lname}_d_rope_k")
    dkv = jnp.concatenate([dk_nope.astype(BF16), dv], axis=1)
    gw["ukv"] = mm(sv["kvn"], dkv, mode="tn", out_dtypes=(BF16,), name=f"{lname}_d_ukv")
    dkvn = mm(dkv, w["ukv"], mode="nt", name=f"{lname}_d_kvn")
    dc_q, gr["norm_qa"] = rmsnorm_bwd(sv["proj"], g["norm_qa"], dqn, width=dm.q_lora, col=0, out_dtype=BF16,
                                      name=f"{lname}_d_norm_qa")
    dc_kv, gr["norm_kva"] = rmsnorm_bwd(sv["proj"], g["norm_kva"], dkvn, width=dm.kv_lora, col=1, out_dtype=BF16,
                                        name=f"{lname}_d_norm_kva")
    dproj = jnp.concatenate([dc_q, dc_kv, dq_na, dk_na.astype(BF16), dv_na.astype(BF16), dga, dgb], axis=1)
    dkpe = jnp.concatenate([dk1, dk2, jnp.zeros((dm.s, LANES - QK_ROPE), BF16)], axis=1)
    gw["in_main"] = mm(dproj, sv["u"], mode="tn", out_dtypes=(BF16,), name=f"{lname}_d_in")
    gw["in_kpe"] = mm(dkpe, sv["u"], mode="tn", out_dtypes=(BF16,), name=f"{lname}_d_in_kpe")
    du_k = mm(dkpe, w["in_kpe"], mode="nn", name=f"{lname}_d_u_kpe")
    du = mm(dproj, w["in_main"], mode="nn", epi=lambda r, res: (r + res,), extras=(du_k,), name=f"{lname}_d_u")
    dx, gr["norm_mix"] = rmsnorm_bwd(sv["x"], g["norm_mix"], du, dx1, name=f"{lname}_d_norm_mix")
    return dx, gw, gr, received


EARLY = ("w_in", "w_uq", "w_ukv")
LATE = ("w_o_mla", "w_o_na", "w_out", "w_ff1", "w_ff2")
LATE_KEYS = ("o_mla", "o_na", "out", "ff1", "ff2")


def _early_weights(dm, full):
    main, kpe = _split_w_in(dm, full["w_in"])
    return dict(in_main=main, in_kpe=kpe,
                uq=jnp.concatenate(_split_heads(full["w_uq"], dm.mla_h, UQ_WIDTHS), axis=1),
                ukv=jnp.concatenate(_split_heads(full["w_ukv"], dm.mla_h, UKV_WIDTHS), axis=1))


def _early_grads(dm, gw):
    h = dm.mla_h
    nope_w, half_w = h * QK_NOPE, h * HALF_ROPE
    uq = gw["uq"]
    ukv = gw["ukv"]
    return {"w_in": _join_w_in(dm, gw["in_main"], gw["in_kpe"]),
            "w_uq": _join_heads([uq[:, :nope_w], uq[:, nope_w:nope_w + half_w], uq[:, nope_w + half_w:]], h),
            "w_ukv": _join_heads([ukv[:, :nope_w], ukv[:, nope_w:]], h)}


def _pack_replicated(dm, parts):
    flat = jnp.concatenate([parts[n].reshape(-1) for n in REPLICATED])
    n = flat.shape[0]
    rows_ = -(-n // (8 * LANES)) * 8
    return jnp.pad(flat, (0, rows_ * LANES - n)).reshape(rows_, LANES)


def kernel(x, norm_mix, w_in, norm_qa, w_uq, norm_kva, w_ukv, rpb, w_o_mla, w_o_na, w_out, norm_mlp, w_ff1, w_ff2, norm_final, loss_target, m_norm_mix, m_w_in, m_norm_qa, m_w_uq, m_norm_kva, m_w_ukv, m_rpb, m_w_o_mla, m_w_o_na, m_w_out, m_norm_mlp, m_w_ff1, m_w_ff2, m_norm_final, v_norm_mix, v_w_in, v_norm_qa, v_w_uq, v_norm_kva, v_w_ukv, v_rpb, v_w_o_mla, v_w_o_na, v_w_out, v_norm_mlp, v_w_ff1, v_w_ff2, v_norm_final):
    dm = Dims(x, w_in, norm_qa, norm_kva, rpb, w_o_mla, w_o_na, w_ff1)
    params = dict(norm_mix=norm_mix, w_in=w_in, norm_qa=norm_qa, w_uq=w_uq, norm_kva=norm_kva, w_ukv=w_ukv, rpb=rpb,
                  w_o_mla=w_o_mla, w_o_na=w_o_na, w_out=w_out, norm_mlp=norm_mlp, w_ff1=w_ff1, w_ff2=w_ff2,
                  norm_final=norm_final)
    mom_m = dict(norm_mix=m_norm_mix, w_in=m_w_in, norm_qa=m_norm_qa, w_uq=m_w_uq, norm_kva=m_norm_kva, w_ukv=m_w_ukv,
                 rpb=m_rpb, w_o_mla=m_w_o_mla, w_o_na=m_w_o_na, w_out=m_w_out, norm_mlp=m_norm_mlp, w_ff1=m_w_ff1,
                 w_ff2=m_w_ff2, norm_final=m_norm_final)
    mom_v = dict(norm_mix=v_norm_mix, w_in=v_w_in, norm_qa=v_norm_qa, w_uq=v_w_uq, norm_kva=v_norm_kva, w_ukv=v_w_ukv,
                 rpb=v_rpb, w_o_mla=v_w_o_mla, w_o_na=v_w_o_na, w_out=v_w_out, norm_mlp=v_norm_mlp, w_ff1=v_w_ff1,
                 w_ff2=v_w_ff2, norm_final=v_norm_final)
    depth, s, h = dm.depth, dm.s, dm.mla_h

    pos = jnp.arange(s, dtype=F32)
    inv_freq = 1.0 / (ROPE_THETA ** (jnp.arange(0, QK_ROPE, 2, dtype=F32) / QK_ROPE))
    ang = pos[:, None] * inv_freq[None, :]
    cos_k, sin_k = jnp.cos(ang), jnp.sin(ang)
    cos_q, sin_q = jnp.tile(cos_k, (1, h)), jnp.tile(sin_k, (1, h))

    held = lambda d, n: d[n].transpose(0, 2, 1) if n in TRANSPOSED else d[n]
    part = {n: part_of(n, held(params, n).shape[1:]) for n in SHARDED}
    shard = lambda n, l: held(params, n)[l].astype(BF16)
    whole = lambda names, arrays: {n: _whole(part[n], a) for n, a in zip(names, arrays)}
    gains = [dict(norm_mix=norm_mix[l][None], norm_qa=norm_qa[l][None], norm_kva=norm_kva[l][None],
                  norm_mlp=norm_mlp[l][None], rpb=rpb[l]) for l in range(depth)]

    xl = x[0]
    saved, weights = [], []
    early = Collective("gather", [part[n] for n in EARLY], [shard(n, 0) for n in EARLY]).run("gather_weights")
    for l in range(depth):
        nxt = EARLY if l + 1 < depth else ()
        ride = Collective("gather", [part[n] for n in LATE + nxt],
                          [shard(n, l) for n in LATE] + [shard(n, l + 1) for n in nxt])
        late_weights = lambda got: dict(zip(LATE_KEYS, whole(LATE, got[:len(LATE)]).values()))
        xl, sv, w, got = layer_fwd(dm, "fwd", xl, _early_weights(dm, whole(EARLY, early)), gains[l], cos_q, sin_q,
                                   cos_k, sin_k, ride, late_weights)
        early = got[len(LATE):]
        saved.append(sv)
        weights.append(w)
    dx, g_final, loss_part = loss_head(xl, norm_final[None], loss_target[0], name="loss_head")
    loss = lax.psum(loss_part[0, 0], MESH_AXES)

    rep = {n: [None] * depth for n in REPLICATED if n != "norm_final"}
    recv = [{} for _ in range(depth)]
    pending = {}
    for l in reversed(range(depth)):
        def ride_of(gw, pending=pending):
            ready = {**dict(zip(LATE, [gw[k] for k in LATE_KEYS])), **pending}
            return Collective("exchange", [part[n] for n in ready], [_for_exchange(part[n], a) for n, a in ready.items()])

        dx, gw, gr, received = layer_bwd(dm, "bwd", dx, weights[l], gains[l], saved[l], cos_q, sin_q, cos_k, sin_k,
                                         ride_of)
        recv[l].update(zip(LATE, received[:len(LATE)]))
        if pending:
            recv[l + 1].update(zip(EARLY, received[len(LATE):]))
        for n in gr:
            rep[n][l] = gr[n]
        pending = _early_grads(dm, gw)
    last = Collective("exchange", [part[n] for n in EARLY], [_for_exchange(part[n], pending[n]) for n in EARLY])
    recv[0].update(zip(EARLY, last.run("scatter_grads")))
    rep_parts = {n: jnp.stack(rep[n]) for n in rep}
    rep_parts["norm_final"] = g_final
    packed = _pack_replicated(dm, rep_parts)
    rep_all = Collective("gather", [Part(packed.shape, "packed", F32)], [packed]).run("gather_small_grads")[0]

    outs = {}
    for n in SHARDED:
        res = adamw_layers(held(params, n), held(mom_m, n), held(mom_v, n), [recv[l][n] for l in range(depth)],
                           name=f"adamw_{n}")
        outs[n] = [a.transpose(0, 2, 1) for a in res] if n in TRANSPOSED else res
    n_rep = sum(int(np.prod(params[n].shape)) for n in REPLICATED)
    pack = lambda d: _pack_replicated(dm, d)
    res = adamw(pack(params), pack(mom_m), pack(mom_v), rep_all, name="adamw_replicated")
    off = 0
    for n in REPLICATED:
        size = int(np.prod(params[n].shape))
        outs[n] = [a.reshape(-1)[off:off + size].reshape(params[n].shape) for a in res]
        off += size
    assert off == n_rep

    grad_x = dx[None]
    return (loss, grad_x, *[outs[n][0] for n in WEIGHTS], *[outs[n][1] for n in WEIGHTS],
            *[outs[n][2] for n in WEIGHTS], *[outs[n][3] for n in WEIGHTS])
```

```python
import functools

import numpy as np
import jax
import jax.numpy as jnp
from jax import lax
from jax.experimental import pallas as pl
from jax.experimental.pallas import tpu as pltpu

F32 = jnp.float32
BF16 = jnp.bfloat16
MESH_AXES = ("x", "y", "c")
N_DEV = 8
LANES = 128

QK_NOPE = 128
QK_ROPE = 64
HALF_ROPE = QK_ROPE // 2
V_HEAD = 128
QK_DIM = QK_NOPE + QK_ROPE
NA_HEAD_DIM = 128
GRID_W = 64
NA_KH = 8
NA_KW = 16
NA_WIN = NA_KH * GRID_W
ROPE_THETA = 10000.0
EPS = 1e-6
NEG = -1e30

ADAM_LR = 0.001
ADAM_B1 = 0.9
ADAM_B2 = 0.999
ADAM_EPS = 1e-08
ADAM_WD = 0.01
ADAM_STEP = 10

VMEM_LIMIT_V7X = 56 * 1024 * 1024
ADAMW_BLOCK_BYTES = 24 * 1024 * 1024

NN = (((1,), (0,)), ((), ()))
NT = (((1,), (1,)), ((), ()))
TN = (((0,), (0,)), ((), ()))


def _pick(dim, target, align=LANES):
    if dim <= target:
        return dim
    t = (target // align) * align
    while t >= align:
        if dim % t == 0:
            return t
        t -= align
    return dim


def _params(sem):
    return pltpu.CompilerParams(dimension_semantics=sem, vmem_limit_bytes=VMEM_LIMIT_V7X)


def mm(a, b, *, mode, name, out_dtypes=(F32,), epi=None, extras=(), tm=1024, tn=1024, tk=2048, exact=False):
    if mode == "nn":
        (m, k), (k2, n) = a.shape, b.shape
    elif mode == "nt":
        (m, k), (n, k2) = a.shape, b.shape
    else:
        (k, m), (k2, n) = a.shape, b.shape
    assert k == k2, (a.shape, b.shape, mode)
    tm, tn, tk = _pick(m, tm), _pick(n, tn), _pick(k, tk)
    nk = k // tk
    a_spec = pl.BlockSpec((tk, tm), lambda i, j, s: (s, i)) if mode == "tn" else pl.BlockSpec((tm, tk), lambda i, j, s: (i, s))
    b_spec = pl.BlockSpec((tn, tk), lambda i, j, s: (j, s)) if mode == "nt" else pl.BlockSpec((tk, tn), lambda i, j, s: (s, j))
    tile = pl.BlockSpec((tm, tn), lambda i, j, s: (i, j))
    dims = {"nn": NN, "nt": NT, "tn": TN}[mode]
    n_extra, n_out = len(extras), len(out_dtypes)

    def product(a_ref, b_ref):
        if exact:
            return lax.dot_general(a_ref[...], b_ref[...], dims, precision=lax.Precision.HIGHEST,
                                   preferred_element_type=F32)
        return lax.dot_general(a_ref[...].astype(BF16), b_ref[...].astype(BF16), dims, preferred_element_type=F32)

    def finish(r, extra_refs, out_refs):
        res = (r,) if epi is None else epi(r, *[e[...] for e in extra_refs])
        for o, v in zip(out_refs, res):
            o[...] = v.astype(o.dtype)

    def body_one_step(a_ref, b_ref, *rest):
        finish(product(a_ref, b_ref), rest[:n_extra], rest[n_extra:])

    def body(a_ref, b_ref, *rest):
        extra_refs, out_refs, acc = rest[:n_extra], rest[n_extra:n_extra + n_out], rest[-1]
        step = pl.program_id(2)

        @pl.when(step == 0)
        def _():
            acc[...] = product(a_ref, b_ref)

        @pl.when(step > 0)
        def _():
            acc[...] += product(a_ref, b_ref)

        @pl.when(step == nk - 1)
        def _():
            finish(acc[...], extra_refs, out_refs)

    outs = pl.pallas_call(
        body_one_step if nk == 1 else body, name=name, grid=(m // tm, n // tn, nk),
        in_specs=[a_spec, b_spec] + [tile] * n_extra,
        out_specs=[tile] * n_out,
        out_shape=[jax.ShapeDtypeStruct((m, n), d) for d in out_dtypes],
        scratch_shapes=[] if nk == 1 else [pltpu.VMEM((tm, tn), F32)],
        compiler_params=_params(("parallel", "parallel", "arbitrary")),
    )(a, b, *extras)
    return outs[0] if n_out == 1 else outs


def blockwise(fn, ins, outs, *, grid, name, sums=()):
    n_in, n_axes = len(ins), len(grid)

    def body(*refs):
        res = fn(*[r[...] for r in refs[:n_in]])
        first = functools.reduce(jnp.logical_and, [pl.program_id(ax) == 0 for ax in range(n_axes)])
        for idx, (o, v) in enumerate(zip(refs[n_in:], res)):
            if idx in sums:
                @pl.when(first)
                def _(o=o):
                    o[...] = jnp.zeros_like(o)

                o[...] += v.astype(o.dtype)
            else:
                o[...] = v.astype(o.dtype)

    sem = ("arbitrary" if sums else "parallel",) * n_axes
    res = pl.pallas_call(
        body, name=name, grid=grid,
        in_specs=[pl.BlockSpec(blk, imap) for _, blk, imap in ins],
        out_specs=[pl.BlockSpec(blk, imap) for _, _, blk, imap in outs],
        out_shape=[jax.ShapeDtypeStruct(shape, dt) for shape, dt, _, _ in outs],
        compiler_params=_params(sem),
    )(*[a for a, _, _ in ins])
    return res


def rows(arr, tr, width=None, col=0):
    width = arr.shape[1] if width is None else width
    return (arr, (tr, width), lambda i, col=col: (i, col))


def whole(arr):
    return (arr, arr.shape, lambda i: (0, 0))


def out_rows(n_rows, width, dtype, tr):
    return ((n_rows, width), dtype, (tr, width), lambda i: (i, 0))


def out_sum(shape, dtype=F32):
    return (shape, dtype, shape, lambda i: (0, 0))


def _rstd(x):
    return lax.rsqrt(jnp.mean(x * x, axis=-1, keepdims=True) + EPS)


def _colsum(v):
    return jnp.sum(v, axis=0, keepdims=True)


def rmsnorm_fwd(x, g, *, width=None, col=0, tr=256, name):
    n = x.shape[0]
    tr = _pick(n, tr, 8)
    width = x.shape[1] if width is None else width

    def fn(xv, gv):
        return ((xv * _rstd(xv)) * gv,)

    return blockwise(fn, [rows(x, tr, width, col), whole(g)], [out_rows(n, width, BF16, tr)],
                     grid=(n // tr,), name=name)[0]


def rmsnorm_bwd(x, g, dy, res=None, *, width=None, col=0, out_dtype=F32, tr=256, name):
    n = x.shape[0]
    tr = _pick(n, tr, 8)
    width = x.shape[1] if width is None else width

    def fn(xv, gv, dyv, *resv):
        dyv = dyv.astype(F32)
        r = _rstd(xv)
        xhat = xv * r
        dxhat = dyv * gv
        dx = r * (dxhat - xhat * jnp.mean(dxhat * xhat, axis=-1, keepdims=True))
        if resv:
            dx = dx + resv[0]
        return dx, _colsum(dyv * xhat)

    ins = [rows(x, tr, width, col), whole(g), rows(dy, tr)] + ([rows(res, tr)] if res is not None else [])
    return blockwise(fn, ins, [out_rows(n, width, out_dtype, tr), out_sum((1, width))],
                     grid=(n // tr,), name=name, sums=(1,))


def rope_pair(x1, x2, cos, sin, *, tr=512, name):
    n, w = x1.shape
    tr = _pick(n, tr, 8)

    def fn(a, b, c, s):
        a, b = a.astype(F32), b.astype(F32)
        return a * c - b * s, b * c + a * s

    return blockwise(fn, [rows(x1, tr), rows(x2, tr), rows(cos, tr), rows(sin, tr)],
                     [out_rows(n, w, BF16, tr), out_rows(n, w, BF16, tr)], grid=(n // tr,), name=name)


def rope_pair_headsum(d, cos, sin, *, tr=512, name):
    h, n, w = d.shape
    tr = _pick(n, tr, 8)

    def fn(dv, c, s):
        t = jnp.sum(dv, axis=0)
        a, b = t[:, QK_NOPE:QK_NOPE + HALF_ROPE], t[:, QK_NOPE + HALF_ROPE:]
        return a * c + b * s, b * c - a * s

    return blockwise(fn, [(d, (h, tr, w), lambda i: (0, i, 0)), rows(cos, tr), rows(sin, tr)],
                     [out_rows(n, HALF_ROPE, BF16, tr), out_rows(n, HALF_ROPE, BF16, tr)], grid=(n // tr,), name=name)


def gate_fwd(proj, y_a, y_b, *, d, ga_col, gb_col, tr=256, name):
    n = proj.shape[0]
    tr = _pick(n, tr, 8)

    def fn(ga, gb, ya, yb):
        return (jax.nn.sigmoid(ga) * ya + jax.nn.sigmoid(gb) * yb,)

    return blockwise(fn, [rows(proj, tr, d, ga_col), rows(proj, tr, d, gb_col), rows(y_a, tr), rows(y_b, tr)],
                     [out_rows(n, d, BF16, tr)], grid=(n // tr,), name=name)[0]


def gate_bwd(proj, y_a, y_b, dmerged, *, d, ga_col, gb_col, tr=256, name):
    n = proj.shape[0]
    tr = _pick(n, tr, 8)

    def fn(ga, gb, ya, yb, dm):
        sa, sb = jax.nn.sigmoid(ga), jax.nn.sigmoid(gb)
        return dm * sa, dm * sb, dm * ya * (sa * (1.0 - sa)), dm * yb * (sb * (1.0 - sb))

    return blockwise(fn, [rows(proj, tr, d, ga_col), rows(proj, tr, d, gb_col), rows(y_a, tr), rows(y_b, tr),
                          rows(dmerged, tr)],
                     [out_rows(n, d, BF16, tr)] * 4, grid=(n // tr,), name=name)


def loss_head(x, g, target, *, tr=256, name):
    n, d = x.shape
    tr = _pick(n, tr, 8)

    def fn(xv, gv, tv):
        r = _rstd(xv)
        xhat = xv * r
        diff = xhat * gv - tv
        loss = 0.5 * jnp.sum(jnp.sum(diff * diff, axis=-1, keepdims=True) / d, axis=0, keepdims=True)
        dy = diff / d
        dxhat = dy * gv
        dx = r * (dxhat - xhat * jnp.mean(dxhat * xhat, axis=-1, keepdims=True))
        return dx, _colsum(dy * xhat), jnp.broadcast_to(loss, (8, LANES))

    return blockwise(fn, [rows(x, tr), whole(g), rows(target, tr)],
                     [out_rows(n, d, F32, tr), out_sum((1, d)), out_sum((8, LANES))],
                     grid=(n // tr,), name=name, sums=(1, 2))


def _adamw_math(wv, mv, vv, gs):
    c1 = 1.0 / (1.0 - ADAM_B1 ** ADAM_STEP)
    c2 = 1.0 / (1.0 - ADAM_B2 ** ADAM_STEP)
    g = gs[0].astype(F32)
    for dev in range(1, N_DEV):
        g = g + gs[dev].astype(F32)
    m_new = ADAM_B1 * mv + (1.0 - ADAM_B1) * g
    v_new = ADAM_B2 * vv + (1.0 - ADAM_B2) * (g * g)
    delta = -ADAM_LR * ((m_new * c1) / (jnp.sqrt(v_new * c2) + ADAM_EPS) + ADAM_WD * wv)
    return g, delta, m_new, v_new


def adamw(w, m, v, g_slots, *, tr=256, name):
    n, c = w.shape
    tr = _pick(n, tr, 8)
    slots = (g_slots, (N_DEV, tr, c), lambda i: (0, i, 0))
    return blockwise(_adamw_math, [rows(w, tr), rows(m, tr), rows(v, tr), slots],
                     [out_rows(n, c, F32, tr)] * 4, grid=(n // tr,), name=name)


def adamw_layers(w, m, v, g_layers, *, name):
    depth, r, c = w.shape
    row_bytes = c * 2 * (depth * N_DEV * g_layers[0].dtype.itemsize + 7 * 4)
    tr = _pick(r, max(8, ADAMW_BLOCK_BYTES // row_bytes // 8 * 8), 8)

    def body(w_ref, m_ref, v_ref, *rest):
        g_refs, out_refs = rest[:depth], rest[depth:]
        for layer in range(depth):
            @pl.when(pl.program_id(0) == layer)
            def _(layer=layer):
                for o, val in zip(out_refs, _adamw_math(w_ref[...], m_ref[...], v_ref[...], g_refs[layer][...])):
                    o[...] = val

    blk = pl.BlockSpec((None, tr, c), lambda l, i: (l, i, 0))
    g_specs = [pl.BlockSpec((N_DEV, tr, c), lambda l, i, layer=layer: (0, jnp.where(l == layer, i, 0), 0))
               for layer in range(depth)]
    return pl.pallas_call(
        body, name=name, grid=(depth, r // tr), in_specs=[blk] * 3 + g_specs, out_specs=[blk] * 4,
        out_shape=[jax.ShapeDtypeStruct(w.shape, F32)] * 4, compiler_params=_params(("arbitrary", "arbitrary")),
    )(w, m, v, *g_layers)


LOG2E = 1.4426950408889634
MLA_FWD_SUB, MLA_FWD_BQ = 512, 4096
MLA_BWD_SUB, MLA_BWD_BQ = 1024, 2048
MLA_FWD_BK, MLA_BWD_BK = 4096, 2048


def mla_fwd(q, k, vt, *, name, ride=None):
    h, s, dq = q.shape
    bq, bk = _pick(s, MLA_FWD_BQ), _pick(s, MLA_FWD_BK)
    sub = min(MLA_FWD_SUB, bq)
    nk = s // bk
    scale = QK_DIM ** -0.5
    c2 = scale * LOG2E
    grid = (h, s // bq, nk)

    def body(q_ref, k_ref, vt_ref, *rest):
        if ride is None:
            o_ref, lse_ref, m_s, l_s, acc = rest
        else:
            n = ride.n
            x_refs, (o_ref, lse_ref), g_refs = rest[:n], rest[n:n + 2], rest[n + 2:2 * n + 2]
            m_s, l_s, acc, *sems = rest[2 * n + 2:]
            first, middle, last = _grid_flags(grid)
            ride_start, ride_middle, ride_finish = ride.steps(x_refs, g_refs, sems)
            pl.when(first)(ride_start)
            pl.when(middle)(ride_middle)
        j = pl.program_id(2)

        @pl.when(j == 0)
        def _():
            m_s[...] = jnp.full_like(m_s, NEG)
            l_s[...] = jnp.zeros_like(l_s)
            acc[...] = jnp.zeros_like(acc)

        def scores(c0):
            return lax.dot_general(k_ref[...], q_ref[pl.ds(c0, sub), :], NT, preferred_element_type=F32)

        starts = list(range(0, bq, sub))
        st_next = scores(starts[0])
        for n, c0 in enumerate(starts):
            cols = pl.ds(c0, sub)
            st = st_next
            if n + 1 < len(starts):
                st_next = scores(starts[n + 1])
            m_prev = m_s[:, cols]
            m_new = jnp.maximum(m_prev, jnp.max(st, axis=0, keepdims=True))
            alpha = jnp.exp2((m_prev - m_new) * c2)
            pt = jnp.exp2((st - m_new) * c2)
            l_s[:, cols] = alpha * l_s[:, cols] + jnp.sum(pt, axis=0, keepdims=True)
            acc[:, cols] = alpha * acc[:, cols] + lax.dot_general(vt_ref[...], pt.astype(BF16), NN,
                                                                  preferred_element_type=F32)
            m_s[:, cols] = m_new

        @pl.when(j == nk - 1)
        def _():
            o_ref[...] = (acc[...] / l_s[...]).T
            lse_ref[...] = m_s[...] * scale + jnp.log(l_s[...])

        if ride is not None:
            pl.when(last)(ride_finish)

    in_specs = [pl.BlockSpec((None, bq, dq), lambda hh, i, j: (hh, i, 0)),
                pl.BlockSpec((None, bk, dq), lambda hh, i, j: (hh, j, 0)),
                pl.BlockSpec((None, V_HEAD, bk), lambda hh, i, j: (hh, 0, j))]
    out_specs = [pl.BlockSpec((bq, V_HEAD), lambda hh, i, j: (i, hh)),
                 pl.BlockSpec((None, 1, bq), lambda hh, i, j: (hh, 0, i))]
    out_shape = [jax.ShapeDtypeStruct((s, h * V_HEAD), F32), jax.ShapeDtypeStruct((h, 1, s), F32)]
    scratch = [pltpu.VMEM((1, bq), F32), pltpu.VMEM((1, bq), F32), pltpu.VMEM((V_HEAD, bq), F32)]
    args = (q, k, vt)
    sem = ("parallel", "parallel", "arbitrary")
    if ride is not None:
        in_specs, out_specs, out_shape, scratch, args = ride.extend(in_specs, out_specs, out_shape, scratch, args)
        sem = ("arbitrary",) * 3
    return pl.pallas_call(body, name=name, grid=grid, in_specs=in_specs, out_specs=out_specs, out_shape=out_shape,
                          scratch_shapes=scratch, compiler_params=_params(sem))(*args)


def mla_delta(do, o, h, *, tr=4096, name):
    s = do.shape[0]
    tr = _pick(s, tr, 8)

    def fn(dov, ov):
        return (jnp.sum(dov.astype(F32) * ov, axis=-1, keepdims=True),)

    blk = lambda arr: (arr, (tr, V_HEAD), lambda hh, i: (i, hh))
    return blockwise(fn, [blk(do), blk(o)], [((h, s, 1), F32, (None, tr, 1), lambda hh, i: (hh, i, 0))],
                     grid=(h, s // tr), name=name)[0]


def mla_bwd(q, k, kt, kv, do, lse, delta, *, v_col0, name, ride=None):
    h, s, dq = q.shape
    bq, bk = _pick(s, MLA_BWD_BQ), _pick(s, MLA_BWD_BK)
    sub = min(MLA_BWD_SUB, bq)
    nq = s // bq
    scale = QK_DIM ** -0.5
    c2 = scale * LOG2E
    grid = (h, s // bk, nq)

    def body(q_ref, k_ref, kt_ref, v_ref, do_ref, lse_ref, delta_ref, *rest):
        if ride is None:
            dq_ref, dk_ref, dv_ref, dk_acc, dv_acc = rest
        else:
            n = ride.n
            x_refs, (dq_ref, dk_ref, dv_ref), g_refs = rest[:n], rest[n:n + 3], rest[n + 3:2 * n + 3]
            dk_acc, dv_acc, *sems = rest[2 * n + 3:]
            first, _, last = _grid_flags(grid)
            ride_start, _, ride_finish = ride.steps(x_refs, g_refs, sems)
            pl.when(first)(ride_start)
        j, i = pl.program_id(1), pl.program_id(2)

        @pl.when(i == 0)
        def _():
            dk_acc[...] = jnp.zeros_like(dk_acc)
            dv_acc[...] = jnp.zeros_like(dv_acc)

        def scores(c0):
            cols = pl.ds(c0, sub)
            return (lax.dot_general(k_ref[...], q_ref[cols, :], NT, preferred_element_type=F32),
                    lax.dot_general(v_ref[...], do_ref[cols, :], NT, preferred_element_type=F32))

        starts = list(range(0, bq, sub))
        nxt = scores(starts[0])
        for n, c0 in enumerate(starts):
            cols = pl.ds(c0, sub)
            st, dpt = nxt
            if n + 1 < len(starts):
                nxt = scores(starts[n + 1])
            q_sub, do_sub = q_ref[cols, :], do_ref[cols, :]
            pt = jnp.exp2(st * c2 - lse_ref[:, cols] * LOG2E)
            ds_b = ((pt * (dpt - delta_ref[:, cols])) * scale).astype(BF16)
            dv_acc[...] += lax.dot_general(pt.astype(BF16), do_sub, NN, preferred_element_type=F32)
            dk_acc[...] += lax.dot_general(ds_b, q_sub, NN, preferred_element_type=F32)
            dq_t = lax.dot_general(kt_ref[...], ds_b, NN, preferred_element_type=F32)

            @pl.when(j == 0)
            def _():
                dq_ref[i, :, cols] = dq_t

            @pl.when(j > 0)
            def _():
                dq_ref[i, :, cols] += dq_t

        @pl.when(i == nq - 1)
        def _():
            dk_ref[...] = dk_acc[...]
            dv_ref[...] = dv_acc[...].astype(dv_ref.dtype)

        if ride is not None:
            pl.when(last)(ride_finish)

    in_specs = [pl.BlockSpec((None, bq, dq), lambda hh, j, i: (hh, i, 0)),
                pl.BlockSpec((None, bk, dq), lambda hh, j, i: (hh, j, 0)),
                pl.BlockSpec((None, dq, bk), lambda hh, j, i: (hh, 0, j)),
                pl.BlockSpec((bk, V_HEAD), lambda hh, j, i: (j, v_col0 + hh)),
                pl.BlockSpec((bq, V_HEAD), lambda hh, j, i: (i, hh)),
                pl.BlockSpec((None, 1, bq), lambda hh, j, i: (hh, 0, i)),
                pl.BlockSpec((None, 1, bq), lambda hh, j, i: (hh, 0, i))]
    out_specs = [pl.BlockSpec((None, nq, dq, bq), lambda hh, j, i: (hh, 0, 0, 0)),
                 pl.BlockSpec((None, bk, dq), lambda hh, j, i: (hh, j, 0)),
                 pl.BlockSpec((bk, V_HEAD), lambda hh, j, i: (j, hh))]
    out_shape = [jax.ShapeDtypeStruct((h, nq, dq, bq), F32), jax.ShapeDtypeStruct((h, s, dq), F32),
                 jax.ShapeDtypeStruct((s, h * V_HEAD), BF16)]
    scratch = [pltpu.VMEM((bk, dq), F32), pltpu.VMEM((bk, V_HEAD), F32)]
    args = (q, k, kt, kv, do, lse, delta)
    sem = ("parallel", "arbitrary", "arbitrary")
    if ride is not None:
        in_specs, out_specs, out_shape, scratch, args = ride.extend(in_specs, out_specs, out_shape, scratch, args)
        sem = ("arbitrary",) * 3
    return pl.pallas_call(body, name=name, grid=grid, in_specs=in_specs, out_specs=out_specs, out_shape=out_shape,
                          scratch_shapes=scratch, compiler_params=_params(sem))(*args)


def _na_window(r, n_rows):
    row_start = jnp.clip(r - NA_KH // 2, 0, n_rows - NA_KH)
    return row_start, row_start - r + (NA_KH - 1)


def _na_probs(sc, bias):
    sc = sc * (NA_HEAD_DIM ** -0.5) + bias
    p = jnp.exp(sc - jnp.max(sc, axis=-1, keepdims=True))
    return p / jnp.sum(p, axis=-1, keepdims=True)


def na_fwd(proj, tables, *, q_col0, k_col0, v_col0, rows_per_step=32, name):
    s = proj.shape[0]
    h = tables.shape[0]
    n_rows = s // GRID_W
    rb = min(rows_per_step, n_rows)
    tq = rb * GRID_W

    def body(q_ref, k_ref, v_ref, b_ref, o_ref):
        i = pl.program_id(1)

        def window(rl):
            row_start, which = _na_window(i * rb + rl, n_rows)
            return pl.ds(pl.multiple_of(row_start * GRID_W, GRID_W), NA_WIN), which

        def scores(rl):
            q_row = q_ref[pl.ds(rl * GRID_W, GRID_W), :].astype(BF16)
            return lax.dot_general(q_row, k_ref[window(rl)[0], :].astype(BF16), NT, preferred_element_type=F32)

        ahead = [scores(rl) for rl in range(rb)]
        for rl in range(rb):
            sc = ahead[rl]
            keys, which = window(rl)
            p = _na_probs(sc, b_ref[which])
            o_ref[pl.ds(rl * GRID_W, GRID_W), :] = lax.dot_general(
                p.astype(BF16), v_ref[keys, :].astype(BF16), NN, preferred_element_type=F32)

    head = lambda col0: pl.BlockSpec((s, NA_HEAD_DIM), lambda hh, i: (0, col0 + hh))
    return pl.pallas_call(
        body, name=name, grid=(h, n_rows // rb),
        in_specs=[pl.BlockSpec((tq, NA_HEAD_DIM), lambda hh, i: (i, q_col0 + hh)), head(k_col0), head(v_col0),
                  pl.BlockSpec((None, NA_KH, GRID_W, NA_WIN), lambda hh, i: (hh, 0, 0, 0))],
        out_specs=pl.BlockSpec((tq, NA_HEAD_DIM), lambda hh, i: (i, hh)),
        out_shape=jax.ShapeDtypeStruct((s, h * NA_HEAD_DIM), F32),
        compiler_params=_params(("parallel", "arbitrary")),
    )(proj, proj, proj, tables)


def na_bwd(proj, tables, do, *, q_col0, k_col0, v_col0, rows_per_step=16, name):
    s = proj.shape[0]
    h = tables.shape[0]
    n_rows = s // GRID_W
    rb = min(rows_per_step, n_rows)
    tq = rb * GRID_W
    scale = NA_HEAD_DIM ** -0.5

    def body(q_ref, k_ref, v_ref, b_ref, do_ref, dq_ref, dk_ref, dv_ref, db_ref):
        i = pl.program_id(1)

        @pl.when(i == 0)
        def _():
            dk_ref[...] = jnp.zeros_like(dk_ref)
            dv_ref[...] = jnp.zeros_like(dv_ref)
            db_ref[...] = jnp.zeros_like(db_ref)

        def window(rl):
            row_start, which = _na_window(i * rb + rl, n_rows)
            return pl.ds(pl.multiple_of(row_start * GRID_W, GRID_W), NA_WIN), which

        def query(rl):
            rows_ = pl.ds(rl * GRID_W, GRID_W)
            return q_ref[rows_, :].astype(BF16), do_ref[rows_, :]

        def scores(rl):
            q_row, do_row = query(rl)
            keys = window(rl)[0]
            return (lax.dot_general(q_row, k_ref[keys, :].astype(BF16), NT, preferred_element_type=F32),
                    lax.dot_general(do_row, v_ref[keys, :].astype(BF16), NT, preferred_element_type=F32))

        ahead = [scores(rl) for rl in range(rb)]
        for rl in range(rb):
            sc, dp = ahead[rl]
            q_row, do_row = query(rl)
            keys, which = window(rl)
            p = _na_probs(sc, b_ref[which])
            ds = p * (dp - jnp.sum(dp * p, axis=-1, keepdims=True))
            db_ref[which] += ds
            ds_b = (ds * scale).astype(BF16)
            dq_ref[pl.ds(rl * GRID_W, GRID_W), :] = lax.dot_general(
                ds_b, k_ref[keys, :].astype(BF16), NN, preferred_element_type=F32).astype(dq_ref.dtype)
            dk_ref[keys, :] += lax.dot_general(ds_b, q_row, TN, preferred_element_type=F32)
            dv_ref[keys, :] += lax.dot_general(p.astype(BF16), do_row, TN, preferred_element_type=F32)

    head = lambda col0: pl.BlockSpec((s, NA_HEAD_DIM), lambda hh, i: (0, col0 + hh))
    rows_of = lambda col0: pl.BlockSpec((tq, NA_HEAD_DIM), lambda hh, i: (i, col0 + hh))
    table = pl.BlockSpec((None, NA_KH, GRID_W, NA_WIN), lambda hh, i: (hh, 0, 0, 0))
    hw = h * NA_HEAD_DIM
    return pl.pallas_call(
        body, name=name, grid=(h, n_rows // rb),
        in_specs=[rows_of(q_col0), head(k_col0), head(v_col0), table, rows_of(0)],
        out_specs=[rows_of(0), head(0), head(0), table],
        out_shape=[jax.ShapeDtypeStruct((s, hw), BF16), jax.ShapeDtypeStruct((s, hw), F32),
                   jax.ShapeDtypeStruct((s, hw), F32), jax.ShapeDtypeStruct(tables.shape, F32)],
        compiler_params=_params(("parallel", "arbitrary")),
    )(proj, proj, proj, tables, do)


def _na_tables():
    qc = np.arange(GRID_W)[:, None]
    kc = np.arange(GRID_W)[None, :]
    col_start = np.clip(qc - NA_KW // 2, 0, GRID_W - NA_KW)
    col_ok = (kc >= col_start) & (kc < col_start + NA_KW)
    dx = np.clip(kc - qc, -(NA_KW - 1), NA_KW - 1) + (NA_KW - 1)
    return col_ok, dx


def na_bias_tables(rpb_l):
    col_ok, _ = _na_tables()
    side = GRID_W - NA_KW
    padded = jnp.pad(rpb_l, ((0, 0), (0, 0), (side, side)))
    t = jnp.stack([padded[:, :, GRID_W - 1 - qc:2 * GRID_W - 1 - qc] for qc in range(GRID_W)], axis=2)
    t = jnp.where(col_ok[None, None], t, NEG)
    t = t.transpose(0, 2, 1, 3)
    return jnp.stack([t[:, :, w:w + NA_KH].reshape(t.shape[0], GRID_W, NA_WIN) for w in range(NA_KH)], axis=1)


def na_bias_grad(dtables, *, name):
    h = dtables.shape[0]
    _, dx = _na_tables()
    n_dy, n_dx = 2 * NA_KH - 1, 2 * NA_KW - 1
    d5 = dtables.reshape(h, NA_KH, GRID_W, NA_KH, GRID_W)
    t = sum(jnp.pad(d5[:, w], ((0, 0), (0, 0), (w, n_dy - NA_KH - w), (0, 0))) for w in range(NA_KH))
    onehot = np.zeros((GRID_W * GRID_W, LANES), np.float32)
    onehot[np.arange(GRID_W * GRID_W), dx.reshape(-1)] = 1.0
    t = t.transpose(0, 2, 1, 3).reshape(h * n_dy, GRID_W * GRID_W)
    t = jnp.pad(t, ((0, (-t.shape[0]) % 8), (0, 0)))
    out = mm(t, jnp.asarray(onehot), mode="nn", name=name, exact=True, tk=1024)
    return out[:h * n_dy, :n_dx].reshape(h, n_dy, n_dx)


def _flip(v, bit):
    return 1 - v if bit else v


class Part:
    def __init__(self, shape, kind, dtype):
        self.r, self.c = shape
        self.kind, self.dtype = kind, dtype

    @property
    def whole_shape(self):
        return {"row": (N_DEV * self.r, self.c), "col": (self.r, N_DEV * self.c),
                "packed": (N_DEV, self.r, self.c)}[self.kind]

    @property
    def packed_shape(self):
        return (N_DEV, self.r, self.c)

    def shard_of(self, ref, j):
        if self.kind == "row":
            return ref.at[pl.ds(pl.multiple_of(j * self.r, 8), self.r), :]
        if self.kind == "col":
            return ref.at[:, pl.ds(pl.multiple_of(j * self.c, LANES), self.c)]
        return ref.at[j]


def comm_scratch(n_parts):
    n = n_parts * (N_DEV - 1)
    return [pltpu.SemaphoreType.DMA((n,)), pltpu.SemaphoreType.DMA((n,)), pltpu.SemaphoreType.DMA((n_parts,))]


def gather_plan(parts, x_refs, out_refs, send_sems, recv_sems, local_sems):
    x, y, c = lax.axis_index("x"), lax.axis_index("y"), lax.axis_index("c")
    me, sibling = (x, y, c), (x, y, 1 - c)
    chips = [(1 - x, y), (x, 1 - y), (1 - x, 1 - y)]

    def place(w, px, py, pc):
        return parts[w].shard_of(out_refs[w], 4 * px + 2 * py + pc)

    def copy(k, blk, to, own=False):
        return [pltpu.make_async_remote_copy(
            src_ref=x_refs[w] if own else place(w, *blk), dst_ref=place(w, *blk),
            send_sem=send_sems.at[w * (N_DEV - 1) + k], recv_sem=recv_sems.at[w * (N_DEV - 1) + k],
            device_id=to, device_id_type=pl.DeviceIdType.MESH) for w in range(len(parts))]

    def mine():
        return [pltpu.make_async_copy(x_refs[w], place(w, *me), local_sems.at[w]) for w in range(len(parts))]

    def first():
        return copy(0, me, sibling, own=True) + [cp for j, chip in enumerate(chips)
                                                 for cp in copy(1 + j, me, (*chip, c), own=True)]

    def passed(j):
        return copy(4 + j, (*chips[j], c), sibling)

    def start():
        for cp in mine() + first():
            cp.start()

    def forward():
        for j, chip in enumerate(chips):
            for cp in copy(1 + j, (*chip, c), me):
                cp.wait_recv()
            for cp in passed(j):
                cp.start()

    def finish():
        for cp in copy(0, sibling, me):
            cp.wait_recv()
        for j, chip in enumerate(chips):
            for cp in copy(4 + j, (*chip, 1 - c), me):
                cp.wait_recv()
        for cp in first() + [cp for j in range(len(chips)) for cp in passed(j)]:
            cp.wait_send()
        for cp in mine():
            cp.wait()

    return start, forward, finish


def exchange_plan(parts, x_refs, out_refs, send_sems, recv_sems, local_sems):
    x, y, c = lax.axis_index("x"), lax.axis_index("y"), lax.axis_index("c")
    me = 4 * x + 2 * y + c

    def peer_of(k):
        peer = (_flip(x, k & 4), _flip(y, k & 2), _flip(c, k & 1))
        return peer, 4 * peer[0] + 2 * peer[1] + peer[2]

    def copies(k, arriving):
        peer, theirs = peer_of(k)
        return [pltpu.make_async_remote_copy(
            src_ref=parts[w].shard_of(x_refs[w], me if arriving else theirs),
            dst_ref=out_refs[w].at[theirs if arriving else me],
            send_sem=send_sems.at[w * (N_DEV - 1) + k - 1], recv_sem=recv_sems.at[w * (N_DEV - 1) + k - 1],
            device_id=peer, device_id_type=pl.DeviceIdType.MESH) for w in range(len(parts))]

    def mine():
        return [pltpu.make_async_copy(parts[w].shard_of(x_refs[w], me), out_refs[w].at[me], local_sems.at[w])
                for w in range(len(parts))]

    def start():
        for cp in mine():
            cp.start()
        for k in range(1, N_DEV):
            for cp in copies(k, arriving=False):
                cp.start()

    def finish():
        for k in range(1, N_DEV):
            for cp in copies(k, arriving=True):
                cp.wait_recv()
        for k in range(1, N_DEV):
            for cp in copies(k, arriving=False):
                cp.wait_send()
        for cp in mine():
            cp.wait()

    return start, finish


class Collective:
    def __init__(self, kind, parts, arrays):
        self.kind, self.parts, self.arrays = kind, parts, list(arrays)
        self.n = len(parts)
        shapes = [p.whole_shape if kind == "gather" else p.packed_shape for p in parts]
        self.out_shape = [jax.ShapeDtypeStruct(s, p.dtype) for s, p in zip(shapes, parts)]

    def extend(self, in_specs, out_specs, out_shape, scratch, args):
        any_spec = pl.BlockSpec(memory_space=pl.ANY)
        return (in_specs + [any_spec] * self.n, out_specs + [any_spec] * self.n, out_shape + self.out_shape,
                scratch + comm_scratch(self.n), args + tuple(self.arrays))

    def steps(self, x_refs, out_refs, sems):
        plan = (gather_plan if self.kind == "gather" else exchange_plan)(self.parts, x_refs, out_refs, *sems)
        return plan[0], (plan[1] if len(plan) == 3 else None), plan[-1]

    def run(self, name):
        n = self.n

        def body(*refs):
            for step in self.steps(refs[:n], refs[n:2 * n], refs[2 * n:]):
                if step is not None:
                    step()

        in_specs, out_specs, out_shape, scratch, args = self.extend([], [], [], [], ())
        return pl.pallas_call(body, name=name, in_specs=in_specs, out_specs=out_specs, out_shape=out_shape,
                              scratch_shapes=scratch)(*args)


def _grid_flags(grid):
    ids = [pl.program_id(ax) for ax in range(len(grid))]
    inner_zero = functools.reduce(jnp.logical_and, [i == 0 for i in ids[1:]])
    first = jnp.logical_and(ids[0] == 0, inner_zero)
    middle = jnp.logical_and(ids[0] == grid[0] - 1, inner_zero)
    last = functools.reduce(jnp.logical_and, [i == g - 1 for i, g in zip(ids, grid)])
    return first, middle, last


SHARDED = ("w_in", "w_uq", "w_ukv", "w_o_mla", "w_o_na", "w_out", "w_ff1", "w_ff2")
ROW_SHARDED = ("w_out", "w_ff2")
TRANSPOSED = ("w_in",)
REPLICATED = ("norm_mix", "norm_qa", "norm_kva", "rpb", "norm_mlp", "norm_final")
WEIGHTS = ("norm_mix", "w_in", "norm_qa", "w_uq", "norm_kva", "w_ukv", "rpb", "w_o_mla", "w_o_na", "w_out",
           "norm_mlp", "w_ff1", "w_ff2", "norm_final")


def part_of(name, shard_shape):
    r, c = shard_shape
    kind = "row" if name in ROW_SHARDED + TRANSPOSED else ("col" if c % LANES == 0 else "packed")
    return Part((r, c), kind, BF16)


def _whole(part, gathered):
    return gathered.transpose(1, 0, 2).reshape(part.r, -1) if part.kind == "packed" else gathered


def _for_exchange(part, full):
    return full.reshape(part.r, N_DEV, part.c).transpose(1, 0, 2) if part.kind == "packed" else full


class Dims:
    def __init__(self, x, w_in, norm_qa, norm_kva, rpb, w_o_mla, w_o_na, w_ff1):
        self.s, self.d = x.shape[1], x.shape[2]
        self.depth = w_in.shape[0]
        self.q_lora, self.kv_lora = norm_qa.shape[1], norm_kva.shape[1]
        self.mla_w, self.na_w = w_o_mla.shape[1], w_o_na.shape[1]
        self.mla_h, self.na_h = self.mla_w // V_HEAD, self.na_w // NA_HEAD_DIM
        self.d_ff = w_ff1.shape[2] * N_DEV
        assert rpb.shape[1] == self.na_h and self.s % GRID_W == 0 and self.s // GRID_W >= NA_KH
        self.in_lo = self.q_lora + self.kv_lora
        self.main_w = self.in_lo + 3 * self.na_w + 2 * self.d
        assert self.q_lora == self.kv_lora and self.in_lo % self.na_w == 0 and self.in_lo % LANES == 0
        assert (self.in_lo + 3 * self.na_w) % self.d == 0
        self.q_col0 = self.in_lo // NA_HEAD_DIM
        self.k_off = self.in_lo + self.na_w
        self.v_off = self.in_lo + 2 * self.na_w
        self.ga_col = (self.in_lo + 3 * self.na_w) // self.d
        self.gb_col = self.ga_col + 1


def _split_w_in(dm, wt):
    lo = dm.in_lo
    main = jnp.concatenate([wt[:lo], wt[lo + QK_ROPE:]], axis=0)
    kpe = jnp.pad(wt[lo:lo + QK_ROPE], ((0, LANES - QK_ROPE), (0, 0)))
    return main, kpe


def _join_w_in(dm, main, kpe):
    lo = dm.in_lo
    return jnp.concatenate([main[:lo], kpe[:QK_ROPE], main[lo:]], axis=0)


def _split_heads(w, h, widths):
    r = w.shape[0]
    w3 = w.reshape(r, h, sum(widths))
    out, o = [], 0
    for wd in widths:
        out.append(w3[:, :, o:o + wd].reshape(r, h * wd))
        o += wd
    return out


def _join_heads(parts, h):
    r = parts[0].shape[0]
    return jnp.concatenate([p.reshape(r, h, -1) for p in parts], axis=2).reshape(r, -1)


UQ_WIDTHS = (QK_NOPE, HALF_ROPE, HALF_ROPE)
UKV_WIDTHS = (QK_NOPE, V_HEAD)


def _by_head(parts, h):
    s = parts[0].shape[0]
    return jnp.concatenate([p.reshape(s, h, -1) for p in parts], axis=2).transpose(1, 0, 2)


def _from_head(t, widths):
    h, s, _ = t.shape
    t = t.transpose(1, 0, 2)
    out, o = [], 0
    for wd in widths:
        out.append(t[:, :, o:o + wd].reshape(s, h * wd))
        o += wd
    return out


def layer_fwd(dm, lname, x, w, g, cos_q, sin_q, cos_k, sin_k, ride, late_weights):
    s, h = dm.s, dm.mla_h
    u = rmsnorm_fwd(x, g["norm_mix"], name=f"{lname}_norm_mix")
    proj = mm(u, w["in_main"], mode="nt", name=f"{lname}_proj")
    kpe = mm(u, w["in_kpe"], mode="nt", name=f"{lname}_proj_kpe")
    qn = rmsnorm_fwd(proj, g["norm_qa"], width=dm.q_lora, col=0, name=f"{lname}_norm_qa")
    kvn = rmsnorm_fwd(proj, g["norm_kva"], width=dm.kv_lora, col=1, name=f"{lname}_norm_kva")
    q = mm(qn, w["uq"], mode="nn", name=f"{lname}_uq")
    kv = mm(kvn, w["ukv"], mode="nn", out_dtypes=(BF16,), name=f"{lname}_ukv")
    nope_w, half_w = h * QK_NOPE, h * HALF_ROPE
    q1, q2 = rope_pair(q[:, nope_w:nope_w + half_w], q[:, nope_w + half_w:], cos_q, sin_q, name=f"{lname}_rope_q")
    k1, k2 = rope_pair(kpe[:, :HALF_ROPE], kpe[:, HALF_ROPE:QK_ROPE], cos_k, sin_k, name=f"{lname}_rope_k")
    qh = _by_head([q[:, :nope_w].astype(BF16), q1, q2], h)
    kh = _by_head([kv[:, :nope_w], jnp.tile(k1, (1, h)), jnp.tile(k2, (1, h))], h)
    vt = kv[:, nope_w:].reshape(s, h, V_HEAD).transpose(1, 2, 0)
    o_a, lse, *gathered = mla_fwd(qh, kh, vt, name=f"{lname}_mla", ride=ride)
    w = {**w, **late_weights(gathered)}
    y_a = mm(o_a, w["o_mla"], mode="nn", out_dtypes=(BF16,), name=f"{lname}_o_mla")

    tables = na_bias_tables(g["rpb"])
    o_b = na_fwd(proj, tables, q_col0=dm.q_col0, k_col0=dm.k_off // NA_HEAD_DIM, v_col0=dm.v_off // NA_HEAD_DIM,
                 name=f"{lname}_na")
    y_b = mm(o_b, w["o_na"], mode="nn", out_dtypes=(BF16,), name=f"{lname}_o_na")

    merged = gate_fwd(proj, y_a, y_b, d=dm.d, ga_col=dm.ga_col, gb_col=dm.gb_col, name=f"{lname}_gate")
    x1 = mm(merged, w["out"], mode="nn", epi=lambda r, res: (r + res,), extras=(x,), name=f"{lname}_out")
    u2 = rmsnorm_fwd(x1, g["norm_mlp"], name=f"{lname}_norm_mlp")
    hid, act = mm(u2, w["ff1"], mode="nn", out_dtypes=(F32, BF16),
                  epi=lambda r: (r, jnp.square(jnp.maximum(r, 0.0))), name=f"{lname}_ff1")
    x2 = mm(act, w["ff2"], mode="nn", epi=lambda r, res: (r + res,), extras=(x1,), name=f"{lname}_ff2")
    saved = dict(x=x, u=u, proj=proj, qn=qn, kvn=kvn, kv=kv, qh=qh, kh=kh, o_a=o_a, lse=lse, y_a=y_a,
                 tables=tables, o_b=o_b, y_b=y_b, merged=merged, x1=x1, u2=u2, hid=hid, act=act)
    return x2, saved, w, gathered


def layer_bwd(dm, lname, dx2, w, g, sv, cos_q, sin_q, cos_k, sin_k, ride_of):
    h = dm.mla_h
    gw, gr = {}, {}
    gw["ff2"] = mm(sv["act"], dx2, mode="tn", out_dtypes=(BF16,), name=f"{lname}_d_ff2")
    dh = mm(dx2, w["ff2"], mode="nt", out_dtypes=(BF16,), extras=(sv["hid"],),
            epi=lambda r, hv: (r * (2.0 * jnp.maximum(hv, 0.0)),), name=f"{lname}_d_act")
    gw["ff1"] = mm(sv["u2"], dh, mode="tn", out_dtypes=(BF16,), name=f"{lname}_d_ff1")
    du2 = mm(dh, w["ff1"], mode="nt", name=f"{lname}_d_u2")
    dx1, gr["norm_mlp"] = rmsnorm_bwd(sv["x1"], g["norm_mlp"], du2, dx2, name=f"{lname}_d_norm_mlp")
    gw["out"] = mm(sv["merged"], dx1, mode="tn", out_dtypes=(BF16,), name=f"{lname}_d_out")
    dmerged = mm(dx1, w["out"], mode="nt", name=f"{lname}_d_merged")
    dy_a, dy_b, dga, dgb = gate_bwd(sv["proj"], sv["y_a"], sv["y_b"], dmerged, d=dm.d, ga_col=dm.ga_col,
                                    gb_col=dm.gb_col, name=f"{lname}_d_gate")
    gw["o_na"] = mm(sv["o_b"], dy_b, mode="tn", out_dtypes=(BF16,), name=f"{lname}_d_o_na")
    do_b = mm(dy_b, w["o_na"], mode="nt", out_dtypes=(BF16,), name=f"{lname}_d_ob")
    dq_na, dk_na, dv_na, dtables = na_bwd(sv["proj"], sv["tables"], do_b, q_col0=dm.q_col0,
                                          k_col0=dm.k_off // NA_HEAD_DIM, v_col0=dm.v_off // NA_HEAD_DIM,
                                          name=f"{lname}_d_na")
    gr["rpb"] = na_bias_grad(dtables, name=f"{lname}_d_rpb")
    gw["o_mla"] = mm(sv["o_a"], dy_a, mode="tn", out_dtypes=(BF16,), name=f"{lname}_d_o_mla")
    do_a = mm(dy_a, w["o_mla"], mode="nt", out_dtypes=(BF16,), name=f"{lname}_d_oa")
    delta = mla_delta(do_a, sv["o_a"], h, name=f"{lname}_d_mla_delta").reshape(h, 1, dm.s)
    dqt, dkh, dv, *received = mla_bwd(sv["qh"], sv["kh"], sv["kh"].transpose(0, 2, 1), sv["kv"], do_a, sv["lse"],
                                      delta, v_col0=h, name=f"{lname}_d_mla", ride=ride_of(gw))
    dqh = dqt.transpose(0, 1, 3, 2).reshape(h, dm.s, QK_DIM)
    dq_nope, dq1, dq2 = _from_head(dqh, UQ_WIDTHS)
    dq1, dq2 = rope_pair(dq1, dq2, cos_q, -sin_q, name=f"{lname}_d_rope_q")
    dq = jnp.concatenate([dq_nope.astype(BF16), dq1, dq2], axis=1)
    gw["uq"] = mm(sv["qn"], dq, mode="tn", out_dtypes=(BF16,), name=f"{lname}_d_uq")
    dqn = mm(dq, w["uq"], mode="nt", name=f"{lname}_d_qn")
    dk_nope = _from_head(dkh[:, :, :QK_NOPE], (QK_NOPE,))[0]
    dk1, dk2 = rope_pair_headsum(dkh, cos_k, sin_k, name=f"{lname}_d_rope_k")
    dkv = jnp.concatenate([dk_nope.astype(BF16), dv], axis=1)
    gw["ukv"] = mm(sv["kvn"], dkv, mode="tn", out_dtypes=(BF16,), name=f"{lname}_d_ukv")
    dkvn = mm(dkv, w["ukv"], mode="nt", name=f"{lname}_d_kvn")
    dc_q, gr["norm_qa"] = rmsnorm_bwd(sv["proj"], g["norm_qa"], dqn, width=dm.q_lora, col=0, out_dtype=BF16,
                                      name=f"{lname}_d_norm_qa")
    dc_kv, gr["norm_kva"] = rmsnorm_bwd(sv["proj"], g["norm_kva"], dkvn, width=dm.kv_lora, col=1, out_dtype=BF16,
                                        name=f"{lname}_d_norm_kva")
    dproj = jnp.concatenate([dc_q, dc_kv, dq_na, dk_na.astype(BF16), dv_na.astype(BF16), dga, dgb], axis=1)
    dkpe = jnp.concatenate([dk1, dk2, jnp.zeros((dm.s, LANES - QK_ROPE), BF16)], axis=1)
    gw["in_main"] = mm(dproj, sv["u"], mode="tn", out_dtypes=(BF16,), name=f"{lname}_d_in")
    gw["in_kpe"] = mm(dkpe, sv["u"], mode="tn", out_dtypes=(BF16,), name=f"{lname}_d_in_kpe")
    du_k = mm(dkpe, w["in_kpe"], mode="nn", name=f"{lname}_d_u_kpe")
    du = mm(dproj, w["in_main"], mode="nn", epi=lambda r, res: (r + res,), extras=(du_k,), name=f"{lname}_d_u")
    dx, gr["norm_mix"] = rmsnorm_bwd(sv["x"], g["norm_mix"], du, dx1, name=f"{lname}_d_norm_mix")
    return dx, gw, gr, received


EARLY = ("w_in", "w_uq", "w_ukv")
LATE = ("w_o_mla", "w_o_na", "w_out", "w_ff1", "w_ff2")
LATE_KEYS = ("o_mla", "o_na", "out", "ff1", "ff2")


def _early_weights(dm, full):
    main, kpe = _split_w_in(dm, full["w_in"])
    return dict(in_main=main, in_kpe=kpe,
                uq=jnp.concatenate(_split_heads(full["w_uq"], dm.mla_h, UQ_WIDTHS), axis=1),
                ukv=jnp.concatenate(_split_heads(full["w_ukv"], dm.mla_h, UKV_WIDTHS), axis=1))


def _early_grads(dm, gw):
    h = dm.mla_h
    nope_w, half_w = h * QK_NOPE, h * HALF_ROPE
    uq = gw["uq"]
    ukv = gw["ukv"]
    return {"w_in": _join_w_in(dm, gw["in_main"], gw["in_kpe"]),
            "w_uq": _join_heads([uq[:, :nope_w], uq[:, nope_w:nope_w + half_w], uq[:, nope_w + half_w:]], h),
            "w_ukv": _join_heads([ukv[:, :nope_w], ukv[:, nope_w:]], h)}


def _pack_replicated(dm, parts):
    flat = jnp.concatenate([parts[n].reshape(-1) for n in REPLICATED])
    n = flat.shape[0]
    rows_ = -(-n // (8 * LANES)) * 8
    return jnp.pad(flat, (0, rows_ * LANES - n)).reshape(rows_, LANES)


def kernel(x, norm_mix, w_in, norm_qa, w_uq, norm_kva, w_ukv, rpb, w_o_mla, w_o_na, w_out, norm_mlp, w_ff1, w_ff2, norm_final, loss_target, m_norm_mix, m_w_in, m_norm_qa, m_w_uq, m_norm_kva, m_w_ukv, m_rpb, m_w_o_mla, m_w_o_na, m_w_out, m_norm_mlp, m_w_ff1, m_w_ff2, m_norm_final, v_norm_mix, v_w_in, v_norm_qa, v_w_uq, v_norm_kva, v_w_ukv, v_rpb, v_w_o_mla, v_w_o_na, v_w_out, v_norm_mlp, v_w_ff1, v_w_ff2, v_norm_final):
    dm = Dims(x, w_in, norm_qa, norm_kva, rpb, w_o_mla, w_o_na, w_ff1)
    params = dict(norm_mix=norm_mix, w_in=w_in, norm_qa=norm_qa, w_uq=w_uq, norm_kva=norm_kva, w_ukv=w_ukv, rpb=rpb,
                  w_o_mla=w_o_mla, w_o_na=w_o_na, w_out=w_out, norm_mlp=norm_mlp, w_ff1=w_ff1, w_ff2=w_ff2,
                  norm_final=norm_final)
    mom_m = dict(norm_mix=m_norm_mix, w_in=m_w_in, norm_qa=m_norm_qa, w_uq=m_w_uq, norm_kva=m_norm_kva, w_ukv=m_w_ukv,
                 rpb=m_rpb, w_o_mla=m_w_o_mla, w_o_na=m_w_o_na, w_out=m_w_out, norm_mlp=m_norm_mlp, w_ff1=m_w_ff1,
                 w_ff2=m_w_ff2, norm_final=m_norm_final)
    mom_v = dict(norm_mix=v_norm_mix, w_in=v_w_in, norm_qa=v_norm_qa, w_uq=v_w_uq, norm_kva=v_norm_kva, w_ukv=v_w_ukv,
                 rpb=v_rpb, w_o_mla=v_w_o_mla, w_o_na=v_w_o_na, w_out=v_w_out, norm_mlp=v_norm_mlp, w_ff1=v_w_ff1,
                 w_ff2=v_w_ff2, norm_final=v_norm_final)
    depth, s, h = dm.depth, dm.s, dm.mla_h

    pos = jnp.arange(s, dtype=F32)
    inv_freq = 1.0 / (ROPE_THETA ** (jnp.arange(0, QK_ROPE, 2, dtype=F32) / QK_ROPE))
    ang = pos[:, None] * inv_freq[None, :]
    cos_k, sin_k = jnp.cos(ang), jnp.sin(ang)
    cos_q, sin_q = jnp.tile(cos_k, (1, h)), jnp.tile(sin_k, (1, h))

    held = lambda d, n: d[n].transpose(0, 2, 1) if n in TRANSPOSED else d[n]
    part = {n: part_of(n, held(params, n).shape[1:]) for n in SHARDED}
    shard = lambda n, l: held(params, n)[l].astype(BF16)
    whole = lambda names, arrays: {n: _whole(part[n], a) for n, a in zip(names, arrays)}
    gains = [dict(norm_mix=norm_mix[l][None], norm_qa=norm_qa[l][None], norm_kva=norm_kva[l][None],
                  norm_mlp=norm_mlp[l][None], rpb=rpb[l]) for l in range(depth)]

    xl = x[0]
    saved, weights = [], []
    early = Collective("gather", [part[n] for n in EARLY], [shard(n, 0) for n in EARLY]).run("gather_weights")
    for l in range(depth):
        nxt = EARLY if l + 1 < depth else ()
        ride = Collective("gather", [part[n] for n in LATE + nxt],
                          [shard(n, l) for n in LATE] + [shard(n, l + 1) for n in nxt])
        late_weights = lambda got: dict(zip(LATE_KEYS, whole(LATE, got[:len(LATE)]).values()))
        xl, sv, w, got = layer_fwd(dm, "fwd", xl, _early_weights(dm, whole(EARLY, early)), gains[l], cos_q, sin_q,
                                   cos_k, sin_k, ride, late_weights)
        early = got[len(LATE):]
        saved.append(sv)
        weights.append(w)
    dx, g_final, loss_part = loss_head(xl, norm_final[None], loss_target[0], name="loss_head")
    loss = lax.psum(loss_part[0, 0], MESH_AXES)

    rep = {n: [None] * depth for n in REPLICATED if n != "norm_final"}
    recv = [{} for _ in range(depth)]
    pending = {}
    for l in reversed(range(depth)):
        def ride_of(gw, pending=pending):
            ready = {**dict(zip(LATE, [gw[k] for k in LATE_KEYS])), **pending}
            return Collective("exchange", [part[n] for n in ready], [_for_exchange(part[n], a) for n, a in ready.items()])

        dx, gw, gr, received = layer_bwd(dm, "bwd", dx, weights[l], gains[l], saved[l], cos_q, sin_q, cos_k, sin_k,
                                         ride_of)
        recv[l].update(zip(LATE, received[:len(LATE)]))
        if pending:
            recv[l + 1].update(zip(EARLY, received[len(LATE):]))
        for n in gr:
            rep[n][l] = gr[n]
        pending = _early_grads(dm, gw)
    last = Collective("exchange", [part[n] for n in EARLY], [_for_exchange(part[n], pending[n]) for n in EARLY])
    recv[0].update(zip(EARLY, last.run("scatter_grads")))
    rep_parts = {n: jnp.stack(rep[n]) for n in rep}
    rep_parts["norm_final"] = g_final
    packed = _pack_replicated(dm, rep_parts)
    rep_all = Collective("gather", [Part(packed.shape, "packed", F32)], [packed]).run("gather_small_grads")[0]

    outs = {}
    for n in SHARDED:
        res = adamw_layers(held(params, n), held(mom_m, n), held(mom_v, n), [recv[l][n] for l in range(depth)],
                           name=f"adamw_{n}")
        outs[n] = [a.transpose(0, 2, 1) for a in res] if n in TRANSPOSED else res
    n_rep = sum(int(np.prod(params[n].shape)) for n in REPLICATED)
    pack = lambda d: _pack_replicated(dm, d)
    res = adamw(pack(params), pack(mom_m), pack(mom_v), rep_all, name="adamw_replicated")
    off = 0
    for n in REPLICATED:
        size = int(np.prod(params[n].shape))
        outs[n] = [a.reshape(-1)[off:off + size].reshape(params[n].shape) for a in res]
        off += size
    assert off == n_rep

    grad_x = dx[None]
    return (loss, grad_x, *[outs[n][0] for n in WEIGHTS], *[outs[n][1] for n in WEIGHTS],
            *[outs[n][2] for n in WEIGHTS], *[outs[n][3] for n in WEIGHTS])
```

```python
import functools

import numpy as np
import jax
import jax.numpy as jnp
from jax import lax
from jax.experimental import pallas as pl
from jax.experimental.pallas import tpu as pltpu

F32 = jnp.float32
BF16 = jnp.bfloat16
MESH_AXES = ("x", "y", "c")
N_DEV = 8
LANES = 128

QK_NOPE = 128
QK_ROPE = 64
HALF_ROPE = QK_ROPE // 2
V_HEAD = 128
QK_DIM = QK_NOPE + QK_ROPE
NA_HEAD_DIM = 128
GRID_W = 64
NA_KH = 8
NA_KW = 16
NA_WIN = NA_KH * GRID_W
ROPE_THETA = 10000.0
EPS = 1e-6
NEG = -1e30

ADAM_LR = 0.001
ADAM_B1 = 0.9
ADAM_B2 = 0.999
ADAM_EPS = 1e-08
ADAM_WD = 0.01
ADAM_STEP = 10

VMEM_LIMIT_V7X = 56 * 1024 * 1024
ADAMW_BLOCK_BYTES = 24 * 1024 * 1024

NN = (((1,), (0,)), ((), ()))
NT = (((1,), (1,)), ((), ()))
TN = (((0,), (0,)), ((), ()))


def _pick(dim, target, align=LANES):
    if dim <= target:
        return dim
    t = (target // align) * align
    while t >= align:
        if dim % t == 0:
            return t
        t -= align
    return dim


def _params(sem):
    return pltpu.CompilerParams(dimension_semantics=sem, vmem_limit_bytes=VMEM_LIMIT_V7X)


def mm(a, b, *, mode, name, out_dtypes=(F32,), epi=None, extras=(), tm=1024, tn=1024, tk=2048, exact=False):
    if mode == "nn":
        (m, k), (k2, n) = a.shape, b.shape
    elif mode == "nt":
        (m, k), (n, k2) = a.shape, b.shape
    else:
        (k, m), (k2, n) = a.shape, b.shape
    assert k == k2, (a.shape, b.shape, mode)
    tm, tn, tk = _pick(m, tm), _pick(n, tn), _pick(k, tk)
    nk = k // tk
    a_spec = pl.BlockSpec((tk, tm), lambda i, j, s: (s, i)) if mode == "tn" else pl.BlockSpec((tm, tk), lambda i, j, s: (i, s))
    b_spec = pl.BlockSpec((tn, tk), lambda i, j, s: (j, s)) if mode == "nt" else pl.BlockSpec((tk, tn), lambda i, j, s: (s, j))
    tile = pl.BlockSpec((tm, tn), lambda i, j, s: (i, j))
    dims = {"nn": NN, "nt": NT, "tn": TN}[mode]
    n_extra, n_out = len(extras), len(out_dtypes)

    def product(a_ref, b_ref):
        if exact:
            return lax.dot_general(a_ref[...], b_ref[...], dims, precision=lax.Precision.HIGHEST,
                                   preferred_element_type=F32)
        return lax.dot_general(a_ref[...].astype(BF16), b_ref[...].astype(BF16), dims, preferred_element_type=F32)

    def finish(r, extra_refs, out_refs):
        res = (r,) if epi is None else epi(r, *[e[...] for e in extra_refs])
        for o, v in zip(out_refs, res):
            o[...] = v.astype(o.dtype)

    def body_one_step(a_ref, b_ref, *rest):
        finish(product(a_ref, b_ref), rest[:n_extra], rest[n_extra:])

    def body(a_ref, b_ref, *rest):
        extra_refs, out_refs, acc = rest[:n_extra], rest[n_extra:n_extra + n_out], rest[-1]
        step = pl.program_id(2)

        @pl.when(step == 0)
        def _():
            acc[...] = product(a_ref, b_ref)

        @pl.when(step > 0)
        def _():
            acc[...] += product(a_ref, b_ref)

        @pl.when(step == nk - 1)
        def _():
            finish(acc[...], extra_refs, out_refs)

    outs = pl.pallas_call(
        body_one_step if nk == 1 else body, name=name, grid=(m // tm, n // tn, nk),
        in_specs=[a_spec, b_spec] + [tile] * n_extra,
        out_specs=[tile] * n_out,
        out_shape=[jax.ShapeDtypeStruct((m, n), d) for d in out_dtypes],
        scratch_shapes=[] if nk == 1 else [pltpu.VMEM((tm, tn), F32)],
        compiler_params=_params(("parallel", "parallel", "arbitrary")),
    )(a, b, *extras)
    return outs[0] if n_out == 1 else outs


def blockwise(fn, ins, outs, *, grid, name, sums=()):
    n_in, n_axes = len(ins), len(grid)

    def body(*refs):
        res = fn(*[r[...] for r in refs[:n_in]])
        first = functools.reduce(jnp.logical_and, [pl.program_id(ax) == 0 for ax in range(n_axes)])
        for idx, (o, v) in enumerate(zip(refs[n_in:], res)):
            if idx in sums:
                @pl.when(first)
                def _(o=o):
                    o[...] = jnp.zeros_like(o)

                o[...] += v.astype(o.dtype)
            else:
                o[...] = v.astype(o.dtype)

    sem = ("arbitrary" if sums else "parallel",) * n_axes
    res = pl.pallas_call(
        body, name=name, grid=grid,
        in_specs=[pl.BlockSpec(blk, imap) for _, blk, imap in ins],
        out_specs=[pl.BlockSpec(blk, imap) for _, _, blk, imap in outs],
        out_shape=[jax.ShapeDtypeStruct(shape, dt) for shape, dt, _, _ in outs],
        compiler_params=_params(sem),
    )(*[a for a, _, _ in ins])
    return res


def rows(arr, tr, width=None, col=0):
    width = arr.shape[1] if width is None else width
    return (arr, (tr, width), lambda i, col=col: (i, col))


def whole(arr):
    return (arr, arr.shape, lambda i: (0, 0))


def out_rows(n_rows, width, dtype, tr):
    return ((n_rows, width), dtype, (tr, width), lambda i: (i, 0))


def out_sum(shape, dtype=F32):
    return (shape, dtype, shape, lambda i: (0, 0))


def _rstd(x):
    return lax.rsqrt(jnp.mean(x * x, axis=-1, keepdims=True) + EPS)


def _colsum(v):
    return jnp.sum(v, axis=0, keepdims=True)


def rmsnorm_fwd(x, g, *, width=None, col=0, tr=256, name):
    n = x.shape[0]
    tr = _pick(n, tr, 8)
    width = x.shape[1] if width is None else width

    def fn(xv, gv):
        return ((xv * _rstd(xv)) * gv,)

    return blockwise(fn, [rows(x, tr, width, col), whole(g)], [out_rows(n, width, BF16, tr)],
                     grid=(n // tr,), name=name)[0]


def rmsnorm_bwd(x, g, dy, res=None, *, width=None, col=0, out_dtype=F32, tr=256, name):
    n = x.shape[0]
    tr = _pick(n, tr, 8)
    width = x.shape[1] if width is None else width

    def fn(xv, gv, dyv, *resv):
        dyv = dyv.astype(F32)
        r = _rstd(xv)
        xhat = xv * r
        dxhat = dyv * gv
        dx = r * (dxhat - xhat * jnp.mean(dxhat * xhat, axis=-1, keepdims=True))
        if resv:
            dx = dx + resv[0]
        return dx, _colsum(dyv * xhat)

    ins = [rows(x, tr, width, col), whole(g), rows(dy, tr)] + ([rows(res, tr)] if res is not None else [])
    return blockwise(fn, ins, [out_rows(n, width, out_dtype, tr), out_sum((1, width))],
                     grid=(n // tr,), name=name, sums=(1,))


def rope_pair(x1, x2, cos, sin, *, tr=512, name):
    n, w = x1.shape
    tr = _pick(n, tr, 8)

    def fn(a, b, c, s):
        a, b = a.astype(F32), b.astype(F32)
        return a * c - b * s, b * c + a * s

    return blockwise(fn, [rows(x1, tr), rows(x2, tr), rows(cos, tr), rows(sin, tr)],
                     [out_rows(n, w, BF16, tr), out_rows(n, w, BF16, tr)], grid=(n // tr,), name=name)


def rope_pair_headsum(d, cos, sin, *, tr=512, name):
    h, n, w = d.shape
    tr = _pick(n, tr, 8)

    def fn(dv, c, s):
        t = jnp.sum(dv, axis=0)
        a, b = t[:, QK_NOPE:QK_NOPE + HALF_ROPE], t[:, QK_NOPE + HALF_ROPE:]
        return a * c + b * s, b * c - a * s

    return blockwise(fn, [(d, (h, tr, w), lambda i: (0, i, 0)), rows(cos, tr), rows(sin, tr)],
                     [out_rows(n, HALF_ROPE, BF16, tr), out_rows(n, HALF_ROPE, BF16, tr)], grid=(n // tr,), name=name)


def gate_fwd(proj, y_a, y_b, *, d, ga_col, gb_col, tr=256, name):
    n = proj.shape[0]
    tr = _pick(n, tr, 8)

    def fn(ga, gb, ya, yb):
        return (jax.nn.sigmoid(ga) * ya + jax.nn.sigmoid(gb) * yb,)

    return blockwise(fn, [rows(proj, tr, d, ga_col), rows(proj, tr, d, gb_col), rows(y_a, tr), rows(y_b, tr)],
                     [out_rows(n, d, BF16, tr)], grid=(n // tr,), name=name)[0]


def gate_bwd(proj, y_a, y_b, dmerged, *, d, ga_col, gb_col, tr=256, name):
    n = proj.shape[0]
    tr = _pick(n, tr, 8)

    def fn(ga, gb, ya, yb, dm):
        sa, sb = jax.nn.sigmoid(ga), jax.nn.sigmoid(gb)
        return dm * sa, dm * sb, dm * ya * (sa * (1.0 - sa)), dm * yb * (sb * (1.0 - sb))

    return blockwise(fn, [rows(proj, tr, d, ga_col), rows(proj, tr, d, gb_col), rows(y_a, tr), rows(y_b, tr),
                          rows(dmerged, tr)],
                     [out_rows(n, d, BF16, tr)] * 4, grid=(n // tr,), name=name)


def loss_head(x, g, target, *, tr=256, name):
    n, d = x.shape
    tr = _pick(n, tr, 8)

    def fn(xv, gv, tv):
        r = _rstd(xv)
        xhat = xv * r
        diff = xhat * gv - tv
        loss = 0.5 * jnp.sum(jnp.sum(diff * diff, axis=-1, keepdims=True) / d, axis=0, keepdims=True)
        dy = diff / d
        dxhat = dy * gv
        dx = r * (dxhat - xhat * jnp.mean(dxhat * xhat, axis=-1, keepdims=True))
        return dx, _colsum(dy * xhat), jnp.broadcast_to(loss, (8, LANES))

    return blockwise(fn, [rows(x, tr), whole(g), rows(target, tr)],
                     [out_rows(n, d, F32, tr), out_sum((1, d)), out_sum((8, LANES))],
                     grid=(n // tr,), name=name, sums=(1, 2))


def _adamw_math(wv, mv, vv, gs):
    c1 = 1.0 / (1.0 - ADAM_B1 ** ADAM_STEP)
    c2 = 1.0 / (1.0 - ADAM_B2 ** ADAM_STEP)
    g = gs[0].astype(F32)
    for dev in range(1, N_DEV):
        g = g + gs[dev].astype(F32)
    m_new = ADAM_B1 * mv + (1.0 - ADAM_B1) * g
    v_new = ADAM_B2 * vv + (1.0 - ADAM_B2) * (g * g)
    delta = -ADAM_LR * ((m_new * c1) / (jnp.sqrt(v_new * c2) + ADAM_EPS) + ADAM_WD * wv)
    return g, delta, m_new, v_new


def adamw(w, m, v, g_slots, *, tr=256, name):
    n, c = w.shape
    tr = _pick(n, tr, 8)
    slots = (g_slots, (N_DEV, tr, c), lambda i: (0, i, 0))
    return blockwise(_adamw_math, [rows(w, tr), rows(m, tr), rows(v, tr), slots],
                     [out_rows(n, c, F32, tr)] * 4, grid=(n // tr,), name=name)


def adamw_layers(w, m, v, g_layers, *, name):
    depth, r, c = w.shape
    row_bytes = c * 2 * (depth * N_DEV * g_layers[0].dtype.itemsize + 7 * 4)
    tr = _pick(r, max(8, ADAMW_BLOCK_BYTES // row_bytes // 8 * 8), 8)

    def body(w_ref, m_ref, v_ref, *rest):
        g_refs, out_refs = rest[:depth], rest[depth:]
        for layer in range(depth):
            @pl.when(pl.program_id(0) == layer)
            def _(layer=layer):
                for o, val in zip(out_refs, _adamw_math(w_ref[...], m_ref[...], v_ref[...], g_refs[layer][...])):
                    o[...] = val

    blk = pl.BlockSpec((None, tr, c), lambda l, i: (l, i, 0))
    g_specs = [pl.BlockSpec((N_DEV, tr, c), lambda l, i, layer=layer: (0, jnp.where(l == layer, i, 0), 0))
               for layer in range(depth)]
    return pl.pallas_call(
        body, name=name, grid=(depth, r // tr), in_specs=[blk] * 3 + g_specs, out_specs=[blk] * 4,
        out_shape=[jax.ShapeDtypeStruct(w.shape, F32)] * 4, compiler_params=_params(("arbitrary", "arbitrary")),
    )(w, m, v, *g_layers)


LOG2E = 1.4426950408889634
MLA_FWD_SUB, MLA_FWD_BQ = 512, 4096
MLA_BWD_SUB, MLA_BWD_BQ = 1024, 2048
MLA_FWD_BK, MLA_BWD_BK = 4096, 2048


def mla_fwd(q, k, vt, *, name, ride=None):
    h, s, dq = q.shape
    bq, bk = _pick(s, MLA_FWD_BQ), _pick(s, MLA_FWD_BK)
    sub = min(MLA_FWD_SUB, bq)
    nk = s // bk
    scale = QK_DIM ** -0.5
    c2 = scale * LOG2E
    grid = (h, s // bq, nk)

    def body(q_ref, k_ref, vt_ref, *rest):
        if ride is None:
            o_ref, lse_ref, m_s, l_s, acc = rest
        else:
            n = ride.n
            x_refs, (o_ref, lse_ref), g_refs = rest[:n], rest[n:n + 2], rest[n + 2:2 * n + 2]
            m_s, l_s, acc, *sems = rest[2 * n + 2:]
            first, middle, last = _grid_flags(grid)
            ride_start, ride_middle, ride_finish = ride.steps(x_refs, g_refs, sems)
            pl.when(first)(ride_start)
            pl.when(middle)(ride_middle)
        j = pl.program_id(2)

        @pl.when(j == 0)
        def _():
            m_s[...] = jnp.full_like(m_s, NEG)
            l_s[...] = jnp.zeros_like(l_s)
            acc[...] = jnp.zeros_like(acc)

        def scores(c0):
            return lax.dot_general(k_ref[...], q_ref[pl.ds(c0, sub), :], NT, preferred_element_type=F32)

        starts = list(range(0, bq, sub))
        st_next = scores(starts[0])
        for n, c0 in enumerate(starts):
            cols = pl.ds(c0, sub)
            st = st_next
            if n + 1 < len(starts):
                st_next = scores(starts[n + 1])
            m_prev = m_s[:, cols]
            m_new = jnp.maximum(m_prev, jnp.max(st, axis=0, keepdims=True))
            alpha = jnp.exp2((m_prev - m_new) * c2)
            pt = jnp.exp2((st - m_new) * c2)
            l_s[:, cols] = alpha * l_s[:, cols] + jnp.sum(pt, axis=0, keepdims=True)
            acc[:, cols] = alpha * acc[:, cols] + lax.dot_general(vt_ref[...], pt.astype(BF16), NN,
                                                                  preferred_element_type=F32)
            m_s[:, cols] = m_new

        @pl.when(j == nk - 1)
        def _():
            o_ref[...] = (acc[...] / l_s[...]).T
            lse_ref[...] = m_s[...] * scale + jnp.log(l_s[...])

        if ride is not None:
            pl.when(last)(ride_finish)

    in_specs = [pl.BlockSpec((None, bq, dq), lambda hh, i, j: (hh, i, 0)),
                pl.BlockSpec((None, bk, dq), lambda hh, i, j: (hh, j, 0)),
                pl.BlockSpec((None, V_HEAD, bk), lambda hh, i, j: (hh, 0, j))]
    out_specs = [pl.BlockSpec((bq, V_HEAD), lambda hh, i, j: (i, hh)),
                 pl.BlockSpec((None, 1, bq), lambda hh, i, j: (hh, 0, i))]
    out_shape = [jax.ShapeDtypeStruct((s, h * V_HEAD), F32), jax.ShapeDtypeStruct((h, 1, s), F32)]
    scratch = [pltpu.VMEM((1, bq), F32), pltpu.VMEM((1, bq), F32), pltpu.VMEM((V_HEAD, bq), F32)]
    args = (q, k, vt)
    sem = ("parallel", "parallel", "arbitrary")
    if ride is not None:
        in_specs, out_specs, out_shape, scratch, args = ride.extend(in_specs, out_specs, out_shape, scratch, args)
        sem = ("arbitrary",) * 3
    return pl.pallas_call(body, name=name, grid=grid, in_specs=in_specs, out_specs=out_specs, out_shape=out_shape,
                          scratch_shapes=scratch, compiler_params=_params(sem))(*args)


def mla_delta(do, o, h, *, tr=4096, name):
    s = do.shape[0]
    tr = _pick(s, tr, 8)

    def fn(dov, ov):
        return (jnp.sum(dov.astype(F32) * ov, axis=-1, keepdims=True),)

    blk = lambda arr: (arr, (tr, V_HEAD), lambda hh, i: (i, hh))
    return blockwise(fn, [blk(do), blk(o)], [((h, s, 1), F32, (None, tr, 1), lambda hh, i: (hh, i, 0))],
                     grid=(h, s // tr), name=name)[0]


def mla_bwd(q, k, kt, kv, do, lse, delta, *, v_col0, name, ride=None):
    h, s, dq = q.shape
    bq, bk = _pick(s, MLA_BWD_BQ), _pick(s, MLA_BWD_BK)
    sub = min(MLA_BWD_SUB, bq)
    nq = s // bq
    scale = QK_DIM ** -0.5
    c2 = scale * LOG2E
    grid = (h, s // bk, nq)

    def body(q_ref, k_ref, kt_ref, v_ref, do_ref, lse_ref, delta_ref, *rest):
        if ride is None:
            dq_ref, dk_ref, dv_ref, dk_acc, dv_acc = rest
        else:
            n = ride.n
            x_refs, (dq_ref, dk_ref, dv_ref), g_refs = rest[:n], rest[n:n + 3], rest[n + 3:2 * n + 3]
            dk_acc, dv_acc, *sems = rest[2 * n + 3:]
            first, _, last = _grid_flags(grid)
            ride_start, _, ride_finish = ride.steps(x_refs, g_refs, sems)
            pl.when(first)(ride_start)
        j, i = pl.program_id(1), pl.program_id(2)

        @pl.when(i == 0)
        def _():
            dk_acc[...] = jnp.zeros_like(dk_acc)
            dv_acc[...] = jnp.zeros_like(dv_acc)

        def scores(c0):
            cols = pl.ds(c0, sub)
            return (lax.dot_general(k_ref[...], q_ref[cols, :], NT, preferred_element_type=F32),
                    lax.dot_general(v_ref[...], do_ref[cols, :], NT, preferred_element_type=F32))

        starts = list(range(0, bq, sub))
        nxt = scores(starts[0])
        for n, c0 in enumerate(starts):
            cols = pl.ds(c0, sub)
            st, dpt = nxt
            if n + 1 < len(starts):
                nxt = scores(starts[n + 1])
            q_sub, do_sub = q_ref[cols, :], do_ref[cols, :]
            pt = jnp.exp2(st * c2 - lse_ref[:, cols] * LOG2E)
            ds_b = ((pt * (dpt - delta_ref[:, cols])) * scale).astype(BF16)
            dv_acc[...] += lax.dot_general(pt.astype(BF16), do_sub, NN, preferred_element_type=F32)
            dk_acc[...] += lax.dot_general(ds_b, q_sub, NN, preferred_element_type=F32)
            dq_t = lax.dot_general(kt_ref[...], ds_b, NN, preferred_element_type=F32)

            @pl.when(j == 0)
            def _():
                dq_ref[i, :, cols] = dq_t

            @pl.when(j > 0)
            def _():
                dq_ref[i, :, cols] += dq_t

        @pl.when(i == nq - 1)
        def _():
            dk_ref[...] = dk_acc[...]
            dv_ref[...] = dv_acc[...].astype(dv_ref.dtype)

        if ride is not None:
            pl.when(last)(ride_finish)

    in_specs = [pl.BlockSpec((None, bq, dq), lambda hh, j, i: (hh, i, 0)),
                pl.BlockSpec((None, bk, dq), lambda hh, j, i: (hh, j, 0)),
                pl.BlockSpec((None, dq, bk), lambda hh, j, i: (hh, 0, j)),
                pl.BlockSpec((bk, V_HEAD), lambda hh, j, i: (j, v_col0 + hh)),
                pl.BlockSpec((bq, V_HEAD), lambda hh, j, i: (i, hh)),
                pl.BlockSpec((None, 1, bq), lambda hh, j, i: (hh, 0, i)),
                pl.BlockSpec((None, 1, bq), lambda hh, j, i: (hh, 0, i))]
    out_specs = [pl.BlockSpec((None, nq, dq, bq), lambda hh, j, i: (hh, 0, 0, 0)),
                 pl.BlockSpec((None, bk, dq), lambda hh, j, i: (hh, j, 0)),
                 pl.BlockSpec((bk, V_HEAD), lambda hh, j, i: (j, hh))]
    out_shape = [jax.ShapeDtypeStruct((h, nq, dq, bq), F32), jax.ShapeDtypeStruct((h, s, dq), F32),
                 jax.ShapeDtypeStruct((s, h * V_HEAD), BF16)]
    scratch = [pltpu.VMEM((bk, dq), F32), pltpu.VMEM((bk, V_HEAD), F32)]
    args = (q, k, kt, kv, do, lse, delta)
    sem = ("parallel", "arbitrary", "arbitrary")
    if ride is not None:
        in_specs, out_specs, out_shape, scratch, args = ride.extend(in_specs, out_specs, out_shape, scratch, args)
        sem = ("arbitrary",) * 3
    return pl.pallas_call(body, name=name, grid=grid, in_specs=in_specs, out_specs=out_specs, out_shape=out_shape,
                          scratch_shapes=scratch, compiler_params=_params(sem))(*args)


def _na_window(r, n_rows):
    row_start = jnp.clip(r - NA_KH // 2, 0, n_rows - NA_KH)
    return row_start, row_start - r + (NA_KH - 1)


def _na_probs(sc, bias):
    sc = sc * (NA_HEAD_DIM ** -0.5) + bias
    p = jnp.exp(sc - jnp.max(sc, axis=-1, keepdims=True))
    return p / jnp.sum(p, axis=-1, keepdims=True)


def na_fwd(proj, tables, *, q_col0, k_col0, v_col0, rows_per_step=32, name):
    s = proj.shape[0]
    h = tables.shape[0]
    n_rows = s // GRID_W
    rb = min(rows_per_step, n_rows)
    tq = rb * GRID_W

    def body(q_ref, k_ref, v_ref, b_ref, o_ref):
        i = pl.program_id(1)

        def window(rl):
            row_start, which = _na_window(i * rb + rl, n_rows)
            return pl.ds(pl.multiple_of(row_start * GRID_W, GRID_W), NA_WIN), which

        def scores(rl):
            q_row = q_ref[pl.ds(rl * GRID_W, GRID_W), :].astype(BF16)
            return lax.dot_general(q_row, k_ref[window(rl)[0], :].astype(BF16), NT, preferred_element_type=F32)

        ahead = [scores(rl) for rl in range(rb)]
        for rl in range(rb):
            sc = ahead[rl]
            keys, which = window(rl)
            p = _na_probs(sc, b_ref[which])
            o_ref[pl.ds(rl * GRID_W, GRID_W), :] = lax.dot_general(
                p.astype(BF16), v_ref[keys, :].astype(BF16), NN, preferred_element_type=F32)

    head = lambda col0: pl.BlockSpec((s, NA_HEAD_DIM), lambda hh, i: (0, col0 + hh))
    return pl.pallas_call(
        body, name=name, grid=(h, n_rows // rb),
        in_specs=[pl.BlockSpec((tq, NA_HEAD_DIM), lambda hh, i: (i, q_col0 + hh)), head(k_col0), head(v_col0),
                  pl.BlockSpec((None, NA_KH, GRID_W, NA_WIN), lambda hh, i: (hh, 0, 0, 0))],
        out_specs=pl.BlockSpec((tq, NA_HEAD_DIM), lambda hh, i: (i, hh)),
        out_shape=jax.ShapeDtypeStruct((s, h * NA_HEAD_DIM), F32),
        compiler_params=_params(("parallel", "arbitrary")),
    )(proj, proj, proj, tables)


def na_bwd(proj, tables, do, *, q_col0, k_col0, v_col0, rows_per_step=16, name):
    s = proj.shape[0]
    h = tables.shape[0]
    n_rows = s // GRID_W
    rb = min(rows_per_step, n_rows)
    tq = rb * GRID_W
    scale = NA_HEAD_DIM ** -0.5

    def body(q_ref, k_ref, v_ref, b_ref, do_ref, dq_ref, dk_ref, dv_ref, db_ref):
        i = pl.program_id(1)

        @pl.when(i == 0)
        def _():
            dk_ref[...] = jnp.zeros_like(dk_ref)
            dv_ref[...] = jnp.zeros_like(dv_ref)
            db_ref[...] = jnp.zeros_like(db_ref)

        def window(rl):
            row_start, which = _na_window(i * rb + rl, n_rows)
            return pl.ds(pl.multiple_of(row_start * GRID_W, GRID_W), NA_WIN), which

        def query(rl):
            rows_ = pl.ds(rl * GRID_W, GRID_W)
            return q_ref[rows_, :].astype(BF16), do_ref[rows_, :]

        def scores(rl):
            q_row, do_row = query(rl)
            keys = window(rl)[0]
            return (lax.dot_general(q_row, k_ref[keys, :].astype(BF16), NT, preferred_element_type=F32),
                    lax.dot_general(do_row, v_ref[keys, :].astype(BF16), NT, preferred_element_type=F32))

        ahead = [scores(rl) for rl in range(rb)]
        for rl in range(rb):
            sc, dp = ahead[rl]
            q_row, do_row = query(rl)
            keys, which = window(rl)
            p = _na_probs(sc, b_ref[which])
            ds = p * (dp - jnp.sum(dp * p, axis=-1, keepdims=True))
            db_ref[which] += ds
            ds_b = (ds * scale).astype(BF16)
            dq_ref[pl.ds(rl * GRID_W, GRID_W), :] = lax.dot_general(
                ds_b, k_ref[keys, :].astype(BF16), NN, preferred_element_type=F32).astype(dq_ref.dtype)
            dk_ref[keys, :] += lax.dot_general(ds_b, q_row, TN, preferred_element_type=F32)
            dv_ref[keys, :] += lax.dot_general(p.astype(BF16), do_row, TN, preferred_element_type=F32)

    head = lambda col0: pl.BlockSpec((s, NA_HEAD_DIM), lambda hh, i: (0, col0 + hh))
    rows_of = lambda col0: pl.BlockSpec((tq, NA_HEAD_DIM), lambda hh, i: (i, col0 + hh))
    table = pl.BlockSpec((None, NA_KH, GRID_W, NA_WIN), lambda hh, i: (hh, 0, 0, 0))
    hw = h * NA_HEAD_DIM
    return pl.pallas_call(
        body, name=name, grid=(h, n_rows // rb),
        in_specs=[rows_of(q_col0), head(k_col0), head(v_col0), table, rows_of(0)],
        out_specs=[rows_of(0), head(0), head(0), table],
        out_shape=[jax.ShapeDtypeStruct((s, hw), BF16), jax.ShapeDtypeStruct((s, hw), F32),
                   jax.ShapeDtypeStruct((s, hw), F32), jax.ShapeDtypeStruct(tables.shape, F32)],
        compiler_params=_params(("parallel", "arbitrary")),
    )(proj, proj, proj, tables, do)


def _na_tables():
    qc = np.arange(GRID_W)[:, None]
    kc = np.arange(GRID_W)[None, :]
    col_start = np.clip(qc - NA_KW // 2, 0, GRID_W - NA_KW)
    col_ok = (kc >= col_start) & (kc < col_start + NA_KW)
    dx = np.clip(kc - qc, -(NA_KW - 1), NA_KW - 1) + (NA_KW - 1)
    return col_ok, dx


def na_bias_tables(rpb_l):
    col_ok, _ = _na_tables()
    side = GRID_W - NA_KW
    padded = jnp.pad(rpb_l, ((0, 0), (0, 0), (side, side)))
    t = jnp.stack([padded[:, :, GRID_W - 1 - qc:2 * GRID_W - 1 - qc] for qc in range(GRID_W)], axis=2)
    t = jnp.where(col_ok[None, None], t, NEG)
    t = t.transpose(0, 2, 1, 3)
    return jnp.stack([t[:, :, w:w + NA_KH].reshape(t.shape[0], GRID_W, NA_WIN) for w in range(NA_KH)], axis=1)


def na_bias_grad(dtables, *, name):
    h = dtables.shape[0]
    _, dx = _na_tables()
    n_dy, n_dx = 2 * NA_KH - 1, 2 * NA_KW - 1
    d5 = dtables.reshape(h, NA_KH, GRID_W, NA_KH, GRID_W)
    t = sum(jnp.pad(d5[:, w], ((0, 0), (0, 0), (w, n_dy - NA_KH - w), (0, 0))) for w in range(NA_KH))
    onehot = np.zeros((GRID_W * GRID_W, LANES), np.float32)
    onehot[np.arange(GRID_W * GRID_W), dx.reshape(-1)] = 1.0
    t = t.transpose(0, 2, 1, 3).reshape(h * n_dy, GRID_W * GRID_W)
    t = jnp.pad(t, ((0, (-t.shape[0]) % 8), (0, 0)))
    out = mm(t, jnp.asarray(onehot), mode="nn", name=name, exact=True, tk=1024)
    return out[:h * n_dy, :n_dx].reshape(h, n_dy, n_dx)


def _flip(v, bit):
    return 1 - v if bit else v


class Part:
    def __init__(self, shape, kind, dtype):
        self.r, self.c = shape
        self.kind, self.dtype = kind, dtype

    @property
    def whole_shape(self):
        return {"row": (N_DEV * self.r, self.c), "col": (self.r, N_DEV * self.c),
                "packed": (N_DEV, self.r, self.c)}[self.kind]

    @property
    def packed_shape(self):
        return (N_DEV, self.r, self.c)

    def shard_of(self, ref, j):
        if self.kind == "row":
            return ref.at[pl.ds(pl.multiple_of(j * self.r, 8), self.r), :]
        if self.kind == "col":
            return ref.at[:, pl.ds(pl.multiple_of(j * self.c, LANES), self.c)]
        return ref.at[j]


def comm_scratch(n_parts):
    n = n_parts * (N_DEV - 1)
    return [pltpu.SemaphoreType.DMA((n,)), pltpu.SemaphoreType.DMA((n,)), pltpu.SemaphoreType.DMA((n_parts,))]


def gather_plan(parts, x_refs, out_refs, send_sems, recv_sems, local_sems):
    x, y, c = lax.axis_index("x"), lax.axis_index("y"), lax.axis_index("c")
    me, sibling = (x, y, c), (x, y, 1 - c)
    chips = [(1 - x, y), (x, 1 - y), (1 - x, 1 - y)]

    def place(w, px, py, pc):
        return parts[w].shard_of(out_refs[w], 4 * px + 2 * py + pc)

    def copy(k, blk, to, own=False):
        return [pltpu.make_async_remote_copy(
            src_ref=x_refs[w] if own else place(w, *blk), dst_ref=place(w, *blk),
            send_sem=send_sems.at[w * (N_DEV - 1) + k], recv_sem=recv_sems.at[w * (N_DEV - 1) + k],
            device_id=to, device_id_type=pl.DeviceIdType.MESH) for w in range(len(parts))]

    def mine():
        return [pltpu.make_async_copy(x_refs[w], place(w, *me), local_sems.at[w]) for w in range(len(parts))]

    def first():
        return copy(0, me, sibling, own=True) + [cp for j, chip in enumerate(chips)
                                                 for cp in copy(1 + j, me, (*chip, c), own=True)]

    def passed(j):
        return copy(4 + j, (*chips[j], c), sibling)

    def start():
        for cp in mine() + first():
            cp.start()

    def forward():
        for j, chip in enumerate(chips):
            for cp in copy(1 + j, (*chip, c), me):
                cp.wait_recv()
            for cp in passed(j):
                cp.start()

    def finish():
        for cp in copy(0, sibling, me):
            cp.wait_recv()
        for j, chip in enumerate(chips):
            for cp in copy(4 + j, (*chip, 1 - c), me):
                cp.wait_recv()
        for cp in first() + [cp for j in range(len(chips)) for cp in passed(j)]:
            cp.wait_send()
        for cp in mine():
            cp.wait()

    return start, forward, finish


def exchange_plan(parts, x_refs, out_refs, send_sems, recv_sems, local_sems):
    x, y, c = lax.axis_index("x"), lax.axis_index("y"), lax.axis_index("c")
    me = 4 * x + 2 * y + c

    def peer_of(k):
        peer = (_flip(x, k & 4), _flip(y, k & 2), _flip(c, k & 1))
        return peer, 4 * peer[0] + 2 * peer[1] + peer[2]

    def copies(k, arriving):
        peer, theirs = peer_of(k)
        return [pltpu.make_async_remote_copy(
            src_ref=parts[w].shard_of(x_refs[w], me if arriving else theirs),
            dst_ref=out_refs[w].at[theirs if arriving else me],
            send_sem=send_sems.at[w * (N_DEV - 1) + k - 1], recv_sem=recv_sems.at[w * (N_DEV - 1) + k - 1],
            device_id=peer, device_id_type=pl.DeviceIdType.MESH) for w in range(len(parts))]

    def mine():
        return [pltpu.make_async_copy(parts[w].shard_of(x_refs[w], me), out_refs[w].at[me], local_sems.at[w])
                for w in range(len(parts))]

    def start():
        for cp in mine():
            cp.start()
        for k in range(1, N_DEV):
            for cp in copies(k, arriving=False):
                cp.start()

    def finish():
        for k in range(1, N_DEV):
            for cp in copies(k, arriving=True):
                cp.wait_recv()
        for k in range(1, N_DEV):
            for cp in copies(k, arriving=False):
                cp.wait_send()
        for cp in mine():
            cp.wait()

    return start, finish


class Collective:
    def __init__(self, kind, parts, arrays):
        self.kind, self.parts, self.arrays = kind, parts, list(arrays)
        self.n = len(parts)
        shapes = [p.whole_shape if kind == "gather" else p.packed_shape for p in parts]
        self.out_shape = [jax.ShapeDtypeStruct(s, p.dtype) for s, p in zip(shapes, parts)]

    def extend(self, in_specs, out_specs, out_shape, scratch, args):
        any_spec = pl.BlockSpec(memory_space=pl.ANY)
        return (in_specs + [any_spec] * self.n, out_specs + [any_spec] * self.n, out_shape + self.out_shape,
                scratch + comm_scratch(self.n), args + tuple(self.arrays))

    def steps(self, x_refs, out_refs, sems):
        plan = (gather_plan if self.kind == "gather" else exchange_plan)(self.parts, x_refs, out_refs, *sems)
        return plan[0], (plan[1] if len(plan) == 3 else None), plan[-1]

    def run(self, name):
        n = self.n

        def body(*refs):
            for step in self.steps(refs[:n], refs[n:2 * n], refs[2 * n:]):
                if step is not None:
                    step()

        in_specs, out_specs, out_shape, scratch, args = self.extend([], [], [], [], ())
        return pl.pallas_call(body, name=name, in_specs=in_specs, out_specs=out_specs, out_shape=out_shape,
                              scratch_shapes=scratch)(*args)


def _grid_flags(grid):
    ids = [pl.program_id(ax) for ax in range(len(grid))]
    inner_zero = functools.reduce(jnp.logical_and, [i == 0 for i in ids[1:]])
    first = jnp.logical_and(ids[0] == 0, inner_zero)
    middle = jnp.logical_and(ids[0] == grid[0] - 1, inner_zero)
    last = functools.reduce(jnp.logical_and, [i == g - 1 for i, g in zip(ids, grid)])
    return first, middle, last


SHARDED = ("w_in", "w_uq", "w_ukv", "w_o_mla", "w_o_na", "w_out", "w_ff1", "w_ff2")
ROW_SHARDED = ("w_out", "w_ff2")
TRANSPOSED = ("w_in",)
REPLICATED = ("norm_mix", "norm_qa", "norm_kva", "rpb", "norm_mlp", "norm_final")
WEIGHTS = ("norm_mix", "w_in", "norm_qa", "w_uq", "norm_kva", "w_ukv", "rpb", "w_o_mla", "w_o_na", "w_out",
           "norm_mlp", "w_ff1", "w_ff2", "norm_final")


def part_of(name, shard_shape):
    r, c = shard_shape
    kind = "row" if name in ROW_SHARDED + TRANSPOSED else ("col" if c % LANES == 0 else "packed")
    return Part((r, c), kind, BF16)


def _whole(part, gathered):
    return gathered.transpose(1, 0, 2).reshape(part.r, -1) if part.kind == "packed" else gathered


def _for_exchange(part, full):
    return full.reshape(part.r, N_DEV, part.c).transpose(1, 0, 2) if part.kind == "packed" else full


class Dims:
    def __init__(self, x, w_in, norm_qa, norm_kva, rpb, w_o_mla, w_o_na, w_ff1):
        self.s, self.d = x.shape[1], x.shape[2]
        self.depth = w_in.shape[0]
        self.q_lora, self.kv_lora = norm_qa.shape[1], norm_kva.shape[1]
        self.mla_w, self.na_w = w_o_mla.shape[1], w_o_na.shape[1]
        self.mla_h, self.na_h = self.mla_w // V_HEAD, self.na_w // NA_HEAD_DIM
        self.d_ff = w_ff1.shape[2] * N_DEV
        assert rpb.shape[1] == self.na_h and self.s % GRID_W == 0 and self.s // GRID_W >= NA_KH
        self.in_lo = self.q_lora + self.kv_lora
        self.main_w = self.in_lo + 3 * self.na_w + 2 * self.d
        assert self.q_lora == self.kv_lora and self.in_lo % self.na_w == 0 and self.in_lo % LANES == 0
        assert (self.in_lo + 3 * self.na_w) % self.d == 0
        self.q_col0 = self.in_lo // NA_HEAD_DIM
        self.k_off = self.in_lo + self.na_w
        self.v_off = self.in_lo + 2 * self.na_w
        self.ga_col = (self.in_lo + 3 * self.na_w) // self.d
        self.gb_col = self.ga_col + 1


def _split_w_in(dm, wt):
    lo = dm.in_lo
    main = jnp.concatenate([wt[:lo], wt[lo + QK_ROPE:]], axis=0)
    kpe = jnp.pad(wt[lo:lo + QK_ROPE], ((0, LANES - QK_ROPE), (0, 0)))
    return main, kpe


def _join_w_in(dm, main, kpe):
    lo = dm.in_lo
    return jnp.concatenate([main[:lo], kpe[:QK_ROPE], main[lo:]], axis=0)


def _split_heads(w, h, widths):
    r = w.shape[0]
    w3 = w.reshape(r, h, sum(widths))
    out, o = [], 0
    for wd in widths:
        out.append(w3[:, :, o:o + wd].reshape(r, h * wd))
        o += wd
    return out


def _join_heads(parts, h):
    r = parts[0].shape[0]
    return jnp.concatenate([p.reshape(r, h, -1) for p in parts], axis=2).reshape(r, -1)


UQ_WIDTHS = (QK_NOPE, HALF_ROPE, HALF_ROPE)
UKV_WIDTHS = (QK_NOPE, V_HEAD)


def _by_head(parts, h):
    s = parts[0].shape[0]
    return jnp.concatenate([p.reshape(s, h, -1) for p in parts], axis=2).transpose(1, 0, 2)


def _from_head(t, widths):
    h, s, _ = t.shape
    t = t.transpose(1, 0, 2)
    out, o = [], 0
    for wd in widths:
        out.append(t[:, :, o:o + wd].reshape(s, h * wd))
        o += wd
    return out


def layer_fwd(dm, lname, x, w, g, cos_q, sin_q, cos_k, sin_k, ride, late_weights):
    s, h = dm.s, dm.mla_h
    u = rmsnorm_fwd(x, g["norm_mix"], name=f"{lname}_norm_mix")
    proj = mm(u, w["in_main"], mode="nt", name=f"{lname}_proj")
    kpe = mm(u, w["in_kpe"], mode="nt", name=f"{lname}_proj_kpe")
    qn = rmsnorm_fwd(proj, g["norm_qa"], width=dm.q_lora, col=0, name=f"{lname}_norm_qa")
    kvn = rmsnorm_fwd(proj, g["norm_kva"], width=dm.kv_lora, col=1, name=f"{lname}_norm_kva")
    q = mm(qn, w["uq"], mode="nn", name=f"{lname}_uq")
    kv = mm(kvn, w["ukv"], mode="nn", out_dtypes=(BF16,), name=f"{lname}_ukv")
    nope_w, half_w = h * QK_NOPE, h * HALF_ROPE
    q1, q2 = rope_pair(q[:, nope_w:nope_w + half_w], q[:, nope_w + half_w:], cos_q, sin_q, name=f"{lname}_rope_q")
    k1, k2 = rope_pair(kpe[:, :HALF_ROPE], kpe[:, HALF_ROPE:QK_ROPE], cos_k, sin_k, name=f"{lname}_rope_k")
    qh = _by_head([q[:, :nope_w].astype(BF16), q1, q2], h)
    kh = _by_head([kv[:, :nope_w], jnp.tile(k1, (1, h)), jnp.tile(k2, (1, h))], h)
    vt = kv[:, nope_w:].reshape(s, h, V_HEAD).transpose(1, 2, 0)
    o_a, lse, *gathered = mla_fwd(qh, kh, vt, name=f"{lname}_mla", ride=ride)
    w = {**w, **late_weights(gathered)}
    y_a = mm(o_a, w["o_mla"], mode="nn", out_dtypes=(BF16,), name=f"{lname}_o_mla")

    tables = na_bias_tables(g["rpb"])
    o_b = na_fwd(proj, tables, q_col0=dm.q_col0, k_col0=dm.k_off // NA_HEAD_DIM, v_col0=dm.v_off // NA_HEAD_DIM,
                 name=f"{lname}_na")
    y_b = mm(o_b, w["o_na"], mode="nn", out_dtypes=(BF16,), name=f"{lname}_o_na")

    merged = gate_fwd(proj, y_a, y_b, d=dm.d, ga_col=dm.ga_col, gb_col=dm.gb_col, name=f"{lname}_gate")
    x1 = mm(merged, w["out"], mode="nn", epi=lambda r, res: (r + res,), extras=(x,), name=f"{lname}_out")
    u2 = rmsnorm_fwd(x1, g["norm_mlp"], name=f"{lname}_norm_mlp")
    hid, act = mm(u2, w["ff1"], mode="nn", out_dtypes=(F32, BF16),
                  epi=lambda r: (r, jnp.square(jnp.maximum(r, 0.0))), name=f"{lname}_ff1")
    x2 = mm(act, w["ff2"], mode="nn", epi=lambda r, res: (r + res,), extras=(x1,), name=f"{lname}_ff2")
    saved = dict(x=x, u=u, proj=proj, qn=qn, kvn=kvn, kv=kv, qh=qh, kh=kh, o_a=o_a, lse=lse, y_a=y_a,
                 tables=tables, o_b=o_b, y_b=y_b, merged=merged, x1=x1, u2=u2, hid=hid, act=act)
    return x2, saved, w, gathered


def layer_bwd(dm, lname, dx2, w, g, sv, cos_q, sin_q, cos_k, sin_k, ride_of):
    h = dm.mla_h
    gw, gr = {}, {}
    gw["ff2"] = mm(sv["act"], dx2, mode="tn", out_dtypes=(BF16,), name=f"{lname}_d_ff2")
    dh = mm(dx2, w["ff2"], mode="nt", out_dtypes=(BF16,), extras=(sv["hid"],),
            epi=lambda r, hv: (r * (2.0 * jnp.maximum(hv, 0.0)),), name=f"{lname}_d_act")
    gw["ff1"] = mm(sv["u2"], dh, mode="tn", out_dtypes=(BF16,), name=f"{lname}_d_ff1")
    du2 = mm(dh, w["ff1"], mode="nt", name=f"{lname}_d_u2")
    dx1, gr["norm_mlp"] = rmsnorm_bwd(sv["x1"], g["norm_mlp"], du2, dx2, name=f"{lname}_d_norm_mlp")
    gw["out"] = mm(sv["merged"], dx1, mode="tn", out_dtypes=(BF16,), name=f"{lname}_d_out")
    dmerged = mm(dx1, w["out"], mode="nt", out_dtypes=(BF16,), name=f"{lname}_d_merged")
    dy_a, dy_b, dga, dgb = gate_bwd(sv["proj"], sv["y_a"], sv["y_b"], dmerged, d=dm.d, ga_col=dm.ga_col,
                                    gb_col=dm.gb_col, name=f"{lname}_d_gate")
    gw["o_na"] = mm(sv["o_b"], dy_b, mode="tn", out_dtypes=(BF16,), name=f"{lname}_d_o_na")
    do_b = mm(dy_b, w["o_na"], mode="nt", out_dtypes=(BF16,), name=f"{lname}_d_ob")
    dq_na, dk_na, dv_na, dtables = na_bwd(sv["proj"], sv["tables"], do_b, q_col0=dm.q_col0,
                                          k_col0=dm.k_off // NA_HEAD_DIM, v_col0=dm.v_off // NA_HEAD_DIM,
                                          name=f"{lname}_d_na")
    gr["rpb"] = na_bias_grad(dtables, name=f"{lname}_d_rpb")
    gw["o_mla"] = mm(sv["o_a"], dy_a, mode="tn", out_dtypes=(BF16,), name=f"{lname}_d_o_mla")
    do_a = mm(dy_a, w["o_mla"], mode="nt", out_dtypes=(BF16,), name=f"{lname}_d_oa")
    delta = mla_delta(do_a, sv["o_a"], h, name=f"{lname}_d_mla_delta").reshape(h, 1, dm.s)
    dqt, dkh, dv, *received = mla_bwd(sv["qh"], sv["kh"], sv["kh"].transpose(0, 2, 1), sv["kv"], do_a, sv["lse"],
                                      delta, v_col0=h, name=f"{lname}_d_mla", ride=ride_of(gw))
    dqh = dqt.transpose(0, 1, 3, 2).reshape(h, dm.s, QK_DIM)
    dq_nope, dq1, dq2 = _from_head(dqh, UQ_WIDTHS)
    dq1, dq2 = rope_pair(dq1, dq2, cos_q, -sin_q, name=f"{lname}_d_rope_q")
    dq = jnp.concatenate([dq_nope.astype(BF16), dq1, dq2], axis=1)
    gw["uq"] = mm(sv["qn"], dq, mode="tn", out_dtypes=(BF16,), name=f"{lname}_d_uq")
    dqn = mm(dq, w["uq"], mode="nt", name=f"{lname}_d_qn")
    dk_nope = _from_head(dkh[:, :, :QK_NOPE], (QK_NOPE,))[0]
    dk1, dk2 = rope_pair_headsum(dkh, cos_k, sin_k, name=f"{lname}_d_rope_k")
    dkv = jnp.concatenate([dk_nope.astype(BF16), dv], axis=1)
    gw["ukv"] = mm(sv["kvn"], dkv, mode="tn", out_dtypes=(BF16,), name=f"{lname}_d_ukv")
    dkvn = mm(dkv, w["ukv"], mode="nt", name=f"{lname}_d_kvn")
    dc_q, gr["norm_qa"] = rmsnorm_bwd(sv["proj"], g["norm_qa"], dqn, width=dm.q_lora, col=0, out_dtype=BF16,
                                      name=f"{lname}_d_norm_qa")
    dc_kv, gr["norm_kva"] = rmsnorm_bwd(sv["proj"], g["norm_kva"], dkvn, width=dm.kv_lora, col=1, out_dtype=BF16,
                                        name=f"{lname}_d_norm_kva")
    dproj = jnp.concatenate([dc_q, dc_kv, dq_na, dk_na.astype(BF16), dv_na.astype(BF16), dga, dgb], axis=1)
    dkpe = jnp.concatenate([dk1, dk2, jnp.zeros((dm.s, LANES - QK_ROPE), BF16)], axis=1)
    gw["in_main"] = mm(dproj, sv["u"], mode="tn", out_dtypes=(BF16,), name=f"{lname}_d_in")
    gw["in_kpe"] = mm(dkpe, sv["u"], mode="tn", out_dtypes=(BF16,), name=f"{lname}_d_in_kpe")
    du_k = mm(dkpe, w["in_kpe"], mode="nn", name=f"{lname}_d_u_kpe")
    du = mm(dproj, w["in_main"], mode="nn", epi=lambda r, res: (r + res,), extras=(du_k,), name=f"{lname}_d_u")
    dx, gr["norm_mix"] = rmsnorm_bwd(sv["x"], g["norm_mix"], du, dx1, name=f"{lname}_d_norm_mix")
    return dx, gw, gr, received


EARLY = ("w_in", "w_uq", "w_ukv")
LATE = ("w_o_mla", "w_o_na", "w_out", "w_ff1", "w_ff2")
LATE_KEYS = ("o_mla", "o_na", "out", "ff1", "ff2")


def _early_weights(dm, full):
    main, kpe = _split_w_in(dm, full["w_in"])
    return dict(in_main=main, in_kpe=kpe,
                uq=jnp.concatenate(_split_heads(full["w_uq"], dm.mla_h, UQ_WIDTHS), axis=1),
                ukv=jnp.concatenate(_split_heads(full["w_ukv"], dm.mla_h, UKV_WIDTHS), axis=1))


def _early_grads(dm, gw):
    h = dm.mla_h
    nope_w, half_w = h * QK_NOPE, h * HALF_ROPE
    uq = gw["uq"]
    ukv = gw["ukv"]
    return {"w_in": _join_w_in(dm, gw["in_main"], gw["in_kpe"]),
            "w_uq": _join_heads([uq[:, :nope_w], uq[:, nope_w:nope_w + half_w], uq[:, nope_w + half_w:]], h),
            "w_ukv": _join_heads([ukv[:, :nope_w], ukv[:, nope_w:]], h)}


def _pack_replicated(dm, parts):
    flat = jnp.concatenate([parts[n].reshape(-1) for n in REPLICATED])
    n = flat.shape[0]
    rows_ = -(-n // (8 * LANES)) * 8
    return jnp.pad(flat, (0, rows_ * LANES - n)).reshape(rows_, LANES)


def kernel(x, norm_mix, w_in, norm_qa, w_uq, norm_kva, w_ukv, rpb, w_o_mla, w_o_na, w_out, norm_mlp, w_ff1, w_ff2, norm_final, loss_target, m_norm_mix, m_w_in, m_norm_qa, m_w_uq, m_norm_kva, m_w_ukv, m_rpb, m_w_o_mla, m_w_o_na, m_w_out, m_norm_mlp, m_w_ff1, m_w_ff2, m_norm_final, v_norm_mix, v_w_in, v_norm_qa, v_w_uq, v_norm_kva, v_w_ukv, v_rpb, v_w_o_mla, v_w_o_na, v_w_out, v_norm_mlp, v_w_ff1, v_w_ff2, v_norm_final):
    dm = Dims(x, w_in, norm_qa, norm_kva, rpb, w_o_mla, w_o_na, w_ff1)
    params = dict(norm_mix=norm_mix, w_in=w_in, norm_qa=norm_qa, w_uq=w_uq, norm_kva=norm_kva, w_ukv=w_ukv, rpb=rpb,
                  w_o_mla=w_o_mla, w_o_na=w_o_na, w_out=w_out, norm_mlp=norm_mlp, w_ff1=w_ff1, w_ff2=w_ff2,
                  norm_final=norm_final)
    mom_m = dict(norm_mix=m_norm_mix, w_in=m_w_in, norm_qa=m_norm_qa, w_uq=m_w_uq, norm_kva=m_norm_kva, w_ukv=m_w_ukv,
                 rpb=m_rpb, w_o_mla=m_w_o_mla, w_o_na=m_w_o_na, w_out=m_w_out, norm_mlp=m_norm_mlp, w_ff1=m_w_ff1,
                 w_ff2=m_w_ff2, norm_final=m_norm_final)
    mom_v = dict(norm_mix=v_norm_mix, w_in=v_w_in, norm_qa=v_norm_qa, w_uq=v_w_uq, norm_kva=v_norm_kva, w_ukv=v_w_ukv,
                 rpb=v_rpb, w_o_mla=v_w_o_mla, w_o_na=v_w_o_na, w_out=v_w_out, norm_mlp=v_norm_mlp, w_ff1=v_w_ff1,
                 w_ff2=v_w_ff2, norm_final=v_norm_final)
    depth, s, h = dm.depth, dm.s, dm.mla_h

    pos = jnp.arange(s, dtype=F32)
    inv_freq = 1.0 / (ROPE_THETA ** (jnp.arange(0, QK_ROPE, 2, dtype=F32) / QK_ROPE))
    ang = pos[:, None] * inv_freq[None, :]
    cos_k, sin_k = jnp.cos(ang), jnp.sin(ang)
    cos_q, sin_q = jnp.tile(cos_k, (1, h)), jnp.tile(sin_k, (1, h))

    held = lambda d, n: d[n].transpose(0, 2, 1) if n in TRANSPOSED else d[n]
    part = {n: part_of(n, held(params, n).shape[1:]) for n in SHARDED}
    shard = lambda n, l: held(params, n)[l].astype(BF16)
    whole = lambda names, arrays: {n: _whole(part[n], a) for n, a in zip(names, arrays)}
    gains = [dict(norm_mix=norm_mix[l][None], norm_qa=norm_qa[l][None], norm_kva=norm_kva[l][None],
                  norm_mlp=norm_mlp[l][None], rpb=rpb[l]) for l in range(depth)]

    xl = x[0]
    saved, weights = [], []
    early = Collective("gather", [part[n] for n in EARLY], [shard(n, 0) for n in EARLY]).run("gather_weights")
    for l in range(depth):
        nxt = EARLY if l + 1 < depth else ()
        ride = Collective("gather", [part[n] for n in LATE + nxt],
                          [shard(n, l) for n in LATE] + [shard(n, l + 1) for n in nxt])
        late_weights = lambda got: dict(zip(LATE_KEYS, whole(LATE, got[:len(LATE)]).values()))
        xl, sv, w, got = layer_fwd(dm, "fwd", xl, _early_weights(dm, whole(EARLY, early)), gains[l], cos_q, sin_q,
                                   cos_k, sin_k, ride, late_weights)
        early = got[len(LATE):]
        saved.append(sv)
        weights.append(w)
    dx, g_final, loss_part = loss_head(xl, norm_final[None], loss_target[0], name="loss_head")
    loss = lax.psum(loss_part[0, 0], MESH_AXES)

    rep = {n: [None] * depth for n in REPLICATED if n != "norm_final"}
    recv = [{} for _ in range(depth)]
    pending = {}
    for l in reversed(range(depth)):
        def ride_of(gw, pending=pending):
            ready = {**dict(zip(LATE, [gw[k] for k in LATE_KEYS])), **pending}
            return Collective("exchange", [part[n] for n in ready], [_for_exchange(part[n], a) for n, a in ready.items()])

        dx, gw, gr, received = layer_bwd(dm, "bwd", dx, weights[l], gains[l], saved[l], cos_q, sin_q, cos_k, sin_k,
                                         ride_of)
        recv[l].update(zip(LATE, received[:len(LATE)]))
        if pending:
            recv[l + 1].update(zip(EARLY, received[len(LATE):]))
        for n in gr:
            rep[n][l] = gr[n]
        pending = _early_grads(dm, gw)
    last = Collective("exchange", [part[n] for n in EARLY], [_for_exchange(part[n], pending[n]) for n in EARLY])
    recv[0].update(zip(EARLY, last.run("scatter_grads")))
    rep_parts = {n: jnp.stack(rep[n]) for n in rep}
    rep_parts["norm_final"] = g_final
    packed = _pack_replicated(dm, rep_parts)
    rep_all = Collective("gather", [Part(packed.shape, "packed", F32)], [packed]).run("gather_small_grads")[0]

    outs = {}
    for n in SHARDED:
        res = adamw_layers(held(params, n), held(mom_m, n), held(mom_v, n), [recv[l][n] for l in range(depth)],
                           name=f"adamw_{n}")
        outs[n] = [a.transpose(0, 2, 1) for a in res] if n in TRANSPOSED else res
    n_rep = sum(int(np.prod(params[n].shape)) for n in REPLICATED)
    pack = lambda d: _pack_replicated(dm, d)
    res = adamw(pack(params), pack(mom_m), pack(mom_v), rep_all, name="adamw_replicated")
    off = 0
    for n in REPLICATED:
        size = int(np.prod(params[n].shape))
        outs[n] = [a.reshape(-1)[off:off + size].reshape(params[n].shape) for a in res]
        off += size
    assert off == n_rep

    grad_x = dx[None]
    return (loss, grad_x, *[outs[n][0] for n in WEIGHTS], *[outs[n][1] for n in WEIGHTS],
            *[outs[n][2] for n in WEIGHTS], *[outs[n][3] for n in WEIGHTS])
```
